```python
import math
import jax
import jax.numpy as jnp
from jax import lax
import numpy as np

D_MODEL = 1024
BATCH = 8
SEQ = 4096
DEPTH = 2

GRID_W = 64
CTX_LEN = 256
N_MOD = 6
EPS = 1e-6

N_BRANCH = 4
BRANCH_W = D_MODEL // N_BRANCH
FN_GROUPS = 4
FN_GW = BRANCH_W // FN_GROUPS
RW_HD = 64
RW_HEADS = BRANCH_W // RW_HD
RW_DECAY_RANK = 32
RW_ICL_RANK = 32
RW_GATE_RANK = 64
RW_COLS = 3 * BRANCH_W + RW_DECAY_RANK + RW_ICL_RANK + RW_GATE_RANK
RW_DECAY_SCALE = math.exp(-0.5)
RW_GN_EPS = 64e-5
DN_HD = 64
DN_HEADS = BRANCH_W // DN_HD
DN_CONV = 5
DN_CHUNK = 64
S5_GW = 16
S5_GROUPS = BRANCH_W // S5_GW
S5_STATE = 64
N_EXPERTS = 16
N_EXPERT_GROUPS = 4
EXPERTS_PER_GROUP = N_EXPERTS // N_EXPERT_GROUPS
TOP_K = 2
D_EXPERT = 256

IN_SIZES = (BRANCH_W, RW_COLS, 3 * BRANCH_W, BRANCH_W, 2 * DN_HEADS, 2 * DN_HEADS, BRANCH_W, N_BRANCH * D_MODEL)
N_IN = sum(IN_SIZES)
RW_SIZES = (BRANCH_W, BRANCH_W, BRANCH_W, RW_DECAY_RANK, RW_ICL_RANK, RW_GATE_RANK)

kernel_name = "hybrid_parallel_mixer_moe_trunk"


def _split_points(sizes):
    return [int(v) for v in np.cumsum(sizes)[:-1]]


def _oriented(t, d):
    return jnp.flip(t, axis=1) if d == 1 else t


def rms_norm(x, g):
    xf = x.astype(jnp.float32)
    y = xf * lax.rsqrt(jnp.mean(xf * xf, axis=-1, keepdims=True) + EPS)
    return (y * g.astype(jnp.float32)).astype(x.dtype)


def l2_normalize(t):
    return t * lax.rsqrt(jnp.sum(t * t, axis=-1, keepdims=True) + EPS)


def adaln(cond, w_mod, b_mod):
    m = jax.nn.silu(cond) @ w_mod + b_mod
    return jnp.split(m[..., None, :], N_MOD, axis=-1)


def raster_to_columns(t, rows):
    b, l, ch = t.shape
    return t.reshape(b, rows, GRID_W, ch).swapaxes(1, 2).reshape(b, l, ch)


def columns_to_raster(t, rows):
    b, l, ch = t.shape
    return t.reshape(b, GRID_W, rows, ch).swapaxes(1, 2).reshape(b, l, ch)


def centred_token_shift(p, mu_prev, mu_next):
    prev = jnp.pad(p, ((0, 0), (1, 0), (0, 0)))[:, :-1]
    nxt = jnp.pad(p, ((0, 0), (0, 1), (0, 0)))[:, 1:]
    return p + mu_prev * (prev - p) + mu_next * (nxt - p)


def centred_depthwise_conv(u, w):
    pad = w.shape[0] // 2
    return lax.conv_general_dilated(u, w[:, None, :].astype(u.dtype), (1,), [(pad, pad)],
                                    dimension_numbers=("NWC", "WIO", "NWC"),
                                    feature_group_count=u.shape[-1])


def fourier_mix(u):
    b, l, _ = u.shape
    ug = u.astype(jnp.float32).reshape(b, l, FN_GROUPS, FN_GW)
    f = jnp.fft.fftn(ug, axes=(1, 3), norm="ortho")
    return jnp.real(f).reshape(b, l, BRANCH_W).astype(u.dtype)


def rwkv_scan(r, w, k, v, kk, a, s0):
    def step(s, inp):
        r_t, w_t, k_t, v_t, kk_t, a_t = inp
        s_kk = jnp.einsum("bhvk,bhk->bhv", s, -kk_t)
        s = (s * w_t[:, :, None, :]
             + jnp.einsum("bhv,bhk->bhvk", s_kk, kk_t * a_t)
             + jnp.einsum("bhv,bhk->bhvk", v_t, k_t))
        return s, jnp.einsum("bhvk,bhk->bhv", s, r_t)
    xs = tuple(jnp.moveaxis(t, 1, 0) for t in (r, w, k, v, kk, a))
    s_fin, ys = lax.scan(step, s0, xs)
    return jnp.moveaxis(ys, 0, 1), s_fin


def rwkv_mixer(p_c, p_l, mu, w0, w_up, a0, a_up, k_k, k_a, r_k, g_up, ln_g, ln_b, want_ctx):
    f32 = jnp.float32

    def heads(t):
        return t.reshape(t.shape[0], t.shape[1], RW_HEADS, RW_HD)

    def streams(p):
        p = centred_token_shift(p, mu[0], mu[1]).astype(f32)
        return jnp.split(p, _split_points(RW_SIZES), axis=-1)

    def scan_inputs(s, d):
        r, k, v, wl, al, _ = s
        w = jnp.exp(-RW_DECAY_SCALE * jax.nn.sigmoid(w0[d] + jnp.tanh(wl) @ w_up[d]))
        a = jax.nn.sigmoid(a0[d] + al @ a_up[d])
        kk = l2_normalize(heads(k * k_k))
        k_mod = k * (1.0 + (a - 1.0) * k_a)
        out = (heads(r), heads(w), heads(k_mod), heads(v), kk, heads(a))
        return tuple(_oriented(t, d) for t in out)

    s_c, s_l = streams(p_c), streams(p_l)
    b = p_l.shape[0]
    y_c, y_l = 0.0, 0.0
    for d in range(2):
        s0 = jnp.zeros((b, RW_HEADS, RW_HD, RW_HD), f32)
        yc_d, s_ctx = rwkv_scan(*scan_inputs(s_c, d), s0)
        yl_d, _ = rwkv_scan(*scan_inputs(s_l, d), s_ctx)
        y_c = y_c + _oriented(yc_d, d)
        y_l = y_l + _oriented(yl_d, d)

    def post(y, s, dtype):
        r, k, v, _, _, gl = s
        mean = jnp.mean(y, axis=-1, keepdims=True)
        var = jnp.mean(jnp.square(y - mean), axis=-1, keepdims=True)
        yn = ((y - mean) * lax.rsqrt(var + RW_GN_EPS)).reshape(r.shape) * ln_g + ln_b
        bonus = (jnp.sum(heads(r) * heads(k) * r_k, axis=-1, keepdims=True) * heads(v)).reshape(r.shape)
        g = jax.nn.sigmoid(gl) @ g_up
        return ((yn + bonus) * g).astype(dtype)

    out_l = post(y_l, s_l, p_l.dtype)
    out_c = post(y_c, s_c, p_c.dtype) if want_ctx else None
    return out_c, out_l


def gated_delta_chunked(q, k, v, log_a, beta, s0):
    b, l, h, n = q.shape
    nc, cs = l // DN_CHUNK, DN_CHUNK

    def chunk(t):
        return jnp.moveaxis(t.reshape((b, nc, cs) + t.shape[2:]), 2, 3)

    q = chunk(q) * (n ** -0.5)
    k = chunk(k)
    v = chunk(v)
    g = jnp.cumsum(chunk(log_a), axis=-1)
    bt = chunk(beta)
    pos = jnp.arange(cs)
    incl = pos[:, None] >= pos[None, :]
    strict = pos[:, None] > pos[None, :]
    decay = jnp.exp(jnp.where(incl, g[..., :, None] - g[..., None, :], -jnp.inf))
    kb = k * bt[..., None]
    a_low = jnp.where(strict, jnp.einsum("bnhik,bnhjk->bnhij", kb, k) * decay, 0.0)
    m = a_low + jnp.eye(cs, dtype=q.dtype)
    rhs = jnp.concatenate([v * bt[..., None], kb * jnp.exp(g)[..., None]], axis=-1)
    sol = lax.linalg.triangular_solve(m, rhs, left_side=True, lower=True, unit_diagonal=True)
    u_val, w_dec = sol[..., :n], sol[..., n:]
    attn = jnp.einsum("bnhik,bnhjk->bnhij", q, k) * decay
    q_dec = q * jnp.exp(g)[..., None]
    k_tail = k * jnp.exp(g[..., -1:] - g)[..., None]
    g_tot = jnp.exp(g[..., -1])

    def step(s, inp):
        u_c, w_c, attn_c, qd_c, kt_c, gt_c = inp
        v_new = u_c - jnp.einsum("bhck,bhkv->bhcv", w_c, s)
        o = jnp.einsum("bhck,bhkv->bhcv", qd_c, s) + jnp.einsum("bhij,bhjv->bhiv", attn_c, v_new)
        s = s * gt_c[..., None, None] + jnp.einsum("bhck,bhcv->bhkv", kt_c, v_new)
        return s, o

    xs = tuple(jnp.moveaxis(t, 1, 0) for t in (u_val, w_dec, attn, q_dec, k_tail, g_tot))
    s_fin, o = lax.scan(step, s0, xs)
    o = jnp.moveaxis(jnp.moveaxis(o, 0, 1), 3, 2).reshape(b, l, h, n)
    return o, s_fin


def deltanet_mixer(qkv_c, qkv_l, gate_c, gate_l, al_c, al_l, be_c, be_l, conv_w, a_log, dt_bias, norm_g, want_ctx):
    f32 = jnp.float32

    def prep(qkv, al, be):
        b, l, _ = qkv.shape
        u = jax.nn.silu(centred_depthwise_conv(qkv, conv_w)).astype(f32).reshape(b, l, 3, DN_HEADS, DN_HD)
        q, k, v = l2_normalize(u[:, :, 0]), l2_normalize(u[:, :, 1]), u[:, :, 2]
        al = al.astype(f32).reshape(b, l, 2, DN_HEADS)
        be = be.astype(f32).reshape(b, l, 2, DN_HEADS)
        log_a = -jnp.exp(a_log) * jax.nn.softplus(al + dt_bias)
        return q, k, v, log_a, jax.nn.sigmoid(be)

    def dir_inputs(p, d):
        q, k, v, la, bt = p
        return tuple(_oriented(t, d) for t in (q, k, v, la[:, :, d], bt[:, :, d]))

    p_c, p_l = prep(qkv_c, al_c, be_c), prep(qkv_l, al_l, be_l)
    b = qkv_l.shape[0]
    o_c, o_l = 0.0, 0.0
    for d in range(2):
        s0 = jnp.zeros((b, DN_HEADS, DN_HD, DN_HD), f32)
        oc_d, s_ctx = gated_delta_chunked(*dir_inputs(p_c, d), s0)
        ol_d, _ = gated_delta_chunked(*dir_inputs(p_l, d), s_ctx)
        o_c = o_c + _oriented(oc_d, d)
        o_l = o_l + _oriented(ol_d, d)

    def post(o, gate):
        b_, l_ = o.shape[:2]
        on = o * lax.rsqrt(jnp.mean(o * o, axis=-1, keepdims=True) + EPS) * norm_g
        return (on.reshape(b_, l_, BRANCH_W) * jax.nn.silu(gate.astype(f32))).astype(gate.dtype)

    out_l = post(o_l, gate_l)
    out_c = post(o_c, gate_c) if want_ctx else None
    return out_c, out_l


def s5_scan(u, lam_bar, b_bar, c_mat, s0):
    bu = jnp.einsum("gph,blgh->blgp", b_bar, u.astype(jnp.complex64))
    bu = bu.at[:, 0].add(lam_bar * s0)
    a = jnp.broadcast_to(lam_bar, bu.shape)

    def combine(e1, e2):
        a1, b1 = e1
        a2, b2 = e2
        return a1 * a2, a2 * b1 + b2

    _, states = lax.associative_scan(combine, (a, bu), axis=1)
    y = jnp.real(jnp.einsum("ghp,blgp->blgh", c_mat, states))
    return y, states[:, -1]


def s5_mixer(u_c, u_l, rows, lam_re, lam_im, log_step, b_re, b_im, c_re, c_im, d_skip, w_glu, b_glu, want_ctx):
    f32 = jnp.float32

    def groups(u):
        return u.astype(f32).reshape(u.shape[0], u.shape[1], S5_GROUPS, S5_GW)

    uc = groups(u_c)
    ul = groups(raster_to_columns(u_l, rows))
    dg = d_skip.astype(f32).reshape(S5_GROUPS, S5_GW)
    y_c, y_l = uc * dg, ul * dg
    b = u_l.shape[0]
    for d in range(2):
        lam = lax.complex(lam_re[d].astype(f32), lam_im[d].astype(f32))
        lam_bar = jnp.exp(lam * jnp.exp(log_step[d].astype(f32))[:, None])
        b_bar = ((lam_bar - 1.0) / lam)[..., None] * lax.complex(b_re[d].astype(f32), b_im[d].astype(f32))
        c_mat = lax.complex(c_re[d].astype(f32), c_im[d].astype(f32))
        s0 = jnp.zeros((b, S5_GROUPS, S5_STATE), jnp.complex64)
        yc_d, s_ctx = s5_scan(_oriented(uc, d), lam_bar, b_bar, c_mat, s0)
        yl_d, _ = s5_scan(_oriented(ul, d), lam_bar, b_bar, c_mat, s_ctx)
        y_c = y_c + _oriented(yc_d, d)
        y_l = y_l + _oriented(yl_d, d)

    def glu(y):
        z = jax.nn.gelu(y.reshape(y.shape[0], y.shape[1], BRANCH_W))
        return z * jax.nn.sigmoid(z @ w_glu.astype(f32) + b_glu.astype(f32))

    out_l = columns_to_raster(glu(y_l), rows).astype(u_l.dtype)
    out_c = glu(y_c).astype(u_c.dtype) if want_ctx else None
    return out_c, out_l


def merge_branches(ys, gates, w_branch, w_out):
    g = gates.reshape(gates.shape[:-1] + (N_BRANCH, D_MODEL))
    m = 0.0
    for i, y in enumerate(ys):
        m = m + jax.nn.sigmoid(g[..., i, :]) * (y @ w_branch[i])
    return m @ w_out


def moe_ffn(h, router_w, router_b, w1, w3, w2):
    f32 = jnp.float32
    scores = jax.nn.sigmoid(h.astype(f32) @ router_w.astype(f32))
    sel = (scores + router_b.astype(f32)).reshape(scores.shape[:-1] + (N_EXPERT_GROUPS, EXPERTS_PER_GROUP))
    group_score = jnp.sum(lax.top_k(sel, TOP_K)[0], axis=-1)
    best = jnp.argmax(group_score, axis=-1)
    in_group = (best[..., None] == jnp.arange(N_EXPERT_GROUPS))[..., None]
    masked = jnp.where(in_group, sel, -jnp.inf).reshape(scores.shape)
    _, idx = lax.top_k(masked, TOP_K)
    w_sel = jnp.take_along_axis(scores, idx, axis=-1)
    w_sel = w_sel / jnp.sum(w_sel, axis=-1, keepdims=True)
    combine = jnp.sum(jax.nn.one_hot(idx, N_EXPERTS, dtype=f32) * w_sel[..., None], axis=-2)
    y = jnp.zeros(h.shape, f32)
    for e in range(N_EXPERTS):
        act = jax.nn.silu(h @ w1[e]) * (h @ w3[e])
        y = y + combine[..., e:e + 1] * (act @ w2[e])
    return y.astype(h.dtype)


def trunk_layer(x, xc, c, c_ctx, rows, want_ctx, router_w, router_b, lp):
    mod = adaln(c, lp["w_mod"], lp["b_mod"])
    mod_c = adaln(c_ctx, lp["w_mod"], lp["b_mod"])
    splits = _split_points(IN_SIZES)
    h = rms_norm(x, lp["norm1_g"]) * (1.0 + mod[1]) + mod[0]
    hc = rms_norm(xc, lp["norm1_g"]) * (1.0 + mod_c[1]) + mod_c[0]
    fn_l, rw_l, dqkv_l, dg_l, dal_l, dbe_l, s5_l, gate_l = jnp.split(h @ lp["w_in"], splits, axis=-1)
    fn_c, rw_c, dqkv_c, dg_c, dal_c, dbe_c, s5_c, gate_c = jnp.split(hc @ lp["w_in"], splits, axis=-1)

    ya_l = fourier_mix(fn_l)
    ya_c = fourier_mix(fn_c) if want_ctx else None
    yb_c, yb_l = rwkv_mixer(rw_c, rw_l, lp["rw_mu"], lp["rw_w0"], lp["rw_w_up"], lp["rw_a0"], lp["rw_a_up"],
                            lp["rw_k_k"], lp["rw_k_a"], lp["rw_r_k"], lp["rw_g_up"], lp["rw_ln_g"], lp["rw_ln_b"],
                            want_ctx)
    yc_c, yc_l = deltanet_mixer(dqkv_c, dqkv_l, dg_c, dg_l, dal_c, dal_l, dbe_c, dbe_l, lp["dn_conv"],
                                lp["dn_a_log"], lp["dn_dt_bias"], lp["dn_norm_g"], want_ctx)
    yd_c, yd_l = s5_mixer(s5_c, s5_l, rows, lp["s5_lam_re"], lp["s5_lam_im"], lp["s5_log_step"],
                          lp["s5_b_re"], lp["s5_b_im"], lp["s5_c_re"], lp["s5_c_im"], lp["s5_d"],
                          lp["s5_w_glu"], lp["s5_b_glu"], want_ctx)

    x = x + mod[2] * merge_branches((ya_l, yb_l, yc_l, yd_l), gate_l, lp["w_branch"], lp["w_out"])
    h2 = rms_norm(x, lp["norm2_g"]) * (1.0 + mod[4]) + mod[3]
    x = x + mod[5] * moe_ffn(h2, router_w, router_b, lp["moe_w1"], lp["moe_w3"], lp["moe_w2"])
    if want_ctx:
        xc = xc + mod_c[2] * merge_branches((ya_c, yb_c, yc_c, yd_c), gate_c, lp["w_branch"], lp["w_out"])
        h2c = rms_norm(xc, lp["norm2_g"]) * (1.0 + mod_c[4]) + mod_c[3]
        xc = xc + mod_c[5] * moe_ffn(h2c, router_w, router_b, lp["moe_w1"], lp["moe_w3"], lp["moe_w2"])
    return x, xc


def setup_inputs(seed: int = 0) -> dict:
    key = jax.random.key(seed)
    keys = iter(jax.random.split(key, 64))
    f32 = jnp.float32

    def nrm(shape, scale):
        return jax.random.normal(next(keys), shape, f32) * scale

    def uni(shape, lo, hi):
        return jax.random.uniform(next(keys), shape, f32, lo, hi)

    nl, d = DEPTH, D_MODEL
    dt = jnp.exp(uni((nl, 2, DN_HEADS), math.log(1e-3), math.log(1e-1)))
    n_idx = jnp.arange(S5_STATE, dtype=f32)
    return {
        "x": nrm((BATCH, SEQ, d), 1.0),
        "c": nrm((BATCH, d), 1.0),
        "ctx": nrm((BATCH, CTX_LEN, d), 1.0),
        "c_ctx": nrm((d,), 1.0),
        "w_mod": nrm((nl, d, N_MOD * d), 0.5 * d ** -0.5),
        "b_mod": nrm((nl, N_MOD * d), 0.02),
        "norm1_g": 1.0 + nrm((nl, d), 0.02),
        "norm2_g": 1.0 + nrm((nl, d), 0.02),
        "w_in": nrm((nl, d, N_IN), d ** -0.5),
        "rw_mu": uni((nl, 2, RW_COLS), 0.0, 0.5),
        "rw_w0": uni((nl, 2, BRANCH_W), -5.0, -0.5),
        "rw_w_up": nrm((nl, 2, RW_DECAY_RANK, BRANCH_W), 0.1),
        "rw_a0": nrm((nl, 2, BRANCH_W), 0.1),
        "rw_a_up": nrm((nl, 2, RW_ICL_RANK, BRANCH_W), 0.1),
        "rw_k_k": 0.85 + nrm((nl, BRANCH_W), 0.02),
        "rw_k_a": 1.0 + nrm((nl, BRANCH_W), 0.02),
        "rw_r_k": nrm((nl, RW_HEADS, RW_HD), 0.1),
        "rw_g_up": nrm((nl, RW_GATE_RANK, BRANCH_W), RW_GATE_RANK ** -0.5),
        "rw_ln_g": 1.0 + nrm((nl, BRANCH_W), 0.02),
        "rw_ln_b": nrm((nl, BRANCH_W), 0.02),
        "dn_conv": nrm((nl, DN_CONV, 3 * BRANCH_W), DN_CONV ** -0.5),
        "dn_a_log": jnp.log(uni((nl, 2, DN_HEADS), 1.0, 16.0)),
        "dn_dt_bias": dt + jnp.log(-jnp.expm1(-dt)),
        "dn_norm_g": 1.0 + nrm((nl, DN_HD), 0.02),
        "s5_lam_re": -0.5 + nrm((nl, 2, S5_GROUPS, S5_STATE), 0.01),
        "s5_lam_im": math.pi * n_idx + nrm((nl, 2, S5_GROUPS, S5_STATE), 0.01),
        "s5_log_step": uni((nl, 2, S5_GROUPS), math.log(1e-3), math.log(1e-1)),
        "s5_b_re": nrm((nl, 2, S5_GROUPS, S5_STATE, S5_GW), (2 * S5_GW) ** -0.5),
        "s5_b_im": nrm((nl, 2, S5_GROUPS, S5_STATE, S5_GW), (2 * S5_GW) ** -0.5),
        "s5_c_re": nrm((nl, 2, S5_GROUPS, S5_GW, S5_STATE), (2 * S5_STATE) ** -0.5),
        "s5_c_im": nrm((nl, 2, S5_GROUPS, S5_GW, S5_STATE), (2 * S5_STATE) ** -0.5),
        "s5_d": nrm((nl, BRANCH_W), 1.0),
        "s5_w_glu": nrm((nl, BRANCH_W, BRANCH_W), BRANCH_W ** -0.5),
        "s5_b_glu": nrm((nl, BRANCH_W), 0.02),
        "w_branch": nrm((nl, N_BRANCH, BRANCH_W, d), BRANCH_W ** -0.5),
        "w_out": nrm((nl, d, d), d ** -0.5),
        "router_w": nrm((d, N_EXPERTS), d ** -0.5),
        "router_b": nrm((N_EXPERTS,), 0.01),
        "moe_w1": nrm((nl, N_EXPERTS, d, D_EXPERT), d ** -0.5),
        "moe_w3": nrm((nl, N_EXPERTS, d, D_EXPERT), d ** -0.5),
        "moe_w2": nrm((nl, N_EXPERTS, D_EXPERT, d), D_EXPERT ** -0.5),
        "final_g": 1.0 + nrm((d,), 0.02),
    }


def reference(x, c, ctx, c_ctx, w_mod, b_mod, norm1_g, norm2_g, w_in, rw_mu, rw_w0, rw_w_up, rw_a0, rw_a_up,
              rw_k_k, rw_k_a, rw_r_k, rw_g_up, rw_ln_g, rw_ln_b, dn_conv, dn_a_log, dn_dt_bias, dn_norm_g,
              s5_lam_re, s5_lam_im, s5_log_step, s5_b_re, s5_b_im, s5_c_re, s5_c_im, s5_d, s5_w_glu, s5_b_glu,
              w_branch, w_out, router_w, router_b, moe_w1, moe_w3, moe_w2, final_g):
    rows = x.shape[1] // GRID_W
    xc = ctx
    for i in range(DEPTH):
        lp = {
            "w_mod": w_mod[i], "b_mod": b_mod[i], "norm1_g": norm1_g[i], "norm2_g": norm2_g[i],
            "w_in": w_in[i], "rw_mu": rw_mu[i], "rw_w0": rw_w0[i], "rw_w_up": rw_w_up[i],
            "rw_a0": rw_a0[i], "rw_a_up": rw_a_up[i], "rw_k_k": rw_k_k[i], "rw_k_a": rw_k_a[i],
            "rw_r_k": rw_r_k[i], "rw_g_up": rw_g_up[i], "rw_ln_g": rw_ln_g[i], "rw_ln_b": rw_ln_b[i],
            "dn_conv": dn_conv[i], "dn_a_log": dn_a_log[i], "dn_dt_bias": dn_dt_bias[i],
            "dn_norm_g": dn_norm_g[i], "s5_lam_re": s5_lam_re[i], "s5_lam_im": s5_lam_im[i],
            "s5_log_step": s5_log_step[i], "s5_b_re": s5_b_re[i], "s5_b_im": s5_b_im[i],
            "s5_c_re": s5_c_re[i], "s5_c_im": s5_c_im[i], "s5_d": s5_d[i], "s5_w_glu": s5_w_glu[i],
            "s5_b_glu": s5_b_glu[i], "w_branch": w_branch[i], "w_out": w_out[i],
            "moe_w1": moe_w1[i], "moe_w3": moe_w3[i], "moe_w2": moe_w2[i],
        }
        x, xc = trunk_layer(x, xc, c, c_ctx, rows, i < DEPTH - 1, router_w, router_b, lp)
    return rms_norm(x, final_g)
```

```python
import functools
import math

import jax
import jax.numpy as jnp
from jax import lax
from jax.experimental import pallas as pl
from jax.experimental.pallas import tpu as pltpu

F32 = jnp.float32
BF16 = jnp.bfloat16
HIGHEST = lax.Precision.HIGHEST

D_MODEL = 1024
N_MOD = 6
EPS = 1e-6
N_BRANCH = 4
BRANCH_W = 256
HEAD_DIM = 64
N_HEADS = 4
FN_GW = 64
RW_COLS = 896
RW_DECAY_SCALE = math.exp(-0.5)
RW_GN_EPS = 64e-5
DN_CONV = 5
S5_GW = 16
S5_GROUPS = 16
S5_STATE = 64
N_EXPERTS = 16
EXPERTS_PER_GROUP = 4
D_EXPERT = 256
W_IN_SPLITS = (256, 896, 768, 256, 8, 8, 256, 4096)

TOKEN_BLOCK = 256
SCAN_CHUNK = 64
S5_CHUNK = 16
HALO = 8
VMEM_LIMIT = 56 * 1024 * 1024


def _cparams(sem, vmem=None, **kw):
    return pltpu.CompilerParams(dimension_semantics=sem, vmem_limit_bytes=vmem, **kw)


def _dot(a, b):
    return jnp.dot(a.astype(BF16), b.astype(BF16), preferred_element_type=F32)


def _dot_nt(a, b):
    return lax.dot_general(a.astype(BF16), b.astype(BF16), (((1,), (1,)), ((), ())),
                           preferred_element_type=F32)


def _dot_tn(a, b):
    return lax.dot_general(a.astype(BF16), b.astype(BF16), (((0,), (0,)), ((), ())),
                           preferred_element_type=F32)


def _dot_hi(a, b):
    return jnp.dot(a, b, preferred_element_type=F32, precision=HIGHEST)


def _dot_nt_hi(a, b):
    return lax.dot_general(a, b, (((1,), (1,)), ((), ())), preferred_element_type=F32, precision=HIGHEST)


def _dot_tn_hi(a, b):
    return lax.dot_general(a, b, (((0,), (0,)), ((), ())), preferred_element_type=F32, precision=HIGHEST)


def _sigmoid(x):
    return jax.nn.sigmoid(x)


def _silu(x):
    return x * jax.nn.sigmoid(x)


def _norm_mod(x, g, scale, shift):
    y = x * lax.rsqrt(jnp.mean(x * x, axis=-1, keepdims=True) + EPS) * g
    return y * (1.0 + scale) + shift


def _mod_index(nblk_ctx, n_batch):
    return lambda b, j: (jnp.where(j < nblk_ctx, n_batch, b), 0, 0)


def _mod_kernel(c_ref, w_ref, b_ref, o_ref):
    o_ref[...] = _dot(_silu(c_ref[...]), w_ref[...]) + b_ref[...]


def _modulation(cond, w_mod, b_mod):
    rows = cond.shape[0]
    n = w_mod.shape[1]
    tn = 512
    out = pl.pallas_call(
        _mod_kernel,
        out_shape=jax.ShapeDtypeStruct((rows, n), F32),
        grid=(n // tn,),
        in_specs=[pl.BlockSpec((rows, D_MODEL), lambda i: (0, 0)),
                  pl.BlockSpec((D_MODEL, tn), lambda i: (0, i)),
                  pl.BlockSpec((1, tn), lambda i: (0, i))],
        out_specs=pl.BlockSpec((rows, tn), lambda i: (0, i)),
        compiler_params=_cparams(("arbitrary",)),
        name="adaln_mod",
    )(cond, w_mod, b_mod.reshape(1, n))
    return out.reshape(rows, N_MOD, D_MODEL)


def _inproj_kernel(x_ref, mod_ref, g_ref, w_ref, cs_ref,
                   fcs_ref, rw_ref, dqkv_ref, dg_ref, s5_ref, dab_ref):
    h = _norm_mod(x_ref[0], g_ref[...], mod_ref[0, 1:2, :], mod_ref[0, 0:1, :]).astype(BF16)
    fn = jnp.dot(h, w_ref[:, 0:256], preferred_element_type=F32)
    fcs_ref[0] = _dot(fn, cs_ref[...]).astype(BF16)
    rw_ref[0] = jnp.dot(h, w_ref[:, 256:1152], preferred_element_type=F32)
    dqkv_ref[0] = jnp.dot(h, w_ref[:, 1152:1920], preferred_element_type=F32)
    dg_ref[0] = jnp.dot(h, w_ref[:, 1920:2176], preferred_element_type=F32)
    s5_ref[0] = jnp.dot(h, w_ref[:, 2176:2432], preferred_element_type=F32)
    dab_ref[0] = jnp.dot(h, w_ref[:, 2432:2560], preferred_element_type=F32)


def _input_projection(xcat, mod, g1, w_mix, cs, nblk_ctx):
    nb, t, _ = xcat.shape
    nblk = t // TOKEN_BLOCK
    widths = (512, 896, 768, 256, 256, 128)
    dtypes = (BF16, F32, F32, F32, F32, F32)
    tok = lambda w: pl.BlockSpec((1, TOKEN_BLOCK, w), lambda b, j: (b, j, 0))
    return pl.pallas_call(
        _inproj_kernel,
        out_shape=[jax.ShapeDtypeStruct((nb, t, w), dt) for w, dt in zip(widths, dtypes)],
        grid=(nb, nblk),
        in_specs=[tok(D_MODEL),
                  pl.BlockSpec((1, N_MOD, D_MODEL), _mod_index(nblk_ctx, nb)),
                  pl.BlockSpec((1, D_MODEL), lambda b, j: (0, 0)),
                  pl.BlockSpec(w_mix.shape, lambda b, j: (0, 0)),
                  pl.BlockSpec(cs.shape, lambda b, j: (0, 0))],
        out_specs=[tok(w) for w in widths],
        compiler_params=_cparams(("parallel", "arbitrary"), VMEM_LIMIT),
        name="norm1_inproj",
    )(xcat, mod, g1, w_mix, cs)


def _dft_kernel(n_batch, wc_ref, ws_ref, u_ref, o_ref):
    @pl.when(pl.program_id(1) == 0)
    def _():
        o_ref[...] = jnp.zeros_like(o_ref)

    wc = wc_ref[...]
    ws = ws_ref[...]
    for b in range(n_batch):
        u = u_ref[b]
        o_ref[b] += (jnp.dot(wc, u[:, 0:256], preferred_element_type=F32)
                     + jnp.dot(ws, u[:, 256:512], preferred_element_type=F32))


def _dft_tables(n):
    idx = jnp.arange(n, dtype=jnp.int32)
    ang = ((idx[:, None] * idx[None, :]) % n).astype(F32) * (2.0 * math.pi / n)
    scale = 1.0 / math.sqrt(n * FN_GW)
    return (jnp.cos(ang) * scale).astype(BF16), (-jnp.sin(ang) * scale).astype(BF16)


def _sequence_dft(fcs, row0, n):
    nb = fcs.shape[0]
    wc, ws = _dft_tables(n)
    tm = min(n, 1024)
    tk = min(n, 256)
    assert n % tm == 0 and n % tk == 0 and row0 % tk == 0
    kb0 = row0 // tk
    return pl.pallas_call(
        functools.partial(_dft_kernel, nb),
        out_shape=jax.ShapeDtypeStruct((nb, n, BRANCH_W), F32),
        grid=(n // tm, n // tk),
        in_specs=[pl.BlockSpec((tm, tk), lambda m, k: (m, k)),
                  pl.BlockSpec((tm, tk), lambda m, k: (m, k)),
                  pl.BlockSpec((nb, tk, 512), lambda m, k: (0, k + kb0, 0))],
        out_specs=pl.BlockSpec((nb, tm, BRANCH_W), lambda m, k: (0, m, 0)),
        compiler_params=_cparams(("parallel", "arbitrary"), VMEM_LIMIT),
        name="fourier_seq_dft",
    )(wc, ws, fcs)


def _shifted(x, s, halo_prev, halo_next):
    n = x.shape[0]
    if s == 0:
        return x
    rows = lax.broadcasted_iota(jnp.int32, x.shape, 0)
    if s < 0:
        y = pltpu.roll(x, -s, 0)
        for t in range(-s):
            y = jnp.where(rows == t, halo_prev[HALO + s + t:HALO + s + t + 1, :], y)
    else:
        y = pltpu.roll(x, n - s, 0)
        for t in range(s):
            y = jnp.where(rows == n - s + t, halo_next[t:t + 1, :], y)
    return y


def _prep_kernel(nblk_ctx, nblk, rw_ref, rwp_ref, rwn_ref, dq_ref, dqp_ref, dqn_ref,
                 mu_ref, cw_ref, ps_ref, u_ref):
    j = pl.program_id(1)
    first = jnp.logical_or(j == 0, j == nblk_ctx)
    last = jnp.logical_or(j == nblk_ctx - 1, j == nblk - 1)
    pv = jnp.where(first, 0.0, 1.0).astype(F32)
    nv = jnp.where(last, 0.0, 1.0).astype(F32)

    x = rw_ref[0]
    hp = rwp_ref[0] * pv
    hn = rwn_ref[0] * nv
    prev = _shifted(x, -1, hp, hn)
    nxt = _shifted(x, 1, hp, hn)
    ps_ref[0] = x + mu_ref[0:1, :] * (prev - x) + mu_ref[1:2, :] * (nxt - x)

    q = dq_ref[0]
    hp = dqp_ref[0] * pv
    hn = dqn_ref[0] * nv
    pad = DN_CONV // 2
    acc = cw_ref[pad:pad + 1, :] * q
    for t in range(DN_CONV):
        if t != pad:
            acc = acc + cw_ref[t:t + 1, :] * _shifted(q, t - pad, hp, hn)
    u_ref[0] = _silu(acc)


def _shift_and_conv(rw, dqkv, mu, conv_w, nblk_ctx):
    nb, t, _ = rw.shape
    nblk = t // TOKEN_BLOCK
    per = TOKEN_BLOCK // HALO
    last_halo = t // HALO - 1

    def specs(w):
        return [pl.BlockSpec((1, TOKEN_BLOCK, w), lambda b, j: (b, j, 0)),
                pl.BlockSpec((1, HALO, w), lambda b, j: (b, jnp.maximum(j * per - 1, 0), 0)),
                pl.BlockSpec((1, HALO, w), lambda b, j: (b, jnp.minimum((j + 1) * per, last_halo), 0))]

    return pl.pallas_call(
        functools.partial(_prep_kernel, nblk_ctx, nblk),
        out_shape=[jax.ShapeDtypeStruct(rw.shape, F32), jax.ShapeDtypeStruct(dqkv.shape, F32)],
        grid=(nb, nblk),
        in_specs=specs(RW_COLS) + specs(3 * BRANCH_W)
        + [pl.BlockSpec(mu.shape, lambda b, j: (0, 0)), pl.BlockSpec(conv_w.shape, lambda b, j: (0, 0))],
        out_specs=[pl.BlockSpec((1, TOKEN_BLOCK, RW_COLS), lambda b, j: (b, j, 0)),
                   pl.BlockSpec((1, TOKEN_BLOCK, 3 * BRANCH_W), lambda b, j: (b, j, 0))],
        compiler_params=_cparams(("parallel", "arbitrary")),
        name="shift_conv",
    )(rw, rw, rw, dqkv, dqkv, dqkv, mu, conv_w)


def _chunk_order(nc_ctx, nc):
    def order(d, c):
        back = jnp.where(c < nc_ctx, nc_ctx - 1 - c, nc - 1 + nc_ctx - c)
        return jnp.where(d == 0, c, back)
    return order


def _time_masks(d, n):
    row = lax.broadcasted_iota(jnp.int32, (n, n), 0)
    col = lax.broadcasted_iota(jnp.int32, (n, n), 1)
    delta = (row - col) * (1 - 2 * d)
    return delta >= 0, delta > 0


def _unit_tri_solve(n_mat, rhs, mm):
    size = n_mat.shape[0]
    x = rhs
    p = n_mat
    steps = int(math.log2(size))
    for i in range(steps):
        x = x + mm(p, x)
        if i + 1 < steps:
            p = mm(p, p)
    return x


def _rwkv_scan_kernel(p_ref, w0_ref, wup_ref, a0_ref, aup_ref, kk_ref, ka_ref, ones_ref,
                      y_ref, s_ref):
    d = pl.program_id(0)

    @pl.when(pl.program_id(2) == 0)
    def _():
        s_ref[...] = jnp.zeros_like(s_ref)

    p = p_ref[0]
    r = p[:, 0:256]
    k = p[:, 256:512]
    v = p[:, 512:768]
    wl = p[:, 768:800]
    al = p[:, 800:832]
    lw = -RW_DECAY_SCALE * _sigmoid(w0_ref[0] + _dot(jnp.tanh(wl), wup_ref[0]))
    a = _sigmoid(a0_ref[0] + _dot(al, aup_ref[0]))
    kkp = k * kk_ref[...]
    kk = kkp * lax.rsqrt(_dot(kkp * kkp, ones_ref[...]) + EPS)
    kmod = k * (1.0 + (a - 1.0) * ka_ref[...])
    alpha = -(a * kk)

    incl, strict = _time_masks(d, SCAN_CHUNK)
    g = _dot_hi(jnp.where(incl, 1.0, 0.0).astype(F32), lw)
    g_tot = jnp.sum(lw, axis=0, keepdims=True)

    outs = []
    for h in range(N_HEADS):
        sl = slice(HEAD_DIM * h, HEAD_DIM * (h + 1))
        gh = g[:, sl]
        e_g = jnp.exp(gh)
        e_prev = jnp.exp(gh - lw[:, sl])
        e_neg = jnp.exp(-gh)
        e_tail = jnp.exp(g_tot[:, sl] - gh)
        b_t = kk[:, sl] * e_prev
        k_h = kmod[:, sl] * e_neg
        a_h = alpha[:, sl] * e_neg
        r_t = r[:, sl] * e_g
        v_h = v[:, sl]
        state = s_ref[h]
        a_bk = jnp.where(strict, _dot_nt(b_t, k_h), 0.0)
        a_ba = jnp.where(strict, _dot_nt(b_t, a_h), 0.0)
        a_rk = jnp.where(incl, _dot_nt(r_t, k_h), 0.0)
        a_ra = jnp.where(incl, _dot_nt(r_t, a_h), 0.0)
        rhs = _dot_nt(b_t, state) + _dot(a_bk, v_h)
        u = _unit_tri_solve(a_ba, rhs, _dot)
        outs.append(_dot_nt(r_t, state) + _dot(a_rk, v_h) + _dot(a_ra, u))
        s_ref[h] = (state * jnp.exp(g_tot[:, sl])
                    + _dot_tn(v_h, kmod[:, sl] * e_tail) + _dot_tn(u, alpha[:, sl] * e_tail))
    y_ref[0, 0] = jnp.concatenate(outs, axis=1)


def _rwkv_scan(ps, w0, w_up, a0, a_up, k_k, k_a, ones_bd, nc_ctx):
    nb, t, _ = ps.shape
    nc = t // SCAN_CHUNK
    order = _chunk_order(nc_ctx, nc)
    per_dir = lambda shape: pl.BlockSpec((1,) + shape, lambda d, b, c: (d, 0, 0))
    const = lambda shape: pl.BlockSpec(shape, lambda d, b, c: (0, 0))
    return pl.pallas_call(
        _rwkv_scan_kernel,
        out_shape=jax.ShapeDtypeStruct((2, nb, t, BRANCH_W), F32),
        grid=(2, nb, nc),
        in_specs=[pl.BlockSpec((1, SCAN_CHUNK, RW_COLS), lambda d, b, c: (b, order(d, c), 0)),
                  per_dir((1, BRANCH_W)), per_dir((32, BRANCH_W)),
                  per_dir((1, BRANCH_W)), per_dir((32, BRANCH_W)),
                  const((1, BRANCH_W)), const((1, BRANCH_W)), const((BRANCH_W, BRANCH_W))],
        out_specs=pl.BlockSpec((1, 1, SCAN_CHUNK, BRANCH_W), lambda d, b, c: (d, b, order(d, c), 0)),
        scratch_shapes=[pltpu.VMEM((N_HEADS, HEAD_DIM, HEAD_DIM), F32)],
        compiler_params=_cparams(("parallel", "parallel", "arbitrary")),
        name="rwkv7_scan",
    )(ps, w0.reshape(2, 1, BRANCH_W), w_up, a0.reshape(2, 1, BRANCH_W), a_up,
      k_k.reshape(1, BRANCH_W), k_a.reshape(1, BRANCH_W), ones_bd)


def _deltanet_scan_kernel(u_ref, dab_ref, nea_ref, dtb_ref, o_ref, s_ref):
    d = pl.program_id(0)

    @pl.when(pl.program_id(2) == 0)
    def _():
        s_ref[...] = jnp.zeros_like(s_ref)

    u = u_ref[0]
    dab = dab_ref[0]
    log_a = nea_ref[...] * jax.nn.softplus(dab + dtb_ref[...])
    beta = _sigmoid(dab)

    incl, strict = _time_masks(d, SCAN_CHUNK)
    tri = jnp.where(incl, 1.0, 0.0).astype(F32)
    g_cols = _dot_hi(tri, log_a)
    g_rows = lax.dot_general(log_a, tri, (((0,), (1,)), ((), ())),
                             preferred_element_type=F32, precision=HIGHEST)
    g_last = jnp.sum(log_a, axis=0, keepdims=True)

    def pick(x, h):
        return jnp.where(d == 0, x[:, h:h + 1], x[:, N_HEADS + h:N_HEADS + h + 1])

    outs = []
    for h in range(N_HEADS):
        sl = slice(HEAD_DIM * h, HEAD_DIM * (h + 1))
        q = u[:, sl]
        k = u[:, BRANCH_W + HEAD_DIM * h:BRANCH_W + HEAD_DIM * (h + 1)]
        v = u[:, 2 * BRANCH_W + HEAD_DIM * h:2 * BRANCH_W + HEAD_DIM * (h + 1)]
        q = q * lax.rsqrt(jnp.sum(q * q, axis=-1, keepdims=True) + EPS) * (HEAD_DIM ** -0.5)
        k = k * lax.rsqrt(jnp.sum(k * k, axis=-1, keepdims=True) + EPS)
        g_col = pick(g_cols, h)
        g_row = jnp.where(d == 0, g_rows[h:h + 1, :], g_rows[N_HEADS + h:N_HEADS + h + 1, :])
        g_end = pick(g_last, h)
        bt = jnp.where(d == 0, beta[:, 2 * N_HEADS + h:2 * N_HEADS + h + 1],
                       beta[:, 3 * N_HEADS + h:3 * N_HEADS + h + 1])
        decay = jnp.exp(jnp.where(incl, g_col - g_row, -jnp.inf))
        kb = k * bt
        a_low = jnp.where(strict, _dot_nt_hi(kb, k) * decay, 0.0)
        e_g = jnp.exp(g_col)
        rhs = jnp.concatenate([v * bt, kb * e_g], axis=1)
        sol = _unit_tri_solve(-a_low, rhs, _dot_hi)
        u_val = sol[:, 0:HEAD_DIM]
        w_dec = sol[:, HEAD_DIM:2 * HEAD_DIM]
        attn = _dot_nt_hi(q, k) * decay
        state = s_ref[h]
        v_new = u_val - _dot_hi(w_dec, state)
        outs.append(_dot_hi(q * e_g, state) + _dot_hi(attn, v_new))
        s_ref[h] = state * jnp.exp(g_end) + _dot_tn_hi(k * jnp.exp(g_end - g_col), v_new)
    o_ref[0, 0] = jnp.concatenate(outs, axis=1)


def _deltanet_scan(u, dab, a_log, dt_bias, nc_ctx):
    nb, t, _ = u.shape
    nc = t // SCAN_CHUNK
    order = _chunk_order(nc_ctx, nc)
    pad = 128 - 2 * N_HEADS
    neg_exp_a = jnp.pad(-jnp.exp(a_log.reshape(1, -1)), ((0, 0), (0, pad)))
    dtb = jnp.pad(dt_bias.reshape(1, -1), ((0, 0), (0, pad)))
    const = lambda shape: pl.BlockSpec(shape, lambda d, b, c: (0, 0))
    return pl.pallas_call(
        _deltanet_scan_kernel,
        out_shape=jax.ShapeDtypeStruct((2, nb, t, BRANCH_W), F32),
        grid=(2, nb, nc),
        in_specs=[pl.BlockSpec((1, SCAN_CHUNK, 3 * BRANCH_W), lambda d, b, c: (b, order(d, c), 0)),
                  pl.BlockSpec((1, SCAN_CHUNK, 128), lambda d, b, c: (b, order(d, c), 0)),
                  const((1, 128)), const((1, 128))],
        out_specs=pl.BlockSpec((1, 1, SCAN_CHUNK, BRANCH_W), lambda d, b, c: (d, b, order(d, c), 0)),
        scratch_shapes=[pltpu.VMEM((N_HEADS, HEAD_DIM, HEAD_DIM), F32)],
        compiler_params=_cparams(("parallel", "parallel", "arbitrary")),
        name="deltanet_scan",
    )(u, dab, neg_exp_a, dtb)


def _s5_kernel(n_batch, nc_ctx, nc, u_ref, toep_ref, winr_ref, wini_ref, wsor_ref, wsoi_ref,
               lr_ref, li_ref, dt_ref, y_ref, injr, inji, xsr, xsi):
    d = pl.program_id(1)
    u = u_ref[0]
    ub = u.astype(BF16)
    injr[...] = jnp.dot(ub, winr_ref[0, 0], preferred_element_type=F32)
    inji[...] = jnp.dot(ub, wini_ref[0, 0], preferred_element_type=F32)
    lam_r = lr_ref[0, 0]
    lam_i = li_ref[0, 0]

    def body(s, carry):
        xr, xi = carry
        back = jnp.where(s < nc_ctx, nc_ctx - 1 - s, nc - 1 + nc_ctx - s)
        row0 = jnp.where(d == 0, s, back) * n_batch
        if n_batch % 8 == 0:
            row0 = pl.multiple_of(row0, 8)
        rows = pl.ds(row0, n_batch)
        xsr[rows, :] = xr
        xsi[rows, :] = xi
        return (lam_r * xr - lam_i * xi + injr[rows, :], lam_r * xi + lam_i * xr + inji[rows, :])

    zero = jnp.zeros((n_batch, S5_STATE), F32)
    lax.fori_loop(0, nc, body, (zero, zero))

    y = (jnp.dot(ub, toep_ref[0, 0], preferred_element_type=F32)
         + _dot(xsr[...], wsor_ref[0, 0]) + _dot(xsi[...], wsoi_ref[0, 0]))

    @pl.when(d == 0)
    def _():
        y_ref[0] = y + u * dt_ref[0]

    @pl.when(d == 1)
    def _():
        y_ref[0] += y


def _s5_tables(lam_re, lam_im, log_step, b_re, b_im, c_re, c_im):
    cs = S5_CHUNK
    lam = lax.complex(lam_re.astype(F32), lam_im.astype(F32))
    step = jnp.exp(log_step.astype(F32))[..., None]
    tau = jnp.arange(cs + 1, dtype=F32)[:, None, None, None]
    lam_pow = jnp.exp(lam[None] * step[None] * tau)
    lam_bar = lam_pow[1]
    b_bar = ((lam_bar - 1.0) / lam)[..., None] * lax.complex(b_re.astype(F32), b_im.astype(F32))
    c_mat = lax.complex(c_re.astype(F32), c_im.astype(F32))
    kern = jnp.real(jnp.einsum("dghp,tdgp,dgpk->tdghk", c_mat, lam_pow[:cs], b_bar))
    i = jnp.arange(cs)
    lag_f = i[None, :] - i[:, None]
    toeps, winr, wini, wsor, wsoi = [], [], [], [], []
    for d in range(2):
        lag = lag_f if d == 0 else -lag_f
        kd = jnp.where((lag >= 0)[:, :, None, None, None], kern[:, d][jnp.clip(lag, 0, cs - 1)], 0.0)
        toeps.append(kd.transpose(2, 0, 4, 1, 3).reshape(S5_GROUPS, cs * S5_GW, cs * S5_GW))
        pw_in = (cs - 1 - i) if d == 0 else i
        e = lam_pow[pw_in, d][..., None] * b_bar[d][None]
        e = e.transpose(1, 0, 3, 2).reshape(S5_GROUPS, cs * S5_GW, S5_STATE)
        winr.append(jnp.real(e))
        wini.append(jnp.imag(e))
        pw_out = (i + 1) if d == 0 else (cs - i)
        m = c_mat[d][None] * lam_pow[pw_out, d][:, :, None, :]
        m = m.transpose(1, 3, 0, 2).reshape(S5_GROUPS, S5_STATE, cs * S5_GW)
        wsor.append(jnp.real(m))
        wsoi.append(-jnp.imag(m))
    stack = lambda xs, dt: jnp.stack(xs).astype(dt)
    lam_c = lam_pow[cs]
    return (stack(toeps, BF16), stack(winr, BF16), stack(wini, BF16), stack(wsor, BF16), stack(wsoi, BF16),
            jnp.real(lam_c)[:, :, None, :], jnp.imag(lam_c)[:, :, None, :])


def _s5_to_chunks(s5, n_ctx):
    nb, t, _ = s5.shape
    cs, g, hw = S5_CHUNK, S5_GROUPS, S5_GW
    n_lat = t - n_ctx
    rows = n_lat // 64
    c = s5[:, :n_ctx].reshape(nb, n_ctx // cs, cs, g, hw)
    c = c.transpose(3, 1, 0, 2, 4).reshape(g, (n_ctx // cs) * nb, cs * hw)
    l = s5[:, n_ctx:].reshape(nb, rows // cs, cs, 64, g, hw)
    l = l.transpose(4, 3, 1, 0, 2, 5).reshape(g, 64 * (rows // cs) * nb, cs * hw)
    return jnp.concatenate([c, l], axis=1)


def _s5_from_chunks(y, nb, n_ctx, n_lat):
    cs, g, hw = S5_CHUNK, S5_GROUPS, S5_GW
    rows = n_lat // 64
    r_ctx = (n_ctx // cs) * nb
    c = y[:, :r_ctx].reshape(g, n_ctx // cs, nb, cs, hw).transpose(2, 1, 3, 0, 4).reshape(nb, n_ctx, g * hw)
    l = y[:, r_ctx:].reshape(g, 64, rows // cs, nb, cs, hw).transpose(3, 2, 4, 1, 0, 5).reshape(nb, n_lat, g * hw)
    return jnp.concatenate([c, l], axis=1)


def _s5_scan(s5, tables, d_skip, n_ctx):
    nb, t, _ = s5.shape
    u = _s5_to_chunks(s5, n_ctx)
    g, r, w = u.shape
    nc = t // S5_CHUNK
    toep, winr, wini, wsor, wsoi, lr, li = tables
    d_tile = jnp.tile(d_skip.astype(F32).reshape(S5_GROUPS, 1, S5_GW), (1, S5_CHUNK, 1)).reshape(g, 1, w)
    per = lambda a, b: pl.BlockSpec((1, 1, a, b), lambda gi, d: (d, gi, 0, 0))
    y = pl.pallas_call(
        functools.partial(_s5_kernel, nb, n_ctx // S5_CHUNK, nc),
        out_shape=jax.ShapeDtypeStruct((g, r, w), F32),
        grid=(g, 2),
        in_specs=[pl.BlockSpec((1, r, w), lambda gi, d: (gi, 0, 0)),
                  per(w, w), per(w, S5_STATE), per(w, S5_STATE), per(S5_STATE, w), per(S5_STATE, w),
                  per(1, S5_STATE), per(1, S5_STATE),
                  pl.BlockSpec((1, 1, w), lambda gi, d: (gi, 0, 0))],
        out_specs=pl.BlockSpec((1, r, w), lambda gi, d: (gi, 0, 0)),
        scratch_shapes=[pltpu.VMEM((r, S5_STATE), F32) for _ in range(4)],
        compiler_params=_cparams(("parallel", "arbitrary"), VMEM_LIMIT),
        name="s5_scan",
    )(u, toep, winr, wini, wsor, wsoi, lr, li, d_tile)
    return _s5_from_chunks(y, nb, n_ctx, t - n_ctx)


def _merge_kernel(x_ref, mod_ref, g1_ref, ya_ref, yb_ref, ps_ref, oc_ref, dg_ref, y5_ref,
                  wg_ref, wb_ref, wo_ref, avg_ref, ones_ref, rk_ref, gup_ref, lng_ref, lnb_ref,
                  dng_ref, wglu_ref, bglu_ref, o_ref):
    x = x_ref[0]
    h = _norm_mod(x, g1_ref[...], mod_ref[0, 1:2, :], mod_ref[0, 0:1, :]).astype(BF16)
    avg = avg_ref[...]

    ps = ps_ref[0]
    r = ps[:, 0:256]
    k = ps[:, 256:512]
    v = ps[:, 512:768]
    gl = ps[:, 832:896]
    y = yb_ref[0, 0] + yb_ref[1, 0]
    dev = y - _dot(y, avg)
    yn = dev * lax.rsqrt(_dot(dev * dev, avg) + RW_GN_EPS) * lng_ref[...] + lnb_ref[...]
    bonus = _dot(r * k * rk_ref[...], ones_ref[...]) * v
    yb = (yn + bonus) * _dot(_sigmoid(gl), gup_ref[...])

    o = oc_ref[0, 0] + oc_ref[1, 0]
    yc = o * lax.rsqrt(_dot(o * o, avg) + EPS) * dng_ref[...] * _silu(dg_ref[0])

    z = jax.nn.gelu(y5_ref[0])
    yd = z * _sigmoid(_dot(z, wglu_ref[...]) + bglu_ref[...])

    m = jnp.zeros((TOKEN_BLOCK, D_MODEL), F32)
    for i, yi in enumerate((ya_ref[0], yb, yc, yd)):
        gate = _sigmoid(jnp.dot(h, wg_ref[:, D_MODEL * i:D_MODEL * (i + 1)], preferred_element_type=F32))
        m = m + gate * _dot(yi, wb_ref[i])
    o_ref[0] = x + mod_ref[0, 2:3, :] * _dot(m, wo_ref[...])


def _merge(xcat, mod, g1, ya, yb, ps, oc, dg, y5, w_gate, w_branch, w_out, avg_bd, ones_bd,
           r_k, g_up, ln_g, ln_b, dn_g, w_glu, b_glu, nblk_ctx):
    nb, t, _ = xcat.shape
    nblk = t // TOKEN_BLOCK
    tok = lambda w: pl.BlockSpec((1, TOKEN_BLOCK, w), lambda b, j: (b, j, 0))
    tok2 = lambda w: pl.BlockSpec((2, 1, TOKEN_BLOCK, w), lambda b, j: (0, b, j, 0))
    full = lambda a: pl.BlockSpec(a.shape, lambda b, j: (0,) * a.ndim)
    consts = (w_gate, w_branch, w_out, avg_bd, ones_bd, r_k, g_up, ln_g, ln_b, dn_g, w_glu, b_glu)
    return pl.pallas_call(
        _merge_kernel,
        out_shape=jax.ShapeDtypeStruct(xcat.shape, F32),
        grid=(nb, nblk),
        in_specs=[tok(D_MODEL), pl.BlockSpec((1, N_MOD, D_MODEL), _mod_index(nblk_ctx, nb)),
                  pl.BlockSpec((1, D_MODEL), lambda b, j: (0, 0)),
                  tok(BRANCH_W), tok2(BRANCH_W), tok(RW_COLS), tok2(BRANCH_W), tok(BRANCH_W), tok(BRANCH_W)]
        + [full(a) for a in consts],
        out_specs=tok(D_MODEL),
        compiler_params=_cparams(("parallel", "arbitrary"), VMEM_LIMIT),
        name="merge_branches",
    )(xcat, mod, g1, ya, yb, ps, oc, dg, y5, *consts)


def _route(sel, score):
    s = [sel[e:e + 1, :] for e in range(N_EXPERTS)]
    sc = [score[e:e + 1, :] for e in range(N_EXPERTS)]
    n_groups = N_EXPERTS // EXPERTS_PER_GROUP
    group_score = []
    for g in range(n_groups):
        m = s[EXPERTS_PER_GROUP * g:EXPERTS_PER_GROUP * (g + 1)]
        best = None
        for i in range(EXPERTS_PER_GROUP):
            for j in range(i + 1, EXPERTS_PER_GROUP):
                pair = m[i] + m[j]
                best = pair if best is None else jnp.maximum(best, pair)
        group_score.append(best)
    best_g = jnp.zeros(group_score[0].shape, jnp.int32)
    best_v = group_score[0]
    for g in range(1, n_groups):
        upd = group_score[g] > best_v
        best_g = jnp.where(upd, g, best_g)
        best_v = jnp.where(upd, group_score[g], best_v)
    chosen = []
    den = jnp.zeros_like(best_v)
    for e in range(N_EXPERTS):
        g = e // EXPERTS_PER_GROUP
        rank = jnp.zeros(best_g.shape, jnp.int32)
        for j in range(EXPERTS_PER_GROUP * g, EXPERTS_PER_GROUP * (g + 1)):
            if j == e:
                continue
            ahead = (s[j] > s[e]) if j > e else (s[j] >= s[e])
            rank = rank + jnp.where(ahead, 1, 0)
        pick = jnp.logical_and(best_g == g, rank < 2)
        chosen.append(pick)
        den = den + jnp.where(pick, sc[e], 0.0)
    return jnp.concatenate([jnp.where(chosen[e], sc[e] / den, 0.0) for e in range(N_EXPERTS)], axis=0)


def _moe_kernel(x_ref, mod_ref, g2_ref, rwt_ref, rb_ref, w1_ref, w3_ref, w2_ref, exp_ref, o_ref):
    x = x_ref[0]
    h = _norm_mod(x, g2_ref[...], mod_ref[0, 4:5, :], mod_ref[0, 3:4, :])
    logits = lax.dot_general(rwt_ref[...], h, (((1,), (1,)), ((), ())),
                             preferred_element_type=F32, precision=HIGHEST)
    score = _sigmoid(logits)
    comb = _route(score + rb_ref[...], score)
    hb = h.astype(BF16)
    combb = comb.astype(BF16)
    acc = jnp.zeros((TOKEN_BLOCK, D_MODEL), F32)
    width = 4 * D_EXPERT
    for q in range(N_EXPERTS * D_EXPERT // width):
        cols = slice(width * q, width * (q + 1))
        cw = lax.dot_general(combb, exp_ref[:, cols], (((0,), (0,)), ((), ())),
                             preferred_element_type=F32)
        a1 = jnp.dot(hb, w1_ref[:, cols], preferred_element_type=F32)
        a3 = jnp.dot(hb, w3_ref[:, cols], preferred_element_type=F32)
        act = (_silu(a1) * a3 * cw).astype(BF16)
        acc = acc + jnp.dot(act, w2_ref[cols, :], preferred_element_type=F32)
    o_ref[0] = x + mod_ref[0, 5:6, :] * acc


def _moe(x1, mod, g2, router_wt, router_b, w1, w3, w2, expand, nblk_ctx):
    nb, t, _ = x1.shape
    nblk = t // TOKEN_BLOCK
    tok = pl.BlockSpec((1, TOKEN_BLOCK, D_MODEL), lambda b, j: (b, j, 0))
    full = lambda a: pl.BlockSpec(a.shape, lambda b, j: (0,) * a.ndim, pipeline_mode=pl.Buffered(1))
    consts = (router_wt, router_b, w1, w3, w2, expand)
    return pl.pallas_call(
        _moe_kernel,
        out_shape=jax.ShapeDtypeStruct(x1.shape, F32),
        grid=(nb, nblk),
        in_specs=[tok, pl.BlockSpec((1, N_MOD, D_MODEL), _mod_index(nblk_ctx, nb)),
                  pl.BlockSpec((1, D_MODEL), lambda b, j: (0, 0))] + [full(a) for a in consts],
        out_specs=tok,
        compiler_params=_cparams(("parallel", "arbitrary"), VMEM_LIMIT),
        name="moe_ffn",
    )(x1, mod, g2, *consts)


def _final_norm_kernel(x_ref, g_ref, o_ref):
    x = x_ref[0]
    o_ref[0] = x * lax.rsqrt(jnp.mean(x * x, axis=-1, keepdims=True) + EPS) * g_ref[...]


def _final_norm(xcat, g, n_ctx):
    nb, t, _ = xcat.shape
    n_lat = t - n_ctx
    off = n_ctx // TOKEN_BLOCK
    return pl.pallas_call(
        _final_norm_kernel,
        out_shape=jax.ShapeDtypeStruct((nb, n_lat, D_MODEL), F32),
        grid=(nb, n_lat // TOKEN_BLOCK),
        in_specs=[pl.BlockSpec((1, TOKEN_BLOCK, D_MODEL), lambda b, j: (b, j + off, 0)),
                  pl.BlockSpec((1, D_MODEL), lambda b, j: (0, 0))],
        out_specs=pl.BlockSpec((1, TOKEN_BLOCK, D_MODEL), lambda b, j: (b, j, 0)),
        compiler_params=_cparams(("parallel", "arbitrary")),
        name="final_norm",
    )(xcat, g.reshape(1, D_MODEL))


def _block_diag_ones(n_blocks, size):
    return jnp.kron(jnp.eye(n_blocks, dtype=F32), jnp.ones((size, size), F32))


def _layer(xcat, cond, n_ctx, lp, router_wt, router_b, consts):
    nb, t, _ = xcat.shape
    nblk_ctx = n_ctx // TOKEN_BLOCK
    nc_ctx = n_ctx // SCAN_CHUNK
    mod = _modulation(cond, lp["w_mod"], lp["b_mod"])
    g1 = lp["norm1_g"].reshape(1, D_MODEL)

    offs = [0]
    for s in W_IN_SPLITS:
        offs.append(offs[-1] + s)
    w_in = lp["w_in"]
    seg = lambda i: w_in[:, offs[i]:offs[i + 1]]
    w_mix = jnp.concatenate([seg(0), seg(1), seg(2), seg(3), seg(6), seg(4), seg(5),
                             jnp.zeros((D_MODEL, 128 - 16), F32)], axis=1).astype(BF16)
    w_gate = seg(7).astype(BF16)

    fcs, rw, dqkv, dg, s5, dab = _input_projection(xcat, mod, g1, w_mix, consts["cs"], nblk_ctx)

    ya = jnp.concatenate([_sequence_dft(fcs, 0, n_ctx), _sequence_dft(fcs, n_ctx, t - n_ctx)], axis=1)

    ps, u = _shift_and_conv(rw, dqkv, lp["rw_mu"], lp["dn_conv"], nblk_ctx)
    yb = _rwkv_scan(ps, lp["rw_w0"], lp["rw_w_up"], lp["rw_a0"], lp["rw_a_up"], lp["rw_k_k"], lp["rw_k_a"],
                    consts["ones_bd"], nc_ctx)
    oc = _deltanet_scan(u, dab, lp["dn_a_log"], lp["dn_dt_bias"], nc_ctx)
    tables = _s5_tables(lp["s5_lam_re"], lp["s5_lam_im"], lp["s5_log_step"], lp["s5_b_re"], lp["s5_b_im"],
                        lp["s5_c_re"], lp["s5_c_im"])
    y5 = _s5_scan(s5, tables, lp["s5_d"], n_ctx)

    row = lambda a: a.reshape(1, -1).astype(F32)
    x1 = _merge(xcat, mod, g1, ya, yb, ps, oc, dg, y5, w_gate, lp["w_branch"].astype(BF16),
                lp["w_out"].astype(BF16), consts["avg_bd"], consts["ones_bd"], row(lp["rw_r_k"]),
                lp["rw_g_up"].astype(BF16), row(lp["rw_ln_g"]), row(lp["rw_ln_b"]),
                row(jnp.tile(lp["dn_norm_g"], N_HEADS)), lp["s5_w_glu"].astype(BF16), row(lp["s5_b_glu"]),
                nblk_ctx)

    w1 = lp["moe_w1"].transpose(1, 0, 2).reshape(D_MODEL, N_EXPERTS * D_EXPERT).astype(BF16)
    w3 = lp["moe_w3"].transpose(1, 0, 2).reshape(D_MODEL, N_EXPERTS * D_EXPERT).astype(BF16)
    w2 = lp["moe_w2"].reshape(N_EXPERTS * D_EXPERT, D_MODEL).astype(BF16)
    return _moe(x1, mod, lp["norm2_g"].reshape(1, D_MODEL), router_wt, router_b, w1, w3, w2,
                consts["expand"], nblk_ctx)


def kernel(x, c, ctx, c_ctx, w_mod, b_mod, norm1_g, norm2_g, w_in, rw_mu, rw_w0, rw_w_up, rw_a0, rw_a_up, rw_k_k, rw_k_a, rw_r_k, rw_g_up, rw_ln_g, rw_ln_b, dn_conv, dn_a_log, dn_dt_bias, dn_norm_g, s5_lam_re, s5_lam_im, s5_log_step, s5_b_re, s5_b_im, s5_c_re, s5_c_im, s5_d, s5_w_glu, s5_b_glu, w_branch, w_out, router_w, router_b, moe_w1, moe_w3, moe_w2, final_g):
    nb, n_lat, _ = x.shape
    n_ctx = ctx.shape[1]
    depth = w_mod.shape[0]
    assert n_ctx % TOKEN_BLOCK == 0 and n_lat % TOKEN_BLOCK == 0 and (n_lat // 64) % S5_CHUNK == 0

    xcat = jnp.concatenate([ctx, x], axis=1).astype(F32)
    cond_rows = -(-(nb + 1) // 8) * 8
    cond = jnp.zeros((cond_rows, D_MODEL), F32).at[:nb].set(c).at[nb].set(c_ctx)

    j = jnp.arange(FN_GW, dtype=jnp.int32)
    ang = ((j[:, None] * j[None, :]) % FN_GW).astype(F32) * (2.0 * math.pi / FN_GW)
    eye = jnp.eye(BRANCH_W // FN_GW, dtype=F32)
    consts = {
        "cs": jnp.concatenate([jnp.kron(eye, jnp.cos(ang)), jnp.kron(eye, jnp.sin(ang))], axis=1).astype(BF16),
        "ones_bd": _block_diag_ones(N_HEADS, HEAD_DIM).astype(BF16),
        "avg_bd": (_block_diag_ones(N_HEADS, HEAD_DIM) / HEAD_DIM).astype(BF16),
        "expand": jnp.kron(jnp.eye(N_EXPERTS, dtype=F32), jnp.ones((1, D_EXPERT), F32)).astype(BF16),
    }
    router_wt = router_w.T.astype(F32)
    router_bc = router_b.reshape(N_EXPERTS, 1).astype(F32)

    names = ("w_mod", "b_mod", "norm1_g", "norm2_g", "w_in", "rw_mu", "rw_w0", "rw_w_up", "rw_a0", "rw_a_up",
             "rw_k_k", "rw_k_a", "rw_r_k", "rw_g_up", "rw_ln_g", "rw_ln_b", "dn_conv", "dn_a_log", "dn_dt_bias",
             "dn_norm_g", "s5_lam_re", "s5_lam_im", "s5_log_step", "s5_b_re", "s5_b_im", "s5_c_re", "s5_c_im",
             "s5_d", "s5_w_glu", "s5_b_glu", "w_branch", "w_out", "moe_w1", "moe_w3", "moe_w2")
    stacked = (w_mod, b_mod, norm1_g, norm2_g, w_in, rw_mu, rw_w0, rw_w_up, rw_a0, rw_a_up, rw_k_k, rw_k_a,
               rw_r_k, rw_g_up, rw_ln_g, rw_ln_b, dn_conv, dn_a_log, dn_dt_bias, dn_norm_g, s5_lam_re,
               s5_lam_im, s5_log_step, s5_b_re, s5_b_im, s5_c_re, s5_c_im, s5_d, s5_w_glu, s5_b_glu,
               w_branch, w_out, moe_w1, moe_w3, moe_w2)
    for i in range(depth):
        lp = {n: a[i] for n, a in zip(names, stacked)}
        xcat = _layer(xcat, cond, n_ctx, lp, router_wt, router_bc, consts)
    return _final_norm(xcat, final_g, n_ctx)
```

```python
import functools
import math

import jax
import jax.numpy as jnp
from jax import lax
from jax.experimental import pallas as pl
from jax.experimental.pallas import tpu as pltpu

F32 = jnp.float32
BF16 = jnp.bfloat16
HIGHEST = lax.Precision.HIGHEST

D_MODEL = 1024
N_MOD = 6
EPS = 1e-6
N_BRANCH = 4
BRANCH_W = 256
HEAD_DIM = 64
N_HEADS = 4
FN_GW = 64
RW_COLS = 896
RW_DECAY_SCALE = math.exp(-0.5)
RW_GN_EPS = 64e-5
DN_CONV = 5
S5_GW = 16
S5_GROUPS = 16
S5_STATE = 64
N_EXPERTS = 16
EXPERTS_PER_GROUP = 4
D_EXPERT = 256
W_IN_SPLITS = (256, 896, 768, 256, 8, 8, 256, 4096)

TOKEN_BLOCK = 256
SCAN_CHUNK = 64
S5_CHUNK = 16
HALO = 8
MXU_DEPTH = 256
VMEM_LIMIT = 56 * 1024 * 1024


def _cparams(sem, vmem=None, **kw):
    return pltpu.CompilerParams(dimension_semantics=sem, vmem_limit_bytes=vmem, **kw)


def _dot(a, b):
    return jnp.dot(a.astype(BF16), b.astype(BF16), preferred_element_type=F32)


def _dot_nt(a, b):
    return lax.dot_general(a.astype(BF16), b.astype(BF16), (((1,), (1,)), ((), ())),
                           preferred_element_type=F32)


def _dot_tn(a, b):
    return lax.dot_general(a.astype(BF16), b.astype(BF16), (((0,), (0,)), ((), ())),
                           preferred_element_type=F32)


def _dot_hi(a, b):
    return jnp.dot(a, b, preferred_element_type=F32, precision=HIGHEST)


def _dot_nt_hi(a, b):
    return lax.dot_general(a, b, (((1,), (1,)), ((), ())), preferred_element_type=F32, precision=HIGHEST)


def _dot_tn_hi(a, b):
    return lax.dot_general(a, b, (((0,), (0,)), ((), ())), preferred_element_type=F32, precision=HIGHEST)


def _sigmoid(x):
    return jax.nn.sigmoid(x)


def _silu(x):
    return x * jax.nn.sigmoid(x)


def _norm_mod(x, g, scale, shift):
    y = x * lax.rsqrt(jnp.mean(x * x, axis=-1, keepdims=True) + EPS) * g
    return y * (1.0 + scale) + shift


def _mod_index(nblk_ctx, n_batch):
    return lambda b, j: (jnp.where(j < nblk_ctx, n_batch, b), 0, 0)


def _mod_kernel(c_ref, w_ref, b_ref, o_ref):
    o_ref[...] = _dot(_silu(c_ref[...]), w_ref[...]) + b_ref[...]


def _modulation(cond, w_mod, b_mod):
    rows = cond.shape[0]
    n = w_mod.shape[1]
    tn = 512
    out = pl.pallas_call(
        _mod_kernel,
        out_shape=jax.ShapeDtypeStruct((rows, n), F32),
        grid=(n // tn,),
        in_specs=[pl.BlockSpec((rows, D_MODEL), lambda i: (0, 0)),
                  pl.BlockSpec((D_MODEL, tn), lambda i: (0, i)),
                  pl.BlockSpec((1, tn), lambda i: (0, i))],
        out_specs=pl.BlockSpec((rows, tn), lambda i: (0, i)),
        compiler_params=_cparams(("arbitrary",)),
        name="adaln_mod",
    )(cond, w_mod, b_mod.reshape(1, n))
    return out.reshape(rows, N_MOD, D_MODEL)


def _inproj_kernel(x_ref, mod_ref, g_ref, w_ref, cs_ref,
                   fcs_ref, rw_ref, dqkv_ref, dg_ref, s5_ref, dab_ref):
    h = _norm_mod(x_ref[0], g_ref[...], mod_ref[0, 1:2, :], mod_ref[0, 0:1, :]).astype(BF16)
    fn = jnp.dot(h, w_ref[:, 0:256], preferred_element_type=F32)
    fcs_ref[0] = _dot(fn, cs_ref[...]).astype(BF16)
    rw_ref[0] = jnp.dot(h, w_ref[:, 256:1152], preferred_element_type=F32)
    dqkv_ref[0] = jnp.dot(h, w_ref[:, 1152:1920], preferred_element_type=F32)
    dg_ref[0] = jnp.dot(h, w_ref[:, 1920:2176], preferred_element_type=F32)
    s5_ref[0] = jnp.dot(h, w_ref[:, 2176:2432], preferred_element_type=F32)
    dab_ref[0] = jnp.dot(h, w_ref[:, 2432:2560], preferred_element_type=F32)


def _input_projection(xcat, mod, g1, w_mix, cs, nblk_ctx):
    nb, t, _ = xcat.shape
    nblk = t // TOKEN_BLOCK
    widths = (512, 896, 768, 256, 256, 128)
    dtypes = (BF16, F32, F32, F32, F32, F32)
    tok = lambda w: pl.BlockSpec((1, TOKEN_BLOCK, w), lambda b, j: (b, j, 0))
    return pl.pallas_call(
        _inproj_kernel,
        out_shape=[jax.ShapeDtypeStruct((nb, t, w), dt) for w, dt in zip(widths, dtypes)],
        grid=(nb, nblk),
        in_specs=[tok(D_MODEL),
                  pl.BlockSpec((1, N_MOD, D_MODEL), _mod_index(nblk_ctx, nb)),
                  pl.BlockSpec((1, D_MODEL), lambda b, j: (0, 0)),
                  pl.BlockSpec(w_mix.shape, lambda b, j: (0, 0)),
                  pl.BlockSpec(cs.shape, lambda b, j: (0, 0))],
        out_specs=[tok(w) for w in widths],
        compiler_params=_cparams(("parallel", "arbitrary"), VMEM_LIMIT),
        name="norm1_inproj",
    )(xcat, mod, g1, w_mix, cs)


def _dft_kernel(n_batch, wc_ref, ws_ref, u_ref, o_ref):
    @pl.when(pl.program_id(1) == 0)
    def _():
        o_ref[...] = jnp.zeros_like(o_ref)

    wc = wc_ref[...]
    ws = ws_ref[...]
    for b in range(n_batch):
        u = u_ref[b]
        o_ref[b] += (jnp.dot(wc, u[:, 0:256], preferred_element_type=F32)
                     + jnp.dot(ws, u[:, 256:512], preferred_element_type=F32))


def _dft_tables(n):
    idx = jnp.arange(n, dtype=jnp.int32)
    ang = ((idx[:, None] * idx[None, :]) % n).astype(F32) * (2.0 * math.pi / n)
    scale = 1.0 / math.sqrt(n * FN_GW)
    return (jnp.cos(ang) * scale).astype(BF16), (-jnp.sin(ang) * scale).astype(BF16)


def _sequence_dft(fcs, row0, n):
    nb = fcs.shape[0]
    wc, ws = _dft_tables(n)
    tm = min(n, 1024)
    tk = min(n, 256)
    assert n % tm == 0 and n % tk == 0 and row0 % tk == 0
    kb0 = row0 // tk
    return pl.pallas_call(
        functools.partial(_dft_kernel, nb),
        out_shape=jax.ShapeDtypeStruct((nb, n, BRANCH_W), F32),
        grid=(n // tm, n // tk),
        in_specs=[pl.BlockSpec((tm, tk), lambda m, k: (m, k)),
                  pl.BlockSpec((tm, tk), lambda m, k: (m, k)),
                  pl.BlockSpec((nb, tk, 512), lambda m, k: (0, k + kb0, 0))],
        out_specs=pl.BlockSpec((nb, tm, BRANCH_W), lambda m, k: (0, m, 0)),
        compiler_params=_cparams(("parallel", "arbitrary"), VMEM_LIMIT),
        name="fourier_seq_dft",
    )(wc, ws, fcs)


def _shifted(x, s, halo_prev, halo_next):
    n = x.shape[0]
    if s == 0:
        return x
    rows = lax.broadcasted_iota(jnp.int32, x.shape, 0)
    if s < 0:
        y = pltpu.roll(x, -s, 0)
        for t in range(-s):
            y = jnp.where(rows == t, halo_prev[HALO + s + t:HALO + s + t + 1, :], y)
    else:
        y = pltpu.roll(x, n - s, 0)
        for t in range(s):
            y = jnp.where(rows == n - s + t, halo_next[t:t + 1, :], y)
    return y


def _prep_kernel(nblk_ctx, nblk, rw_ref, rwp_ref, rwn_ref, dq_ref, dqp_ref, dqn_ref,
                 mu_ref, cw_ref, ps_ref, u_ref):
    j = pl.program_id(1)
    first = jnp.logical_or(j == 0, j == nblk_ctx)
    last = jnp.logical_or(j == nblk_ctx - 1, j == nblk - 1)
    pv = jnp.where(first, 0.0, 1.0).astype(F32)
    nv = jnp.where(last, 0.0, 1.0).astype(F32)

    x = rw_ref[0]
    hp = rwp_ref[0] * pv
    hn = rwn_ref[0] * nv
    prev = _shifted(x, -1, hp, hn)
    nxt = _shifted(x, 1, hp, hn)
    ps_ref[0] = x + mu_ref[0:1, :] * (prev - x) + mu_ref[1:2, :] * (nxt - x)

    q = dq_ref[0]
    hp = dqp_ref[0] * pv
    hn = dqn_ref[0] * nv
    pad = DN_CONV // 2
    acc = cw_ref[pad:pad + 1, :] * q
    for t in range(DN_CONV):
        if t != pad:
            acc = acc + cw_ref[t:t + 1, :] * _shifted(q, t - pad, hp, hn)
    u_ref[0] = _silu(acc)


def _shift_and_conv(rw, dqkv, mu, conv_w, nblk_ctx):
    nb, t, _ = rw.shape
    nblk = t // TOKEN_BLOCK
    per = TOKEN_BLOCK // HALO
    last_halo = t // HALO - 1

    def specs(w):
        return [pl.BlockSpec((1, TOKEN_BLOCK, w), lambda b, j: (b, j, 0)),
                pl.BlockSpec((1, HALO, w), lambda b, j: (b, jnp.maximum(j * per - 1, 0), 0)),
                pl.BlockSpec((1, HALO, w), lambda b, j: (b, jnp.minimum((j + 1) * per, last_halo), 0))]

    return pl.pallas_call(
        functools.partial(_prep_kernel, nblk_ctx, nblk),
        out_shape=[jax.ShapeDtypeStruct(rw.shape, F32), jax.ShapeDtypeStruct(dqkv.shape, F32)],
        grid=(nb, nblk),
        in_specs=specs(RW_COLS) + specs(3 * BRANCH_W)
        + [pl.BlockSpec(mu.shape, lambda b, j: (0, 0)), pl.BlockSpec(conv_w.shape, lambda b, j: (0, 0))],
        out_specs=[pl.BlockSpec((1, TOKEN_BLOCK, RW_COLS), lambda b, j: (b, j, 0)),
                   pl.BlockSpec((1, TOKEN_BLOCK, 3 * BRANCH_W), lambda b, j: (b, j, 0))],
        compiler_params=_cparams(("parallel", "arbitrary")),
        name="shift_conv",
    )(rw, rw, rw, dqkv, dqkv, dqkv, mu, conv_w)


def _chunk_order(nc_ctx, nc):
    def order(d, c):
        back = jnp.where(c < nc_ctx, nc_ctx - 1 - c, nc - 1 + nc_ctx - c)
        return jnp.where(d == 0, c, back)
    return order


PAIR = 2 * HEAD_DIM
N_PAIRS = N_HEADS // 2
CHUNKS_PER_BLOCK = TOKEN_BLOCK // SCAN_CHUNK
MAP_W = 4 * PAIR
_NN = (((1,), (0,)), ((), ()))
_NT = (((1,), (1,)), ((), ()))
_TN = (((0,), (0,)), ((), ()))
_LOG2_CHUNK = int(math.log2(SCAN_CHUNK))


def _split(x):
    hi = x.astype(BF16)
    return hi, (x - hi.astype(F32)).astype(BF16)


def _mm3(a, b, dims=_NN):
    a_hi, a_lo = _split(a)
    b_hi, b_lo = _split(b)
    dg = lambda x, y: lax.dot_general(x, y, dims, preferred_element_type=F32)
    ca, cb = dims[0][0][0], dims[0][1][0]
    if a.shape[ca] <= MXU_DEPTH // 2:
        return (dg(jnp.concatenate([a_hi, a_lo], axis=ca), jnp.concatenate([b_hi, b_hi], axis=cb))
                + dg(a_hi, b_lo))
    return dg(a_hi, b_hi) + dg(a_hi, b_lo) + dg(a_lo, b_hi)


def _pair_masks(d):
    row = lax.broadcasted_iota(jnp.int32, (PAIR, PAIR), 0)
    col = lax.broadcasted_iota(jnp.int32, (PAIR, PAIR), 1)
    same = jnp.right_shift(row, _LOG2_CHUNK) == jnp.right_shift(col, _LOG2_CHUNK)
    delta = (jnp.bitwise_and(row, SCAN_CHUNK - 1) - jnp.bitwise_and(col, SCAN_CHUNK - 1)) * (1 - 2 * d)
    return same, jnp.logical_and(same, delta >= 0), jnp.logical_and(same, delta > 0), row == col


def _block_time_operators(d):
    row = lax.broadcasted_iota(jnp.int32, (TOKEN_BLOCK, TOKEN_BLOCK), 0)
    col = lax.broadcasted_iota(jnp.int32, (TOKEN_BLOCK, TOKEN_BLOCK), 1)
    same = jnp.right_shift(row, _LOG2_CHUNK) == jnp.right_shift(col, _LOG2_CHUNK)
    delta = (jnp.bitwise_and(row, SCAN_CHUNK - 1) - jnp.bitwise_and(col, SCAN_CHUNK - 1)) * (1 - 2 * d)
    cum = jnp.where(jnp.logical_and(same, delta >= 0), 1.0, 0.0).astype(F32)
    return cum, jnp.where(same, 1.0, 0.0).astype(F32)


def _chunk_problems():
    return [(slice(SCAN_CHUNK * c, SCAN_CHUNK * (c + 1)), slice(PAIR * pr, PAIR * (pr + 1)))
            for c in range(CHUNKS_PER_BLOCK) for pr in range(N_PAIRS)]


def _bd(x, same):
    return jnp.where(same, jnp.concatenate([x, x], axis=0), 0.0)


def _pack(x_bd):
    return x_bd[0:HEAD_DIM] + x_bd[HEAD_DIM:PAIR]


def _unit_tri_inverse(n_mats, eye, mm):
    xs = [eye + n for n in n_mats]
    ps = [mm(n, n) for n in n_mats]
    for level in range(1, _LOG2_CHUNK):
        if level + 1 < _LOG2_CHUNK:
            xps = [mm(p, jnp.concatenate([x, p], axis=1)) for x, p in zip(xs, ps)]
            xs = [x + xp[:, 0:PAIR] for x, xp in zip(xs, xps)]
            ps = [xp[:, PAIR:2 * PAIR] for xp in xps]
        else:
            xs = [x + mm(p, x) for x, p in zip(xs, ps)]
    return xs


def _affine_scan_kernel(n_batch, m_ref, y_ref, h_ref):
    @pl.when(pl.program_id(1) == 0)
    def _():
        h_ref[...] = jnp.zeros_like(h_ref)

    same, _, _, _ = _pair_masks(0)
    chains = [(b, pr) for b in range(n_batch) for pr in range(N_PAIRS)]
    part = lambda b, pr, i: m_ref[0, b, :, MAP_W * pr + PAIR * i:MAP_W * pr + PAIR * (i + 1)]
    lhs = [jnp.concatenate([_bd(part(b, pr, 0), same), _bd(part(b, pr, 2), same)], axis=0) for b, pr in chains]
    outs = [_mm3(x, h_ref[b, pr]) for x, (b, pr) in zip(lhs, chains)]
    for out, (b, pr) in zip(outs, chains):
        h_ref[b, pr] = out[0:PAIR] + _bd(part(b, pr, 1), same)
        y_ref[0, b, :, PAIR * pr:PAIR * (pr + 1)] = _pack(out[PAIR:2 * PAIR]) + part(b, pr, 3)


def _affine_scan(maps, nc_ctx, name):
    _, nb, t, _ = maps.shape
    nc = t // SCAN_CHUNK
    order = _chunk_order(nc_ctx, nc)
    return pl.pallas_call(
        functools.partial(_affine_scan_kernel, nb),
        out_shape=jax.ShapeDtypeStruct((2, nb, t, BRANCH_W), F32),
        grid=(2, nc),
        in_specs=[pl.BlockSpec((1, nb, SCAN_CHUNK, N_PAIRS * MAP_W), lambda d, c: (d, 0, order(d, c), 0))],
        out_specs=pl.BlockSpec((1, nb, SCAN_CHUNK, BRANCH_W), lambda d, c: (d, 0, order(d, c), 0)),
        scratch_shapes=[pltpu.VMEM((nb, N_PAIRS, PAIR, PAIR), F32)],
        compiler_params=_cparams(("parallel", "arbitrary")),
        name=name,
    )(maps)


def _rwkv_chunk_kernel(p_ref, w0_ref, wup_ref, a0_ref, aup_ref, kk_ref, ka_ref, ones_ref, o_ref):
    d = pl.program_id(0)
    p = p_ref[0]
    r = p[:, 0:256]
    k = p[:, 256:512]
    v = p[:, 512:768]
    wl = p[:, 768:800]
    al = p[:, 800:832]
    lw = -RW_DECAY_SCALE * _sigmoid(w0_ref[0] + _dot(jnp.tanh(wl), wup_ref[0]))
    a = _sigmoid(a0_ref[0] + _dot(al, aup_ref[0]))
    kkp = k * kk_ref[...]
    kk = kkp * lax.rsqrt(_mm3(kkp * kkp, ones_ref[...]) + EPS)
    kmod = k * (1.0 + (a - 1.0) * ka_ref[...])
    alpha = -(a * kk)

    cum, tot = _block_time_operators(d)
    g = _dot_hi(cum, lw)
    g_tot = _dot_hi(tot, lw)
    e_neg = jnp.exp(-g)
    e_tail = jnp.exp(g_tot - g)
    gam = jnp.exp(g_tot)
    b_all = kk * jnp.exp(g - lw)
    r_all = r * jnp.exp(g)
    kh_all = kmod * e_neg
    ah_all = alpha * e_neg
    kt_all = kmod * e_tail
    at_all = alpha * e_tail

    same, incl, strict, eye = _pair_masks(d)
    eye_f = jnp.where(eye, 1.0, 0.0).astype(F32)
    zeros = jnp.zeros((PAIR, PAIR), F32)
    probs = _chunk_problems()
    bd = lambda x: [_bd(x[rows, lanes], same) for rows, lanes in probs]
    cat = jnp.concatenate
    b_t, r_t, k_h, a_h, k_t, a_t, v_b = bd(b_all), bd(r_all), bd(kh_all), bd(ah_all), bd(kt_all), bd(at_all), bd(v)
    m = [_mm3(cat([b, r_], axis=0), cat([kh, ah], axis=0), _NT) for b, r_, kh, ah in zip(b_t, r_t, k_h, a_h)]
    a_bk = [jnp.where(strict, x[0:PAIR, 0:PAIR], 0.0) for x in m]
    a_ba = [jnp.where(strict, x[0:PAIR, PAIR:], 0.0) for x in m]
    a_rk = [jnp.where(incl, x[PAIR:, 0:PAIR], 0.0) for x in m]
    a_ra = [jnp.where(incl, x[PAIR:, PAIR:], 0.0) for x in m]
    av = [_mm3(cat([x, y], axis=0), vb) for x, y, vb in zip(a_bk, a_rk, v_b)]
    t_inv = _unit_tri_inverse(a_ba, eye_f, _mm3)
    sol = [_mm3(t, cat([b, x[0:PAIR]], axis=1)) for t, b, x in zip(t_inv, b_t, av)]
    qy = [cat([r_, x[PAIR:]], axis=1) + _mm3(a, s) for r_, x, a, s in zip(r_t, av, a_ra, sol)]
    kb = [_mm3(cat([at, kt], axis=0), cat([s, cat([zeros, vb], axis=1)], axis=0), _TN)
          for at, kt, s, vb in zip(a_t, k_t, sol, v_b)]
    for (rows, lanes), kb_i, qy_i in zip(probs, kb, qy):
        a_c = kb_i[:, 0:PAIR] + jnp.where(eye, gam[rows.start:rows.start + 1, lanes], 0.0)
        base = MAP_W * (lanes.start // PAIR)
        o_ref[0, 0, rows, base:base + PAIR] = _pack(a_c)
        o_ref[0, 0, rows, base + PAIR:base + 2 * PAIR] = _pack(kb_i[:, PAIR:])
        o_ref[0, 0, rows, base + 2 * PAIR:base + 3 * PAIR] = _pack(qy_i[:, 0:PAIR])
        o_ref[0, 0, rows, base + 3 * PAIR:base + 4 * PAIR] = _pack(qy_i[:, PAIR:])


def _rwkv_scan(ps, w0, w_up, a0, a_up, k_k, k_a, ones_bd, nc_ctx):
    nb, t, _ = ps.shape
    per_dir = lambda shape: pl.BlockSpec((1,) + shape, lambda d, b, j: (d, 0, 0))
    const = lambda shape: pl.BlockSpec(shape, lambda d, b, j: (0, 0))
    maps = pl.pallas_call(
        _rwkv_chunk_kernel,
        out_shape=jax.ShapeDtypeStruct((2, nb, t, N_PAIRS * MAP_W), F32),
        grid=(2, nb, t // TOKEN_BLOCK),
        in_specs=[pl.BlockSpec((1, TOKEN_BLOCK, RW_COLS), lambda d, b, j: (b, j, 0)),
                  per_dir((1, BRANCH_W)), per_dir((32, BRANCH_W)),
                  per_dir((1, BRANCH_W)), per_dir((32, BRANCH_W)),
                  const((1, BRANCH_W)), const((1, BRANCH_W)), const((BRANCH_W, BRANCH_W))],
        out_specs=pl.BlockSpec((1, 1, TOKEN_BLOCK, N_PAIRS * MAP_W), lambda d, b, j: (d, b, j, 0)),
        compiler_params=_cparams(("parallel", "parallel", "arbitrary"), VMEM_LIMIT),
        name="rwkv7_chunks",
    )(ps, w0.reshape(2, 1, BRANCH_W), w_up, a0.reshape(2, 1, BRANCH_W), a_up,
      k_k.reshape(1, BRANCH_W), k_a.reshape(1, BRANCH_W), ones_bd)
    return _affine_scan(maps, nc_ctx, "rwkv7_state_scan")


def _deltanet_chunk_kernel(u_ref, dab_ref, nea_ref, dtb_ref, ea_ref, eb_ref, ones_ref, o_ref):
    d = pl.program_id(0)
    u = u_ref[0]
    dab = dab_ref[0]
    log_a8 = nea_ref[...] * jax.nn.softplus(dab + dtb_ref[...])
    la = _dot_hi(log_a8, ea_ref[0])
    beta = _dot_hi(_sigmoid(dab), eb_ref[0])
    ones = ones_ref[...]
    q = u[:, 0:256]
    k = u[:, 256:512]
    v = u[:, 512:768]
    q = q * lax.rsqrt(_mm3(q * q, ones) + EPS) * (HEAD_DIM ** -0.5)
    k = k * lax.rsqrt(_mm3(k * k, ones) + EPS)

    cum, tot = _block_time_operators(d)
    g = _dot_hi(cum, la)
    g_tot = _dot_hi(tot, la)
    e_g = jnp.exp(g)
    gam = jnp.exp(g_tot)
    kb_all = k * beta
    vb_all = v * beta
    kbe_all = kb_all * e_g
    qe_all = q * e_g
    kt_all = k * jnp.exp(g_tot - g)

    same, incl, strict, eye = _pair_masks(d)
    eye_f = jnp.where(eye, 1.0, 0.0).astype(F32)
    probs = _chunk_problems()
    bd = lambda x: [_bd(x[rows, lanes], same) for rows, lanes in probs]
    cat = jnp.concatenate
    g_b = bd(g)
    decay = [jnp.exp(jnp.where(incl, x - x.T, -jnp.inf)) for x in g_b]
    m = [_mm3(cat([kb, q_], axis=0), k_, _NT) for kb, q_, k_ in zip(bd(kb_all), bd(q), bd(k))]
    a_low = [jnp.where(strict, x[0:PAIR] * dc, 0.0) for x, dc in zip(m, decay)]
    attn = [x[PAIR:] * dc for x, dc in zip(m, decay)]
    t_inv = _unit_tri_inverse([-x for x in a_low], eye_f, _mm3)
    sol = [_mm3(t, cat([vb, kbe], axis=1)) for t, vb, kbe in zip(t_inv, bd(vb_all), bd(kbe_all))]
    att_sol = [_mm3(a, s) for a, s in zip(attn, sol)]
    ks = [_mm3(kt, s, _TN) for kt, s in zip(bd(kt_all), sol)]
    for (rows, lanes), ks_i, as_i, qe_i in zip(probs, ks, att_sol, bd(qe_all)):
        a_c = jnp.where(eye, gam[rows.start:rows.start + 1, lanes], 0.0) - ks_i[:, PAIR:]
        base = MAP_W * (lanes.start // PAIR)
        o_ref[0, 0, rows, base:base + PAIR] = _pack(a_c)
        o_ref[0, 0, rows, base + PAIR:base + 2 * PAIR] = _pack(ks_i[:, 0:PAIR])
        o_ref[0, 0, rows, base + 2 * PAIR:base + 3 * PAIR] = _pack(qe_i - as_i[:, PAIR:])
        o_ref[0, 0, rows, base + 3 * PAIR:base + 4 * PAIR] = _pack(as_i[:, 0:PAIR])


def _deltanet_scan(u, dab, a_log, dt_bias, ones_bd, nc_ctx):
    nb, t, _ = u.shape
    pad = 128 - 2 * N_HEADS
    neg_exp_a = jnp.pad(-jnp.exp(a_log.reshape(1, -1)), ((0, 0), (0, pad)))
    dtb = jnp.pad(dt_bias.reshape(1, -1), ((0, 0), (0, pad)))
    col = jnp.arange(128)[None, :, None]
    head = (jnp.arange(BRANCH_W) // HEAD_DIM)[None, None, :]
    dirs = jnp.arange(2)[:, None, None]
    expand_a = (col == dirs * N_HEADS + head).astype(F32)
    expand_b = (col == 2 * N_HEADS + dirs * N_HEADS + head).astype(F32)
    const = lambda shape: pl.BlockSpec(shape, lambda d, b, j: (0, 0))
    per_dir = pl.BlockSpec((1, 128, BRANCH_W), lambda d, b, j: (d, 0, 0))
    maps = pl.pallas_call(
        _deltanet_chunk_kernel,
        out_shape=jax.ShapeDtypeStruct((2, nb, t, N_PAIRS * MAP_W), F32),
        grid=(2, nb, t // TOKEN_BLOCK),
        in_specs=[pl.BlockSpec((1, TOKEN_BLOCK, 3 * BRANCH_W), lambda d, b, j: (b, j, 0)),
                  pl.BlockSpec((1, TOKEN_BLOCK, 128), lambda d, b, j: (b, j, 0)),
                  const((1, 128)), const((1, 128)), per_dir, per_dir, const((BRANCH_W, BRANCH_W))],
        out_specs=pl.BlockSpec((1, 1, TOKEN_BLOCK, N_PAIRS * MAP_W), lambda d, b, j: (d, b, j, 0)),
        compiler_params=_cparams(("parallel", "parallel", "arbitrary"), VMEM_LIMIT),
        name="deltanet_chunks",
    )(u, dab, neg_exp_a, dtb, expand_a, expand_b, ones_bd)
    return _affine_scan(maps, nc_ctx, "deltanet_state_scan")


def _s5_kernel(n_batch, nc_ctx, nc, u_ref, toep_ref, winr_ref, wini_ref, wsor_ref, wsoi_ref,
               lr_ref, li_ref, dt_ref, y_ref, injr, inji, xsr, xsi):
    d = pl.program_id(1)
    u = u_ref[0]
    ub = u.astype(BF16)
    injr[...] = jnp.dot(ub, winr_ref[0, 0], preferred_element_type=F32)
    inji[...] = jnp.dot(ub, wini_ref[0, 0], preferred_element_type=F32)
    lam_r = lr_ref[0, 0]
    lam_i = li_ref[0, 0]

    def body(s, carry):
        xr, xi = carry
        back = jnp.where(s < nc_ctx, nc_ctx - 1 - s, nc - 1 + nc_ctx - s)
        row0 = jnp.where(d == 0, s, back) * n_batch
        if n_batch % 8 == 0:
            row0 = pl.multiple_of(row0, 8)
        rows = pl.ds(row0, n_batch)
        xsr[rows, :] = xr
        xsi[rows, :] = xi
        return (lam_r * xr - lam_i * xi + injr[rows, :], lam_r * xi + lam_i * xr + inji[rows, :])

    zero = jnp.zeros((n_batch, S5_STATE), F32)
    lax.fori_loop(0, nc, body, (zero, zero))

    y = (jnp.dot(ub, toep_ref[0, 0], preferred_element_type=F32)
         + _dot(xsr[...], wsor_ref[0, 0]) + _dot(xsi[...], wsoi_ref[0, 0]))

    @pl.when(d == 0)
    def _():
        y_ref[0] = y + u * dt_ref[0]

    @pl.when(d == 1)
    def _():
        y_ref[0] += y


def _s5_tables(lam_re, lam_im, log_step, b_re, b_im, c_re, c_im):
    cs = S5_CHUNK
    lam = lax.complex(lam_re.astype(F32), lam_im.astype(F32))
    step = jnp.exp(log_step.astype(F32))[..., None]
    tau = jnp.arange(cs + 1, dtype=F32)[:, None, None, None]
    lam_pow = jnp.exp(lam[None] * step[None] * tau)
    lam_bar = lam_pow[1]
    b_bar = ((lam_bar - 1.0) / lam)[..., None] * lax.complex(b_re.astype(F32), b_im.astype(F32))
    c_mat = lax.complex(c_re.astype(F32), c_im.astype(F32))
    kern = jnp.real(jnp.einsum("dghp,tdgp,dgpk->tdghk", c_mat, lam_pow[:cs], b_bar))
    i = jnp.arange(cs)
    lag_f = i[None, :] - i[:, None]
    toeps, winr, wini, wsor, wsoi = [], [], [], [], []
    for d in range(2):
        lag = lag_f if d == 0 else -lag_f
        kd = jnp.where((lag >= 0)[:, :, None, None, None], kern[:, d][jnp.clip(lag, 0, cs - 1)], 0.0)
        toeps.append(kd.transpose(2, 0, 4, 1, 3).reshape(S5_GROUPS, cs * S5_GW, cs * S5_GW))
        pw_in = (cs - 1 - i) if d == 0 else i
        e = lam_pow[pw_in, d][..., None] * b_bar[d][None]
        e = e.transpose(1, 0, 3, 2).reshape(S5_GROUPS, cs * S5_GW, S5_STATE)
        winr.append(jnp.real(e))
        wini.append(jnp.imag(e))
        pw_out = (i + 1) if d == 0 else (cs - i)
        m = c_mat[d][None] * lam_pow[pw_out, d][:, :, None, :]
        m = m.transpose(1, 3, 0, 2).reshape(S5_GROUPS, S5_STATE, cs * S5_GW)
        wsor.append(jnp.real(m))
        wsoi.append(-jnp.imag(m))
    stack = lambda xs, dt: jnp.stack(xs).astype(dt)
    lam_c = lam_pow[cs]
    return (stack(toeps, BF16), stack(winr, BF16), stack(wini, BF16), stack(wsor, BF16), stack(wsoi, BF16),
            jnp.real(lam_c)[:, :, None, :], jnp.imag(lam_c)[:, :, None, :])


def _s5_to_chunks(s5, n_ctx):
    nb, t, _ = s5.shape
    cs, g, hw = S5_CHUNK, S5_GROUPS, S5_GW
    n_lat = t - n_ctx
    rows = n_lat // 64
    c = s5[:, :n_ctx].reshape(nb, n_ctx // cs, cs, g, hw)
    c = c.transpose(3, 1, 0, 2, 4).reshape(g, (n_ctx // cs) * nb, cs * hw)
    l = s5[:, n_ctx:].reshape(nb, rows // cs, cs, 64, g, hw)
    l = l.transpose(4, 3, 1, 0, 2, 5).reshape(g, 64 * (rows // cs) * nb, cs * hw)
    return jnp.concatenate([c, l], axis=1)


def _s5_from_chunks(y, nb, n_ctx, n_lat):
    cs, g, hw = S5_CHUNK, S5_GROUPS, S5_GW
    rows = n_lat // 64
    r_ctx = (n_ctx // cs) * nb
    c = y[:, :r_ctx].reshape(g, n_ctx // cs, nb, cs, hw).transpose(2, 1, 3, 0, 4).reshape(nb, n_ctx, g * hw)
    l = y[:, r_ctx:].reshape(g, 64, rows // cs, nb, cs, hw).transpose(3, 2, 4, 1, 0, 5).reshape(nb, n_lat, g * hw)
    return jnp.concatenate([c, l], axis=1)


def _s5_scan(s5, tables, d_skip, n_ctx):
    nb, t, _ = s5.shape
    u = _s5_to_chunks(s5, n_ctx)
    g, r, w = u.shape
    nc = t // S5_CHUNK
    toep, winr, wini, wsor, wsoi, lr, li = tables
    d_tile = jnp.tile(d_skip.astype(F32).reshape(S5_GROUPS, 1, S5_GW), (1, S5_CHUNK, 1)).reshape(g, 1, w)
    per = lambda a, b: pl.BlockSpec((1, 1, a, b), lambda gi, d: (d, gi, 0, 0))
    y = pl.pallas_call(
        functools.partial(_s5_kernel, nb, n_ctx // S5_CHUNK, nc),
        out_shape=jax.ShapeDtypeStruct((g, r, w), F32),
        grid=(g, 2),
        in_specs=[pl.BlockSpec((1, r, w), lambda gi, d: (gi, 0, 0)),
                  per(w, w), per(w, S5_STATE), per(w, S5_STATE), per(S5_STATE, w), per(S5_STATE, w),
                  per(1, S5_STATE), per(1, S5_STATE),
                  pl.BlockSpec((1, 1, w), lambda gi, d: (gi, 0, 0))],
        out_specs=pl.BlockSpec((1, r, w), lambda gi, d: (gi, 0, 0)),
        scratch_shapes=[pltpu.VMEM((r, S5_STATE), F32) for _ in range(4)],
        compiler_params=_cparams(("parallel", "arbitrary"), VMEM_LIMIT),
        name="s5_scan",
    )(u, toep, winr, wini, wsor, wsoi, lr, li, d_tile)
    return _s5_from_chunks(y, nb, n_ctx, t - n_ctx)


def _merge_kernel(x_ref, mod_ref, g1_ref, ya_ref, yb_ref, ps_ref, oc_ref, dg_ref, y5_ref,
                  wg_ref, wb_ref, wo_ref, avg_ref, ones_ref, rk_ref, gup_ref, lng_ref, lnb_ref,
                  dng_ref, wglu_ref, bglu_ref, o_ref):
    x = x_ref[0]
    h = _norm_mod(x, g1_ref[...], mod_ref[0, 1:2, :], mod_ref[0, 0:1, :]).astype(BF16)
    avg = avg_ref[...]

    ps = ps_ref[0]
    r = ps[:, 0:256]
    k = ps[:, 256:512]
    v = ps[:, 512:768]
    gl = ps[:, 832:896]
    y = yb_ref[0, 0] + yb_ref[1, 0]
    dev = y - _dot(y, avg)
    yn = dev * lax.rsqrt(_dot(dev * dev, avg) + RW_GN_EPS) * lng_ref[...] + lnb_ref[...]
    bonus = _dot(r * k * rk_ref[...], ones_ref[...]) * v
    yb = (yn + bonus) * _dot(_sigmoid(gl), gup_ref[...])

    o = oc_ref[0, 0] + oc_ref[1, 0]
    yc = o * lax.rsqrt(_dot(o * o, avg) + EPS) * dng_ref[...] * _silu(dg_ref[0])

    z = jax.nn.gelu(y5_ref[0])
    yd = z * _sigmoid(_dot(z, wglu_ref[...]) + bglu_ref[...])

    m = jnp.zeros((TOKEN_BLOCK, D_MODEL), F32)
    for i, yi in enumerate((ya_ref[0], yb, yc, yd)):
        gate = _sigmoid(jnp.dot(h, wg_ref[:, D_MODEL * i:D_MODEL * (i + 1)], preferred_element_type=F32))
        m = m + gate * _dot(yi, wb_ref[i])
    o_ref[0] = x + mod_ref[0, 2:3, :] * _dot(m, wo_ref[...])


def _merge(xcat, mod, g1, ya, yb, ps, oc, dg, y5, w_gate, w_branch, w_out, avg_bd, ones_bd,
           r_k, g_up, ln_g, ln_b, dn_g, w_glu, b_glu, nblk_ctx):
    nb, t, _ = xcat.shape
    nblk = t // TOKEN_BLOCK
    tok = lambda w: pl.BlockSpec((1, TOKEN_BLOCK, w), lambda b, j: (b, j, 0))
    tok2 = lambda w: pl.BlockSpec((2, 1, TOKEN_BLOCK, w), lambda b, j: (0, b, j, 0))
    full = lambda a: pl.BlockSpec(a.shape, lambda b, j: (0,) * a.ndim)
    consts = (w_gate, w_branch, w_out, avg_bd, ones_bd, r_k, g_up, ln_g, ln_b, dn_g, w_glu, b_glu)
    return pl.pallas_call(
        _merge_kernel,
        out_shape=jax.ShapeDtypeStruct(xcat.shape, F32),
        grid=(nb, nblk),
        in_specs=[tok(D_MODEL), pl.BlockSpec((1, N_MOD, D_MODEL), _mod_index(nblk_ctx, nb)),
                  pl.BlockSpec((1, D_MODEL), lambda b, j: (0, 0)),
                  tok(BRANCH_W), tok2(BRANCH_W), tok(RW_COLS), tok2(BRANCH_W), tok(BRANCH_W), tok(BRANCH_W)]
        + [full(a) for a in consts],
        out_specs=tok(D_MODEL),
        compiler_params=_cparams(("parallel", "arbitrary"), VMEM_LIMIT),
        name="merge_branches",
    )(xcat, mod, g1, ya, yb, ps, oc, dg, y5, *consts)


def _route(sel, score):
    s = [sel[e:e + 1, :] for e in range(N_EXPERTS)]
    sc = [score[e:e + 1, :] for e in range(N_EXPERTS)]
    n_groups = N_EXPERTS // EXPERTS_PER_GROUP
    group_score = []
    for g in range(n_groups):
        m = s[EXPERTS_PER_GROUP * g:EXPERTS_PER_GROUP * (g + 1)]
        best = None
        for i in range(EXPERTS_PER_GROUP):
            for j in range(i + 1, EXPERTS_PER_GROUP):
                pair = m[i] + m[j]
                best = pair if best is None else jnp.maximum(best, pair)
        group_score.append(best)
    best_g = jnp.zeros(group_score[0].shape, jnp.int32)
    best_v = group_score[0]
    for g in range(1, n_groups):
        upd = group_score[g] > best_v
        best_g = jnp.where(upd, g, best_g)
        best_v = jnp.where(upd, group_score[g], best_v)
    chosen = []
    den = jnp.zeros_like(best_v)
    for e in range(N_EXPERTS):
        g = e // EXPERTS_PER_GROUP
        rank = jnp.zeros(best_g.shape, jnp.int32)
        for j in range(EXPERTS_PER_GROUP * g, EXPERTS_PER_GROUP * (g + 1)):
            if j == e:
                continue
            ahead = (s[j] > s[e]) if j > e else (s[j] >= s[e])
            rank = rank + jnp.where(ahead, 1, 0)
        pick = jnp.logical_and(best_g == g, rank < 2)
        chosen.append(pick)
        den = den + jnp.where(pick, sc[e], 0.0)
    return jnp.concatenate([jnp.where(chosen[e], sc[e] / den, 0.0) for e in range(N_EXPERTS)], axis=0)


def _moe_kernel(x_ref, mod_ref, g2_ref, rwt_ref, rb_ref, w1_ref, w3_ref, w2_ref, exp_ref, o_ref):
    x = x_ref[0]
    h = _norm_mod(x, g2_ref[...], mod_ref[0, 4:5, :], mod_ref[0, 3:4, :])
    logits = lax.dot_general(rwt_ref[...], h, (((1,), (1,)), ((), ())),
                             preferred_element_type=F32, precision=HIGHEST)
    score = _sigmoid(logits)
    comb = _route(score + rb_ref[...], score)
    hb = h.astype(BF16)
    combb = comb.astype(BF16)
    acc = jnp.zeros((TOKEN_BLOCK, D_MODEL), F32)
    width = 4 * D_EXPERT
    for q in range(N_EXPERTS * D_EXPERT // width):
        cols = slice(width * q, width * (q + 1))
        cw = lax.dot_general(combb, exp_ref[:, cols], (((0,), (0,)), ((), ())),
                             preferred_element_type=F32)
        a1 = jnp.dot(hb, w1_ref[:, cols], preferred_element_type=F32)
        a3 = jnp.dot(hb, w3_ref[:, cols], preferred_element_type=F32)
        act = (_silu(a1) * a3 * cw).astype(BF16)
        acc = acc + jnp.dot(act, w2_ref[cols, :], preferred_element_type=F32)
    o_ref[0] = x + mod_ref[0, 5:6, :] * acc


def _moe(x1, mod, g2, router_wt, router_b, w1, w3, w2, expand, nblk_ctx):
    nb, t, _ = x1.shape
    nblk = t // TOKEN_BLOCK
    tok = pl.BlockSpec((1, TOKEN_BLOCK, D_MODEL), lambda b, j: (b, j, 0))
    full = lambda a: pl.BlockSpec(a.shape, lambda b, j: (0,) * a.ndim, pipeline_mode=pl.Buffered(1))
    consts = (router_wt, router_b, w1, w3, w2, expand)
    return pl.pallas_call(
        _moe_kernel,
        out_shape=jax.ShapeDtypeStruct(x1.shape, F32),
        grid=(nb, nblk),
        in_specs=[tok, pl.BlockSpec((1, N_MOD, D_MODEL), _mod_index(nblk_ctx, nb)),
                  pl.BlockSpec((1, D_MODEL), lambda b, j: (0, 0))] + [full(a) for a in consts],
        out_specs=tok,
        compiler_params=_cparams(("parallel", "arbitrary"), VMEM_LIMIT),
        name="moe_ffn",
    )(x1, mod, g2, *consts)


def _final_norm_kernel(x_ref, g_ref, o_ref):
    x = x_ref[0]
    o_ref[0] = x * lax.rsqrt(jnp.mean(x * x, axis=-1, keepdims=True) + EPS) * g_ref[...]


def _final_norm(xcat, g, n_ctx):
    nb, t, _ = xcat.shape
    n_lat = t - n_ctx
    off = n_ctx // TOKEN_BLOCK
    return pl.pallas_call(
        _final_norm_kernel,
        out_shape=jax.ShapeDtypeStruct((nb, n_lat, D_MODEL), F32),
        grid=(nb, n_lat // TOKEN_BLOCK),
        in_specs=[pl.BlockSpec((1, TOKEN_BLOCK, D_MODEL), lambda b, j: (b, j + off, 0)),
                  pl.BlockSpec((1, D_MODEL), lambda b, j: (0, 0))],
        out_specs=pl.BlockSpec((1, TOKEN_BLOCK, D_MODEL), lambda b, j: (b, j, 0)),
        compiler_params=_cparams(("parallel", "arbitrary")),
        name="final_norm",
    )(xcat, g.reshape(1, D_MODEL))


def _block_diag_ones(n_blocks, size):
    return jnp.kron(jnp.eye(n_blocks, dtype=F32), jnp.ones((size, size), F32))


def _layer(xcat, cond, n_ctx, lp, router_wt, router_b, consts):
    nb, t, _ = xcat.shape
    nblk_ctx = n_ctx // TOKEN_BLOCK
    nc_ctx = n_ctx // SCAN_CHUNK
    mod = _modulation(cond, lp["w_mod"], lp["b_mod"])
    g1 = lp["norm1_g"].reshape(1, D_MODEL)

    offs = [0]
    for s in W_IN_SPLITS:
        offs.append(offs[-1] + s)
    w_in = lp["w_in"]
    seg = lambda i: w_in[:, offs[i]:offs[i + 1]]
    w_mix = jnp.concatenate([seg(0), seg(1), seg(2), seg(3), seg(6), seg(4), seg(5),
                             jnp.zeros((D_MODEL, 128 - 16), F32)], axis=1).astype(BF16)
    w_gate = seg(7).astype(BF16)

    fcs, rw, dqkv, dg, s5, dab = _input_projection(xcat, mod, g1, w_mix, consts["cs"], nblk_ctx)

    ya = jnp.concatenate([_sequence_dft(fcs, 0, n_ctx), _sequence_dft(fcs, n_ctx, t - n_ctx)], axis=1)

    ps, u = _shift_and_conv(rw, dqkv, lp["rw_mu"], lp["dn_conv"], nblk_ctx)
    yb = _rwkv_scan(ps, lp["rw_w0"], lp["rw_w_up"], lp["rw_a0"], lp["rw_a_up"], lp["rw_k_k"], lp["rw_k_a"],
                    consts["ones_bd"], nc_ctx)
    oc = _deltanet_scan(u, dab, lp["dn_a_log"], lp["dn_dt_bias"], consts["ones_bd"], nc_ctx)
    tables = _s5_tables(lp["s5_lam_re"], lp["s5_lam_im"], lp["s5_log_step"], lp["s5_b_re"], lp["s5_b_im"],
                        lp["s5_c_re"], lp["s5_c_im"])
    y5 = _s5_scan(s5, tables, lp["s5_d"], n_ctx)

    row = lambda a: a.reshape(1, -1).astype(F32)
    x1 = _merge(xcat, mod, g1, ya, yb, ps, oc, dg, y5, w_gate, lp["w_branch"].astype(BF16),
                lp["w_out"].astype(BF16), consts["avg_bd"], consts["ones_bd"], row(lp["rw_r_k"]),
                lp["rw_g_up"].astype(BF16), row(lp["rw_ln_g"]), row(lp["rw_ln_b"]),
                row(jnp.tile(lp["dn_norm_g"], N_HEADS)), lp["s5_w_glu"].astype(BF16), row(lp["s5_b_glu"]),
                nblk_ctx)

    w1 = lp["moe_w1"].transpose(1, 0, 2).reshape(D_MODEL, N_EXPERTS * D_EXPERT).astype(BF16)
    w3 = lp["moe_w3"].transpose(1, 0, 2).reshape(D_MODEL, N_EXPERTS * D_EXPERT).astype(BF16)
    w2 = lp["moe_w2"].reshape(N_EXPERTS * D_EXPERT, D_MODEL).astype(BF16)
    return _moe(x1, mod, lp["norm2_g"].reshape(1, D_MODEL), router_wt, router_b, w1, w3, w2,
                consts["expand"], nblk_ctx)


def kernel(x, c, ctx, c_ctx, w_mod, b_mod, norm1_g, norm2_g, w_in, rw_mu, rw_w0, rw_w_up, rw_a0, rw_a_up, rw_k_k, rw_k_a, rw_r_k, rw_g_up, rw_ln_g, rw_ln_b, dn_conv, dn_a_log, dn_dt_bias, dn_norm_g, s5_lam_re, s5_lam_im, s5_log_step, s5_b_re, s5_b_im, s5_c_re, s5_c_im, s5_d, s5_w_glu, s5_b_glu, w_branch, w_out, router_w, router_b, moe_w1, moe_w3, moe_w2, final_g):
    nb, n_lat, _ = x.shape
    n_ctx = ctx.shape[1]
    depth = w_mod.shape[0]
    assert n_ctx % TOKEN_BLOCK == 0 and n_lat % TOKEN_BLOCK == 0 and (n_lat // 64) % S5_CHUNK == 0

    xcat = jnp.concatenate([ctx, x], axis=1).astype(F32)
    cond_rows = -(-(nb + 1) // 8) * 8
    cond = jnp.zeros((cond_rows, D_MODEL), F32).at[:nb].set(c).at[nb].set(c_ctx)

    j = jnp.arange(FN_GW, dtype=jnp.int32)
    ang = ((j[:, None] * j[None, :]) % FN_GW).astype(F32) * (2.0 * math.pi / FN_GW)
    eye = jnp.eye(BRANCH_W // FN_GW, dtype=F32)
    consts = {
        "cs": jnp.concatenate([jnp.kron(eye, jnp.cos(ang)), jnp.kron(eye, jnp.sin(ang))], axis=1).astype(BF16),
        "ones_bd": _block_diag_ones(N_HEADS, HEAD_DIM).astype(BF16),
        "avg_bd": (_block_diag_ones(N_HEADS, HEAD_DIM) / HEAD_DIM).astype(BF16),
        "expand": jnp.kron(jnp.eye(N_EXPERTS, dtype=F32), jnp.ones((1, D_EXPERT), F32)).astype(BF16),
    }
    router_wt = router_w.T.astype(F32)
    router_bc = router_b.reshape(N_EXPERTS, 1).astype(F32)

    names = ("w_mod", "b_mod", "norm1_g", "norm2_g", "w_in", "rw_mu", "rw_w0", "rw_w_up", "rw_a0", "rw_a_up",
             "rw_k_k", "rw_k_a", "rw_r_k", "rw_g_up", "rw_ln_g", "rw_ln_b", "dn_conv", "dn_a_log", "dn_dt_bias",
             "dn_norm_g", "s5_lam_re", "s5_lam_im", "s5_log_step", "s5_b_re", "s5_b_im", "s5_c_re", "s5_c_im",
             "s5_d", "s5_w_glu", "s5_b_glu", "w_branch", "w_out", "moe_w1", "moe_w3", "moe_w2")
    stacked = (w_mod, b_mod, norm1_g, norm2_g, w_in, rw_mu, rw_w0, rw_w_up, rw_a0, rw_a_up, rw_k_k, rw_k_a,
               rw_r_k, rw_g_up, rw_ln_g, rw_ln_b, dn_conv, dn_a_log, dn_dt_bias, dn_norm_g, s5_lam_re,
               s5_lam_im, s5_log_step, s5_b_re, s5_b_im, s5_c_re, s5_c_im, s5_d, s5_w_glu, s5_b_glu,
               w_branch, w_out, moe_w1, moe_w3, moe_w2)
    for i in range(depth):
        lp = {n: a[i] for n, a in zip(names, stacked)}
        xcat = _layer(xcat, cond, n_ctx, lp, router_wt, router_bc, consts)
    return _final_norm(xcat, final_g, n_ctx)
```

```python
import functools
import math

import jax
import jax.numpy as jnp
from jax import lax
from jax.experimental import pallas as pl
from jax.experimental.pallas import tpu as pltpu

F32 = jnp.float32
BF16 = jnp.bfloat16
HIGHEST = lax.Precision.HIGHEST

D_MODEL = 1024
N_MOD = 6
EPS = 1e-6
N_BRANCH = 4
BRANCH_W = 256
HEAD_DIM = 64
N_HEADS = 4
FN_GW = 64
RW_COLS = 896
RW_DECAY_SCALE = math.exp(-0.5)
RW_GN_EPS = 64e-5
DN_CONV = 5
S5_GW = 16
S5_GROUPS = 16
S5_STATE = 64
N_EXPERTS = 16
EXPERTS_PER_GROUP = 4
D_EXPERT = 256
W_IN_SPLITS = (256, 896, 768, 256, 8, 8, 256, 4096)

TOKEN_BLOCK = 256
SCAN_CHUNK = 64
S5_CHUNK = 16
HALO = 8
MXU_DEPTH = 256
VMEM_LIMIT = 56 * 1024 * 1024


def _cparams(sem, vmem=None, **kw):
    return pltpu.CompilerParams(dimension_semantics=sem, vmem_limit_bytes=vmem, **kw)


def _dot(a, b):
    return jnp.dot(a.astype(BF16), b.astype(BF16), preferred_element_type=F32)


def _dot_nt(a, b):
    return lax.dot_general(a.astype(BF16), b.astype(BF16), (((1,), (1,)), ((), ())),
                           preferred_element_type=F32)


def _dot_tn(a, b):
    return lax.dot_general(a.astype(BF16), b.astype(BF16), (((0,), (0,)), ((), ())),
                           preferred_element_type=F32)


def _dot_hi(a, b):
    return jnp.dot(a, b, preferred_element_type=F32, precision=HIGHEST)


def _dot_nt_hi(a, b):
    return lax.dot_general(a, b, (((1,), (1,)), ((), ())), preferred_element_type=F32, precision=HIGHEST)


def _dot_tn_hi(a, b):
    return lax.dot_general(a, b, (((0,), (0,)), ((), ())), preferred_element_type=F32, precision=HIGHEST)


def _sigmoid(x):
    return jax.nn.sigmoid(x)


def _silu(x):
    return x * jax.nn.sigmoid(x)


def _norm_mod(x, g, scale, shift):
    y = x * lax.rsqrt(jnp.mean(x * x, axis=-1, keepdims=True) + EPS) * g
    return y * (1.0 + scale) + shift


def _mod_index(nblk_ctx, n_batch):
    return lambda b, j: (jnp.where(j < nblk_ctx, n_batch, b), 0, 0)


def _mod_kernel(c_ref, w_ref, b_ref, o_ref):
    o_ref[...] = _dot(_silu(c_ref[...]), w_ref[...]) + b_ref[...]


def _modulation(cond, w_mod, b_mod):
    rows = cond.shape[0]
    n = w_mod.shape[1]
    tn = 512
    out = pl.pallas_call(
        _mod_kernel,
        out_shape=jax.ShapeDtypeStruct((rows, n), F32),
        grid=(n // tn,),
        in_specs=[pl.BlockSpec((rows, D_MODEL), lambda i: (0, 0)),
                  pl.BlockSpec((D_MODEL, tn), lambda i: (0, i)),
                  pl.BlockSpec((1, tn), lambda i: (0, i))],
        out_specs=pl.BlockSpec((rows, tn), lambda i: (0, i)),
        compiler_params=_cparams(("arbitrary",)),
        name="adaln_mod",
    )(cond, w_mod, b_mod.reshape(1, n))
    return out.reshape(rows, N_MOD, D_MODEL)


def _inproj_kernel(x_ref, mod_ref, g_ref, w_ref, cs_ref,
                   fcs_ref, rw_ref, dqkv_ref, dg_ref, s5_ref, dab_ref):
    h = _norm_mod(x_ref[0], g_ref[...], mod_ref[0, 1:2, :], mod_ref[0, 0:1, :]).astype(BF16)
    fn = jnp.dot(h, w_ref[:, 0:256], preferred_element_type=F32)
    fcs_ref[0] = _dot(fn, cs_ref[...]).astype(BF16)
    rw_ref[0] = jnp.dot(h, w_ref[:, 256:1152], preferred_element_type=F32)
    dqkv_ref[0] = jnp.dot(h, w_ref[:, 1152:1920], preferred_element_type=F32)
    dg_ref[0] = jnp.dot(h, w_ref[:, 1920:2176], preferred_element_type=F32)
    s5_ref[0] = jnp.dot(h, w_ref[:, 2176:2432], preferred_element_type=F32)
    dab_ref[0] = jnp.dot(h, w_ref[:, 2432:2560], preferred_element_type=F32)


def _input_projection(xcat, mod, g1, w_mix, cs, nblk_ctx):
    nb, t, _ = xcat.shape
    nblk = t // TOKEN_BLOCK
    widths = (512, 896, 768, 256, 256, 128)
    dtypes = (BF16, F32, F32, F32, F32, F32)
    tok = lambda w: pl.BlockSpec((1, TOKEN_BLOCK, w), lambda b, j: (b, j, 0))
    return pl.pallas_call(
        _inproj_kernel,
        out_shape=[jax.ShapeDtypeStruct((nb, t, w), dt) for w, dt in zip(widths, dtypes)],
        grid=(nb, nblk),
        in_specs=[tok(D_MODEL),
                  pl.BlockSpec((1, N_MOD, D_MODEL), _mod_index(nblk_ctx, nb)),
                  pl.BlockSpec((1, D_MODEL), lambda b, j: (0, 0)),
                  pl.BlockSpec(w_mix.shape, lambda b, j: (0, 0)),
                  pl.BlockSpec(cs.shape, lambda b, j: (0, 0))],
        out_specs=[tok(w) for w in widths],
        compiler_params=_cparams(("parallel", "arbitrary"), VMEM_LIMIT),
        name="norm1_inproj",
    )(xcat, mod, g1, w_mix, cs)


def _dft_kernel(n_batch, wc_ref, ws_ref, u_ref, o_ref):
    @pl.when(pl.program_id(1) == 0)
    def _():
        o_ref[...] = jnp.zeros_like(o_ref)

    wc = wc_ref[...]
    ws = ws_ref[...]
    for b in range(n_batch):
        u = u_ref[b]
        o_ref[b] += (jnp.dot(wc, u[:, 0:256], preferred_element_type=F32)
                     + jnp.dot(ws, u[:, 256:512], preferred_element_type=F32))


def _dft_tables(n):
    idx = jnp.arange(n, dtype=jnp.int32)
    ang = ((idx[:, None] * idx[None, :]) % n).astype(F32) * (2.0 * math.pi / n)
    scale = 1.0 / math.sqrt(n * FN_GW)
    return (jnp.cos(ang) * scale).astype(BF16), (-jnp.sin(ang) * scale).astype(BF16)


def _sequence_dft(fcs, row0, n):
    nb = fcs.shape[0]
    wc, ws = _dft_tables(n)
    tm = min(n, 1024)
    tk = min(n, 256)
    assert n % tm == 0 and n % tk == 0 and row0 % tk == 0
    kb0 = row0 // tk
    return pl.pallas_call(
        functools.partial(_dft_kernel, nb),
        out_shape=jax.ShapeDtypeStruct((nb, n, BRANCH_W), F32),
        grid=(n // tm, n // tk),
        in_specs=[pl.BlockSpec((tm, tk), lambda m, k: (m, k)),
                  pl.BlockSpec((tm, tk), lambda m, k: (m, k)),
                  pl.BlockSpec((nb, tk, 512), lambda m, k: (0, k + kb0, 0))],
        out_specs=pl.BlockSpec((nb, tm, BRANCH_W), lambda m, k: (0, m, 0)),
        compiler_params=_cparams(("parallel", "arbitrary"), VMEM_LIMIT),
        name="fourier_seq_dft",
    )(wc, ws, fcs)


def _shifted(x, s, halo_prev, halo_next):
    n = x.shape[0]
    if s == 0:
        return x
    rows = lax.broadcasted_iota(jnp.int32, x.shape, 0)
    if s < 0:
        y = pltpu.roll(x, -s, 0)
        for t in range(-s):
            y = jnp.where(rows == t, halo_prev[HALO + s + t:HALO + s + t + 1, :], y)
    else:
        y = pltpu.roll(x, n - s, 0)
        for t in range(s):
            y = jnp.where(rows == n - s + t, halo_next[t:t + 1, :], y)
    return y


def _prep_kernel(nblk_ctx, nblk, rw_ref, rwp_ref, rwn_ref, dq_ref, dqp_ref, dqn_ref,
                 mu_ref, cw_ref, ps_ref, u_ref):
    j = pl.program_id(1)
    first = jnp.logical_or(j == 0, j == nblk_ctx)
    last = jnp.logical_or(j == nblk_ctx - 1, j == nblk - 1)
    pv = jnp.where(first, 0.0, 1.0).astype(F32)
    nv = jnp.where(last, 0.0, 1.0).astype(F32)

    x = rw_ref[0]
    hp = rwp_ref[0] * pv
    hn = rwn_ref[0] * nv
    prev = _shifted(x, -1, hp, hn)
    nxt = _shifted(x, 1, hp, hn)
    ps_ref[0] = x + mu_ref[0:1, :] * (prev - x) + mu_ref[1:2, :] * (nxt - x)

    q = dq_ref[0]
    hp = dqp_ref[0] * pv
    hn = dqn_ref[0] * nv
    pad = DN_CONV // 2
    acc = cw_ref[pad:pad + 1, :] * q
    for t in range(DN_CONV):
        if t != pad:
            acc = acc + cw_ref[t:t + 1, :] * _shifted(q, t - pad, hp, hn)
    u_ref[0] = _silu(acc)


def _shift_and_conv(rw, dqkv, mu, conv_w, nblk_ctx):
    nb, t, _ = rw.shape
    nblk = t // TOKEN_BLOCK
    per = TOKEN_BLOCK // HALO
    last_halo = t // HALO - 1

    def specs(w):
        return [pl.BlockSpec((1, TOKEN_BLOCK, w), lambda b, j: (b, j, 0)),
                pl.BlockSpec((1, HALO, w), lambda b, j: (b, jnp.maximum(j * per - 1, 0), 0)),
                pl.BlockSpec((1, HALO, w), lambda b, j: (b, jnp.minimum((j + 1) * per, last_halo), 0))]

    return pl.pallas_call(
        functools.partial(_prep_kernel, nblk_ctx, nblk),
        out_shape=[jax.ShapeDtypeStruct(rw.shape, F32), jax.ShapeDtypeStruct(dqkv.shape, F32)],
        grid=(nb, nblk),
        in_specs=specs(RW_COLS) + specs(3 * BRANCH_W)
        + [pl.BlockSpec(mu.shape, lambda b, j: (0, 0)), pl.BlockSpec(conv_w.shape, lambda b, j: (0, 0))],
        out_specs=[pl.BlockSpec((1, TOKEN_BLOCK, RW_COLS), lambda b, j: (b, j, 0)),
                   pl.BlockSpec((1, TOKEN_BLOCK, 3 * BRANCH_W), lambda b, j: (b, j, 0))],
        compiler_params=_cparams(("parallel", "arbitrary")),
        name="shift_conv",
    )(rw, rw, rw, dqkv, dqkv, dqkv, mu, conv_w)


def _chunk_order(nc_ctx, nc):
    def order(d, c):
        back = jnp.where(c < nc_ctx, nc_ctx - 1 - c, nc - 1 + nc_ctx - c)
        return jnp.where(d == 0, c, back)
    return order


PAIR = 2 * HEAD_DIM
N_PAIRS = N_HEADS // 2
CHUNKS_PER_BLOCK = TOKEN_BLOCK // SCAN_CHUNK
MAP_W = 4 * PAIR
_NN = (((1,), (0,)), ((), ()))
_NT = (((1,), (1,)), ((), ()))
_TN = (((0,), (0,)), ((), ()))
_LOG2_CHUNK = int(math.log2(SCAN_CHUNK))


def _split(x):
    hi = x.astype(BF16)
    return hi, (x - hi.astype(F32)).astype(BF16)


def _mm3(a, b, dims=_NN):
    a_hi, a_lo = _split(a)
    b_hi, b_lo = _split(b)
    dg = lambda x, y: lax.dot_general(x, y, dims, preferred_element_type=F32)
    ca, cb = dims[0][0][0], dims[0][1][0]
    if a.shape[ca] <= MXU_DEPTH // 2:
        return (dg(jnp.concatenate([a_hi, a_lo], axis=ca), jnp.concatenate([b_hi, b_hi], axis=cb))
                + dg(a_hi, b_lo))
    return dg(a_hi, b_hi) + dg(a_hi, b_lo) + dg(a_lo, b_hi)


def _split3(x):
    hi = x.astype(BF16)
    rest = x - hi.astype(F32)
    mid = rest.astype(BF16)
    return hi, mid, (rest - mid.astype(F32)).astype(BF16)


def _select_rows(op, x):
    op = op.astype(BF16)
    hi, mid, lo = _split3(x)
    dot = lambda y: jnp.dot(op, y, preferred_element_type=F32)
    return dot(hi) + dot(mid) + dot(lo)


def _spread_cols(x, sel):
    sel = sel.astype(BF16)
    hi, mid, lo = _split3(x)
    assert x.shape[1] <= MXU_DEPTH // 2
    return (jnp.dot(jnp.concatenate([hi, mid], axis=1), jnp.concatenate([sel, sel], axis=0),
                    preferred_element_type=F32)
            + jnp.dot(lo, sel, preferred_element_type=F32))


def _pair_masks(d):
    row = lax.broadcasted_iota(jnp.int32, (PAIR, PAIR), 0)
    col = lax.broadcasted_iota(jnp.int32, (PAIR, PAIR), 1)
    same = jnp.right_shift(row, _LOG2_CHUNK) == jnp.right_shift(col, _LOG2_CHUNK)
    t_row = lax.broadcasted_iota(jnp.int32, (SCAN_CHUNK, PAIR), 0)
    t_col = jnp.bitwise_and(lax.broadcasted_iota(jnp.int32, (SCAN_CHUNK, PAIR), 1), SCAN_CHUNK - 1)
    delta = (t_row - t_col) * (1 - 2 * d)
    return same, delta >= 0, delta > 0, t_row == t_col


def _block_time_operators(d):
    row = lax.broadcasted_iota(jnp.int32, (TOKEN_BLOCK, TOKEN_BLOCK), 0)
    col = lax.broadcasted_iota(jnp.int32, (TOKEN_BLOCK, TOKEN_BLOCK), 1)
    same = jnp.right_shift(row, _LOG2_CHUNK) == jnp.right_shift(col, _LOG2_CHUNK)
    delta = (jnp.bitwise_and(row, SCAN_CHUNK - 1) - jnp.bitwise_and(col, SCAN_CHUNK - 1)) * (1 - 2 * d)
    cum = jnp.where(jnp.logical_and(same, delta >= 0), 1.0, 0.0).astype(F32)
    return jnp.concatenate([cum, jnp.where(same, 1.0, 0.0).astype(F32)], axis=0)


def _chunk_problems():
    return [(slice(SCAN_CHUNK * c, SCAN_CHUNK * (c + 1)), slice(PAIR * pr, PAIR * (pr + 1)))
            for c in range(CHUNKS_PER_BLOCK) for pr in range(N_PAIRS)]


def _bd(x, same):
    return jnp.where(same, jnp.concatenate([x, x], axis=0), jnp.zeros((), x.dtype))


def _pack(x, same):
    x = jnp.where(same, x, 0.0)
    return x[0:HEAD_DIM] + x[HEAD_DIM:PAIR]


def _mm_pk(a, bs, same):
    a_hi, a_lo = _split(a)
    parts = [_split(b) for b in bs]
    r_hi = jnp.concatenate([_bd(hi, same) for hi, _ in parts], axis=1)
    r_lo = jnp.concatenate([_bd(lo, same) for _, lo in parts], axis=1)
    return (jnp.dot(jnp.concatenate([a_hi, a_lo], axis=1), jnp.concatenate([r_hi, r_hi], axis=0),
                    preferred_element_type=F32)
            + jnp.dot(a_hi, r_lo, preferred_element_type=F32))


def _mm_pk_nt(a, bs, same):
    a_hi, a_lo = _split(a)
    parts = [_split(b) for b in bs]
    r_hi = jnp.concatenate([_bd(hi, same) for hi, _ in parts], axis=0)
    r_lo = jnp.concatenate([_bd(lo, same) for _, lo in parts], axis=0)
    dg = lambda x, y: lax.dot_general(x, y, _NT, preferred_element_type=F32)
    return dg(jnp.concatenate([a_hi, a_lo], axis=1), jnp.concatenate([r_hi, r_hi], axis=1)) + dg(a_hi, r_lo)


def _unit_tri_inverse(n_pks, eye_pk, same):
    xs = [eye_pk + n for n in n_pks]
    ps = [_mm_pk(n, [n], same) for n in n_pks]
    for level in range(1, _LOG2_CHUNK):
        if level + 1 < _LOG2_CHUNK:
            xps = [_mm_pk(p, [x, p], same) for x, p in zip(xs, ps)]
            xs = [x + xp[:, 0:PAIR] for x, xp in zip(xs, xps)]
            ps = [xp[:, PAIR:2 * PAIR] for xp in xps]
        else:
            xs = [x + _mm_pk(p, [x], same) for x, p in zip(xs, ps)]
    return xs


def _affine_scan_kernel(n_batch, m_ref, y_ref, h_ref):
    @pl.when(pl.program_id(1) == 0)
    def _():
        h_ref[...] = jnp.zeros_like(h_ref)

    same, _, _, _ = _pair_masks(0)
    chains = [(b, pr) for b in range(n_batch) for pr in range(N_PAIRS)]
    part = lambda b, pr, i: m_ref[0, b, :, MAP_W * pr + PAIR * i:MAP_W * pr + PAIR * (i + 1)]
    outs = [_mm_pk(jnp.concatenate([part(b, pr, 0), part(b, pr, 2)], axis=0), [h_ref[b, pr]], same)
            for b, pr in chains]
    for out, (b, pr) in zip(outs, chains):
        h_ref[b, pr] = out[0:SCAN_CHUNK] + part(b, pr, 1)
        y_ref[0, b, :, PAIR * pr:PAIR * (pr + 1)] = out[SCAN_CHUNK:] + part(b, pr, 3)


def _affine_scan(maps, nc_ctx, name):
    _, nb, t, _ = maps.shape
    nc = t // SCAN_CHUNK
    order = _chunk_order(nc_ctx, nc)
    return pl.pallas_call(
        functools.partial(_affine_scan_kernel, nb),
        out_shape=jax.ShapeDtypeStruct((2, nb, t, BRANCH_W), F32),
        grid=(2, nc),
        in_specs=[pl.BlockSpec((1, nb, SCAN_CHUNK, N_PAIRS * MAP_W), lambda d, c: (d, 0, order(d, c), 0))],
        out_specs=pl.BlockSpec((1, nb, SCAN_CHUNK, BRANCH_W), lambda d, c: (d, 0, order(d, c), 0)),
        scratch_shapes=[pltpu.VMEM((nb, N_PAIRS, HEAD_DIM, PAIR), F32)],
        compiler_params=_cparams(("parallel", "arbitrary")),
        name=name,
    )(maps)


def _rwkv_chunk_kernel(p_ref, w0_ref, wup_ref, a0_ref, aup_ref, kk_ref, ka_ref, ones_ref, o_ref):
    d = pl.program_id(0)
    p = p_ref[0]
    r = p[:, 0:256]
    k = p[:, 256:512]
    v = p[:, 512:768]
    wl = p[:, 768:800]
    al = p[:, 800:832]
    lw = -RW_DECAY_SCALE * _sigmoid(w0_ref[0] + _dot(jnp.tanh(wl), wup_ref[0]))
    a = _sigmoid(a0_ref[0] + _dot(al, aup_ref[0]))
    kkp = k * kk_ref[...]
    kk = kkp * lax.rsqrt(_mm3(kkp * kkp, ones_ref[...]) + EPS)
    kmod = k * (1.0 + (a - 1.0) * ka_ref[...])
    alpha = -(a * kk)

    sums = _select_rows(_block_time_operators(d), lw)
    g = sums[0:TOKEN_BLOCK]
    g_tot = sums[TOKEN_BLOCK:]
    e_neg = jnp.exp(-g)
    e_tail = jnp.exp(g_tot - g)
    gam = jnp.exp(g_tot)
    b_all = kk * jnp.exp(g - lw)
    r_all = r * jnp.exp(g)
    kh_all = kmod * e_neg
    ah_all = alpha * e_neg
    kt_all = kmod * e_tail
    at_all = alpha * e_tail

    same, incl, strict, eye = _pair_masks(d)
    eye_f = jnp.where(eye, 1.0, 0.0).astype(F32)
    zeros = jnp.zeros((SCAN_CHUNK, PAIR), F32)
    probs = _chunk_problems()
    pk = lambda x: [x[rows, lanes] for rows, lanes in probs]
    cat = jnp.concatenate
    b_t, r_t, k_h, a_h, k_t, a_t, v_p = pk(b_all), pk(r_all), pk(kh_all), pk(ah_all), pk(kt_all), pk(at_all), pk(v)
    m = [_mm_pk_nt(cat([b, r_], axis=0), [kh, ah], same) for b, r_, kh, ah in zip(b_t, r_t, k_h, a_h)]
    a_bk = [jnp.where(strict, x[0:SCAN_CHUNK, 0:PAIR], 0.0) for x in m]
    a_ba = [jnp.where(strict, x[0:SCAN_CHUNK, PAIR:], 0.0) for x in m]
    a_rk = [jnp.where(incl, x[SCAN_CHUNK:, 0:PAIR], 0.0) for x in m]
    a_ra = [jnp.where(incl, x[SCAN_CHUNK:, PAIR:], 0.0) for x in m]
    av = [_mm_pk(cat([x, y], axis=0), [vp], same) for x, y, vp in zip(a_bk, a_rk, v_p)]
    t_inv = _unit_tri_inverse(a_ba, eye_f, same)
    sol = [_mm_pk(t, [b, x[0:SCAN_CHUNK]], same) for t, b, x in zip(t_inv, b_t, av)]
    qy = [cat([r_, x[SCAN_CHUNK:]], axis=1) + _mm_pk(a, [s[:, 0:PAIR], s[:, PAIR:]], same)
          for r_, x, a, s in zip(r_t, av, a_ra, sol)]
    kb = [_mm3(cat([at, kt], axis=0), cat([s, cat([zeros, vp], axis=1)], axis=0), _TN)
          for at, kt, s, vp in zip(a_t, k_t, sol, v_p)]
    for (rows, lanes), kb_i, qy_i in zip(probs, kb, qy):
        base = MAP_W * (lanes.start // PAIR)
        o_ref[0, 0, rows, base:base + PAIR] = (_pack(kb_i[:, 0:PAIR], same)
                                               + jnp.where(eye, gam[rows.start:rows.start + 1, lanes], 0.0))
        o_ref[0, 0, rows, base + PAIR:base + 2 * PAIR] = _pack(kb_i[:, PAIR:], same)
        o_ref[0, 0, rows, base + 2 * PAIR:base + 4 * PAIR] = qy_i


def _rwkv_scan(ps, w0, w_up, a0, a_up, k_k, k_a, ones_bd, nc_ctx):
    nb, t, _ = ps.shape
    per_dir = lambda shape: pl.BlockSpec((1,) + shape, lambda d, b, j: (d, 0, 0))
    const = lambda shape: pl.BlockSpec(shape, lambda d, b, j: (0, 0))
    maps = pl.pallas_call(
        _rwkv_chunk_kernel,
        out_shape=jax.ShapeDtypeStruct((2, nb, t, N_PAIRS * MAP_W), F32),
        grid=(2, nb, t // TOKEN_BLOCK),
        in_specs=[pl.BlockSpec((1, TOKEN_BLOCK, RW_COLS), lambda d, b, j: (b, j, 0)),
                  per_dir((1, BRANCH_W)), per_dir((32, BRANCH_W)),
                  per_dir((1, BRANCH_W)), per_dir((32, BRANCH_W)),
                  const((1, BRANCH_W)), const((1, BRANCH_W)), const((BRANCH_W, BRANCH_W))],
        out_specs=pl.BlockSpec((1, 1, TOKEN_BLOCK, N_PAIRS * MAP_W), lambda d, b, j: (d, b, j, 0)),
        compiler_params=_cparams(("parallel", "parallel", "arbitrary"), VMEM_LIMIT),
        name="rwkv7_chunks",
    )(ps, w0.reshape(2, 1, BRANCH_W), w_up, a0.reshape(2, 1, BRANCH_W), a_up,
      k_k.reshape(1, BRANCH_W), k_a.reshape(1, BRANCH_W), ones_bd)
    return _affine_scan(maps, nc_ctx, "rwkv7_state_scan")


def _deltanet_chunk_kernel(u_ref, dab_ref, nea_ref, dtb_ref, ea_ref, eb_ref, ones_ref, o_ref):
    d = pl.program_id(0)
    u = u_ref[0]
    dab = dab_ref[0]
    log_a8 = nea_ref[...] * jax.nn.softplus(dab + dtb_ref[...])
    la = _spread_cols(log_a8, ea_ref[0])
    beta = _spread_cols(_sigmoid(dab), eb_ref[0])
    ones = ones_ref[...]
    q = u[:, 0:256]
    k = u[:, 256:512]
    v = u[:, 512:768]
    q = q * lax.rsqrt(_mm3(q * q, ones) + EPS) * (HEAD_DIM ** -0.5)
    k = k * lax.rsqrt(_mm3(k * k, ones) + EPS)

    sums = _select_rows(_block_time_operators(d), la)
    g = sums[0:TOKEN_BLOCK]
    g_tot = sums[TOKEN_BLOCK:]
    e_g = jnp.exp(g)
    gam = jnp.exp(g_tot)
    kb_all = k * beta
    vb_all = v * beta
    kbe_all = kb_all * e_g
    qe_all = q * e_g
    kt_all = k * jnp.exp(g_tot - g)

    same, incl, strict, eye = _pair_masks(d)
    eye_f = jnp.where(eye, 1.0, 0.0).astype(F32)
    probs = _chunk_problems()
    pk = lambda x: [x[rows, lanes] for rows, lanes in probs]
    cat = jnp.concatenate
    g_p = pk(g)
    g_t = [_pack(_bd(x, same).T, same) for x in g_p]
    decay = [jnp.exp(jnp.where(incl, x - y, -jnp.inf)) for x, y in zip(g_p, g_t)]
    m = [_mm_pk_nt(cat([kb, q_], axis=0), [k_], same) for kb, q_, k_ in zip(pk(kb_all), pk(q), pk(k))]
    a_low = [jnp.where(strict, x[0:SCAN_CHUNK] * dc, 0.0) for x, dc in zip(m, decay)]
    attn = [x[SCAN_CHUNK:] * dc for x, dc in zip(m, decay)]
    t_inv = _unit_tri_inverse([-x for x in a_low], eye_f, same)
    sol = [_mm_pk(t, [vb, kbe], same) for t, vb, kbe in zip(t_inv, pk(vb_all), pk(kbe_all))]
    att_sol = [_mm_pk(a, [s[:, 0:PAIR], s[:, PAIR:]], same) for a, s in zip(attn, sol)]
    ks = [_mm3(kt, s, _TN) for kt, s in zip(pk(kt_all), sol)]
    for (rows, lanes), ks_i, as_i, qe_i in zip(probs, ks, att_sol, pk(qe_all)):
        base = MAP_W * (lanes.start // PAIR)
        o_ref[0, 0, rows, base:base + PAIR] = (jnp.where(eye, gam[rows.start:rows.start + 1, lanes], 0.0)
                                               - _pack(ks_i[:, PAIR:], same))
        o_ref[0, 0, rows, base + PAIR:base + 2 * PAIR] = _pack(ks_i[:, 0:PAIR], same)
        o_ref[0, 0, rows, base + 2 * PAIR:base + 3 * PAIR] = qe_i - as_i[:, PAIR:]
        o_ref[0, 0, rows, base + 3 * PAIR:base + 4 * PAIR] = as_i[:, 0:PAIR]


def _deltanet_scan(u, dab, a_log, dt_bias, ones_bd, nc_ctx):
    nb, t, _ = u.shape
    pad = 128 - 2 * N_HEADS
    neg_exp_a = jnp.pad(-jnp.exp(a_log.reshape(1, -1)), ((0, 0), (0, pad)))
    dtb = jnp.pad(dt_bias.reshape(1, -1), ((0, 0), (0, pad)))
    col = jnp.arange(128)[None, :, None]
    head = (jnp.arange(BRANCH_W) // HEAD_DIM)[None, None, :]
    dirs = jnp.arange(2)[:, None, None]
    expand_a = (col == dirs * N_HEADS + head).astype(F32)
    expand_b = (col == 2 * N_HEADS + dirs * N_HEADS + head).astype(F32)
    const = lambda shape: pl.BlockSpec(shape, lambda d, b, j: (0, 0))
    per_dir = pl.BlockSpec((1, 128, BRANCH_W), lambda d, b, j: (d, 0, 0))
    maps = pl.pallas_call(
        _deltanet_chunk_kernel,
        out_shape=jax.ShapeDtypeStruct((2, nb, t, N_PAIRS * MAP_W), F32),
        grid=(2, nb, t // TOKEN_BLOCK),
        in_specs=[pl.BlockSpec((1, TOKEN_BLOCK, 3 * BRANCH_W), lambda d, b, j: (b, j, 0)),
                  pl.BlockSpec((1, TOKEN_BLOCK, 128), lambda d, b, j: (b, j, 0)),
                  const((1, 128)), const((1, 128)), per_dir, per_dir, const((BRANCH_W, BRANCH_W))],
        out_specs=pl.BlockSpec((1, 1, TOKEN_BLOCK, N_PAIRS * MAP_W), lambda d, b, j: (d, b, j, 0)),
        compiler_params=_cparams(("parallel", "parallel", "arbitrary"), VMEM_LIMIT),
        name="deltanet_chunks",
    )(u, dab, neg_exp_a, dtb, expand_a, expand_b, ones_bd)
    return _affine_scan(maps, nc_ctx, "deltanet_state_scan")


def _s5_kernel(n_batch, nc_ctx, nc, u_ref, toep_ref, winr_ref, wini_ref, wsor_ref, wsoi_ref,
               lr_ref, li_ref, dt_ref, y_ref, injr, inji, xsr, xsi):
    d = pl.program_id(1)
    u = u_ref[0]
    ub = u.astype(BF16)
    injr[...] = jnp.dot(ub, winr_ref[0, 0], preferred_element_type=F32)
    inji[...] = jnp.dot(ub, wini_ref[0, 0], preferred_element_type=F32)
    lam_r = lr_ref[0, 0]
    lam_i = li_ref[0, 0]

    def body(s, carry):
        xr, xi = carry
        back = jnp.where(s < nc_ctx, nc_ctx - 1 - s, nc - 1 + nc_ctx - s)
        row0 = jnp.where(d == 0, s, back) * n_batch
        if n_batch % 8 == 0:
            row0 = pl.multiple_of(row0, 8)
        rows = pl.ds(row0, n_batch)
        xsr[rows, :] = xr
        xsi[rows, :] = xi
        return (lam_r * xr - lam_i * xi + injr[rows, :], lam_r * xi + lam_i * xr + inji[rows, :])

    zero = jnp.zeros((n_batch, S5_STATE), F32)
    lax.fori_loop(0, nc, body, (zero, zero))

    y = (jnp.dot(ub, toep_ref[0, 0], preferred_element_type=F32)
         + _dot(xsr[...], wsor_ref[0, 0]) + _dot(xsi[...], wsoi_ref[0, 0]))

    @pl.when(d == 0)
    def _():
        y_ref[0] = y + u * dt_ref[0]

    @pl.when(d == 1)
    def _():
        y_ref[0] += y


def _s5_tables(lam_re, lam_im, log_step, b_re, b_im, c_re, c_im):
    cs = S5_CHUNK
    lam = lax.complex(lam_re.astype(F32), lam_im.astype(F32))
    step = jnp.exp(log_step.astype(F32))[..., None]
    tau = jnp.arange(cs + 1, dtype=F32)[:, None, None, None]
    lam_pow = jnp.exp(lam[None] * step[None] * tau)
    lam_bar = lam_pow[1]
    b_bar = ((lam_bar - 1.0) / lam)[..., None] * lax.complex(b_re.astype(F32), b_im.astype(F32))
    c_mat = lax.complex(c_re.astype(F32), c_im.astype(F32))
    kern = jnp.real(jnp.einsum("dghp,tdgp,dgpk->tdghk", c_mat, lam_pow[:cs], b_bar))
    i = jnp.arange(cs)
    lag_f = i[None, :] - i[:, None]
    toeps, winr, wini, wsor, wsoi = [], [], [], [], []
    for d in range(2):
        lag = lag_f if d == 0 else -lag_f
        kd = jnp.where((lag >= 0)[:, :, None, None, None], kern[:, d][jnp.clip(lag, 0, cs - 1)], 0.0)
        toeps.append(kd.transpose(2, 0, 4, 1, 3).reshape(S5_GROUPS, cs * S5_GW, cs * S5_GW))
        pw_in = (cs - 1 - i) if d == 0 else i
        e = lam_pow[pw_in, d][..., None] * b_bar[d][None]
        e = e.transpose(1, 0, 3, 2).reshape(S5_GROUPS, cs * S5_GW, S5_STATE)
        winr.append(jnp.real(e))
        wini.append(jnp.imag(e))
        pw_out = (i + 1) if d == 0 else (cs - i)
        m = c_mat[d][None] * lam_pow[pw_out, d][:, :, None, :]
        m = m.transpose(1, 3, 0, 2).reshape(S5_GROUPS, S5_STATE, cs * S5_GW)
        wsor.append(jnp.real(m))
        wsoi.append(-jnp.imag(m))
    stack = lambda xs, dt: jnp.stack(xs).astype(dt)
    lam_c = lam_pow[cs]
    return (stack(toeps, BF16), stack(winr, BF16), stack(wini, BF16), stack(wsor, BF16), stack(wsoi, BF16),
            jnp.real(lam_c)[:, :, None, :], jnp.imag(lam_c)[:, :, None, :])


def _s5_to_chunks(s5, n_ctx):
    nb, t, _ = s5.shape
    cs, g, hw = S5_CHUNK, S5_GROUPS, S5_GW
    n_lat = t - n_ctx
    rows = n_lat // 64
    c = s5[:, :n_ctx].reshape(nb, n_ctx // cs, cs, g, hw)
    c = c.transpose(3, 1, 0, 2, 4).reshape(g, (n_ctx // cs) * nb, cs * hw)
    l = s5[:, n_ctx:].reshape(nb, rows // cs, cs, 64, g, hw)
    l = l.transpose(4, 3, 1, 0, 2, 5).reshape(g, 64 * (rows // cs) * nb, cs * hw)
    return jnp.concatenate([c, l], axis=1)


def _s5_from_chunks(y, nb, n_ctx, n_lat):
    cs, g, hw = S5_CHUNK, S5_GROUPS, S5_GW
    rows = n_lat // 64
    r_ctx = (n_ctx // cs) * nb
    c = y[:, :r_ctx].reshape(g, n_ctx // cs, nb, cs, hw).transpose(2, 1, 3, 0, 4).reshape(nb, n_ctx, g * hw)
    l = y[:, r_ctx:].reshape(g, 64, rows // cs, nb, cs, hw).transpose(3, 2, 4, 1, 0, 5).reshape(nb, n_lat, g * hw)
    return jnp.concatenate([c, l], axis=1)


def _s5_scan(s5, tables, d_skip, n_ctx):
    nb, t, _ = s5.shape
    u = _s5_to_chunks(s5, n_ctx)
    g, r, w = u.shape
    nc = t // S5_CHUNK
    toep, winr, wini, wsor, wsoi, lr, li = tables
    d_tile = jnp.tile(d_skip.astype(F32).reshape(S5_GROUPS, 1, S5_GW), (1, S5_CHUNK, 1)).reshape(g, 1, w)
    per = lambda a, b: pl.BlockSpec((1, 1, a, b), lambda gi, d: (d, gi, 0, 0))
    y = pl.pallas_call(
        functools.partial(_s5_kernel, nb, n_ctx // S5_CHUNK, nc),
        out_shape=jax.ShapeDtypeStruct((g, r, w), F32),
        grid=(g, 2),
        in_specs=[pl.BlockSpec((1, r, w), lambda gi, d: (gi, 0, 0)),
                  per(w, w), per(w, S5_STATE), per(w, S5_STATE), per(S5_STATE, w), per(S5_STATE, w),
                  per(1, S5_STATE), per(1, S5_STATE),
                  pl.BlockSpec((1, 1, w), lambda gi, d: (gi, 0, 0))],
        out_specs=pl.BlockSpec((1, r, w), lambda gi, d: (gi, 0, 0)),
        scratch_shapes=[pltpu.VMEM((r, S5_STATE), F32) for _ in range(4)],
        compiler_params=_cparams(("parallel", "arbitrary"), VMEM_LIMIT),
        name="s5_scan",
    )(u, toep, winr, wini, wsor, wsoi, lr, li, d_tile)
    return _s5_from_chunks(y, nb, n_ctx, t - n_ctx)


def _merge_kernel(x_ref, mod_ref, g1_ref, ya_ref, yb_ref, ps_ref, oc_ref, dg_ref, y5_ref,
                  wg_ref, wb_ref, wo_ref, avg_ref, ones_ref, rk_ref, gup_ref, lng_ref, lnb_ref,
                  dng_ref, wglu_ref, bglu_ref, o_ref):
    x = x_ref[0]
    h = _norm_mod(x, g1_ref[...], mod_ref[0, 1:2, :], mod_ref[0, 0:1, :]).astype(BF16)
    avg = avg_ref[...]

    ps = ps_ref[0]
    r = ps[:, 0:256]
    k = ps[:, 256:512]
    v = ps[:, 512:768]
    gl = ps[:, 832:896]
    y = yb_ref[0, 0] + yb_ref[1, 0]
    dev = y - _dot(y, avg)
    yn = dev * lax.rsqrt(_dot(dev * dev, avg) + RW_GN_EPS) * lng_ref[...] + lnb_ref[...]
    bonus = _dot(r * k * rk_ref[...], ones_ref[...]) * v
    yb = (yn + bonus) * _dot(_sigmoid(gl), gup_ref[...])

    o = oc_ref[0, 0] + oc_ref[1, 0]
    yc = o * lax.rsqrt(_dot(o * o, avg) + EPS) * dng_ref[...] * _silu(dg_ref[0])

    z = jax.nn.gelu(y5_ref[0])
    yd = z * _sigmoid(_dot(z, wglu_ref[...]) + bglu_ref[...])

    m = jnp.zeros((TOKEN_BLOCK, D_MODEL), F32)
    for i, yi in enumerate((ya_ref[0], yb, yc, yd)):
        gate = _sigmoid(jnp.dot(h, wg_ref[:, D_MODEL * i:D_MODEL * (i + 1)], preferred_element_type=F32))
        m = m + gate * _dot(yi, wb_ref[i])
    o_ref[0] = x + mod_ref[0, 2:3, :] * _dot(m, wo_ref[...])


def _merge(xcat, mod, g1, ya, yb, ps, oc, dg, y5, w_gate, w_branch, w_out, avg_bd, ones_bd,
           r_k, g_up, ln_g, ln_b, dn_g, w_glu, b_glu, nblk_ctx):
    nb, t, _ = xcat.shape
    nblk = t // TOKEN_BLOCK
    tok = lambda w: pl.BlockSpec((1, TOKEN_BLOCK, w), lambda b, j: (b, j, 0))
    tok2 = lambda w: pl.BlockSpec((2, 1, TOKEN_BLOCK, w), lambda b, j: (0, b, j, 0))
    full = lambda a: pl.BlockSpec(a.shape, lambda b, j: (0,) * a.ndim)
    consts = (w_gate, w_branch, w_out, avg_bd, ones_bd, r_k, g_up, ln_g, ln_b, dn_g, w_glu, b_glu)
    return pl.pallas_call(
        _merge_kernel,
        out_shape=jax.ShapeDtypeStruct(xcat.shape, F32),
        grid=(nb, nblk),
        in_specs=[tok(D_MODEL), pl.BlockSpec((1, N_MOD, D_MODEL), _mod_index(nblk_ctx, nb)),
                  pl.BlockSpec((1, D_MODEL), lambda b, j: (0, 0)),
                  tok(BRANCH_W), tok2(BRANCH_W), tok(RW_COLS), tok2(BRANCH_W), tok(BRANCH_W), tok(BRANCH_W)]
        + [full(a) for a in consts],
        out_specs=tok(D_MODEL),
        compiler_params=_cparams(("parallel", "arbitrary"), VMEM_LIMIT),
        name="merge_branches",
    )(xcat, mod, g1, ya, yb, ps, oc, dg, y5, *consts)


def _route(sel, score):
    s = [sel[e:e + 1, :] for e in range(N_EXPERTS)]
    sc = [score[e:e + 1, :] for e in range(N_EXPERTS)]
    n_groups = N_EXPERTS // EXPERTS_PER_GROUP
    group_score = []
    for g in range(n_groups):
        m = s[EXPERTS_PER_GROUP * g:EXPERTS_PER_GROUP * (g + 1)]
        best = None
        for i in range(EXPERTS_PER_GROUP):
            for j in range(i + 1, EXPERTS_PER_GROUP):
                pair = m[i] + m[j]
                best = pair if best is None else jnp.maximum(best, pair)
        group_score.append(best)
    best_g = jnp.zeros(group_score[0].shape, jnp.int32)
    best_v = group_score[0]
    for g in range(1, n_groups):
        upd = group_score[g] > best_v
        best_g = jnp.where(upd, g, best_g)
        best_v = jnp.where(upd, group_score[g], best_v)
    chosen = []
    den = jnp.zeros_like(best_v)
    for e in range(N_EXPERTS):
        g = e // EXPERTS_PER_GROUP
        rank = jnp.zeros(best_g.shape, jnp.int32)
        for j in range(EXPERTS_PER_GROUP * g, EXPERTS_PER_GROUP * (g + 1)):
            if j == e:
                continue
            ahead = (s[j] > s[e]) if j > e else (s[j] >= s[e])
            rank = rank + jnp.where(ahead, 1, 0)
        pick = jnp.logical_and(best_g == g, rank < 2)
        chosen.append(pick)
        den = den + jnp.where(pick, sc[e], 0.0)
    return jnp.concatenate([jnp.where(chosen[e], sc[e] / den, 0.0) for e in range(N_EXPERTS)], axis=0)


def _moe_kernel(x_ref, mod_ref, g2_ref, rwt_ref, rb_ref, w1_ref, w3_ref, w2_ref, exp_ref, o_ref):
    x = x_ref[0]
    h = _norm_mod(x, g2_ref[...], mod_ref[0, 4:5, :], mod_ref[0, 3:4, :])
    logits = lax.dot_general(rwt_ref[...], h, (((1,), (1,)), ((), ())),
                             preferred_element_type=F32, precision=HIGHEST)
    score = _sigmoid(logits)
    comb = _route(score + rb_ref[...], score)
    hb = h.astype(BF16)
    combb = comb.astype(BF16)
    acc = jnp.zeros((TOKEN_BLOCK, D_MODEL), F32)
    width = 4 * D_EXPERT
    for q in range(N_EXPERTS * D_EXPERT // width):
        cols = slice(width * q, width * (q + 1))
        cw = lax.dot_general(combb, exp_ref[:, cols], (((0,), (0,)), ((), ())),
                             preferred_element_type=F32)
        a1 = jnp.dot(hb, w1_ref[:, cols], preferred_element_type=F32)
        a3 = jnp.dot(hb, w3_ref[:, cols], preferred_element_type=F32)
        act = (_silu(a1) * a3 * cw).astype(BF16)
        acc = acc + jnp.dot(act, w2_ref[cols, :], preferred_element_type=F32)
    o_ref[0] = x + mod_ref[0, 5:6, :] * acc


def _moe(x1, mod, g2, router_wt, router_b, w1, w3, w2, expand, nblk_ctx):
    nb, t, _ = x1.shape
    nblk = t // TOKEN_BLOCK
    tok = pl.BlockSpec((1, TOKEN_BLOCK, D_MODEL), lambda b, j: (b, j, 0))
    full = lambda a: pl.BlockSpec(a.shape, lambda b, j: (0,) * a.ndim, pipeline_mode=pl.Buffered(1))
    consts = (router_wt, router_b, w1, w3, w2, expand)
    return pl.pallas_call(
        _moe_kernel,
        out_shape=jax.ShapeDtypeStruct(x1.shape, F32),
        grid=(nb, nblk),
        in_specs=[tok, pl.BlockSpec((1, N_MOD, D_MODEL), _mod_index(nblk_ctx, nb)),
                  pl.BlockSpec((1, D_MODEL), lambda b, j: (0, 0))] + [full(a) for a in consts],
        out_specs=tok,
        compiler_params=_cparams(("parallel", "arbitrary"), VMEM_LIMIT),
        name="moe_ffn",
    )(x1, mod, g2, *consts)


def _final_norm_kernel(x_ref, g_ref, o_ref):
    x = x_ref[0]
    o_ref[0] = x * lax.rsqrt(jnp.mean(x * x, axis=-1, keepdims=True) + EPS) * g_ref[...]


def _final_norm(xcat, g, n_ctx):
    nb, t, _ = xcat.shape
    n_lat = t - n_ctx
    off = n_ctx // TOKEN_BLOCK
    return pl.pallas_call(
        _final_norm_kernel,
        out_shape=jax.ShapeDtypeStruct((nb, n_lat, D_MODEL), F32),
        grid=(nb, n_lat // TOKEN_BLOCK),
        in_specs=[pl.BlockSpec((1, TOKEN_BLOCK, D_MODEL), lambda b, j: (b, j + off, 0)),
                  pl.BlockSpec((1, D_MODEL), lambda b, j: (0, 0))],
        out_specs=pl.BlockSpec((1, TOKEN_BLOCK, D_MODEL), lambda b, j: (b, j, 0)),
        compiler_params=_cparams(("parallel", "arbitrary")),
        name="final_norm",
    )(xcat, g.reshape(1, D_MODEL))


def _block_diag_ones(n_blocks, size):
    return jnp.kron(jnp.eye(n_blocks, dtype=F32), jnp.ones((size, size), F32))


def _layer(xcat, cond, n_ctx, lp, router_wt, router_b, consts):
    nb, t, _ = xcat.shape
    nblk_ctx = n_ctx // TOKEN_BLOCK
    nc_ctx = n_ctx // SCAN_CHUNK
    mod = _modulation(cond, lp["w_mod"], lp["b_mod"])
    g1 = lp["norm1_g"].reshape(1, D_MODEL)

    offs = [0]
    for s in W_IN_SPLITS:
        offs.append(offs[-1] + s)
    w_in = lp["w_in"]
    seg = lambda i: w_in[:, offs[i]:offs[i + 1]]
    w_mix = jnp.concatenate([seg(0), seg(1), seg(2), seg(3), seg(6), seg(4), seg(5),
                             jnp.zeros((D_MODEL, 128 - 16), F32)], axis=1).astype(BF16)
    w_gate = seg(7).astype(BF16)

    fcs, rw, dqkv, dg, s5, dab = _input_projection(xcat, mod, g1, w_mix, consts["cs"], nblk_ctx)

    ya = jnp.concatenate([_sequence_dft(fcs, 0, n_ctx), _sequence_dft(fcs, n_ctx, t - n_ctx)], axis=1)

    ps, u = _shift_and_conv(rw, dqkv, lp["rw_mu"], lp["dn_conv"], nblk_ctx)
    yb = _rwkv_scan(ps, lp["rw_w0"], lp["rw_w_up"], lp["rw_a0"], lp["rw_a_up"], lp["rw_k_k"], lp["rw_k_a"],
                    consts["ones_bd"], nc_ctx)
    oc = _deltanet_scan(u, dab, lp["dn_a_log"], lp["dn_dt_bias"], consts["ones_bd"], nc_ctx)
    tables = _s5_tables(lp["s5_lam_re"], lp["s5_lam_im"], lp["s5_log_step"], lp["s5_b_re"], lp["s5_b_im"],
                        lp["s5_c_re"], lp["s5_c_im"])
    y5 = _s5_scan(s5, tables, lp["s5_d"], n_ctx)

    row = lambda a: a.reshape(1, -1).astype(F32)
    x1 = _merge(xcat, mod, g1, ya, yb, ps, oc, dg, y5, w_gate, lp["w_branch"].astype(BF16),
                lp["w_out"].astype(BF16), consts["avg_bd"], consts["ones_bd"], row(lp["rw_r_k"]),
                lp["rw_g_up"].astype(BF16), row(lp["rw_ln_g"]), row(lp["rw_ln_b"]),
                row(jnp.tile(lp["dn_norm_g"], N_HEADS)), lp["s5_w_glu"].astype(BF16), row(lp["s5_b_glu"]),
                nblk_ctx)

    w1 = lp["moe_w1"].transpose(1, 0, 2).reshape(D_MODEL, N_EXPERTS * D_EXPERT).astype(BF16)
    w3 = lp["moe_w3"].transpose(1, 0, 2).reshape(D_MODEL, N_EXPERTS * D_EXPERT).astype(BF16)
    w2 = lp["moe_w2"].reshape(N_EXPERTS * D_EXPERT, D_MODEL).astype(BF16)
    return _moe(x1, mod, lp["norm2_g"].reshape(1, D_MODEL), router_wt, router_b, w1, w3, w2,
                consts["expand"], nblk_ctx)


def kernel(x, c, ctx, c_ctx, w_mod, b_mod, norm1_g, norm2_g, w_in, rw_mu, rw_w0, rw_w_up, rw_a0, rw_a_up, rw_k_k, rw_k_a, rw_r_k, rw_g_up, rw_ln_g, rw_ln_b, dn_conv, dn_a_log, dn_dt_bias, dn_norm_g, s5_lam_re, s5_lam_im, s5_log_step, s5_b_re, s5_b_im, s5_c_re, s5_c_im, s5_d, s5_w_glu, s5_b_glu, w_branch, w_out, router_w, router_b, moe_w1, moe_w3, moe_w2, final_g):
    nb, n_lat, _ = x.shape
    n_ctx = ctx.shape[1]
    depth = w_mod.shape[0]
    assert n_ctx % TOKEN_BLOCK == 0 and n_lat % TOKEN_BLOCK == 0 and (n_lat // 64) % S5_CHUNK == 0

    xcat = jnp.concatenate([ctx, x], axis=1).astype(F32)
    cond_rows = -(-(nb + 1) // 8) * 8
    cond = jnp.zeros((cond_rows, D_MODEL), F32).at[:nb].set(c).at[nb].set(c_ctx)

    j = jnp.arange(FN_GW, dtype=jnp.int32)
    ang = ((j[:, None] * j[None, :]) % FN_GW).astype(F32) * (2.0 * math.pi / FN_GW)
    eye = jnp.eye(BRANCH_W // FN_GW, dtype=F32)
    consts = {
        "cs": jnp.concatenate([jnp.kron(eye, jnp.cos(ang)), jnp.kron(eye, jnp.sin(ang))], axis=1).astype(BF16),
        "ones_bd": _block_diag_ones(N_HEADS, HEAD_DIM).astype(BF16),
        "avg_bd": (_block_diag_ones(N_HEADS, HEAD_DIM) / HEAD_DIM).astype(BF16),
        "expand": jnp.kron(jnp.eye(N_EXPERTS, dtype=F32), jnp.ones((1, D_EXPERT), F32)).astype(BF16),
    }
    router_wt = router_w.T.astype(F32)
    router_bc = router_b.reshape(N_EXPERTS, 1).astype(F32)

    names = ("w_mod", "b_mod", "norm1_g", "norm2_g", "w_in", "rw_mu", "rw_w0", "rw_w_up", "rw_a0", "rw_a_up",
             "rw_k_k", "rw_k_a", "rw_r_k", "rw_g_up", "rw_ln_g", "rw_ln_b", "dn_conv", "dn_a_log", "dn_dt_bias",
             "dn_norm_g", "s5_lam_re", "s5_lam_im", "s5_log_step", "s5_b_re", "s5_b_im", "s5_c_re", "s5_c_im",
             "s5_d", "s5_w_glu", "s5_b_glu", "w_branch", "w_out", "moe_w1", "moe_w3", "moe_w2")
    stacked = (w_mod, b_mod, norm1_g, norm2_g, w_in, rw_mu, rw_w0, rw_w_up, rw_a0, rw_a_up, rw_k_k, rw_k_a,
               rw_r_k, rw_g_up, rw_ln_g, rw_ln_b, dn_conv, dn_a_log, dn_dt_bias, dn_norm_g, s5_lam_re,
               s5_lam_im, s5_log_step, s5_b_re, s5_b_im, s5_c_re, s5_c_im, s5_d, s5_w_glu, s5_b_glu,
               w_branch, w_out, moe_w1, moe_w3, moe_w2)
    for i in range(depth):
        lp = {n: a[i] for n, a in zip(names, stacked)}
        xcat = _layer(xcat, cond, n_ctx, lp, router_wt, router_bc, consts)
    return _final_norm(xcat, final_g, n_ctx)
```

```python
import functools
import math

import jax
import jax.numpy as jnp
from jax import lax
from jax.experimental import pallas as pl
from jax.experimental.pallas import tpu as pltpu

F32 = jnp.float32
BF16 = jnp.bfloat16
HIGHEST = lax.Precision.HIGHEST

D_MODEL = 1024
N_MOD = 6
EPS = 1e-6
N_BRANCH = 4
BRANCH_W = 256
HEAD_DIM = 64
N_HEADS = 4
FN_GW = 64
RW_COLS = 896
RW_DECAY_SCALE = math.exp(-0.5)
RW_GN_EPS = 64e-5
DN_CONV = 5
S5_GW = 16
S5_GROUPS = 16
S5_STATE = 64
N_EXPERTS = 16
EXPERTS_PER_GROUP = 4
D_EXPERT = 256
W_IN_SPLITS = (256, 896, 768, 256, 8, 8, 256, 4096)

TOKEN_BLOCK = 256
SCAN_CHUNK = 64
S5_CHUNK = 16
HALO = 8
MXU_DEPTH = 256
VMEM_LIMIT = 56 * 1024 * 1024


def _cparams(sem, vmem=None, **kw):
    return pltpu.CompilerParams(dimension_semantics=sem, vmem_limit_bytes=vmem, **kw)


def _dot(a, b):
    return jnp.dot(a.astype(BF16), b.astype(BF16), preferred_element_type=F32)


def _dot_nt(a, b):
    return lax.dot_general(a.astype(BF16), b.astype(BF16), (((1,), (1,)), ((), ())),
                           preferred_element_type=F32)


def _dot_tn(a, b):
    return lax.dot_general(a.astype(BF16), b.astype(BF16), (((0,), (0,)), ((), ())),
                           preferred_element_type=F32)


def _dot_hi(a, b):
    return jnp.dot(a, b, preferred_element_type=F32, precision=HIGHEST)


def _dot_nt_hi(a, b):
    return lax.dot_general(a, b, (((1,), (1,)), ((), ())), preferred_element_type=F32, precision=HIGHEST)


def _dot_tn_hi(a, b):
    return lax.dot_general(a, b, (((0,), (0,)), ((), ())), preferred_element_type=F32, precision=HIGHEST)


def _sigmoid(x):
    return jax.nn.sigmoid(x)


def _silu(x):
    return x * jax.nn.sigmoid(x)


def _norm_mod(x, g, scale, shift):
    y = x * lax.rsqrt(jnp.mean(x * x, axis=-1, keepdims=True) + EPS) * g
    return y * (1.0 + scale) + shift


def _mod_index(nblk_ctx, n_batch, per_step=1):
    return lambda b, j: (jnp.where(j < nblk_ctx, n_batch // per_step, b), 0, 0)


def _mod_kernel(c_ref, w_ref, b_ref, o_ref):
    o_ref[...] = _dot(_silu(c_ref[...]), w_ref[...]) + b_ref[...]


def _modulation(cond, w_mod, b_mod):
    rows = cond.shape[0]
    n = w_mod.shape[1]
    tn = 512
    out = pl.pallas_call(
        _mod_kernel,
        out_shape=jax.ShapeDtypeStruct((rows, n), F32),
        grid=(n // tn,),
        in_specs=[pl.BlockSpec((rows, D_MODEL), lambda i: (0, 0)),
                  pl.BlockSpec((D_MODEL, tn), lambda i: (0, i)),
                  pl.BlockSpec((1, tn), lambda i: (0, i))],
        out_specs=pl.BlockSpec((rows, tn), lambda i: (0, i)),
        compiler_params=_cparams(("arbitrary",)),
        name="adaln_mod",
    )(cond, w_mod, b_mod.reshape(1, n))
    return out.reshape(rows, N_MOD, D_MODEL)


def _shifted(x, s, halo_prev, halo_next):
    n = x.shape[0]
    if s == 0:
        return x
    rows = lax.broadcasted_iota(jnp.int32, x.shape, 0)
    if s < 0:
        y = pltpu.roll(x, -s, 0)
        for t in range(-s):
            y = jnp.where(rows == t, halo_prev[HALO + s + t:HALO + s + t + 1, :], y)
    else:
        y = pltpu.roll(x, n - s, 0)
        for t in range(s):
            y = jnp.where(rows == n - s + t, halo_next[t:t + 1, :], y)
    return y


def _inproj_kernel(nblk_ctx, nblk, x_ref, xp_ref, xn_ref, mod_ref, g_ref, w_ref, cs_ref, mu_ref, cw_ref,
                   fcs_ref, ps_ref, u_ref, dg_ref, s5_ref, dab_ref):
    j = pl.program_id(1)
    first = jnp.logical_or(j == 0, j == nblk_ctx)
    last = jnp.logical_or(j == nblk_ctx - 1, j == nblk - 1)
    pv = jnp.where(first, 0.0, 1.0).astype(F32)
    nv = jnp.where(last, 0.0, 1.0).astype(F32)

    x_all = jnp.concatenate([xp_ref[0], x_ref[0], xn_ref[0]], axis=0)
    h_all = _norm_mod(x_all, g_ref[...], mod_ref[0, 1:2, :], mod_ref[0, 0:1, :])
    h = h_all[HALO:HALO + TOKEN_BLOCK].astype(BF16)
    h_all = h_all.astype(BF16)

    fn = jnp.dot(h, w_ref[:, 0:256], preferred_element_type=F32)
    fcs_ref[0] = _dot(fn, cs_ref[...]).astype(BF16)
    dg_ref[0] = jnp.dot(h, w_ref[:, 1920:2176], preferred_element_type=F32)
    s5_ref[0] = jnp.dot(h, w_ref[:, 2176:2432], preferred_element_type=F32)
    dab_ref[0] = jnp.dot(h, w_ref[:, 2432:2560], preferred_element_type=F32)

    def with_halo(cols):
        p = jnp.dot(h_all, w_ref[:, cols], preferred_element_type=F32)
        return p[HALO:HALO + TOKEN_BLOCK], p[0:HALO] * pv, p[HALO + TOKEN_BLOCK:] * nv

    p, hp, hn = with_halo(slice(256, 1152))
    ps_ref[0] = p + mu_ref[0:1, :] * (_shifted(p, -1, hp, hn) - p) + mu_ref[1:2, :] * (_shifted(p, 1, hp, hn) - p)

    q, hp, hn = with_halo(slice(1152, 1920))
    pad = DN_CONV // 2
    acc = cw_ref[pad:pad + 1, :] * q
    for t in range(DN_CONV):
        if t != pad:
            acc = acc + cw_ref[t:t + 1, :] * _shifted(q, t - pad, hp, hn)
    u_ref[0] = _silu(acc)


def _input_projection(xcat, mod, g1, w_mix, cs, mu, conv_w, nblk_ctx):
    nb, t, _ = xcat.shape
    nblk = t // TOKEN_BLOCK
    per = TOKEN_BLOCK // HALO
    last_halo = t // HALO - 1
    widths = (512, RW_COLS, 3 * BRANCH_W, 256, 256, 128)
    dtypes = (BF16, F32, F32, F32, F32, F32)
    tok = lambda w: pl.BlockSpec((1, TOKEN_BLOCK, w), lambda b, j: (b, j, 0))
    full = lambda a: pl.BlockSpec(a.shape, lambda b, j: (0,) * a.ndim)
    return pl.pallas_call(
        functools.partial(_inproj_kernel, nblk_ctx, nblk),
        out_shape=[jax.ShapeDtypeStruct((nb, t, w), dt) for w, dt in zip(widths, dtypes)],
        grid=(nb, nblk),
        in_specs=[tok(D_MODEL),
                  pl.BlockSpec((1, HALO, D_MODEL), lambda b, j: (b, jnp.maximum(j * per - 1, 0), 0)),
                  pl.BlockSpec((1, HALO, D_MODEL), lambda b, j: (b, jnp.minimum((j + 1) * per, last_halo), 0)),
                  pl.BlockSpec((1, N_MOD, D_MODEL), _mod_index(nblk_ctx, nb)),
                  pl.BlockSpec((1, D_MODEL), lambda b, j: (0, 0)),
                  full(w_mix), full(cs), full(mu), full(conv_w)],
        out_specs=[tok(w) for w in widths],
        compiler_params=_cparams(("parallel", "arbitrary"), VMEM_LIMIT),
        name="norm1_inproj",
    )(xcat, xcat, xcat, mod, g1, w_mix, cs, mu, conv_w)


def _dft_kernel(n_batch, wc_ref, ws_ref, u_ref, o_ref):
    @pl.when(pl.program_id(1) == 0)
    def _():
        o_ref[...] = jnp.zeros_like(o_ref)

    wc = wc_ref[...]
    ws = ws_ref[...]
    for b in range(n_batch):
        u = u_ref[b]
        o_ref[b] += (jnp.dot(wc, u[:, 0:256], preferred_element_type=F32)
                     + jnp.dot(ws, u[:, 256:512], preferred_element_type=F32))


def _dft_tables(n):
    n2 = math.isqrt(n)
    assert n2 * n2 == n
    k = jnp.arange(n, dtype=jnp.int32)[None, :]
    j = jnp.arange(n2, dtype=jnp.int32)[:, None]
    ang_a = ((j * k) % n2).astype(F32) * (2.0 * math.pi / n2)
    ang_b = ((j * k) % n).astype(F32) * (2.0 * math.pi / n)
    ca, sa, cb, sb = jnp.cos(ang_a)[:, None], jnp.sin(ang_a)[:, None], jnp.cos(ang_b)[None], jnp.sin(ang_b)[None]
    scale = 1.0 / math.sqrt(n * FN_GW)
    cos = ((ca * cb - sa * sb) * scale).reshape(n, n)
    sin = ((sa * cb + ca * sb) * scale).reshape(n, n)
    return cos.astype(BF16), (-sin).astype(BF16)


def _sequence_dft(fcs, row0, n):
    nb = fcs.shape[0]
    wc, ws = _dft_tables(n)
    tm = min(n, 1024)
    tk = min(n, 256)
    assert n % tm == 0 and n % tk == 0 and row0 % tk == 0
    kb0 = row0 // tk
    return pl.pallas_call(
        functools.partial(_dft_kernel, nb),
        out_shape=jax.ShapeDtypeStruct((nb, n, BRANCH_W), F32),
        grid=(n // tm, n // tk),
        in_specs=[pl.BlockSpec((tm, tk), lambda m, k: (m, k)),
                  pl.BlockSpec((tm, tk), lambda m, k: (m, k)),
                  pl.BlockSpec((nb, tk, 512), lambda m, k: (0, k + kb0, 0))],
        out_specs=pl.BlockSpec((nb, tm, BRANCH_W), lambda m, k: (0, m, 0)),
        compiler_params=_cparams(("parallel", "arbitrary"), VMEM_LIMIT),
        name="fourier_seq_dft",
    )(wc, ws, fcs)


def _chunk_order(nc_ctx, nc):
    def order(d, c):
        back = jnp.where(c < nc_ctx, nc_ctx - 1 - c, nc - 1 + nc_ctx - c)
        return jnp.where(d == 0, c, back)
    return order


PAIR = 2 * HEAD_DIM
N_PAIRS = N_HEADS // 2
CHUNKS_PER_BLOCK = TOKEN_BLOCK // SCAN_CHUNK
MAP_W = 4 * PAIR
CHUNK_BATCH = 2
MOE_BATCH = 2
_NN = (((1,), (0,)), ((), ()))
_NT = (((1,), (1,)), ((), ()))
_TN = (((0,), (0,)), ((), ()))
_LOG2_CHUNK = int(math.log2(SCAN_CHUNK))


def _split(x):
    hi = x.astype(BF16)
    return hi, (x - hi.astype(F32)).astype(BF16)


def _mm3(a, b, dims=_NN):
    a_hi, a_lo = _split(a)
    b_hi, b_lo = _split(b)
    dg = lambda x, y: lax.dot_general(x, y, dims, preferred_element_type=F32)
    ca, cb = dims[0][0][0], dims[0][1][0]
    if a.shape[ca] <= MXU_DEPTH // 2:
        return (dg(jnp.concatenate([a_hi, a_lo], axis=ca), jnp.concatenate([b_hi, b_hi], axis=cb))
                + dg(a_hi, b_lo))
    return dg(a_hi, b_hi) + dg(a_hi, b_lo) + dg(a_lo, b_hi)


def _split3(x):
    hi = x.astype(BF16)
    rest = x - hi.astype(F32)
    mid = rest.astype(BF16)
    return hi, mid, (rest - mid.astype(F32)).astype(BF16)


def _select_rows(op, x):
    op = op.astype(BF16)
    hi, mid, lo = _split3(x)
    dot = lambda y: jnp.dot(op, y, preferred_element_type=F32)
    return dot(hi) + dot(mid) + dot(lo)


def _spread_cols(x, sel):
    sel = sel.astype(BF16)
    hi, mid, lo = _split3(x)
    assert x.shape[1] <= MXU_DEPTH // 2
    return (jnp.dot(jnp.concatenate([hi, mid], axis=1), jnp.concatenate([sel, sel], axis=0),
                    preferred_element_type=F32)
            + jnp.dot(lo, sel, preferred_element_type=F32))


def _pair_masks(d):
    row = lax.broadcasted_iota(jnp.int32, (PAIR, PAIR), 0)
    col = lax.broadcasted_iota(jnp.int32, (PAIR, PAIR), 1)
    same = jnp.right_shift(row, _LOG2_CHUNK) == jnp.right_shift(col, _LOG2_CHUNK)
    t_row = lax.broadcasted_iota(jnp.int32, (SCAN_CHUNK, PAIR), 0)
    t_col = jnp.bitwise_and(lax.broadcasted_iota(jnp.int32, (SCAN_CHUNK, PAIR), 1), SCAN_CHUNK - 1)
    delta = (t_row - t_col) * (1 - 2 * d)
    return same, delta >= 0, delta > 0, t_row == t_col


def _block_time_operators(d):
    row = lax.broadcasted_iota(jnp.int32, (TOKEN_BLOCK, TOKEN_BLOCK), 0)
    col = lax.broadcasted_iota(jnp.int32, (TOKEN_BLOCK, TOKEN_BLOCK), 1)
    same = jnp.right_shift(row, _LOG2_CHUNK) == jnp.right_shift(col, _LOG2_CHUNK)
    delta = (jnp.bitwise_and(row, SCAN_CHUNK - 1) - jnp.bitwise_and(col, SCAN_CHUNK - 1)) * (1 - 2 * d)
    cum = jnp.where(jnp.logical_and(same, delta >= 0), 1.0, 0.0).astype(F32)
    return jnp.concatenate([cum, jnp.where(same, 1.0, 0.0).astype(F32)], axis=0)


def _chunk_problems():
    return [(i, slice(SCAN_CHUNK * c, SCAN_CHUNK * (c + 1)), slice(PAIR * pr, PAIR * (pr + 1)))
            for i in range(CHUNK_BATCH) for c in range(CHUNKS_PER_BLOCK) for pr in range(N_PAIRS)]


def _bd(x, same):
    return jnp.where(same, jnp.concatenate([x, x], axis=0), jnp.zeros((), x.dtype))


def _pack(x, same):
    x = jnp.where(same, x, 0.0)
    return x[0:HEAD_DIM] + x[HEAD_DIM:PAIR]


def _mm_pk(a, bs, same):
    a_hi, a_lo = _split(a)
    parts = [_split(b) for b in bs]
    r_hi = jnp.concatenate([_bd(hi, same) for hi, _ in parts], axis=1)
    r_lo = jnp.concatenate([_bd(lo, same) for _, lo in parts], axis=1)
    return (jnp.dot(jnp.concatenate([a_hi, a_lo], axis=1), jnp.concatenate([r_hi, r_hi], axis=0),
                    preferred_element_type=F32)
            + jnp.dot(a_hi, r_lo, preferred_element_type=F32))


def _mm_pk_nt(a, bs, same):
    a_hi, a_lo = _split(a)
    parts = [_split(b) for b in bs]
    r_hi = jnp.concatenate([_bd(hi, same) for hi, _ in parts], axis=0)
    r_lo = jnp.concatenate([_bd(lo, same) for _, lo in parts], axis=0)
    dg = lambda x, y: lax.dot_general(x, y, _NT, preferred_element_type=F32)
    return dg(jnp.concatenate([a_hi, a_lo], axis=1), jnp.concatenate([r_hi, r_hi], axis=1)) + dg(a_hi, r_lo)


def _unit_tri_inverse(n_pks, eye_pk, same):
    xs = [eye_pk + n for n in n_pks]
    ps = [_mm_pk(n, [n], same) for n in n_pks]
    for level in range(1, _LOG2_CHUNK):
        if level + 1 < _LOG2_CHUNK:
            xps = [_mm_pk(p, [x, p], same) for x, p in zip(xs, ps)]
            xs = [x + xp[:, 0:PAIR] for x, xp in zip(xs, xps)]
            ps = [xp[:, PAIR:2 * PAIR] for xp in xps]
        else:
            xs = [x + _mm_pk(p, [x], same) for x, p in zip(xs, ps)]
    return xs


def _affine_scan_kernel(n_batch, m_ref, y_ref, h_ref):
    @pl.when(pl.program_id(1) == 0)
    def _():
        h_ref[...] = jnp.zeros_like(h_ref)

    same, _, _, _ = _pair_masks(0)
    chains = [(b, pr) for b in range(n_batch) for pr in range(N_PAIRS)]
    part = lambda b, pr, i: m_ref[0, b, :, MAP_W * pr + PAIR * i:MAP_W * pr + PAIR * (i + 1)]
    outs = [_mm_pk(jnp.concatenate([part(b, pr, 0), part(b, pr, 2)], axis=0), [h_ref[b, pr]], same)
            for b, pr in chains]
    for out, (b, pr) in zip(outs, chains):
        h_ref[b, pr] = out[0:SCAN_CHUNK] + part(b, pr, 1)
        y_ref[0, b, :, PAIR * pr:PAIR * (pr + 1)] = out[SCAN_CHUNK:] + part(b, pr, 3)


def _affine_scan(maps, nc_ctx, name):
    _, nb, t, _ = maps.shape
    nc = t // SCAN_CHUNK
    order = _chunk_order(nc_ctx, nc)
    return pl.pallas_call(
        functools.partial(_affine_scan_kernel, nb),
        out_shape=jax.ShapeDtypeStruct((2, nb, t, BRANCH_W), F32),
        grid=(2, nc),
        in_specs=[pl.BlockSpec((1, nb, SCAN_CHUNK, N_PAIRS * MAP_W), lambda d, c: (d, 0, order(d, c), 0))],
        out_specs=pl.BlockSpec((1, nb, SCAN_CHUNK, BRANCH_W), lambda d, c: (d, 0, order(d, c), 0)),
        scratch_shapes=[pltpu.VMEM((nb, N_PAIRS, HEAD_DIM, PAIR), F32)],
        compiler_params=_cparams(("parallel", "arbitrary")),
        name=name,
    )(maps)


def _rwkv_block_inputs(p, w0, w_up, a0, a_up, k_k, k_a, ones, time_ops):
    r = p[:, 0:256]
    k = p[:, 256:512]
    v = p[:, 512:768]
    wl = p[:, 768:800]
    al = p[:, 800:832]
    lw = -RW_DECAY_SCALE * _sigmoid(w0 + _dot(jnp.tanh(wl), w_up))
    a = _sigmoid(a0 + _dot(al, a_up))
    kkp = k * k_k
    kk = kkp * lax.rsqrt(_mm3(kkp * kkp, ones) + EPS)
    kmod = k * (1.0 + (a - 1.0) * k_a)
    alpha = -(a * kk)
    sums = _select_rows(time_ops, lw)
    g = sums[0:TOKEN_BLOCK]
    g_tot = sums[TOKEN_BLOCK:]
    e_neg = jnp.exp(-g)
    e_tail = jnp.exp(g_tot - g)
    return dict(b=kk * jnp.exp(g - lw), r=r * jnp.exp(g), kh=kmod * e_neg, ah=alpha * e_neg,
                kt=kmod * e_tail, at=alpha * e_tail, v=v, gam=jnp.exp(g_tot))


def _rwkv_chunk_kernel(p_ref, w0_ref, wup_ref, a0_ref, aup_ref, kk_ref, ka_ref, ones_ref, o_ref):
    d = pl.program_id(0)
    ops = _block_time_operators(d)
    same, incl, strict, eye = _pair_masks(d)
    eye_f = jnp.where(eye, 1.0, 0.0).astype(F32)
    zeros = jnp.zeros((SCAN_CHUNK, PAIR), F32)
    pre = [_rwkv_block_inputs(p_ref[i], w0_ref[0], wup_ref[0], a0_ref[0], aup_ref[0], kk_ref[...], ka_ref[...],
                              ones_ref[...], ops) for i in range(CHUNK_BATCH)]
    probs = _chunk_problems()
    pk = lambda name: [pre[i][name][rows, lanes] for i, rows, lanes in probs]
    cat = jnp.concatenate
    b_t, r_t, k_h, a_h, k_t, a_t, v_p = pk("b"), pk("r"), pk("kh"), pk("ah"), pk("kt"), pk("at"), pk("v")
    m = [_mm_pk_nt(cat([b, r_], axis=0), [kh, ah], same) for b, r_, kh, ah in zip(b_t, r_t, k_h, a_h)]
    a_bk = [jnp.where(strict, x[0:SCAN_CHUNK, 0:PAIR], 0.0) for x in m]
    a_ba = [jnp.where(strict, x[0:SCAN_CHUNK, PAIR:], 0.0) for x in m]
    a_rk = [jnp.where(incl, x[SCAN_CHUNK:, 0:PAIR], 0.0) for x in m]
    a_ra = [jnp.where(incl, x[SCAN_CHUNK:, PAIR:], 0.0) for x in m]
    av = [_mm_pk(cat([x, y], axis=0), [vp], same) for x, y, vp in zip(a_bk, a_rk, v_p)]
    t_inv = _unit_tri_inverse(a_ba, eye_f, same)
    sol = [_mm_pk(t, [b, x[0:SCAN_CHUNK]], same) for t, b, x in zip(t_inv, b_t, av)]
    qy = [cat([r_, x[SCAN_CHUNK:]], axis=1) + _mm_pk(a, [s[:, 0:PAIR], s[:, PAIR:]], same)
          for r_, x, a, s in zip(r_t, av, a_ra, sol)]
    kb = [_mm3(cat([at, kt], axis=0), cat([s, cat([zeros, vp], axis=1)], axis=0), _TN)
          for at, kt, s, vp in zip(a_t, k_t, sol, v_p)]
    for (i, rows, lanes), kb_i, qy_i in zip(probs, kb, qy):
        base = MAP_W * (lanes.start // PAIR)
        o_ref[0, i, rows, base:base + PAIR] = (_pack(kb_i[:, 0:PAIR], same)
                                               + jnp.where(eye, pre[i]["gam"][rows.start:rows.start + 1, lanes], 0.0))
        o_ref[0, i, rows, base + PAIR:base + 2 * PAIR] = _pack(kb_i[:, PAIR:], same)
        o_ref[0, i, rows, base + 2 * PAIR:base + 4 * PAIR] = qy_i


def _rwkv_scan(ps, w0, w_up, a0, a_up, k_k, k_a, ones_bd, nc_ctx):
    nb, t, _ = ps.shape
    per_dir = lambda shape: pl.BlockSpec((1,) + shape, lambda d, b, j: (d, 0, 0))
    const = lambda shape: pl.BlockSpec(shape, lambda d, b, j: (0, 0))
    maps = pl.pallas_call(
        _rwkv_chunk_kernel,
        out_shape=jax.ShapeDtypeStruct((2, nb, t, N_PAIRS * MAP_W), F32),
        grid=(2, nb // CHUNK_BATCH, t // TOKEN_BLOCK),
        in_specs=[pl.BlockSpec((CHUNK_BATCH, TOKEN_BLOCK, RW_COLS), lambda d, b, j: (b, j, 0)),
                  per_dir((1, BRANCH_W)), per_dir((32, BRANCH_W)),
                  per_dir((1, BRANCH_W)), per_dir((32, BRANCH_W)),
                  const((1, BRANCH_W)), const((1, BRANCH_W)), const((BRANCH_W, BRANCH_W))],
        out_specs=pl.BlockSpec((1, CHUNK_BATCH, TOKEN_BLOCK, N_PAIRS * MAP_W), lambda d, b, j: (d, b, j, 0)),
        compiler_params=_cparams(("parallel", "parallel", "arbitrary"), VMEM_LIMIT),
        name="rwkv7_chunks",
    )(ps, w0.reshape(2, 1, BRANCH_W), w_up, a0.reshape(2, 1, BRANCH_W), a_up,
      k_k.reshape(1, BRANCH_W), k_a.reshape(1, BRANCH_W), ones_bd)
    return _affine_scan(maps, nc_ctx, "rwkv7_state_scan")


def _deltanet_block_inputs(u, dab, neg_exp_a, dt_bias, expand_a, expand_b, ones, time_ops):
    log_a8 = neg_exp_a * jax.nn.softplus(dab + dt_bias)
    la = _spread_cols(log_a8, expand_a)
    beta = _spread_cols(_sigmoid(dab), expand_b)
    q = u[:, 0:256]
    k = u[:, 256:512]
    v = u[:, 512:768]
    q = q * lax.rsqrt(_mm3(q * q, ones) + EPS) * (HEAD_DIM ** -0.5)
    k = k * lax.rsqrt(_mm3(k * k, ones) + EPS)
    sums = _select_rows(time_ops, la)
    g = sums[0:TOKEN_BLOCK]
    g_tot = sums[TOKEN_BLOCK:]
    e_g = jnp.exp(g)
    kb = k * beta
    return dict(g=g, q=q, k=k, kb=kb, vb=v * beta, kbe=kb * e_g, qe=q * e_g, kt=k * jnp.exp(g_tot - g),
                gam=jnp.exp(g_tot))


def _deltanet_chunk_kernel(u_ref, dab_ref, nea_ref, dtb_ref, ea_ref, eb_ref, ones_ref, o_ref):
    d = pl.program_id(0)
    ops = _block_time_operators(d)
    same, incl, strict, eye = _pair_masks(d)
    eye_f = jnp.where(eye, 1.0, 0.0).astype(F32)
    pre = [_deltanet_block_inputs(u_ref[i], dab_ref[i], nea_ref[...], dtb_ref[...], ea_ref[0], eb_ref[0],
                                  ones_ref[...], ops) for i in range(CHUNK_BATCH)]
    probs = _chunk_problems()
    pk = lambda name: [pre[i][name][rows, lanes] for i, rows, lanes in probs]
    cat = jnp.concatenate
    g_p = pk("g")
    g_t = [_pack(_bd(x, same).T, same) for x in g_p]
    decay = [jnp.exp(jnp.where(incl, x - y, -jnp.inf)) for x, y in zip(g_p, g_t)]
    m = [_mm_pk_nt(cat([kb, q_], axis=0), [k_], same) for kb, q_, k_ in zip(pk("kb"), pk("q"), pk("k"))]
    a_low = [jnp.where(strict, x[0:SCAN_CHUNK] * dc, 0.0) for x, dc in zip(m, decay)]
    attn = [x[SCAN_CHUNK:] * dc for x, dc in zip(m, decay)]
    t_inv = _unit_tri_inverse([-x for x in a_low], eye_f, same)
    sol = [_mm_pk(t, [vb, kbe], same) for t, vb, kbe in zip(t_inv, pk("vb"), pk("kbe"))]
    att_sol = [_mm_pk(a, [s[:, 0:PAIR], s[:, PAIR:]], same) for a, s in zip(attn, sol)]
    ks = [_mm3(kt, s, _TN) for kt, s in zip(pk("kt"), sol)]
    for (i, rows, lanes), ks_i, as_i, qe_i in zip(probs, ks, att_sol, pk("qe")):
        base = MAP_W * (lanes.start // PAIR)
        o_ref[0, i, rows, base:base + PAIR] = (jnp.where(eye, pre[i]["gam"][rows.start:rows.start + 1, lanes], 0.0)
                                               - _pack(ks_i[:, PAIR:], same))
        o_ref[0, i, rows, base + PAIR:base + 2 * PAIR] = _pack(ks_i[:, 0:PAIR], same)
        o_ref[0, i, rows, base + 2 * PAIR:base + 3 * PAIR] = qe_i - as_i[:, PAIR:]
        o_ref[0, i, rows, base + 3 * PAIR:base + 4 * PAIR] = as_i[:, 0:PAIR]


def _deltanet_scan(u, dab, a_log, dt_bias, ones_bd, nc_ctx):
    nb, t, _ = u.shape
    pad = 128 - 2 * N_HEADS
    neg_exp_a = jnp.pad(-jnp.exp(a_log.reshape(1, -1)), ((0, 0), (0, pad)))
    dtb = jnp.pad(dt_bias.reshape(1, -1), ((0, 0), (0, pad)))
    col = jnp.arange(128)[None, :, None]
    head = (jnp.arange(BRANCH_W) // HEAD_DIM)[None, None, :]
    dirs = jnp.arange(2)[:, None, None]
    expand_a = (col == dirs * N_HEADS + head).astype(F32)
    expand_b = (col == 2 * N_HEADS + dirs * N_HEADS + head).astype(F32)
    const = lambda shape: pl.BlockSpec(shape, lambda d, b, j: (0, 0))
    per_dir = pl.BlockSpec((1, 128, BRANCH_W), lambda d, b, j: (d, 0, 0))
    maps = pl.pallas_call(
        _deltanet_chunk_kernel,
        out_shape=jax.ShapeDtypeStruct((2, nb, t, N_PAIRS * MAP_W), F32),
        grid=(2, nb // CHUNK_BATCH, t // TOKEN_BLOCK),
        in_specs=[pl.BlockSpec((CHUNK_BATCH, TOKEN_BLOCK, 3 * BRANCH_W), lambda d, b, j: (b, j, 0)),
                  pl.BlockSpec((CHUNK_BATCH, TOKEN_BLOCK, 128), lambda d, b, j: (b, j, 0)),
                  const((1, 128)), const((1, 128)), per_dir, per_dir, const((BRANCH_W, BRANCH_W))],
        out_specs=pl.BlockSpec((1, CHUNK_BATCH, TOKEN_BLOCK, N_PAIRS * MAP_W), lambda d, b, j: (d, b, j, 0)),
        compiler_params=_cparams(("parallel", "parallel", "arbitrary"), VMEM_LIMIT),
        name="deltanet_chunks",
    )(u, dab, neg_exp_a, dtb, expand_a, expand_b, ones_bd)
    return _affine_scan(maps, nc_ctx, "deltanet_state_scan")


def _s5_kernel(n_batch, nc_ctx, nc, u_ref, toep_ref, winr_ref, wini_ref, wsor_ref, wsoi_ref,
               lr_ref, li_ref, dt_ref, y_ref, injr, inji, xsr, xsi):
    d = pl.program_id(1)
    u = u_ref[0]
    ub = u.astype(BF16)
    injr[...] = jnp.dot(ub, winr_ref[0, 0], preferred_element_type=F32)
    inji[...] = jnp.dot(ub, wini_ref[0, 0], preferred_element_type=F32)
    lam_r = lr_ref[0, 0]
    lam_i = li_ref[0, 0]

    def body(s, carry):
        xr, xi = carry
        back = jnp.where(s < nc_ctx, nc_ctx - 1 - s, nc - 1 + nc_ctx - s)
        row0 = jnp.where(d == 0, s, back) * n_batch
        if n_batch % 8 == 0:
            row0 = pl.multiple_of(row0, 8)
        rows = pl.ds(row0, n_batch)
        xsr[rows, :] = xr
        xsi[rows, :] = xi
        return (lam_r * xr - lam_i * xi + injr[rows, :], lam_r * xi + lam_i * xr + inji[rows, :])

    zero = jnp.zeros((n_batch, S5_STATE), F32)
    lax.fori_loop(0, nc, body, (zero, zero))

    y = (jnp.dot(ub, toep_ref[0, 0], preferred_element_type=F32)
         + _dot(xsr[...], wsor_ref[0, 0]) + _dot(xsi[...], wsoi_ref[0, 0]))

    @pl.when(d == 0)
    def _():
        y_ref[0] = y + u * dt_ref[0]

    @pl.when(d == 1)
    def _():
        y_ref[0] += y


def _s5_tables(lam_re, lam_im, log_step, b_re, b_im, c_re, c_im):
    cs = S5_CHUNK
    lam = lax.complex(lam_re.astype(F32), lam_im.astype(F32))
    step = jnp.exp(log_step.astype(F32))[..., None]
    tau = jnp.arange(cs + 1, dtype=F32)[:, None, None, None]
    lam_pow = jnp.exp(lam[None] * step[None] * tau)
    lam_bar = lam_pow[1]
    b_bar = ((lam_bar - 1.0) / lam)[..., None] * lax.complex(b_re.astype(F32), b_im.astype(F32))
    c_mat = lax.complex(c_re.astype(F32), c_im.astype(F32))
    kern = jnp.real(jnp.einsum("dghp,tdgp,dgpk->tdghk", c_mat, lam_pow[:cs], b_bar))
    i = jnp.arange(cs)
    lag_f = i[None, :] - i[:, None]
    toeps, winr, wini, wsor, wsoi = [], [], [], [], []
    for d in range(2):
        lag = lag_f if d == 0 else -lag_f
        kd = jnp.where((lag >= 0)[:, :, None, None, None], kern[:, d][jnp.clip(lag, 0, cs - 1)], 0.0)
        toeps.append(kd.transpose(2, 0, 4, 1, 3).reshape(S5_GROUPS, cs * S5_GW, cs * S5_GW))
        pw_in = (cs - 1 - i) if d == 0 else i
        e = lam_pow[pw_in, d][..., None] * b_bar[d][None]
        e = e.transpose(1, 0, 3, 2).reshape(S5_GROUPS, cs * S5_GW, S5_STATE)
        winr.append(jnp.real(e))
        wini.append(jnp.imag(e))
        pw_out = (i + 1) if d == 0 else (cs - i)
        m = c_mat[d][None] * lam_pow[pw_out, d][:, :, None, :]
        m = m.transpose(1, 3, 0, 2).reshape(S5_GROUPS, S5_STATE, cs * S5_GW)
        wsor.append(jnp.real(m))
        wsoi.append(-jnp.imag(m))
    stack = lambda xs, dt: jnp.stack(xs).astype(dt)
    lam_c = lam_pow[cs]
    return (stack(toeps, BF16), stack(winr, BF16), stack(wini, BF16), stack(wsor, BF16), stack(wsoi, BF16),
            jnp.real(lam_c)[:, :, None, :], jnp.imag(lam_c)[:, :, None, :])


def _s5_to_chunks(s5, n_ctx):
    nb, t, _ = s5.shape
    cs, g, hw = S5_CHUNK, S5_GROUPS, S5_GW
    n_lat = t - n_ctx
    rows = n_lat // 64
    c = s5[:, :n_ctx].reshape(nb, n_ctx // cs, cs, g, hw)
    c = c.transpose(3, 1, 0, 2, 4).reshape(g, (n_ctx // cs) * nb, cs * hw)
    l = s5[:, n_ctx:].reshape(nb, rows // cs, cs, 64, g, hw)
    l = l.transpose(4, 3, 1, 0, 2, 5).reshape(g, 64 * (rows // cs) * nb, cs * hw)
    return jnp.concatenate([c, l], axis=1)


def _s5_from_chunks(y, nb, n_ctx, n_lat):
    cs, g, hw = S5_CHUNK, S5_GROUPS, S5_GW
    rows = n_lat // 64
    r_ctx = (n_ctx // cs) * nb
    c = y[:, :r_ctx].reshape(g, n_ctx // cs, nb, cs, hw).transpose(2, 1, 3, 0, 4).reshape(nb, n_ctx, g * hw)
    l = y[:, r_ctx:].reshape(g, 64, rows // cs, nb, cs, hw).transpose(3, 2, 4, 1, 0, 5).reshape(nb, n_lat, g * hw)
    return jnp.concatenate([c, l], axis=1)


def _s5_scan(s5, tables, d_skip, n_ctx):
    nb, t, _ = s5.shape
    u = _s5_to_chunks(s5, n_ctx)
    g, r, w = u.shape
    nc = t // S5_CHUNK
    toep, winr, wini, wsor, wsoi, lr, li = tables
    d_tile = jnp.tile(d_skip.astype(F32).reshape(S5_GROUPS, 1, S5_GW), (1, S5_CHUNK, 1)).reshape(g, 1, w)
    per = lambda a, b: pl.BlockSpec((1, 1, a, b), lambda gi, d: (d, gi, 0, 0))
    y = pl.pallas_call(
        functools.partial(_s5_kernel, nb, n_ctx // S5_CHUNK, nc),
        out_shape=jax.ShapeDtypeStruct((g, r, w), F32),
        grid=(g, 2),
        in_specs=[pl.BlockSpec((1, r, w), lambda gi, d: (gi, 0, 0)),
                  per(w, w), per(w, S5_STATE), per(w, S5_STATE), per(S5_STATE, w), per(S5_STATE, w),
                  per(1, S5_STATE), per(1, S5_STATE),
                  pl.BlockSpec((1, 1, w), lambda gi, d: (gi, 0, 0))],
        out_specs=pl.BlockSpec((1, r, w), lambda gi, d: (gi, 0, 0)),
        scratch_shapes=[pltpu.VMEM((r, S5_STATE), F32) for _ in range(4)],
        compiler_params=_cparams(("parallel", "arbitrary"), VMEM_LIMIT),
        name="s5_scan",
    )(u, toep, winr, wini, wsor, wsoi, lr, li, d_tile)
    return _s5_from_chunks(y, nb, n_ctx, t - n_ctx)


def _merge_kernel(x_ref, mod_ref, g1_ref, ya_ref, yb_ref, ps_ref, oc_ref, dg_ref, y5_ref,
                  wg_ref, wb_ref, wo_ref, avg_ref, ones_ref, rk_ref, gup_ref, lng_ref, lnb_ref,
                  dng_ref, wglu_ref, bglu_ref, o_ref):
    x = x_ref[0]
    h = _norm_mod(x, g1_ref[...], mod_ref[0, 1:2, :], mod_ref[0, 0:1, :]).astype(BF16)
    avg = avg_ref[...]

    ps = ps_ref[0]
    r = ps[:, 0:256]
    k = ps[:, 256:512]
    v = ps[:, 512:768]
    gl = ps[:, 832:896]
    y = yb_ref[0, 0] + yb_ref[1, 0]
    dev = y - _dot(y, avg)
    yn = dev * lax.rsqrt(_dot(dev * dev, avg) + RW_GN_EPS) * lng_ref[...] + lnb_ref[...]
    bonus = _dot(r * k * rk_ref[...], ones_ref[...]) * v
    yb = (yn + bonus) * _dot(_sigmoid(gl), gup_ref[...])

    o = oc_ref[0, 0] + oc_ref[1, 0]
    yc = o * lax.rsqrt(_dot(o * o, avg) + EPS) * dng_ref[...] * _silu(dg_ref[0])

    z = jax.nn.gelu(y5_ref[0])
    yd = z * _sigmoid(_dot(z, wglu_ref[...]) + bglu_ref[...])

    m = jnp.zeros((TOKEN_BLOCK, D_MODEL), F32)
    for i, yi in enumerate((ya_ref[0], yb, yc, yd)):
        gate = _sigmoid(jnp.dot(h, wg_ref[:, D_MODEL * i:D_MODEL * (i + 1)], preferred_element_type=F32))
        m = m + gate * _dot(yi, wb_ref[i])
    o_ref[0] = x + mod_ref[0, 2:3, :] * _dot(m, wo_ref[...])


def _merge(xcat, mod, g1, ya, yb, ps, oc, dg, y5, w_gate, w_branch, w_out, avg_bd, ones_bd,
           r_k, g_up, ln_g, ln_b, dn_g, w_glu, b_glu, nblk_ctx):
    nb, t, _ = xcat.shape
    nblk = t // TOKEN_BLOCK
    tok = lambda w: pl.BlockSpec((1, TOKEN_BLOCK, w), lambda b, j: (b, j, 0))
    tok2 = lambda w: pl.BlockSpec((2, 1, TOKEN_BLOCK, w), lambda b, j: (0, b, j, 0))
    full = lambda a: pl.BlockSpec(a.shape, lambda b, j: (0,) * a.ndim)
    consts = (w_gate, w_branch, w_out, avg_bd, ones_bd, r_k, g_up, ln_g, ln_b, dn_g, w_glu, b_glu)
    return pl.pallas_call(
        _merge_kernel,
        out_shape=jax.ShapeDtypeStruct(xcat.shape, F32),
        grid=(nb, nblk),
        in_specs=[tok(D_MODEL), pl.BlockSpec((1, N_MOD, D_MODEL), _mod_index(nblk_ctx, nb)),
                  pl.BlockSpec((1, D_MODEL), lambda b, j: (0, 0)),
                  tok(BRANCH_W), tok2(BRANCH_W), tok(RW_COLS), tok2(BRANCH_W), tok(BRANCH_W), tok(BRANCH_W)]
        + [full(a) for a in consts],
        out_specs=tok(D_MODEL),
        compiler_params=_cparams(("parallel", "arbitrary"), VMEM_LIMIT),
        name="merge_branches",
    )(xcat, mod, g1, ya, yb, ps, oc, dg, y5, *consts)


def _route(sel, score):
    s = [sel[e:e + 1, :] for e in range(N_EXPERTS)]
    sc = [score[e:e + 1, :] for e in range(N_EXPERTS)]
    n_groups = N_EXPERTS // EXPERTS_PER_GROUP
    group_score = []
    for g in range(n_groups):
        m = s[EXPERTS_PER_GROUP * g:EXPERTS_PER_GROUP * (g + 1)]
        best = None
        for i in range(EXPERTS_PER_GROUP):
            for j in range(i + 1, EXPERTS_PER_GROUP):
                pair = m[i] + m[j]
                best = pair if best is None else jnp.maximum(best, pair)
        group_score.append(best)
    best_g = jnp.zeros(group_score[0].shape, jnp.int32)
    best_v = group_score[0]
    for g in range(1, n_groups):
        upd = group_score[g] > best_v
        best_g = jnp.where(upd, g, best_g)
        best_v = jnp.where(upd, group_score[g], best_v)
    chosen = []
    den = jnp.zeros_like(best_v)
    for e in range(N_EXPERTS):
        g = e // EXPERTS_PER_GROUP
        rank = jnp.zeros(best_g.shape, jnp.int32)
        for j in range(EXPERTS_PER_GROUP * g, EXPERTS_PER_GROUP * (g + 1)):
            if j == e:
                continue
            ahead = (s[j] > s[e]) if j > e else (s[j] >= s[e])
            rank = rank + jnp.where(ahead, 1, 0)
        pick = jnp.logical_and(best_g == g, rank < 2)
        chosen.append(pick)
        den = den + jnp.where(pick, sc[e], 0.0)
    return jnp.concatenate([jnp.where(chosen[e], sc[e] / den, 0.0) for e in range(N_EXPERTS)], axis=0)


def _moe_kernel(final, x_ref, mod_ref, g2_ref, rwt_ref, rb_ref, w1_ref, w3_ref, w2_ref, exp_ref, fg_ref, o_ref):
    steps = range(MOE_BATCH)
    xs = [x_ref[i] for i in steps]
    hs = [_norm_mod(xs[i], g2_ref[...], mod_ref[i, 4:5, :], mod_ref[i, 3:4, :]) for i in steps]
    scores = [_sigmoid(lax.dot_general(rwt_ref[...], h, _NT, preferred_element_type=F32, precision=HIGHEST))
              for h in hs]
    combs = [_route(sc + rb_ref[...], sc).astype(BF16) for sc in scores]
    hbs = [h.astype(BF16) for h in hs]
    accs = [jnp.zeros((TOKEN_BLOCK, D_MODEL), F32) for _ in steps]
    width = 4 * D_EXPERT
    for q in range(N_EXPERTS * D_EXPERT // width):
        cols = slice(width * q, width * (q + 1))
        for i in steps:
            cw = lax.dot_general(combs[i], exp_ref[:, cols], _TN, preferred_element_type=F32)
            a1 = jnp.dot(hbs[i], w1_ref[:, cols], preferred_element_type=F32)
            a3 = jnp.dot(hbs[i], w3_ref[:, cols], preferred_element_type=F32)
            act = (_silu(a1) * a3 * cw).astype(BF16)
            accs[i] = accs[i] + jnp.dot(act, w2_ref[cols, :], preferred_element_type=F32)
    for i in steps:
        y = xs[i] + mod_ref[i, 5:6, :] * accs[i]
        if final:
            y = y * lax.rsqrt(jnp.mean(y * y, axis=-1, keepdims=True) + EPS) * fg_ref[...]
        o_ref[i] = y


def _moe(x1, mod, g2, router_wt, router_b, w1, w3, w2, expand, nblk_ctx, final_g=None):
    nb, t, _ = x1.shape
    nblk = t // TOKEN_BLOCK
    final = final_g is not None
    tok = pl.BlockSpec((MOE_BATCH, TOKEN_BLOCK, D_MODEL), lambda b, j: (b, j, 0))
    full = lambda a: pl.BlockSpec(a.shape, lambda b, j: (0,) * a.ndim, pipeline_mode=pl.Buffered(1))
    fg = (final_g if final else jnp.ones((D_MODEL,), F32)).reshape(1, D_MODEL).astype(F32)
    consts = (router_wt, router_b, w1, w3, w2, expand, fg)
    if final:
        out_shape = jax.ShapeDtypeStruct((nb, t - nblk_ctx * TOKEN_BLOCK, D_MODEL), F32)
        out_spec = pl.BlockSpec((MOE_BATCH, TOKEN_BLOCK, D_MODEL), lambda b, j: (b, jnp.maximum(j - nblk_ctx, 0), 0))
    else:
        out_shape, out_spec = jax.ShapeDtypeStruct(x1.shape, F32), tok
    return pl.pallas_call(
        functools.partial(_moe_kernel, final),
        out_shape=out_shape,
        grid=(nb // MOE_BATCH, nblk),
        in_specs=[tok, pl.BlockSpec((MOE_BATCH, N_MOD, D_MODEL), _mod_index(nblk_ctx, nb, MOE_BATCH)),
                  pl.BlockSpec((1, D_MODEL), lambda b, j: (0, 0))] + [full(a) for a in consts],
        out_specs=out_spec,
        compiler_params=_cparams(("parallel", "arbitrary"), VMEM_LIMIT),
        name="moe_ffn",
    )(x1, mod, g2, *consts)


def _block_diag_ones(n_blocks, size):
    return jnp.kron(jnp.eye(n_blocks, dtype=F32), jnp.ones((size, size), F32))


def _layer(xcat, cond, n_ctx, lp, router_wt, router_b, consts, final_g=None):
    nb, t, _ = xcat.shape
    nblk_ctx = n_ctx // TOKEN_BLOCK
    nc_ctx = n_ctx // SCAN_CHUNK
    mod = _modulation(cond, lp["w_mod"], lp["b_mod"])
    g1 = lp["norm1_g"].reshape(1, D_MODEL)

    offs = [0]
    for s in W_IN_SPLITS:
        offs.append(offs[-1] + s)
    w_in = lp["w_in"]
    seg = lambda i: w_in[:, offs[i]:offs[i + 1]]
    w_mix = jnp.concatenate([seg(0), seg(1), seg(2), seg(3), seg(6), seg(4), seg(5),
                             jnp.zeros((D_MODEL, 128 - 16), F32)], axis=1).astype(BF16)
    w_gate = seg(7).astype(BF16)

    fcs, ps, u, dg, s5, dab = _input_projection(xcat, mod, g1, w_mix, consts["cs"], lp["rw_mu"], lp["dn_conv"],
                                                nblk_ctx)

    ya = jnp.concatenate([_sequence_dft(fcs, 0, n_ctx), _sequence_dft(fcs, n_ctx, t - n_ctx)], axis=1)

    yb = _rwkv_scan(ps, lp["rw_w0"], lp["rw_w_up"], lp["rw_a0"], lp["rw_a_up"], lp["rw_k_k"], lp["rw_k_a"],
                    consts["ones_bd"], nc_ctx)
    oc = _deltanet_scan(u, dab, lp["dn_a_log"], lp["dn_dt_bias"], consts["ones_bd"], nc_ctx)
    tables = _s5_tables(lp["s5_lam_re"], lp["s5_lam_im"], lp["s5_log_step"], lp["s5_b_re"], lp["s5_b_im"],
                        lp["s5_c_re"], lp["s5_c_im"])
    y5 = _s5_scan(s5, tables, lp["s5_d"], n_ctx)

    row = lambda a: a.reshape(1, -1).astype(F32)
    x1 = _merge(xcat, mod, g1, ya, yb, ps, oc, dg, y5, w_gate, lp["w_branch"].astype(BF16),
                lp["w_out"].astype(BF16), consts["avg_bd"], consts["ones_bd"], row(lp["rw_r_k"]),
                lp["rw_g_up"].astype(BF16), row(lp["rw_ln_g"]), row(lp["rw_ln_b"]),
                row(jnp.tile(lp["dn_norm_g"], N_HEADS)), lp["s5_w_glu"].astype(BF16), row(lp["s5_b_glu"]),
                nblk_ctx)

    w1 = lp["moe_w1"].transpose(1, 0, 2).reshape(D_MODEL, N_EXPERTS * D_EXPERT).astype(BF16)
    w3 = lp["moe_w3"].transpose(1, 0, 2).reshape(D_MODEL, N_EXPERTS * D_EXPERT).astype(BF16)
    w2 = lp["moe_w2"].reshape(N_EXPERTS * D_EXPERT, D_MODEL).astype(BF16)
    return _moe(x1, mod, lp["norm2_g"].reshape(1, D_MODEL), router_wt, router_b, w1, w3, w2,
                consts["expand"], nblk_ctx, final_g)


def kernel(x, c, ctx, c_ctx, w_mod, b_mod, norm1_g, norm2_g, w_in, rw_mu, rw_w0, rw_w_up, rw_a0, rw_a_up, rw_k_k, rw_k_a, rw_r_k, rw_g_up, rw_ln_g, rw_ln_b, dn_conv, dn_a_log, dn_dt_bias, dn_norm_g, s5_lam_re, s5_lam_im, s5_log_step, s5_b_re, s5_b_im, s5_c_re, s5_c_im, s5_d, s5_w_glu, s5_b_glu, w_branch, w_out, router_w, router_b, moe_w1, moe_w3, moe_w2, final_g):
    nb, n_lat, _ = x.shape
    n_ctx = ctx.shape[1]
    depth = w_mod.shape[0]
    assert n_ctx % TOKEN_BLOCK == 0 and n_lat % TOKEN_BLOCK == 0 and (n_lat // 64) % S5_CHUNK == 0
    assert nb % CHUNK_BATCH == 0 and nb % MOE_BATCH == 0

    xcat = jnp.concatenate([ctx, x], axis=1).astype(F32)
    cond_rows = -(-(nb + MOE_BATCH) // 8) * 8
    cond = jnp.zeros((cond_rows, D_MODEL), F32).at[:nb].set(c).at[nb:nb + MOE_BATCH].set(c_ctx)

    j = jnp.arange(FN_GW, dtype=jnp.int32)
    ang = ((j[:, None] * j[None, :]) % FN_GW).astype(F32) * (2.0 * math.pi / FN_GW)
    eye = jnp.eye(BRANCH_W // FN_GW, dtype=F32)
    consts = {
        "cs": jnp.concatenate([jnp.kron(eye, jnp.cos(ang)), jnp.kron(eye, jnp.sin(ang))], axis=1).astype(BF16),
        "ones_bd": _block_diag_ones(N_HEADS, HEAD_DIM).astype(BF16),
        "avg_bd": (_block_diag_ones(N_HEADS, HEAD_DIM) / HEAD_DIM).astype(BF16),
        "expand": jnp.kron(jnp.eye(N_EXPERTS, dtype=F32), jnp.ones((1, D_EXPERT), F32)).astype(BF16),
    }
    router_wt = router_w.T.astype(F32)
    router_bc = router_b.reshape(N_EXPERTS, 1).astype(F32)

    names = ("w_mod", "b_mod", "norm1_g", "norm2_g", "w_in", "rw_mu", "rw_w0", "rw_w_up", "rw_a0", "rw_a_up",
             "rw_k_k", "rw_k_a", "rw_r_k", "rw_g_up", "rw_ln_g", "rw_ln_b", "dn_conv", "dn_a_log", "dn_dt_bias",
             "dn_norm_g", "s5_lam_re", "s5_lam_im", "s5_log_step", "s5_b_re", "s5_b_im", "s5_c_re", "s5_c_im",
             "s5_d", "s5_w_glu", "s5_b_glu", "w_branch", "w_out", "moe_w1", "moe_w3", "moe_w2")
    stacked = (w_mod, b_mod, norm1_g, norm2_g, w_in, rw_mu, rw_w0, rw_w_up, rw_a0, rw_a_up, rw_k_k, rw_k_a,
               rw_r_k, rw_g_up, rw_ln_g, rw_ln_b, dn_conv, dn_a_log, dn_dt_bias, dn_norm_g, s5_lam_re,
               s5_lam_im, s5_log_step, s5_b_re, s5_b_im, s5_c_re, s5_c_im, s5_d, s5_w_glu, s5_b_glu,
               w_branch, w_out, moe_w1, moe_w3, moe_w2)
    for i in range(depth):
        lp = {n: a[i] for n, a in zip(names, stacked)}
        xcat = _layer(xcat, cond, n_ctx, lp, router_wt, router_bc, consts, final_g if i == depth - 1 else None)
    return xcat
```

```python
import functools
import math

import jax
import jax.numpy as jnp
from jax import lax
from jax.experimental import pallas as pl
from jax.experimental.pallas import tpu as pltpu

F32 = jnp.float32
BF16 = jnp.bfloat16
HIGHEST = lax.Precision.HIGHEST

D_MODEL = 1024
N_MOD = 6
EPS = 1e-6
N_BRANCH = 4
BRANCH_W = 256
HEAD_DIM = 64
N_HEADS = 4
FN_GW = 64
RW_COLS = 896
RW_DECAY_SCALE = math.exp(-0.5)
RW_GN_EPS = 64e-5
DN_CONV = 5
S5_GW = 16
S5_GROUPS = 16
S5_STATE = 64
N_EXPERTS = 16
EXPERTS_PER_GROUP = 4
D_EXPERT = 256
W_IN_SPLITS = (256, 896, 768, 256, 8, 8, 256, 4096)

TOKEN_BLOCK = 256
SCAN_CHUNK = 64
S5_CHUNK = 16
HALO = 8
MXU_DEPTH = 256
VMEM_LIMIT = 56 * 1024 * 1024


def _cparams(sem, vmem=None, **kw):
    return pltpu.CompilerParams(dimension_semantics=sem, vmem_limit_bytes=vmem, **kw)


def _dot(a, b):
    return jnp.dot(a.astype(BF16), b.astype(BF16), preferred_element_type=F32)


def _dot_nt(a, b):
    return lax.dot_general(a.astype(BF16), b.astype(BF16), (((1,), (1,)), ((), ())),
                           preferred_element_type=F32)


def _dot_tn(a, b):
    return lax.dot_general(a.astype(BF16), b.astype(BF16), (((0,), (0,)), ((), ())),
                           preferred_element_type=F32)


def _dot_hi(a, b):
    return jnp.dot(a, b, preferred_element_type=F32, precision=HIGHEST)


def _dot_nt_hi(a, b):
    return lax.dot_general(a, b, (((1,), (1,)), ((), ())), preferred_element_type=F32, precision=HIGHEST)


def _dot_tn_hi(a, b):
    return lax.dot_general(a, b, (((0,), (0,)), ((), ())), preferred_element_type=F32, precision=HIGHEST)


def _sigmoid(x):
    return jax.nn.sigmoid(x)


def _silu(x):
    return x * jax.nn.sigmoid(x)


def _norm_mod(x, g, scale, shift):
    y = x * lax.rsqrt(jnp.mean(x * x, axis=-1, keepdims=True) + EPS) * g
    return y * (1.0 + scale) + shift


def _layer_spec(a, layer, **kw):
    return pl.BlockSpec((1,) + a.shape[1:], lambda *_: (layer,) + (0,) * (a.ndim - 1), **kw)


def _mod_index(nblk_ctx, n_batch, per_step=1):
    return lambda b, j: (jnp.where(j < nblk_ctx, n_batch // per_step, b), 0, 0)


def _mod_kernel(c_ref, w_ref, b_ref, o_ref):
    o_ref[...] = _dot(_silu(c_ref[...]), w_ref[...]) + b_ref[...]


def _modulation(cond, w_mod, b_mod):
    rows = cond.shape[0]
    n = w_mod.shape[1]
    tn = 512
    out = pl.pallas_call(
        _mod_kernel,
        out_shape=jax.ShapeDtypeStruct((rows, n), F32),
        grid=(n // tn,),
        in_specs=[pl.BlockSpec((rows, D_MODEL), lambda i: (0, 0)),
                  pl.BlockSpec((D_MODEL, tn), lambda i: (0, i)),
                  pl.BlockSpec((1, tn), lambda i: (0, i))],
        out_specs=pl.BlockSpec((rows, tn), lambda i: (0, i)),
        compiler_params=_cparams(("arbitrary",)),
        name="adaln_mod",
    )(cond, w_mod, b_mod.reshape(1, n))
    return out.reshape(rows, N_MOD, D_MODEL)


def _shifted(x, s, halo_prev, halo_next):
    n = x.shape[0]
    if s == 0:
        return x
    rows = lax.broadcasted_iota(jnp.int32, x.shape, 0)
    if s < 0:
        y = pltpu.roll(x, -s, 0)
        for t in range(-s):
            y = jnp.where(rows == t, halo_prev[HALO + s + t:HALO + s + t + 1, :], y)
    else:
        y = pltpu.roll(x, n - s, 0)
        for t in range(s):
            y = jnp.where(rows == n - s + t, halo_next[t:t + 1, :], y)
    return y


def _inproj_kernel(nblk_ctx, nblk, x_ref, xp_ref, xn_ref, mod_ref, g_ref, w_ref, cs_ref, mu_ref, cw_ref,
                   fcs_ref, ps_ref, u_ref, dg_ref, s5_ref, dab_ref):
    j = pl.program_id(1)
    first = jnp.logical_or(j == 0, j == nblk_ctx)
    last = jnp.logical_or(j == nblk_ctx - 1, j == nblk - 1)
    pv = jnp.where(first, 0.0, 1.0).astype(F32)
    nv = jnp.where(last, 0.0, 1.0).astype(F32)

    x_all = jnp.concatenate([xp_ref[0], x_ref[0], xn_ref[0]], axis=0)
    h_all = _norm_mod(x_all, g_ref[...], mod_ref[0, 1:2, :], mod_ref[0, 0:1, :])
    h = h_all[HALO:HALO + TOKEN_BLOCK].astype(BF16)
    h_all = h_all.astype(BF16)

    fn = jnp.dot(h, w_ref[0, :, 0:256], preferred_element_type=F32)
    fcs_ref[0] = _dot(fn, cs_ref[...]).astype(BF16)
    dg_ref[0] = jnp.dot(h, w_ref[0, :, 1920:2176], preferred_element_type=F32)
    s5_ref[0] = jnp.dot(h, w_ref[0, :, 2176:2432], preferred_element_type=F32)
    dab_ref[0] = jnp.dot(h, w_ref[0, :, 2432:2560], preferred_element_type=F32)

    def with_halo(cols):
        p = jnp.dot(h_all, w_ref[0, :, cols], preferred_element_type=F32)
        return p[HALO:HALO + TOKEN_BLOCK], p[0:HALO] * pv, p[HALO + TOKEN_BLOCK:] * nv

    p, hp, hn = with_halo(slice(256, 1152))
    ps_ref[0] = p + mu_ref[0:1, :] * (_shifted(p, -1, hp, hn) - p) + mu_ref[1:2, :] * (_shifted(p, 1, hp, hn) - p)

    q, hp, hn = with_halo(slice(1152, 1920))
    pad = DN_CONV // 2
    acc = cw_ref[pad:pad + 1, :] * q
    for t in range(DN_CONV):
        if t != pad:
            acc = acc + cw_ref[t:t + 1, :] * _shifted(q, t - pad, hp, hn)
    u_ref[0] = _silu(acc)


def _input_projection(xcat, mod, g1, w_mix, layer, cs, mu, conv_w, nblk_ctx):
    nb, t, _ = xcat.shape
    nblk = t // TOKEN_BLOCK
    per = TOKEN_BLOCK // HALO
    last_halo = t // HALO - 1
    widths = (512, RW_COLS, 3 * BRANCH_W, 256, 256, 128)
    dtypes = (BF16, F32, F32, F32, F32, F32)
    tok = lambda w: pl.BlockSpec((1, TOKEN_BLOCK, w), lambda b, j: (b, j, 0))
    full = lambda a: pl.BlockSpec(a.shape, lambda b, j: (0,) * a.ndim)
    return pl.pallas_call(
        functools.partial(_inproj_kernel, nblk_ctx, nblk),
        out_shape=[jax.ShapeDtypeStruct((nb, t, w), dt) for w, dt in zip(widths, dtypes)],
        grid=(nb, nblk),
        in_specs=[tok(D_MODEL),
                  pl.BlockSpec((1, HALO, D_MODEL), lambda b, j: (b, jnp.maximum(j * per - 1, 0), 0)),
                  pl.BlockSpec((1, HALO, D_MODEL), lambda b, j: (b, jnp.minimum((j + 1) * per, last_halo), 0)),
                  pl.BlockSpec((1, N_MOD, D_MODEL), _mod_index(nblk_ctx, nb)),
                  pl.BlockSpec((1, D_MODEL), lambda b, j: (0, 0)),
                  _layer_spec(w_mix, layer), full(cs), full(mu), full(conv_w)],
        out_specs=[tok(w) for w in widths],
        compiler_params=_cparams(("parallel", "arbitrary"), VMEM_LIMIT),
        name="norm1_inproj",
    )(xcat, xcat, xcat, mod, g1, w_mix, cs, mu, conv_w)


def _dft_kernel(n_batch, wc_ref, ws_ref, u_ref, o_ref):
    @pl.when(pl.program_id(1) == 0)
    def _():
        o_ref[...] = jnp.zeros_like(o_ref)

    wc = wc_ref[...]
    ws = ws_ref[...]
    for b in range(n_batch):
        u = u_ref[b]
        o_ref[b] += (jnp.dot(wc, u[:, 0:256], preferred_element_type=F32)
                     + jnp.dot(ws, u[:, 256:512], preferred_element_type=F32))


def _dft_tables(n):
    n2 = math.isqrt(n)
    assert n2 * n2 == n
    k = jnp.arange(n, dtype=jnp.int32)[None, :]
    j = jnp.arange(n2, dtype=jnp.int32)[:, None]
    ang_a = ((j * k) % n2).astype(F32) * (2.0 * math.pi / n2)
    ang_b = ((j * k) % n).astype(F32) * (2.0 * math.pi / n)
    ca, sa, cb, sb = jnp.cos(ang_a)[:, None], jnp.sin(ang_a)[:, None], jnp.cos(ang_b)[None], jnp.sin(ang_b)[None]
    scale = 1.0 / math.sqrt(n * FN_GW)
    cos = ((ca * cb - sa * sb) * scale).reshape(n, n)
    sin = ((sa * cb + ca * sb) * scale).reshape(n, n)
    return cos.astype(BF16), (-sin).astype(BF16)


def _sequence_dft(fcs, row0, n):
    nb = fcs.shape[0]
    wc, ws = _dft_tables(n)
    tm = min(n, 1024)
    tk = min(n, 256)
    assert n % tm == 0 and n % tk == 0 and row0 % tk == 0
    kb0 = row0 // tk
    return pl.pallas_call(
        functools.partial(_dft_kernel, nb),
        out_shape=jax.ShapeDtypeStruct((nb, n, BRANCH_W), F32),
        grid=(n // tm, n // tk),
        in_specs=[pl.BlockSpec((tm, tk), lambda m, k: (m, k)),
                  pl.BlockSpec((tm, tk), lambda m, k: (m, k)),
                  pl.BlockSpec((nb, tk, 512), lambda m, k: (0, k + kb0, 0))],
        out_specs=pl.BlockSpec((nb, tm, BRANCH_W), lambda m, k: (0, m, 0)),
        compiler_params=_cparams(("parallel", "arbitrary"), VMEM_LIMIT),
        name="fourier_seq_dft",
    )(wc, ws, fcs)


def _chunk_order(nc_ctx, nc):
    def order(d, c):
        back = jnp.where(c < nc_ctx, nc_ctx - 1 - c, nc - 1 + nc_ctx - c)
        return jnp.where(d == 0, c, back)
    return order


PAIR = 2 * HEAD_DIM
N_PAIRS = N_HEADS // 2
CHUNKS_PER_BLOCK = TOKEN_BLOCK // SCAN_CHUNK
MAP_W = 4 * PAIR
CHUNK_BATCH = 2
MOE_BATCH = 2
_NN = (((1,), (0,)), ((), ()))
_NT = (((1,), (1,)), ((), ()))
_TN = (((0,), (0,)), ((), ()))
_LOG2_CHUNK = int(math.log2(SCAN_CHUNK))


def _split(x):
    hi = x.astype(BF16)
    return hi, (x - hi.astype(F32)).astype(BF16)


def _mm3(a, b, dims=_NN):
    a_hi, a_lo = _split(a)
    b_hi, b_lo = _split(b)
    dg = lambda x, y: lax.dot_general(x, y, dims, preferred_element_type=F32)
    ca, cb = dims[0][0][0], dims[0][1][0]
    if a.shape[ca] <= MXU_DEPTH // 2:
        return (dg(jnp.concatenate([a_hi, a_lo], axis=ca), jnp.concatenate([b_hi, b_hi], axis=cb))
                + dg(a_hi, b_lo))
    return dg(a_hi, b_hi) + dg(a_hi, b_lo) + dg(a_lo, b_hi)


def _split3(x):
    hi = x.astype(BF16)
    rest = x - hi.astype(F32)
    mid = rest.astype(BF16)
    return hi, mid, (rest - mid.astype(F32)).astype(BF16)


def _select_rows(op, x):
    op = op.astype(BF16)
    hi, mid, lo = _split3(x)
    dot = lambda y: jnp.dot(op, y, preferred_element_type=F32)
    return dot(hi) + dot(mid) + dot(lo)


def _spread_cols(x, sel):
    sel = sel.astype(BF16)
    hi, mid, lo = _split3(x)
    assert x.shape[1] <= MXU_DEPTH // 2
    return (jnp.dot(jnp.concatenate([hi, mid], axis=1), jnp.concatenate([sel, sel], axis=0),
                    preferred_element_type=F32)
            + jnp.dot(lo, sel, preferred_element_type=F32))


def _pair_masks(d):
    row = lax.broadcasted_iota(jnp.int32, (PAIR, PAIR), 0)
    col = lax.broadcasted_iota(jnp.int32, (PAIR, PAIR), 1)
    same = jnp.right_shift(row, _LOG2_CHUNK) == jnp.right_shift(col, _LOG2_CHUNK)
    t_row = lax.broadcasted_iota(jnp.int32, (SCAN_CHUNK, PAIR), 0)
    t_col = jnp.bitwise_and(lax.broadcasted_iota(jnp.int32, (SCAN_CHUNK, PAIR), 1), SCAN_CHUNK - 1)
    delta = (t_row - t_col) * (1 - 2 * d)
    return same, delta >= 0, delta > 0, t_row == t_col


def _block_time_operators(d):
    row = lax.broadcasted_iota(jnp.int32, (TOKEN_BLOCK, TOKEN_BLOCK), 0)
    col = lax.broadcasted_iota(jnp.int32, (TOKEN_BLOCK, TOKEN_BLOCK), 1)
    same = jnp.right_shift(row, _LOG2_CHUNK) == jnp.right_shift(col, _LOG2_CHUNK)
    delta = (jnp.bitwise_and(row, SCAN_CHUNK - 1) - jnp.bitwise_and(col, SCAN_CHUNK - 1)) * (1 - 2 * d)
    cum = jnp.where(jnp.logical_and(same, delta >= 0), 1.0, 0.0).astype(F32)
    return jnp.concatenate([cum, jnp.where(same, 1.0, 0.0).astype(F32)], axis=0)


def _chunk_problems():
    return [(i, slice(SCAN_CHUNK * c, SCAN_CHUNK * (c + 1)), slice(PAIR * pr, PAIR * (pr + 1)))
            for i in range(CHUNK_BATCH) for c in range(CHUNKS_PER_BLOCK) for pr in range(N_PAIRS)]


def _bd(x, same):
    return jnp.where(same, jnp.concatenate([x, x], axis=0), jnp.zeros((), x.dtype))


def _pack(x, same):
    x = jnp.where(same, x, 0.0)
    return x[0:HEAD_DIM] + x[HEAD_DIM:PAIR]


def _mm_pk(a, bs, same):
    a_hi, a_lo = _split(a)
    parts = [_split(b) for b in bs]
    r_hi = jnp.concatenate([_bd(hi, same) for hi, _ in parts], axis=1)
    r_lo = jnp.concatenate([_bd(lo, same) for _, lo in parts], axis=1)
    return (jnp.dot(jnp.concatenate([a_hi, a_lo], axis=1), jnp.concatenate([r_hi, r_hi], axis=0),
                    preferred_element_type=F32)
            + jnp.dot(a_hi, r_lo, preferred_element_type=F32))


def _mm_pk_nt(a, bs, same):
    a_hi, a_lo = _split(a)
    parts = [_split(b) for b in bs]
    r_hi = jnp.concatenate([_bd(hi, same) for hi, _ in parts], axis=0)
    r_lo = jnp.concatenate([_bd(lo, same) for _, lo in parts], axis=0)
    dg = lambda x, y: lax.dot_general(x, y, _NT, preferred_element_type=F32)
    return dg(jnp.concatenate([a_hi, a_lo], axis=1), jnp.concatenate([r_hi, r_hi], axis=1)) + dg(a_hi, r_lo)


def _unit_tri_inverse(n_pks, eye_pk, same):
    xs = [eye_pk + n for n in n_pks]
    ps = [_mm_pk(n, [n], same) for n in n_pks]
    for level in range(1, _LOG2_CHUNK):
        if level + 1 < _LOG2_CHUNK:
            xps = [_mm_pk(p, [x, p], same) for x, p in zip(xs, ps)]
            xs = [x + xp[:, 0:PAIR] for x, xp in zip(xs, xps)]
            ps = [xp[:, PAIR:2 * PAIR] for xp in xps]
        else:
            xs = [x + _mm_pk(p, [x], same) for x, p in zip(xs, ps)]
    return xs


def _affine_scan_kernel(n_batch, m_ref, y_ref, h_ref):
    @pl.when(pl.program_id(1) == 0)
    def _():
        h_ref[...] = jnp.zeros_like(h_ref)

    same, _, _, _ = _pair_masks(0)
    chains = [(b, pr) for b in range(n_batch) for pr in range(N_PAIRS)]
    part = lambda b, pr, i: m_ref[0, b, :, MAP_W * pr + PAIR * i:MAP_W * pr + PAIR * (i + 1)]
    outs = [_mm_pk(jnp.concatenate([part(b, pr, 0), part(b, pr, 2)], axis=0), [h_ref[b, pr]], same)
            for b, pr in chains]
    for out, (b, pr) in zip(outs, chains):
        h_ref[b, pr] = out[0:SCAN_CHUNK] + part(b, pr, 1)
        y_ref[0, b, :, PAIR * pr:PAIR * (pr + 1)] = out[SCAN_CHUNK:] + part(b, pr, 3)


def _affine_scan(maps, nc_ctx, name):
    _, nb, t, _ = maps.shape
    nc = t // SCAN_CHUNK
    order = _chunk_order(nc_ctx, nc)
    return pl.pallas_call(
        functools.partial(_affine_scan_kernel, nb),
        out_shape=jax.ShapeDtypeStruct((2, nb, t, BRANCH_W), F32),
        grid=(2, nc),
        in_specs=[pl.BlockSpec((1, nb, SCAN_CHUNK, N_PAIRS * MAP_W), lambda d, c: (d, 0, order(d, c), 0))],
        out_specs=pl.BlockSpec((1, nb, SCAN_CHUNK, BRANCH_W), lambda d, c: (d, 0, order(d, c), 0)),
        scratch_shapes=[pltpu.VMEM((nb, N_PAIRS, HEAD_DIM, PAIR), F32)],
        compiler_params=_cparams(("parallel", "arbitrary")),
        name=name,
    )(maps)


def _rwkv_block_inputs(p, w0, w_up, a0, a_up, k_k, k_a, ones, time_ops):
    r = p[:, 0:256]
    k = p[:, 256:512]
    v = p[:, 512:768]
    wl = p[:, 768:800]
    al = p[:, 800:832]
    lw = -RW_DECAY_SCALE * _sigmoid(w0 + _dot(jnp.tanh(wl), w_up))
    a = _sigmoid(a0 + _dot(al, a_up))
    kkp = k * k_k
    kk = kkp * lax.rsqrt(_mm3(kkp * kkp, ones) + EPS)
    kmod = k * (1.0 + (a - 1.0) * k_a)
    alpha = -(a * kk)
    sums = _select_rows(time_ops, lw)
    g = sums[0:TOKEN_BLOCK]
    g_tot = sums[TOKEN_BLOCK:]
    e_neg = jnp.exp(-g)
    e_tail = jnp.exp(g_tot - g)
    return dict(b=kk * jnp.exp(g - lw), r=r * jnp.exp(g), kh=kmod * e_neg, ah=alpha * e_neg,
                kt=kmod * e_tail, at=alpha * e_tail, v=v, gam=jnp.exp(g_tot))


def _rwkv_chunk_kernel(p_ref, w0_ref, wup_ref, a0_ref, aup_ref, kk_ref, ka_ref, ones_ref, o_ref):
    d = pl.program_id(0)
    ops = _block_time_operators(d)
    same, incl, strict, eye = _pair_masks(d)
    eye_f = jnp.where(eye, 1.0, 0.0).astype(F32)
    zeros = jnp.zeros((SCAN_CHUNK, PAIR), F32)
    pre = [_rwkv_block_inputs(p_ref[i], w0_ref[0], wup_ref[0], a0_ref[0], aup_ref[0], kk_ref[...], ka_ref[...],
                              ones_ref[...], ops) for i in range(CHUNK_BATCH)]
    probs = _chunk_problems()
    pk = lambda name: [pre[i][name][rows, lanes] for i, rows, lanes in probs]
    cat = jnp.concatenate
    b_t, r_t, k_h, a_h, k_t, a_t, v_p = pk("b"), pk("r"), pk("kh"), pk("ah"), pk("kt"), pk("at"), pk("v")
    m = [_mm_pk_nt(cat([b, r_], axis=0), [kh, ah], same) for b, r_, kh, ah in zip(b_t, r_t, k_h, a_h)]
    a_bk = [jnp.where(strict, x[0:SCAN_CHUNK, 0:PAIR], 0.0) for x in m]
    a_ba = [jnp.where(strict, x[0:SCAN_CHUNK, PAIR:], 0.0) for x in m]
    a_rk = [jnp.where(incl, x[SCAN_CHUNK:, 0:PAIR], 0.0) for x in m]
    a_ra = [jnp.where(incl, x[SCAN_CHUNK:, PAIR:], 0.0) for x in m]
    av = [_mm_pk(cat([x, y], axis=0), [vp], same) for x, y, vp in zip(a_bk, a_rk, v_p)]
    t_inv = _unit_tri_inverse(a_ba, eye_f, same)
    sol = [_mm_pk(t, [b, x[0:SCAN_CHUNK]], same) for t, b, x in zip(t_inv, b_t, av)]
    qy = [cat([r_, x[SCAN_CHUNK:]], axis=1) + _mm_pk(a, [s[:, 0:PAIR], s[:, PAIR:]], same)
          for r_, x, a, s in zip(r_t, av, a_ra, sol)]
    kb = [_mm3(cat([at, kt], axis=0), cat([s, cat([zeros, vp], axis=1)], axis=0), _TN)
          for at, kt, s, vp in zip(a_t, k_t, sol, v_p)]
    for (i, rows, lanes), kb_i, qy_i in zip(probs, kb, qy):
        base = MAP_W * (lanes.start // PAIR)
        o_ref[0, i, rows, base:base + PAIR] = (_pack(kb_i[:, 0:PAIR], same)
                                               + jnp.where(eye, pre[i]["gam"][rows.start:rows.start + 1, lanes], 0.0))
        o_ref[0, i, rows, base + PAIR:base + 2 * PAIR] = _pack(kb_i[:, PAIR:], same)
        o_ref[0, i, rows, base + 2 * PAIR:base + 4 * PAIR] = qy_i


def _rwkv_scan(ps, w0, w_up, a0, a_up, k_k, k_a, ones_bd, nc_ctx):
    nb, t, _ = ps.shape
    per_dir = lambda shape: pl.BlockSpec((1,) + shape, lambda d, b, j: (d, 0, 0))
    const = lambda shape: pl.BlockSpec(shape, lambda d, b, j: (0, 0))
    maps = pl.pallas_call(
        _rwkv_chunk_kernel,
        out_shape=jax.ShapeDtypeStruct((2, nb, t, N_PAIRS * MAP_W), F32),
        grid=(2, nb // CHUNK_BATCH, t // TOKEN_BLOCK),
        in_specs=[pl.BlockSpec((CHUNK_BATCH, TOKEN_BLOCK, RW_COLS), lambda d, b, j: (b, j, 0)),
                  per_dir((1, BRANCH_W)), per_dir((32, BRANCH_W)),
                  per_dir((1, BRANCH_W)), per_dir((32, BRANCH_W)),
                  const((1, BRANCH_W)), const((1, BRANCH_W)), const((BRANCH_W, BRANCH_W))],
        out_specs=pl.BlockSpec((1, CHUNK_BATCH, TOKEN_BLOCK, N_PAIRS * MAP_W), lambda d, b, j: (d, b, j, 0)),
        compiler_params=_cparams(("parallel", "parallel", "arbitrary"), VMEM_LIMIT),
        name="rwkv7_chunks",
    )(ps, w0.reshape(2, 1, BRANCH_W), w_up, a0.reshape(2, 1, BRANCH_W), a_up,
      k_k.reshape(1, BRANCH_W), k_a.reshape(1, BRANCH_W), ones_bd)
    return _affine_scan(maps, nc_ctx, "rwkv7_state_scan")


def _deltanet_block_inputs(u, dab, neg_exp_a, dt_bias, expand_a, expand_b, ones, time_ops):
    log_a8 = neg_exp_a * jax.nn.softplus(dab + dt_bias)
    la = _spread_cols(log_a8, expand_a)
    beta = _spread_cols(_sigmoid(dab), expand_b)
    q = u[:, 0:256]
    k = u[:, 256:512]
    v = u[:, 512:768]
    q = q * lax.rsqrt(_mm3(q * q, ones) + EPS) * (HEAD_DIM ** -0.5)
    k = k * lax.rsqrt(_mm3(k * k, ones) + EPS)
    sums = _select_rows(time_ops, la)
    g = sums[0:TOKEN_BLOCK]
    g_tot = sums[TOKEN_BLOCK:]
    e_g = jnp.exp(g)
    kb = k * beta
    return dict(g=g, q=q, k=k, kb=kb, vb=v * beta, kbe=kb * e_g, qe=q * e_g, kt=k * jnp.exp(g_tot - g),
                gam=jnp.exp(g_tot))


def _deltanet_chunk_kernel(u_ref, dab_ref, nea_ref, dtb_ref, ea_ref, eb_ref, ones_ref, o_ref):
    d = pl.program_id(0)
    ops = _block_time_operators(d)
    same, incl, strict, eye = _pair_masks(d)
    eye_f = jnp.where(eye, 1.0, 0.0).astype(F32)
    pre = [_deltanet_block_inputs(u_ref[i], dab_ref[i], nea_ref[...], dtb_ref[...], ea_ref[0], eb_ref[0],
                                  ones_ref[...], ops) for i in range(CHUNK_BATCH)]
    probs = _chunk_problems()
    pk = lambda name: [pre[i][name][rows, lanes] for i, rows, lanes in probs]
    cat = jnp.concatenate
    g_p = pk("g")
    g_t = [_pack(_bd(x, same).T, same) for x in g_p]
    decay = [jnp.exp(jnp.where(incl, x - y, -jnp.inf)) for x, y in zip(g_p, g_t)]
    m = [_mm_pk_nt(cat([kb, q_], axis=0), [k_], same) for kb, q_, k_ in zip(pk("kb"), pk("q"), pk("k"))]
    a_low = [jnp.where(strict, x[0:SCAN_CHUNK] * dc, 0.0) for x, dc in zip(m, decay)]
    attn = [x[SCAN_CHUNK:] * dc for x, dc in zip(m, decay)]
    t_inv = _unit_tri_inverse([-x for x in a_low], eye_f, same)
    sol = [_mm_pk(t, [vb, kbe], same) for t, vb, kbe in zip(t_inv, pk("vb"), pk("kbe"))]
    att_sol = [_mm_pk(a, [s[:, 0:PAIR], s[:, PAIR:]], same) for a, s in zip(attn, sol)]
    ks = [_mm3(kt, s, _TN) for kt, s in zip(pk("kt"), sol)]
    for (i, rows, lanes), ks_i, as_i, qe_i in zip(probs, ks, att_sol, pk("qe")):
        base = MAP_W * (lanes.start // PAIR)
        o_ref[0, i, rows, base:base + PAIR] = (jnp.where(eye, pre[i]["gam"][rows.start:rows.start + 1, lanes], 0.0)
                                               - _pack(ks_i[:, PAIR:], same))
        o_ref[0, i, rows, base + PAIR:base + 2 * PAIR] = _pack(ks_i[:, 0:PAIR], same)
        o_ref[0, i, rows, base + 2 * PAIR:base + 3 * PAIR] = qe_i - as_i[:, PAIR:]
        o_ref[0, i, rows, base + 3 * PAIR:base + 4 * PAIR] = as_i[:, 0:PAIR]


def _deltanet_scan(u, dab, a_log, dt_bias, ones_bd, nc_ctx):
    nb, t, _ = u.shape
    pad = 128 - 2 * N_HEADS
    neg_exp_a = jnp.pad(-jnp.exp(a_log.reshape(1, -1)), ((0, 0), (0, pad)))
    dtb = jnp.pad(dt_bias.reshape(1, -1), ((0, 0), (0, pad)))
    col = jnp.arange(128)[None, :, None]
    head = (jnp.arange(BRANCH_W) // HEAD_DIM)[None, None, :]
    dirs = jnp.arange(2)[:, None, None]
    expand_a = (col == dirs * N_HEADS + head).astype(F32)
    expand_b = (col == 2 * N_HEADS + dirs * N_HEADS + head).astype(F32)
    const = lambda shape: pl.BlockSpec(shape, lambda d, b, j: (0, 0))
    per_dir = pl.BlockSpec((1, 128, BRANCH_W), lambda d, b, j: (d, 0, 0))
    maps = pl.pallas_call(
        _deltanet_chunk_kernel,
        out_shape=jax.ShapeDtypeStruct((2, nb, t, N_PAIRS * MAP_W), F32),
        grid=(2, nb // CHUNK_BATCH, t // TOKEN_BLOCK),
        in_specs=[pl.BlockSpec((CHUNK_BATCH, TOKEN_BLOCK, 3 * BRANCH_W), lambda d, b, j: (b, j, 0)),
                  pl.BlockSpec((CHUNK_BATCH, TOKEN_BLOCK, 128), lambda d, b, j: (b, j, 0)),
                  const((1, 128)), const((1, 128)), per_dir, per_dir, const((BRANCH_W, BRANCH_W))],
        out_specs=pl.BlockSpec((1, CHUNK_BATCH, TOKEN_BLOCK, N_PAIRS * MAP_W), lambda d, b, j: (d, b, j, 0)),
        compiler_params=_cparams(("parallel", "parallel", "arbitrary"), VMEM_LIMIT),
        name="deltanet_chunks",
    )(u, dab, neg_exp_a, dtb, expand_a, expand_b, ones_bd)
    return _affine_scan(maps, nc_ctx, "deltanet_state_scan")


def _s5_kernel(n_batch, nc_ctx, nc, u_ref, toep_ref, winr_ref, wini_ref, wsor_ref, wsoi_ref,
               lr_ref, li_ref, dt_ref, y_ref, injr, inji, xsr, xsi):
    d = pl.program_id(1)
    u = u_ref[0]
    ub = u.astype(BF16)
    injr[...] = jnp.dot(ub, winr_ref[0, 0], preferred_element_type=F32)
    inji[...] = jnp.dot(ub, wini_ref[0, 0], preferred_element_type=F32)
    lam_r = lr_ref[0, 0]
    lam_i = li_ref[0, 0]

    def body(s, carry):
        xr, xi = carry
        back = jnp.where(s < nc_ctx, nc_ctx - 1 - s, nc - 1 + nc_ctx - s)
        row0 = jnp.where(d == 0, s, back) * n_batch
        if n_batch % 8 == 0:
            row0 = pl.multiple_of(row0, 8)
        rows = pl.ds(row0, n_batch)
        xsr[rows, :] = xr
        xsi[rows, :] = xi
        return (lam_r * xr - lam_i * xi + injr[rows, :], lam_r * xi + lam_i * xr + inji[rows, :])

    zero = jnp.zeros((n_batch, S5_STATE), F32)
    lax.fori_loop(0, nc, body, (zero, zero))

    y = (jnp.dot(ub, toep_ref[0, 0], preferred_element_type=F32)
         + _dot(xsr[...], wsor_ref[0, 0]) + _dot(xsi[...], wsoi_ref[0, 0]))

    @pl.when(d == 0)
    def _():
        y_ref[0] = y + u * dt_ref[0]

    @pl.when(d == 1)
    def _():
        y_ref[0] += y


def _s5_tables(lam_re, lam_im, log_step, b_re, b_im, c_re, c_im):
    cs = S5_CHUNK
    lam = lax.complex(lam_re.astype(F32), lam_im.astype(F32))
    step = jnp.exp(log_step.astype(F32))[..., None]
    tau = jnp.arange(cs + 1, dtype=F32)[:, None, None, None]
    lam_pow = jnp.exp(lam[None] * step[None] * tau)
    lam_bar = lam_pow[1]
    b_bar = ((lam_bar - 1.0) / lam)[..., None] * lax.complex(b_re.astype(F32), b_im.astype(F32))
    c_mat = lax.complex(c_re.astype(F32), c_im.astype(F32))
    kern = jnp.real(jnp.einsum("dghp,tdgp,dgpk->tdghk", c_mat, lam_pow[:cs], b_bar))
    i = jnp.arange(cs)
    lag_f = i[None, :] - i[:, None]
    toeps, winr, wini, wsor, wsoi = [], [], [], [], []
    for d in range(2):
        lag = lag_f if d == 0 else -lag_f
        kd = jnp.where((lag >= 0)[:, :, None, None, None], kern[:, d][jnp.clip(lag, 0, cs - 1)], 0.0)
        toeps.append(kd.transpose(2, 0, 4, 1, 3).reshape(S5_GROUPS, cs * S5_GW, cs * S5_GW))
        pw_in = (cs - 1 - i) if d == 0 else i
        e = lam_pow[pw_in, d][..., None] * b_bar[d][None]
        e = e.transpose(1, 0, 3, 2).reshape(S5_GROUPS, cs * S5_GW, S5_STATE)
        winr.append(jnp.real(e))
        wini.append(jnp.imag(e))
        pw_out = (i + 1) if d == 0 else (cs - i)
        m = c_mat[d][None] * lam_pow[pw_out, d][:, :, None, :]
        m = m.transpose(1, 3, 0, 2).reshape(S5_GROUPS, S5_STATE, cs * S5_GW)
        wsor.append(jnp.real(m))
        wsoi.append(-jnp.imag(m))
    stack = lambda xs, dt: jnp.stack(xs).astype(dt)
    lam_c = lam_pow[cs]
    return (stack(toeps, BF16), stack(winr, BF16), stack(wini, BF16), stack(wsor, BF16), stack(wsoi, BF16),
            jnp.real(lam_c)[:, :, None, :], jnp.imag(lam_c)[:, :, None, :])


def _s5_to_chunks(s5, n_ctx):
    nb, t, _ = s5.shape
    cs, g, hw = S5_CHUNK, S5_GROUPS, S5_GW
    n_lat = t - n_ctx
    rows = n_lat // 64
    c = s5[:, :n_ctx].reshape(nb, n_ctx // cs, cs, g, hw)
    c = c.transpose(3, 1, 0, 2, 4).reshape(g, (n_ctx // cs) * nb, cs * hw)
    l = s5[:, n_ctx:].reshape(nb, rows // cs, cs, 64, g, hw)
    l = l.transpose(4, 3, 1, 0, 2, 5).reshape(g, 64 * (rows // cs) * nb, cs * hw)
    return jnp.concatenate([c, l], axis=1)


def _s5_from_chunks(y, nb, n_ctx, n_lat):
    cs, g, hw = S5_CHUNK, S5_GROUPS, S5_GW
    rows = n_lat // 64
    r_ctx = (n_ctx // cs) * nb
    c = y[:, :r_ctx].reshape(g, n_ctx // cs, nb, cs, hw).transpose(2, 1, 3, 0, 4).reshape(nb, n_ctx, g * hw)
    l = y[:, r_ctx:].reshape(g, 64, rows // cs, nb, cs, hw).transpose(3, 2, 4, 1, 0, 5).reshape(nb, n_lat, g * hw)
    return jnp.concatenate([c, l], axis=1)


def _s5_scan(s5, tables, d_skip, n_ctx):
    nb, t, _ = s5.shape
    u = _s5_to_chunks(s5, n_ctx)
    g, r, w = u.shape
    nc = t // S5_CHUNK
    toep, winr, wini, wsor, wsoi, lr, li = tables
    d_tile = jnp.tile(d_skip.astype(F32).reshape(S5_GROUPS, 1, S5_GW), (1, S5_CHUNK, 1)).reshape(g, 1, w)
    per = lambda a, b: pl.BlockSpec((1, 1, a, b), lambda gi, d: (d, gi, 0, 0))
    y = pl.pallas_call(
        functools.partial(_s5_kernel, nb, n_ctx // S5_CHUNK, nc),
        out_shape=jax.ShapeDtypeStruct((g, r, w), F32),
        grid=(g, 2),
        in_specs=[pl.BlockSpec((1, r, w), lambda gi, d: (gi, 0, 0)),
                  per(w, w), per(w, S5_STATE), per(w, S5_STATE), per(S5_STATE, w), per(S5_STATE, w),
                  per(1, S5_STATE), per(1, S5_STATE),
                  pl.BlockSpec((1, 1, w), lambda gi, d: (gi, 0, 0))],
        out_specs=pl.BlockSpec((1, r, w), lambda gi, d: (gi, 0, 0)),
        scratch_shapes=[pltpu.VMEM((r, S5_STATE), F32) for _ in range(4)],
        compiler_params=_cparams(("parallel", "arbitrary"), VMEM_LIMIT),
        name="s5_scan",
    )(u, toep, winr, wini, wsor, wsoi, lr, li, d_tile)
    return _s5_from_chunks(y, nb, n_ctx, t - n_ctx)


def _merge_kernel(x_ref, mod_ref, g1_ref, ya_ref, yb_ref, ps_ref, oc_ref, dg_ref, y5_ref,
                  wg_ref, wb_ref, wo_ref, avg_ref, ones_ref, rk_ref, gup_ref, lng_ref, lnb_ref,
                  dng_ref, wglu_ref, bglu_ref, o_ref):
    x = x_ref[0]
    h = _norm_mod(x, g1_ref[...], mod_ref[0, 1:2, :], mod_ref[0, 0:1, :]).astype(BF16)
    avg = avg_ref[...]

    ps = ps_ref[0]
    r = ps[:, 0:256]
    k = ps[:, 256:512]
    v = ps[:, 512:768]
    gl = ps[:, 832:896]
    y = yb_ref[0, 0] + yb_ref[1, 0]
    dev = y - _dot(y, avg)
    yn = dev * lax.rsqrt(_dot(dev * dev, avg) + RW_GN_EPS) * lng_ref[...] + lnb_ref[...]
    bonus = _dot(r * k * rk_ref[...], ones_ref[...]) * v
    yb = (yn + bonus) * _dot(_sigmoid(gl), gup_ref[...])

    o = oc_ref[0, 0] + oc_ref[1, 0]
    yc = o * lax.rsqrt(_dot(o * o, avg) + EPS) * dng_ref[...] * _silu(dg_ref[0])

    z = jax.nn.gelu(y5_ref[0])
    yd = z * _sigmoid(_dot(z, wglu_ref[...]) + bglu_ref[...])

    m = jnp.zeros((TOKEN_BLOCK, D_MODEL), F32)
    for i, yi in enumerate((ya_ref[0], yb, yc, yd)):
        gate = _sigmoid(jnp.dot(h, wg_ref[0, :, D_MODEL * i:D_MODEL * (i + 1)], preferred_element_type=F32))
        m = m + gate * _dot(yi, wb_ref[0, i])
    o_ref[0] = x + mod_ref[0, 2:3, :] * _dot(m, wo_ref[0])


def _merge(xcat, mod, g1, ya, yb, ps, oc, dg, y5, w_gate, w_branch, w_out, layer, avg_bd, ones_bd,
           r_k, g_up, ln_g, ln_b, dn_g, w_glu, b_glu, nblk_ctx):
    nb, t, _ = xcat.shape
    nblk = t // TOKEN_BLOCK
    tok = lambda w: pl.BlockSpec((1, TOKEN_BLOCK, w), lambda b, j: (b, j, 0))
    tok2 = lambda w: pl.BlockSpec((2, 1, TOKEN_BLOCK, w), lambda b, j: (0, b, j, 0))
    full = lambda a: pl.BlockSpec(a.shape, lambda b, j: (0,) * a.ndim)
    stacked = (w_gate, w_branch, w_out)
    consts = (avg_bd, ones_bd, r_k, g_up, ln_g, ln_b, dn_g, w_glu, b_glu)
    return pl.pallas_call(
        _merge_kernel,
        out_shape=jax.ShapeDtypeStruct(xcat.shape, F32),
        grid=(nb, nblk),
        in_specs=[tok(D_MODEL), pl.BlockSpec((1, N_MOD, D_MODEL), _mod_index(nblk_ctx, nb)),
                  pl.BlockSpec((1, D_MODEL), lambda b, j: (0, 0)),
                  tok(BRANCH_W), tok2(BRANCH_W), tok(RW_COLS), tok2(BRANCH_W), tok(BRANCH_W), tok(BRANCH_W)]
        + [_layer_spec(a, layer) for a in stacked] + [full(a) for a in consts],
        out_specs=tok(D_MODEL),
        compiler_params=_cparams(("parallel", "arbitrary"), VMEM_LIMIT),
        name="merge_branches",
    )(xcat, mod, g1, ya, yb, ps, oc, dg, y5, *stacked, *consts)


def _route(sel, score):
    s = [sel[e:e + 1, :] for e in range(N_EXPERTS)]
    sc = [score[e:e + 1, :] for e in range(N_EXPERTS)]
    n_groups = N_EXPERTS // EXPERTS_PER_GROUP
    group_score = []
    for g in range(n_groups):
        m = s[EXPERTS_PER_GROUP * g:EXPERTS_PER_GROUP * (g + 1)]
        best = None
        for i in range(EXPERTS_PER_GROUP):
            for j in range(i + 1, EXPERTS_PER_GROUP):
                pair = m[i] + m[j]
                best = pair if best is None else jnp.maximum(best, pair)
        group_score.append(best)
    best_g = jnp.zeros(group_score[0].shape, jnp.int32)
    best_v = group_score[0]
    for g in range(1, n_groups):
        upd = group_score[g] > best_v
        best_g = jnp.where(upd, g, best_g)
        best_v = jnp.where(upd, group_score[g], best_v)
    chosen = []
    den = jnp.zeros_like(best_v)
    for e in range(N_EXPERTS):
        g = e // EXPERTS_PER_GROUP
        rank = jnp.zeros(best_g.shape, jnp.int32)
        for j in range(EXPERTS_PER_GROUP * g, EXPERTS_PER_GROUP * (g + 1)):
            if j == e:
                continue
            ahead = (s[j] > s[e]) if j > e else (s[j] >= s[e])
            rank = rank + jnp.where(ahead, 1, 0)
        pick = jnp.logical_and(best_g == g, rank < 2)
        chosen.append(pick)
        den = den + jnp.where(pick, sc[e], 0.0)
    return jnp.concatenate([jnp.where(chosen[e], sc[e] / den, 0.0) for e in range(N_EXPERTS)], axis=0)


def _moe_kernel(final, x_ref, mod_ref, g2_ref, w1_ref, w3_ref, w2_ref, rwt_ref, rb_ref, exp_ref, fg_ref, o_ref):
    steps = range(MOE_BATCH)
    xs = [x_ref[i] for i in steps]
    hs = [_norm_mod(xs[i], g2_ref[...], mod_ref[i, 4:5, :], mod_ref[i, 3:4, :]) for i in steps]
    scores = [_sigmoid(lax.dot_general(rwt_ref[...], h, _NT, preferred_element_type=F32, precision=HIGHEST))
              for h in hs]
    combs = [_route(sc + rb_ref[...], sc).astype(BF16) for sc in scores]
    hbs = [h.astype(BF16) for h in hs]
    accs = [jnp.zeros((TOKEN_BLOCK, D_MODEL), F32) for _ in steps]
    for q in range(N_EXPERTS // EXPERTS_PER_GROUP):
        experts = range(EXPERTS_PER_GROUP * q, EXPERTS_PER_GROUP * (q + 1))
        cols = slice(D_EXPERT * experts[0], D_EXPERT * (experts[-1] + 1))
        for i in steps:
            cw = lax.dot_general(combs[i], exp_ref[:, cols], _TN, preferred_element_type=F32)
            a1 = jnp.concatenate([jnp.dot(hbs[i], w1_ref[0, e], preferred_element_type=F32) for e in experts], axis=1)
            a3 = jnp.concatenate([jnp.dot(hbs[i], w3_ref[0, e], preferred_element_type=F32) for e in experts], axis=1)
            act = (_silu(a1) * a3 * cw).astype(BF16)
            accs[i] = accs[i] + jnp.dot(act, w2_ref[0, cols, :], preferred_element_type=F32)
    for i in steps:
        y = xs[i] + mod_ref[i, 5:6, :] * accs[i]
        if final:
            y = y * lax.rsqrt(jnp.mean(y * y, axis=-1, keepdims=True) + EPS) * fg_ref[...]
        o_ref[i] = y


def _moe(x1, mod, g2, router_wt, router_b, w1, w3, w2, layer, expand, nblk_ctx, final_g=None):
    nb, t, _ = x1.shape
    nblk = t // TOKEN_BLOCK
    final = final_g is not None
    tok = pl.BlockSpec((MOE_BATCH, TOKEN_BLOCK, D_MODEL), lambda b, j: (b, j, 0))
    full = lambda a: pl.BlockSpec(a.shape, lambda b, j: (0,) * a.ndim, pipeline_mode=pl.Buffered(1))
    fg = (final_g if final else jnp.ones((D_MODEL,), F32)).reshape(1, D_MODEL).astype(F32)
    consts = (router_wt, router_b, expand, fg)
    if final:
        out_shape = jax.ShapeDtypeStruct((nb, t - nblk_ctx * TOKEN_BLOCK, D_MODEL), F32)
        out_spec = pl.BlockSpec((MOE_BATCH, TOKEN_BLOCK, D_MODEL), lambda b, j: (b, jnp.maximum(j - nblk_ctx, 0), 0))
    else:
        out_shape, out_spec = jax.ShapeDtypeStruct(x1.shape, F32), tok
    return pl.pallas_call(
        functools.partial(_moe_kernel, final),
        out_shape=out_shape,
        grid=(nb // MOE_BATCH, nblk),
        in_specs=[tok, pl.BlockSpec((MOE_BATCH, N_MOD, D_MODEL), _mod_index(nblk_ctx, nb, MOE_BATCH)),
                  pl.BlockSpec((1, D_MODEL), lambda b, j: (0, 0))]
        + [_layer_spec(a, layer, pipeline_mode=pl.Buffered(1)) for a in (w1, w3, w2)] + [full(a) for a in consts],
        out_specs=out_spec,
        compiler_params=_cparams(("parallel", "arbitrary"), VMEM_LIMIT),
        name="moe_ffn",
    )(x1, mod, g2, w1, w3, w2, *consts)


def _block_diag_ones(n_blocks, size):
    return jnp.kron(jnp.eye(n_blocks, dtype=F32), jnp.ones((size, size), F32))


def _layer(xcat, cond, n_ctx, layer, lp, wts, router_wt, router_b, consts, final_g=None):
    nb, t, _ = xcat.shape
    nblk_ctx = n_ctx // TOKEN_BLOCK
    nc_ctx = n_ctx // SCAN_CHUNK
    mod = _modulation(cond, lp["w_mod"], lp["b_mod"])
    g1 = lp["norm1_g"].reshape(1, D_MODEL)

    fcs, ps, u, dg, s5, dab = _input_projection(xcat, mod, g1, wts["w_mix"], layer, consts["cs"], lp["rw_mu"],
                                                lp["dn_conv"], nblk_ctx)

    ya = jnp.concatenate([_sequence_dft(fcs, 0, n_ctx), _sequence_dft(fcs, n_ctx, t - n_ctx)], axis=1)

    yb = _rwkv_scan(ps, lp["rw_w0"], lp["rw_w_up"], lp["rw_a0"], lp["rw_a_up"], lp["rw_k_k"], lp["rw_k_a"],
                    consts["ones_bd"], nc_ctx)
    oc = _deltanet_scan(u, dab, lp["dn_a_log"], lp["dn_dt_bias"], consts["ones_bd"], nc_ctx)
    y5 = _s5_scan(s5, lp["s5_tables"], lp["s5_d"], n_ctx)

    row = lambda a: a.reshape(1, -1).astype(F32)
    x1 = _merge(xcat, mod, g1, ya, yb, ps, oc, dg, y5, wts["w_gate"], wts["w_branch"], wts["w_out"], layer,
                consts["avg_bd"], consts["ones_bd"], row(lp["rw_r_k"]),
                lp["rw_g_up"].astype(BF16), row(lp["rw_ln_g"]), row(lp["rw_ln_b"]),
                row(jnp.tile(lp["dn_norm_g"], N_HEADS)), lp["s5_w_glu"].astype(BF16), row(lp["s5_b_glu"]),
                nblk_ctx)

    return _moe(x1, mod, lp["norm2_g"].reshape(1, D_MODEL), router_wt, router_b, wts["moe_w1"], wts["moe_w3"],
                wts["moe_w2"], layer, consts["expand"], nblk_ctx, final_g)


def _mixer_column_order():
    offs = [0]
    for width in W_IN_SPLITS:
        offs.append(offs[-1] + width)
    order = [0, 1, 2, 3, 6, 4, 5]
    idx = [c for i in order for c in range(offs[i], offs[i + 1])]
    pad = -len(idx) % 128
    keep = [1.0] * len(idx) + [0.0] * pad
    return jnp.asarray(idx + [0] * pad, jnp.int32), jnp.asarray(keep, F32), offs[7]


def kernel(x, c, ctx, c_ctx, w_mod, b_mod, norm1_g, norm2_g, w_in, rw_mu, rw_w0, rw_w_up, rw_a0, rw_a_up, rw_k_k, rw_k_a, rw_r_k, rw_g_up, rw_ln_g, rw_ln_b, dn_conv, dn_a_log, dn_dt_bias, dn_norm_g, s5_lam_re, s5_lam_im, s5_log_step, s5_b_re, s5_b_im, s5_c_re, s5_c_im, s5_d, s5_w_glu, s5_b_glu, w_branch, w_out, router_w, router_b, moe_w1, moe_w3, moe_w2, final_g):
    nb, n_lat, _ = x.shape
    n_ctx = ctx.shape[1]
    depth = w_mod.shape[0]
    assert n_ctx % TOKEN_BLOCK == 0 and n_lat % TOKEN_BLOCK == 0 and (n_lat // 64) % S5_CHUNK == 0
    assert nb % CHUNK_BATCH == 0 and nb % MOE_BATCH == 0

    xcat = jnp.concatenate([ctx, x], axis=1).astype(F32)
    cond_rows = -(-(nb + MOE_BATCH) // 8) * 8
    cond = jnp.zeros((cond_rows, D_MODEL), F32).at[:nb].set(c).at[nb:nb + MOE_BATCH].set(c_ctx)

    j = jnp.arange(FN_GW, dtype=jnp.int32)
    ang = ((j[:, None] * j[None, :]) % FN_GW).astype(F32) * (2.0 * math.pi / FN_GW)
    eye = jnp.eye(BRANCH_W // FN_GW, dtype=F32)
    consts = {
        "cs": jnp.concatenate([jnp.kron(eye, jnp.cos(ang)), jnp.kron(eye, jnp.sin(ang))], axis=1).astype(BF16),
        "ones_bd": _block_diag_ones(N_HEADS, HEAD_DIM).astype(BF16),
        "avg_bd": (_block_diag_ones(N_HEADS, HEAD_DIM) / HEAD_DIM).astype(BF16),
        "expand": jnp.kron(jnp.eye(N_EXPERTS, dtype=F32), jnp.ones((1, D_EXPERT), F32)).astype(BF16),
    }
    router_wt = router_w.T.astype(F32)
    router_bc = router_b.reshape(N_EXPERTS, 1).astype(F32)

    cols, keep, gate0 = _mixer_column_order()
    wts = {
        "w_mix": (jnp.take(w_in, cols, axis=2) * keep).astype(BF16),
        "w_gate": w_in[:, :, gate0:].astype(BF16),
        "w_branch": w_branch.astype(BF16),
        "w_out": w_out.astype(BF16),
        "moe_w1": moe_w1.astype(BF16),
        "moe_w3": moe_w3.astype(BF16),
        "moe_w2": moe_w2.reshape(depth, N_EXPERTS * D_EXPERT, D_MODEL).astype(BF16),
    }
    s5_tables = jax.vmap(_s5_tables)(s5_lam_re, s5_lam_im, s5_log_step, s5_b_re, s5_b_im, s5_c_re, s5_c_im)
    small = dict(w_mod=w_mod, b_mod=b_mod, norm1_g=norm1_g, norm2_g=norm2_g, rw_mu=rw_mu, rw_w0=rw_w0,
                 rw_w_up=rw_w_up, rw_a0=rw_a0, rw_a_up=rw_a_up, rw_k_k=rw_k_k, rw_k_a=rw_k_a, rw_r_k=rw_r_k,
                 rw_g_up=rw_g_up, rw_ln_g=rw_ln_g, rw_ln_b=rw_ln_b, dn_conv=dn_conv, dn_a_log=dn_a_log,
                 dn_dt_bias=dn_dt_bias, dn_norm_g=dn_norm_g, s5_d=s5_d, s5_w_glu=s5_w_glu, s5_b_glu=s5_b_glu)
    for i in range(depth):
        lp = {n: a[i] for n, a in small.items()}
        lp["s5_tables"] = tuple(tb[i] for tb in s5_tables)
        xcat = _layer(xcat, cond, n_ctx, i, lp, wts, router_wt, router_bc, consts,
                      final_g if i == depth - 1 else None)
    return xcat
```

```python
import functools
import math

import jax
import jax.numpy as jnp
from jax import lax
from jax.experimental import pallas as pl
from jax.experimental.pallas import tpu as pltpu

F32 = jnp.float32
BF16 = jnp.bfloat16
HIGHEST = lax.Precision.HIGHEST

D_MODEL = 1024
N_MOD = 6
EPS = 1e-6
N_BRANCH = 4
BRANCH_W = 256
HEAD_DIM = 64
N_HEADS = 4
FN_GW = 64
RW_COLS = 896
RW_DECAY_SCALE = math.exp(-0.5)
RW_GN_EPS = 64e-5
DN_CONV = 5
S5_GW = 16
S5_GROUPS = 16
S5_STATE = 64
N_EXPERTS = 16
EXPERTS_PER_GROUP = 4
D_EXPERT = 256
W_IN_SPLITS = (256, 896, 768, 256, 8, 8, 256, 4096)

TOKEN_BLOCK = 256
SCAN_CHUNK = 64
S5_CHUNK = 16
HALO = 8
MXU_DEPTH = 256
VMEM_LIMIT = 56 * 1024 * 1024


def _cparams(sem, vmem=None, **kw):
    return pltpu.CompilerParams(dimension_semantics=sem, vmem_limit_bytes=vmem, **kw)


def _dot(a, b):
    return jnp.dot(a.astype(BF16), b.astype(BF16), preferred_element_type=F32)


def _dot_nt(a, b):
    return lax.dot_general(a.astype(BF16), b.astype(BF16), (((1,), (1,)), ((), ())),
                           preferred_element_type=F32)


def _dot_tn(a, b):
    return lax.dot_general(a.astype(BF16), b.astype(BF16), (((0,), (0,)), ((), ())),
                           preferred_element_type=F32)


def _dot_hi(a, b):
    return jnp.dot(a, b, preferred_element_type=F32, precision=HIGHEST)


def _dot_nt_hi(a, b):
    return lax.dot_general(a, b, (((1,), (1,)), ((), ())), preferred_element_type=F32, precision=HIGHEST)


def _dot_tn_hi(a, b):
    return lax.dot_general(a, b, (((0,), (0,)), ((), ())), preferred_element_type=F32, precision=HIGHEST)


def _sigmoid(x):
    return jax.nn.sigmoid(x)


def _silu(x):
    return x * jax.nn.sigmoid(x)


def _norm_mod(x, g, scale, shift):
    y = x * lax.rsqrt(jnp.mean(x * x, axis=-1, keepdims=True) + EPS) * g
    return y * (1.0 + scale) + shift


def _layer_spec(a, layer, **kw):
    return pl.BlockSpec((1,) + a.shape[1:], lambda *_: (layer,) + (0,) * (a.ndim - 1), **kw)


def _mod_index(nblk_ctx, n_batch, per_step=1):
    return lambda b, j: (jnp.where(j < nblk_ctx, n_batch // per_step, b), 0, 0)


def _mod_kernel(c_ref, w_ref, b_ref, o_ref):
    o_ref[...] = _dot(_silu(c_ref[...]), w_ref[...]) + b_ref[...]


def _modulation(cond, w_mod, b_mod):
    rows = cond.shape[0]
    n = w_mod.shape[1]
    tn = 512
    out = pl.pallas_call(
        _mod_kernel,
        out_shape=jax.ShapeDtypeStruct((rows, n), F32),
        grid=(n // tn,),
        in_specs=[pl.BlockSpec((rows, D_MODEL), lambda i: (0, 0)),
                  pl.BlockSpec((D_MODEL, tn), lambda i: (0, i)),
                  pl.BlockSpec((1, tn), lambda i: (0, i))],
        out_specs=pl.BlockSpec((rows, tn), lambda i: (0, i)),
        compiler_params=_cparams(("arbitrary",)),
        name="adaln_mod",
    )(cond, w_mod, b_mod.reshape(1, n))
    return out.reshape(rows, N_MOD, D_MODEL)


def _shifted(x, s, halo_prev, halo_next):
    n = x.shape[0]
    if s == 0:
        return x
    rows = lax.broadcasted_iota(jnp.int32, x.shape, 0)
    if s < 0:
        y = pltpu.roll(x, -s, 0)
        for t in range(-s):
            y = jnp.where(rows == t, halo_prev[HALO + s + t:HALO + s + t + 1, :], y)
    else:
        y = pltpu.roll(x, n - s, 0)
        for t in range(s):
            y = jnp.where(rows == n - s + t, halo_next[t:t + 1, :], y)
    return y


def _inproj_kernel(nblk_ctx, nblk, x_ref, xp_ref, xn_ref, mod_ref, g_ref, w_ref, cs_ref, mu_ref, cw_ref,
                   fcs_ref, ps_ref, u_ref, dg_ref, s5_ref, dab_ref):
    j = pl.program_id(1)
    first = jnp.logical_or(j == 0, j == nblk_ctx)
    last = jnp.logical_or(j == nblk_ctx - 1, j == nblk - 1)
    pv = jnp.where(first, 0.0, 1.0).astype(F32)
    nv = jnp.where(last, 0.0, 1.0).astype(F32)

    x_all = jnp.concatenate([xp_ref[0], x_ref[0], xn_ref[0]], axis=0)
    h_all = _norm_mod(x_all, g_ref[...], mod_ref[0, 1:2, :], mod_ref[0, 0:1, :])
    h = h_all[HALO:HALO + TOKEN_BLOCK].astype(BF16)
    h_all = h_all.astype(BF16)

    fn = jnp.dot(h, w_ref[0, :, 0:256], preferred_element_type=F32)
    fcs_ref[0] = _dot(fn, cs_ref[...]).astype(BF16)
    dg_ref[0] = jnp.dot(h, w_ref[0, :, 1920:2176], preferred_element_type=F32)
    s5_ref[0] = jnp.dot(h, w_ref[0, :, 2176:2432], preferred_element_type=F32)
    dab_ref[0] = jnp.dot(h, w_ref[0, :, 2432:2560], preferred_element_type=F32)

    def with_halo(cols):
        p = jnp.dot(h_all, w_ref[0, :, cols], preferred_element_type=F32)
        return p[HALO:HALO + TOKEN_BLOCK], p[0:HALO] * pv, p[HALO + TOKEN_BLOCK:] * nv

    p, hp, hn = with_halo(slice(256, 1152))
    ps_ref[0] = p + mu_ref[0:1, :] * (_shifted(p, -1, hp, hn) - p) + mu_ref[1:2, :] * (_shifted(p, 1, hp, hn) - p)

    q, hp, hn = with_halo(slice(1152, 1920))
    pad = DN_CONV // 2
    acc = cw_ref[pad:pad + 1, :] * q
    for t in range(DN_CONV):
        if t != pad:
            acc = acc + cw_ref[t:t + 1, :] * _shifted(q, t - pad, hp, hn)
    u_ref[0] = _silu(acc)


def _input_projection(xcat, mod, g1, w_mix, layer, cs, mu, conv_w, nblk_ctx):
    nb, t, _ = xcat.shape
    nblk = t // TOKEN_BLOCK
    per = TOKEN_BLOCK // HALO
    last_halo = t // HALO - 1
    widths = (512, RW_COLS, 3 * BRANCH_W, 256, 256, 128)
    dtypes = (BF16, F32, F32, F32, F32, F32)
    tok = lambda w: pl.BlockSpec((1, TOKEN_BLOCK, w), lambda b, j: (b, j, 0))
    full = lambda a: pl.BlockSpec(a.shape, lambda b, j: (0,) * a.ndim)
    return pl.pallas_call(
        functools.partial(_inproj_kernel, nblk_ctx, nblk),
        out_shape=[jax.ShapeDtypeStruct((nb, t, w), dt) for w, dt in zip(widths, dtypes)],
        grid=(nb, nblk),
        in_specs=[tok(D_MODEL),
                  pl.BlockSpec((1, HALO, D_MODEL), lambda b, j: (b, jnp.maximum(j * per - 1, 0), 0)),
                  pl.BlockSpec((1, HALO, D_MODEL), lambda b, j: (b, jnp.minimum((j + 1) * per, last_halo), 0)),
                  pl.BlockSpec((1, N_MOD, D_MODEL), _mod_index(nblk_ctx, nb)),
                  pl.BlockSpec((1, D_MODEL), lambda b, j: (0, 0)),
                  _layer_spec(w_mix, layer), full(cs), full(mu), full(conv_w)],
        out_specs=[tok(w) for w in widths],
        compiler_params=_cparams(("parallel", "arbitrary"), VMEM_LIMIT),
        name="norm1_inproj",
    )(xcat, xcat, xcat, mod, g1, w_mix, cs, mu, conv_w)


def _dft_kernel(n_batch, wc_ref, ws_ref, u_ref, o_ref):
    @pl.when(pl.program_id(1) == 0)
    def _():
        o_ref[...] = jnp.zeros_like(o_ref)

    wc = wc_ref[...]
    ws = ws_ref[...]
    for b in range(n_batch):
        u = u_ref[b]
        o_ref[b] += (jnp.dot(wc, u[:, 0:256], preferred_element_type=F32)
                     + jnp.dot(ws, u[:, 256:512], preferred_element_type=F32))


def _dft_tables(n):
    n2 = math.isqrt(n)
    assert n2 * n2 == n
    k = jnp.arange(n, dtype=jnp.int32)[None, :]
    j = jnp.arange(n2, dtype=jnp.int32)[:, None]
    ang_a = ((j * k) % n2).astype(F32) * (2.0 * math.pi / n2)
    ang_b = ((j * k) % n).astype(F32) * (2.0 * math.pi / n)
    ca, sa, cb, sb = jnp.cos(ang_a)[:, None], jnp.sin(ang_a)[:, None], jnp.cos(ang_b)[None], jnp.sin(ang_b)[None]
    scale = 1.0 / math.sqrt(n * FN_GW)
    cos = ((ca * cb - sa * sb) * scale).reshape(n, n)
    sin = ((sa * cb + ca * sb) * scale).reshape(n, n)
    return cos.astype(BF16), (-sin).astype(BF16)


def _sequence_dft(fcs, row0, n):
    nb = fcs.shape[0]
    wc, ws = _dft_tables(n)
    tm = min(n, 1024)
    tk = min(n, 256)
    assert n % tm == 0 and n % tk == 0 and row0 % tk == 0
    kb0 = row0 // tk
    return pl.pallas_call(
        functools.partial(_dft_kernel, nb),
        out_shape=jax.ShapeDtypeStruct((nb, n, BRANCH_W), F32),
        grid=(n // tm, n // tk),
        in_specs=[pl.BlockSpec((tm, tk), lambda m, k: (m, k)),
                  pl.BlockSpec((tm, tk), lambda m, k: (m, k)),
                  pl.BlockSpec((nb, tk, 512), lambda m, k: (0, k + kb0, 0))],
        out_specs=pl.BlockSpec((nb, tm, BRANCH_W), lambda m, k: (0, m, 0)),
        compiler_params=_cparams(("parallel", "arbitrary"), VMEM_LIMIT),
        name="fourier_seq_dft",
    )(wc, ws, fcs)


def _chunk_order(nc_ctx, nc):
    def order(d, c):
        back = jnp.where(c < nc_ctx, nc_ctx - 1 - c, nc - 1 + nc_ctx - c)
        return jnp.where(d == 0, c, back)
    return order


PAIR = 2 * HEAD_DIM
N_PAIRS = N_HEADS // 2
CHUNKS_PER_BLOCK = TOKEN_BLOCK // SCAN_CHUNK
MAP_W = 4 * PAIR
CHUNK_BATCH = 2
MOE_BATCH = 2
_NN = (((1,), (0,)), ((), ()))
_NT = (((1,), (1,)), ((), ()))
_TN = (((0,), (0,)), ((), ()))
_LOG2_CHUNK = int(math.log2(SCAN_CHUNK))


def _split(x):
    hi = x.astype(BF16)
    return hi, (x - hi.astype(F32)).astype(BF16)


def _mm3(a, b, dims=_NN):
    a_hi, a_lo = _split(a)
    b_hi, b_lo = _split(b)
    dg = lambda x, y: lax.dot_general(x, y, dims, preferred_element_type=F32)
    ca, cb = dims[0][0][0], dims[0][1][0]
    if a.shape[ca] <= MXU_DEPTH // 2:
        return (dg(jnp.concatenate([a_hi, a_lo], axis=ca), jnp.concatenate([b_hi, b_hi], axis=cb))
                + dg(a_hi, b_lo))
    return dg(a_hi, b_hi) + dg(a_hi, b_lo) + dg(a_lo, b_hi)


def _split3(x):
    hi = x.astype(BF16)
    rest = x - hi.astype(F32)
    mid = rest.astype(BF16)
    return hi, mid, (rest - mid.astype(F32)).astype(BF16)


def _select_rows(op, x):
    op = op.astype(BF16)
    hi, mid, lo = _split3(x)
    dot = lambda y: jnp.dot(op, y, preferred_element_type=F32)
    return dot(hi) + dot(mid) + dot(lo)


def _spread_cols(x, sel):
    sel = sel.astype(BF16)
    hi, mid, lo = _split3(x)
    assert x.shape[1] <= MXU_DEPTH // 2
    return (jnp.dot(jnp.concatenate([hi, mid], axis=1), jnp.concatenate([sel, sel], axis=0),
                    preferred_element_type=F32)
            + jnp.dot(lo, sel, preferred_element_type=F32))


def _pair_masks(d):
    row = lax.broadcasted_iota(jnp.int32, (PAIR, PAIR), 0)
    col = lax.broadcasted_iota(jnp.int32, (PAIR, PAIR), 1)
    same = jnp.right_shift(row, _LOG2_CHUNK) == jnp.right_shift(col, _LOG2_CHUNK)
    t_row = lax.broadcasted_iota(jnp.int32, (SCAN_CHUNK, PAIR), 0)
    t_col = jnp.bitwise_and(lax.broadcasted_iota(jnp.int32, (SCAN_CHUNK, PAIR), 1), SCAN_CHUNK - 1)
    delta = (t_row - t_col) * (1 - 2 * d)
    return same, delta >= 0, delta > 0, t_row == t_col


def _block_time_operators(d):
    row = lax.broadcasted_iota(jnp.int32, (TOKEN_BLOCK, TOKEN_BLOCK), 0)
    col = lax.broadcasted_iota(jnp.int32, (TOKEN_BLOCK, TOKEN_BLOCK), 1)
    same = jnp.right_shift(row, _LOG2_CHUNK) == jnp.right_shift(col, _LOG2_CHUNK)
    delta = (jnp.bitwise_and(row, SCAN_CHUNK - 1) - jnp.bitwise_and(col, SCAN_CHUNK - 1)) * (1 - 2 * d)
    cum = jnp.where(jnp.logical_and(same, delta >= 0), 1.0, 0.0).astype(F32)
    return jnp.concatenate([cum, jnp.where(same, 1.0, 0.0).astype(F32)], axis=0)


def _chunk_problems():
    return [(i, slice(SCAN_CHUNK * c, SCAN_CHUNK * (c + 1)), slice(PAIR * pr, PAIR * (pr + 1)))
            for i in range(CHUNK_BATCH) for c in range(CHUNKS_PER_BLOCK) for pr in range(N_PAIRS)]


def _bd(x, same):
    return jnp.where(same, jnp.concatenate([x, x], axis=0), jnp.zeros((), x.dtype))


def _pack(x, same):
    x = jnp.where(same, x, 0.0)
    return x[0:HEAD_DIM] + x[HEAD_DIM:PAIR]


def _mm_pk(a, bs, same):
    a_hi, a_lo = _split(a)
    parts = [_split(b) for b in bs]
    r_hi = jnp.concatenate([_bd(hi, same) for hi, _ in parts], axis=1)
    r_lo = jnp.concatenate([_bd(lo, same) for _, lo in parts], axis=1)
    return (jnp.dot(jnp.concatenate([a_hi, a_lo], axis=1), jnp.concatenate([r_hi, r_hi], axis=0),
                    preferred_element_type=F32)
            + jnp.dot(a_hi, r_lo, preferred_element_type=F32))


def _mm_pk_fast(a, bs, same):
    rhs = jnp.concatenate([_bd(b.astype(BF16), same) for b in bs], axis=1)
    return jnp.dot(a.astype(BF16), rhs, preferred_element_type=F32)


def _mm_pk_nt(a, bs, same):
    a_hi, a_lo = _split(a)
    parts = [_split(b) for b in bs]
    r_hi = jnp.concatenate([_bd(hi, same) for hi, _ in parts], axis=0)
    r_lo = jnp.concatenate([_bd(lo, same) for _, lo in parts], axis=0)
    dg = lambda x, y: lax.dot_general(x, y, _NT, preferred_element_type=F32)
    return dg(jnp.concatenate([a_hi, a_lo], axis=1), jnp.concatenate([r_hi, r_hi], axis=1)) + dg(a_hi, r_lo)


def _unit_tri_inverse(n_pks, eye_pk, same):
    xs = [eye_pk + n for n in n_pks]
    ps = [_mm_pk(n, [n], same) for n in n_pks]
    for level in range(1, _LOG2_CHUNK):
        if level + 1 < _LOG2_CHUNK:
            xps = [_mm_pk(p, [x, p], same) for x, p in zip(xs, ps)]
            xs = [x + xp[:, 0:PAIR] for x, xp in zip(xs, xps)]
            ps = [xp[:, PAIR:2 * PAIR] for xp in xps]
        else:
            xs = [x + _mm_pk(p, [x], same) for x, p in zip(xs, ps)]
    return xs


def _reset_chunk_state(maps_ref, h_ref):
    @pl.when(pl.program_id(2) == 0)
    def _():
        h_ref[...] = jnp.zeros_like(h_ref)
        maps_ref[...] = jnp.zeros_like(maps_ref)


def _apply_chunk_maps(d, maps_ref, h_ref, y_ref, same):
    chains = [(i, pr) for i in range(CHUNK_BATCH) for pr in range(N_PAIRS)]
    states = [h_ref[i, pr] for i, pr in chains]
    for k in range(CHUNKS_PER_BLOCK):
        row0 = pl.multiple_of(jnp.where(d == 0, k, CHUNKS_PER_BLOCK - 1 - k) * SCAN_CHUNK, SCAN_CHUNK)
        rows = pl.ds(row0, SCAN_CHUNK)
        part = lambda i, pr, n: maps_ref[i, rows, MAP_W * pr + PAIR * n:MAP_W * pr + PAIR * (n + 1)]
        outs = [_mm_pk(jnp.concatenate([part(i, pr, 0), part(i, pr, 2)], axis=0), [h], same)
                for (i, pr), h in zip(chains, states)]
        states = [out[0:SCAN_CHUNK] + part(i, pr, 1) for out, (i, pr) in zip(outs, chains)]
        for out, (i, pr) in zip(outs, chains):
            y_ref[0, i, rows, PAIR * pr:PAIR * (pr + 1)] = out[SCAN_CHUNK:] + part(i, pr, 3)
    for (i, pr), h in zip(chains, states):
        h_ref[i, pr] = h


def _chunk_scan_call(kernel, name, operands, in_specs, nb, t, nblk_ctx):
    nblk = t // TOKEN_BLOCK
    order = _chunk_order(nblk_ctx, nblk)
    blk_in = lambda d, j: order(d, jnp.minimum(j, nblk - 1))
    blk_out = lambda d, j: order(d, jnp.maximum(j - 1, 0))
    return pl.pallas_call(
        kernel,
        out_shape=jax.ShapeDtypeStruct((2, nb, t, BRANCH_W), F32),
        grid=(2, nb // CHUNK_BATCH, nblk + 1),
        in_specs=in_specs(blk_in),
        out_specs=pl.BlockSpec((1, CHUNK_BATCH, TOKEN_BLOCK, BRANCH_W), lambda d, b, j: (d, b, blk_out(d, j), 0)),
        scratch_shapes=[pltpu.VMEM((CHUNK_BATCH, TOKEN_BLOCK, N_PAIRS * MAP_W), F32),
                        pltpu.VMEM((CHUNK_BATCH, N_PAIRS, HEAD_DIM, PAIR), F32)],
        compiler_params=_cparams(("parallel", "parallel", "arbitrary"), VMEM_LIMIT),
        name=name,
    )(*operands)


def _rwkv_block_inputs(p, w0, w_up, a0, a_up, k_k, k_a, ones, time_ops):
    r = p[:, 0:256]
    k = p[:, 256:512]
    v = p[:, 512:768]
    wl = p[:, 768:800]
    al = p[:, 800:832]
    lw = -RW_DECAY_SCALE * _sigmoid(w0 + _dot(jnp.tanh(wl), w_up))
    a = _sigmoid(a0 + _dot(al, a_up))
    kkp = k * k_k
    kk = kkp * lax.rsqrt(_mm3(kkp * kkp, ones) + EPS)
    kmod = k * (1.0 + (a - 1.0) * k_a)
    alpha = -(a * kk)
    sums = _select_rows(time_ops, lw)
    g = sums[0:TOKEN_BLOCK]
    g_tot = sums[TOKEN_BLOCK:]
    e_neg = jnp.exp(-g)
    e_tail = jnp.exp(g_tot - g)
    return dict(b=kk * jnp.exp(g - lw), r=r * jnp.exp(g), kh=kmod * e_neg, ah=alpha * e_neg,
                kt=kmod * e_tail, at=alpha * e_tail, v=v, gam=jnp.exp(g_tot))


def _rwkv_chunk_kernel(p_ref, w0_ref, wup_ref, a0_ref, aup_ref, kk_ref, ka_ref, ones_ref, y_ref, o_ref, h_ref):
    d = pl.program_id(0)
    ops = _block_time_operators(d)
    same, incl, strict, eye = _pair_masks(d)
    eye_f = jnp.where(eye, 1.0, 0.0).astype(F32)
    zeros = jnp.zeros((SCAN_CHUNK, PAIR), F32)
    _reset_chunk_state(o_ref, h_ref)
    pre = [_rwkv_block_inputs(p_ref[i], w0_ref[0], wup_ref[0], a0_ref[0], aup_ref[0], kk_ref[...], ka_ref[...],
                              ones_ref[...], ops) for i in range(CHUNK_BATCH)]
    probs = _chunk_problems()
    pk = lambda name: [pre[i][name][rows, lanes] for i, rows, lanes in probs]
    cat = jnp.concatenate
    b_t, r_t, k_h, a_h, k_t, a_t, v_p = pk("b"), pk("r"), pk("kh"), pk("ah"), pk("kt"), pk("at"), pk("v")
    m = [_mm_pk_nt(cat([b, r_], axis=0), [kh, ah], same) for b, r_, kh, ah in zip(b_t, r_t, k_h, a_h)]
    _apply_chunk_maps(d, o_ref, h_ref, y_ref, same)
    a_bk =[jnp.where(strict, x[0:SCAN_CHUNK, 0:PAIR], 0.0) for x in m]
    a_ba = [jnp.where(strict, x[0:SCAN_CHUNK, PAIR:], 0.0) for x in m]
    a_rk = [jnp.where(incl, x[SCAN_CHUNK:, 0:PAIR], 0.0) for x in m]
    a_ra = [jnp.where(incl, x[SCAN_CHUNK:, PAIR:], 0.0) for x in m]
    av = [_mm_pk(cat([x, y], axis=0), [vp], same) for x, y, vp in zip(a_bk, a_rk, v_p)]
    t_inv = _unit_tri_inverse(a_ba, eye_f, same)
    sol = [_mm_pk(t, [b, x[0:SCAN_CHUNK]], same) for t, b, x in zip(t_inv, b_t, av)]
    qy = [cat([r_, x[SCAN_CHUNK:]], axis=1) + _mm_pk_fast(a, [s[:, 0:PAIR], s[:, PAIR:]], same)
          for r_, x, a, s in zip(r_t, av, a_ra, sol)]
    kb = [_mm3(cat([at, kt], axis=0), cat([s, cat([zeros, vp], axis=1)], axis=0), _TN)
          for at, kt, s, vp in zip(a_t, k_t, sol, v_p)]
    for (i, rows, lanes), kb_i, qy_i in zip(probs, kb, qy):
        base = MAP_W * (lanes.start // PAIR)
        o_ref[i, rows, base:base + PAIR] = (_pack(kb_i[:, 0:PAIR], same)
                                               + jnp.where(eye, pre[i]["gam"][rows.start:rows.start + 1, lanes], 0.0))
        o_ref[i, rows, base + PAIR:base + 2 * PAIR] = _pack(kb_i[:, PAIR:], same)
        o_ref[i, rows, base + 2 * PAIR:base + 4 * PAIR] = qy_i


def _rwkv_scan(ps, w0, w_up, a0, a_up, k_k, k_a, ones_bd, nblk_ctx):
    nb, t, _ = ps.shape
    per_dir = lambda shape: pl.BlockSpec((1,) + shape, lambda d, b, j: (d, 0, 0))
    const = lambda shape: pl.BlockSpec(shape, lambda d, b, j: (0, 0))
    in_specs = lambda blk: [pl.BlockSpec((CHUNK_BATCH, TOKEN_BLOCK, RW_COLS), lambda d, b, j: (b, blk(d, j), 0)),
                            per_dir((1, BRANCH_W)), per_dir((32, BRANCH_W)),
                            per_dir((1, BRANCH_W)), per_dir((32, BRANCH_W)),
                            const((1, BRANCH_W)), const((1, BRANCH_W)), const((BRANCH_W, BRANCH_W))]
    operands = (ps, w0.reshape(2, 1, BRANCH_W), w_up, a0.reshape(2, 1, BRANCH_W), a_up,
                k_k.reshape(1, BRANCH_W), k_a.reshape(1, BRANCH_W), ones_bd)
    return _chunk_scan_call(_rwkv_chunk_kernel, "rwkv7_scan", operands, in_specs, nb, t, nblk_ctx)


def _deltanet_block_inputs(u, dab, neg_exp_a, dt_bias, expand_a, expand_b, ones, time_ops):
    log_a8 = neg_exp_a * jax.nn.softplus(dab + dt_bias)
    la = _spread_cols(log_a8, expand_a)
    beta = _spread_cols(_sigmoid(dab), expand_b)
    q = u[:, 0:256]
    k = u[:, 256:512]
    v = u[:, 512:768]
    q = q * lax.rsqrt(_mm3(q * q, ones) + EPS) * (HEAD_DIM ** -0.5)
    k = k * lax.rsqrt(_mm3(k * k, ones) + EPS)
    sums = _select_rows(time_ops, la)
    g = sums[0:TOKEN_BLOCK]
    g_tot = sums[TOKEN_BLOCK:]
    e_g = jnp.exp(g)
    kb = k * beta
    return dict(g=g, q=q, k=k, kb=kb, vb=v * beta, kbe=kb * e_g, qe=q * e_g, kt=k * jnp.exp(g_tot - g),
                gam=jnp.exp(g_tot))


def _deltanet_chunk_kernel(u_ref, dab_ref, nea_ref, dtb_ref, ea_ref, eb_ref, ones_ref, y_ref, o_ref, h_ref):
    d = pl.program_id(0)
    ops = _block_time_operators(d)
    same, incl, strict, eye = _pair_masks(d)
    eye_f = jnp.where(eye, 1.0, 0.0).astype(F32)
    _reset_chunk_state(o_ref, h_ref)
    pre = [_deltanet_block_inputs(u_ref[i], dab_ref[i], nea_ref[...], dtb_ref[...], ea_ref[0], eb_ref[0],
                                  ones_ref[...], ops) for i in range(CHUNK_BATCH)]
    probs = _chunk_problems()
    pk = lambda name: [pre[i][name][rows, lanes] for i, rows, lanes in probs]
    cat = jnp.concatenate
    g_p = pk("g")
    g_t = [_pack(_bd(x, same).T, same) for x in g_p]
    decay = [jnp.exp(jnp.where(incl, x - y, -jnp.inf)) for x, y in zip(g_p, g_t)]
    m = [_mm_pk_nt(cat([kb, q_], axis=0), [k_], same) for kb, q_, k_ in zip(pk("kb"), pk("q"), pk("k"))]
    _apply_chunk_maps(d, o_ref, h_ref, y_ref, same)
    a_low =[jnp.where(strict, x[0:SCAN_CHUNK] * dc, 0.0) for x, dc in zip(m, decay)]
    attn = [x[SCAN_CHUNK:] * dc for x, dc in zip(m, decay)]
    t_inv = _unit_tri_inverse([-x for x in a_low], eye_f, same)
    sol = [_mm_pk(t, [vb, kbe], same) for t, vb, kbe in zip(t_inv, pk("vb"), pk("kbe"))]
    att_sol = [_mm_pk_fast(a, [s[:, 0:PAIR], s[:, PAIR:]], same) for a, s in zip(attn, sol)]
    ks = [_mm3(kt, s, _TN) for kt, s in zip(pk("kt"), sol)]
    for (i, rows, lanes), ks_i, as_i, qe_i in zip(probs, ks, att_sol, pk("qe")):
        base = MAP_W * (lanes.start // PAIR)
        o_ref[i, rows, base:base + PAIR] = (jnp.where(eye, pre[i]["gam"][rows.start:rows.start + 1, lanes], 0.0)
                                               - _pack(ks_i[:, PAIR:], same))
        o_ref[i, rows, base + PAIR:base + 2 * PAIR] = _pack(ks_i[:, 0:PAIR], same)
        o_ref[i, rows, base + 2 * PAIR:base + 3 * PAIR] = qe_i - as_i[:, PAIR:]
        o_ref[i, rows, base + 3 * PAIR:base + 4 * PAIR] = as_i[:, 0:PAIR]


def _deltanet_scan(u, dab, a_log, dt_bias, ones_bd, nblk_ctx):
    nb, t, _ = u.shape
    pad = 128 - 2 * N_HEADS
    neg_exp_a = jnp.pad(-jnp.exp(a_log.reshape(1, -1)), ((0, 0), (0, pad)))
    dtb = jnp.pad(dt_bias.reshape(1, -1), ((0, 0), (0, pad)))
    col = jnp.arange(128)[None, :, None]
    head = (jnp.arange(BRANCH_W) // HEAD_DIM)[None, None, :]
    dirs = jnp.arange(2)[:, None, None]
    expand_a = (col == dirs * N_HEADS + head).astype(F32)
    expand_b = (col == 2 * N_HEADS + dirs * N_HEADS + head).astype(F32)
    const = lambda shape: pl.BlockSpec(shape, lambda d, b, j: (0, 0))
    per_dir = pl.BlockSpec((1, 128, BRANCH_W), lambda d, b, j: (d, 0, 0))
    in_specs = lambda blk: [pl.BlockSpec((CHUNK_BATCH, TOKEN_BLOCK, 3 * BRANCH_W), lambda d, b, j: (b, blk(d, j), 0)),
                            pl.BlockSpec((CHUNK_BATCH, TOKEN_BLOCK, 128), lambda d, b, j: (b, blk(d, j), 0)),
                            const((1, 128)), const((1, 128)), per_dir, per_dir, const((BRANCH_W, BRANCH_W))]
    operands = (u, dab, neg_exp_a, dtb, expand_a, expand_b, ones_bd)
    return _chunk_scan_call(_deltanet_chunk_kernel, "deltanet_scan", operands, in_specs, nb, t, nblk_ctx)


def _s5_kernel(n_batch, nc_ctx, nc, u_ref, toep_ref, winr_ref, wini_ref, wsor_ref, wsoi_ref,
               lr_ref, li_ref, dt_ref, y_ref, injr, inji, xsr, xsi):
    d = pl.program_id(1)
    u = u_ref[0]
    ub = u.astype(BF16)
    injr[...] = jnp.dot(ub, winr_ref[0, 0], preferred_element_type=F32)
    inji[...] = jnp.dot(ub, wini_ref[0, 0], preferred_element_type=F32)
    lam_r = lr_ref[0, 0]
    lam_i = li_ref[0, 0]

    def body(s, carry):
        xr, xi = carry
        back = jnp.where(s < nc_ctx, nc_ctx - 1 - s, nc - 1 + nc_ctx - s)
        row0 = jnp.where(d == 0, s, back) * n_batch
        if n_batch % 8 == 0:
            row0 = pl.multiple_of(row0, 8)
        rows = pl.ds(row0, n_batch)
        xsr[rows, :] = xr
        xsi[rows, :] = xi
        return (lam_r * xr - lam_i * xi + injr[rows, :], lam_r * xi + lam_i * xr + inji[rows, :])

    zero = jnp.zeros((n_batch, S5_STATE), F32)
    lax.fori_loop(0, nc, body, (zero, zero))

    y = (jnp.dot(ub, toep_ref[0, 0], preferred_element_type=F32)
         + _dot(xsr[...], wsor_ref[0, 0]) + _dot(xsi[...], wsoi_ref[0, 0]))

    @pl.when(d == 0)
    def _():
        y_ref[0] = y + u * dt_ref[0]

    @pl.when(d == 1)
    def _():
        y_ref[0] += y


def _s5_tables(lam_re, lam_im, log_step, b_re, b_im, c_re, c_im):
    cs = S5_CHUNK
    lam = lax.complex(lam_re.astype(F32), lam_im.astype(F32))
    step = jnp.exp(log_step.astype(F32))[..., None]
    tau = jnp.arange(cs + 1, dtype=F32)[:, None, None, None]
    lam_pow = jnp.exp(lam[None] * step[None] * tau)
    lam_bar = lam_pow[1]
    b_bar = ((lam_bar - 1.0) / lam)[..., None] * lax.complex(b_re.astype(F32), b_im.astype(F32))
    c_mat = lax.complex(c_re.astype(F32), c_im.astype(F32))
    kern = jnp.real(jnp.einsum("dghp,tdgp,dgpk->tdghk", c_mat, lam_pow[:cs], b_bar))
    i = jnp.arange(cs)
    lag_f = i[None, :] - i[:, None]
    toeps, winr, wini, wsor, wsoi = [], [], [], [], []
    for d in range(2):
        lag = lag_f if d == 0 else -lag_f
        kd = jnp.where((lag >= 0)[:, :, None, None, None], kern[:, d][jnp.clip(lag, 0, cs - 1)], 0.0)
        toeps.append(kd.transpose(2, 0, 4, 1, 3).reshape(S5_GROUPS, cs * S5_GW, cs * S5_GW))
        pw_in = (cs - 1 - i) if d == 0 else i
        e = lam_pow[pw_in, d][..., None] * b_bar[d][None]
        e = e.transpose(1, 0, 3, 2).reshape(S5_GROUPS, cs * S5_GW, S5_STATE)
        winr.append(jnp.real(e))
        wini.append(jnp.imag(e))
        pw_out = (i + 1) if d == 0 else (cs - i)
        m = c_mat[d][None] * lam_pow[pw_out, d][:, :, None, :]
        m = m.transpose(1, 3, 0, 2).reshape(S5_GROUPS, S5_STATE, cs * S5_GW)
        wsor.append(jnp.real(m))
        wsoi.append(-jnp.imag(m))
    stack = lambda xs, dt: jnp.stack(xs).astype(dt)
    lam_c = lam_pow[cs]
    return (stack(toeps, BF16), stack(winr, BF16), stack(wini, BF16), stack(wsor, BF16), stack(wsoi, BF16),
            jnp.real(lam_c)[:, :, None, :], jnp.imag(lam_c)[:, :, None, :])


def _s5_to_chunks(s5, n_ctx):
    nb, t, _ = s5.shape
    cs, g, hw = S5_CHUNK, S5_GROUPS, S5_GW
    n_lat = t - n_ctx
    rows = n_lat // 64
    c = s5[:, :n_ctx].reshape(nb, n_ctx // cs, cs, g, hw)
    c = c.transpose(3, 1, 0, 2, 4).reshape(g, (n_ctx // cs) * nb, cs * hw)
    l = s5[:, n_ctx:].reshape(nb, rows // cs, cs, 64, g, hw)
    l = l.transpose(4, 3, 1, 0, 2, 5).reshape(g, 64 * (rows // cs) * nb, cs * hw)
    return jnp.concatenate([c, l], axis=1)


def _s5_from_chunks(y, nb, n_ctx, n_lat):
    cs, g, hw = S5_CHUNK, S5_GROUPS, S5_GW
    rows = n_lat // 64
    r_ctx = (n_ctx // cs) * nb
    c = y[:, :r_ctx].reshape(g, n_ctx // cs, nb, cs, hw).transpose(2, 1, 3, 0, 4).reshape(nb, n_ctx, g * hw)
    l = y[:, r_ctx:].reshape(g, 64, rows // cs, nb, cs, hw).transpose(3, 2, 4, 1, 0, 5).reshape(nb, n_lat, g * hw)
    return jnp.concatenate([c, l], axis=1)


def _s5_scan(s5, tables, d_skip, n_ctx):
    nb, t, _ = s5.shape
    u = _s5_to_chunks(s5, n_ctx)
    g, r, w = u.shape
    nc = t // S5_CHUNK
    toep, winr, wini, wsor, wsoi, lr, li = tables
    d_tile = jnp.tile(d_skip.astype(F32).reshape(S5_GROUPS, 1, S5_GW), (1, S5_CHUNK, 1)).reshape(g, 1, w)
    per = lambda a, b: pl.BlockSpec((1, 1, a, b), lambda gi, d: (d, gi, 0, 0))
    y = pl.pallas_call(
        functools.partial(_s5_kernel, nb, n_ctx // S5_CHUNK, nc),
        out_shape=jax.ShapeDtypeStruct((g, r, w), F32),
        grid=(g, 2),
        in_specs=[pl.BlockSpec((1, r, w), lambda gi, d: (gi, 0, 0)),
                  per(w, w), per(w, S5_STATE), per(w, S5_STATE), per(S5_STATE, w), per(S5_STATE, w),
                  per(1, S5_STATE), per(1, S5_STATE),
                  pl.BlockSpec((1, 1, w), lambda gi, d: (gi, 0, 0))],
        out_specs=pl.BlockSpec((1, r, w), lambda gi, d: (gi, 0, 0)),
        scratch_shapes=[pltpu.VMEM((r, S5_STATE), F32) for _ in range(4)],
        compiler_params=_cparams(("parallel", "arbitrary"), VMEM_LIMIT),
        name="s5_scan",
    )(u, toep, winr, wini, wsor, wsoi, lr, li, d_tile)
    return _s5_from_chunks(y, nb, n_ctx, t - n_ctx)


def _merge_kernel(x_ref, mod_ref, g1_ref, ya_ref, yb_ref, ps_ref, oc_ref, dg_ref, y5_ref,
                  wg_ref, wb_ref, wo_ref, avg_ref, ones_ref, rk_ref, gup_ref, lng_ref, lnb_ref,
                  dng_ref, wglu_ref, bglu_ref, o_ref):
    x = x_ref[0]
    h = _norm_mod(x, g1_ref[...], mod_ref[0, 1:2, :], mod_ref[0, 0:1, :]).astype(BF16)
    avg = avg_ref[...]

    ps = ps_ref[0]
    r = ps[:, 0:256]
    k = ps[:, 256:512]
    v = ps[:, 512:768]
    gl = ps[:, 832:896]
    y = yb_ref[0, 0] + yb_ref[1, 0]
    dev = y - _dot(y, avg)
    yn = dev * lax.rsqrt(_dot(dev * dev, avg) + RW_GN_EPS) * lng_ref[...] + lnb_ref[...]
    bonus = _dot(r * k * rk_ref[...], ones_ref[...]) * v
    yb = (yn + bonus) * _dot(_sigmoid(gl), gup_ref[...])

    o = oc_ref[0, 0] + oc_ref[1, 0]
    yc = o * lax.rsqrt(_dot(o * o, avg) + EPS) * dng_ref[...] * _silu(dg_ref[0])

    z = jax.nn.gelu(y5_ref[0])
    yd = z * _sigmoid(_dot(z, wglu_ref[...]) + bglu_ref[...])

    m = jnp.zeros((TOKEN_BLOCK, D_MODEL), F32)
    for i, yi in enumerate((ya_ref[0], yb, yc, yd)):
        gate = _sigmoid(jnp.dot(h, wg_ref[0, :, D_MODEL * i:D_MODEL * (i + 1)], preferred_element_type=F32))
        m = m + gate * _dot(yi, wb_ref[0, i])
    o_ref[0] = x + mod_ref[0, 2:3, :] * _dot(m, wo_ref[0])


def _merge(xcat, mod, g1, ya, yb, ps, oc, dg, y5, w_gate, w_branch, w_out, layer, avg_bd, ones_bd,
           r_k, g_up, ln_g, ln_b, dn_g, w_glu, b_glu, nblk_ctx):
    nb, t, _ = xcat.shape
    nblk = t // TOKEN_BLOCK
    tok = lambda w: pl.BlockSpec((1, TOKEN_BLOCK, w), lambda b, j: (b, j, 0))
    tok2 = lambda w: pl.BlockSpec((2, 1, TOKEN_BLOCK, w), lambda b, j: (0, b, j, 0))
    full = lambda a: pl.BlockSpec(a.shape, lambda b, j: (0,) * a.ndim)
    stacked = (w_gate, w_branch, w_out)
    consts = (avg_bd, ones_bd, r_k, g_up, ln_g, ln_b, dn_g, w_glu, b_glu)
    return pl.pallas_call(
        _merge_kernel,
        out_shape=jax.ShapeDtypeStruct(xcat.shape, F32),
        grid=(nb, nblk),
        in_specs=[tok(D_MODEL), pl.BlockSpec((1, N_MOD, D_MODEL), _mod_index(nblk_ctx, nb)),
                  pl.BlockSpec((1, D_MODEL), lambda b, j: (0, 0)),
                  tok(BRANCH_W), tok2(BRANCH_W), tok(RW_COLS), tok2(BRANCH_W), tok(BRANCH_W), tok(BRANCH_W)]
        + [_layer_spec(a, layer) for a in stacked] + [full(a) for a in consts],
        out_specs=tok(D_MODEL),
        compiler_params=_cparams(("parallel", "arbitrary"), VMEM_LIMIT),
        name="merge_branches",
    )(xcat, mod, g1, ya, yb, ps, oc, dg, y5, *stacked, *consts)


def _route(sel, score):
    s = [sel[e:e + 1, :] for e in range(N_EXPERTS)]
    sc = [score[e:e + 1, :] for e in range(N_EXPERTS)]
    n_groups = N_EXPERTS // EXPERTS_PER_GROUP
    group_score = []
    for g in range(n_groups):
        m = s[EXPERTS_PER_GROUP * g:EXPERTS_PER_GROUP * (g + 1)]
        best = None
        for i in range(EXPERTS_PER_GROUP):
            for j in range(i + 1, EXPERTS_PER_GROUP):
                pair = m[i] + m[j]
                best = pair if best is None else jnp.maximum(best, pair)
        group_score.append(best)
    best_g = jnp.zeros(group_score[0].shape, jnp.int32)
    best_v = group_score[0]
    for g in range(1, n_groups):
        upd = group_score[g] > best_v
        best_g = jnp.where(upd, g, best_g)
        best_v = jnp.where(upd, group_score[g], best_v)
    chosen = []
    den = jnp.zeros_like(best_v)
    for e in range(N_EXPERTS):
        g = e // EXPERTS_PER_GROUP
        rank = jnp.zeros(best_g.shape, jnp.int32)
        for j in range(EXPERTS_PER_GROUP * g, EXPERTS_PER_GROUP * (g + 1)):
            if j == e:
                continue
            ahead = (s[j] > s[e]) if j > e else (s[j] >= s[e])
            rank = rank + jnp.where(ahead, 1, 0)
        pick = jnp.logical_and(best_g == g, rank < 2)
        chosen.append(pick)
        den = den + jnp.where(pick, sc[e], 0.0)
    return jnp.concatenate([jnp.where(chosen[e], sc[e] / den, 0.0) for e in range(N_EXPERTS)], axis=0)


def _moe_kernel(final, x_ref, mod_ref, g2_ref, w1_ref, w3_ref, w2_ref, rwt_ref, rb_ref, exp_ref, fg_ref, o_ref):
    steps = range(MOE_BATCH)
    xs = [x_ref[i] for i in steps]
    hs = [_norm_mod(xs[i], g2_ref[...], mod_ref[i, 4:5, :], mod_ref[i, 3:4, :]) for i in steps]
    scores = [_sigmoid(lax.dot_general(rwt_ref[...], h, _NT, preferred_element_type=F32, precision=HIGHEST))
              for h in hs]
    combs = [_route(sc + rb_ref[...], sc).astype(BF16) for sc in scores]
    hbs = [h.astype(BF16) for h in hs]
    accs = [jnp.zeros((TOKEN_BLOCK, D_MODEL), F32) for _ in steps]
    for q in range(N_EXPERTS // EXPERTS_PER_GROUP):
        experts = range(EXPERTS_PER_GROUP * q, EXPERTS_PER_GROUP * (q + 1))
        cols = slice(D_EXPERT * experts[0], D_EXPERT * (experts[-1] + 1))
        for i in steps:
            cw = lax.dot_general(combs[i], exp_ref[:, cols], _TN, preferred_element_type=F32)
            a1 = jnp.concatenate([jnp.dot(hbs[i], w1_ref[0, e], preferred_element_type=F32) for e in experts], axis=1)
            a3 = jnp.concatenate([jnp.dot(hbs[i], w3_ref[0, e], preferred_element_type=F32) for e in experts], axis=1)
            act = (_silu(a1) * a3 * cw).astype(BF16)
            accs[i] = accs[i] + jnp.dot(act, w2_ref[0, cols, :], preferred_element_type=F32)
    for i in steps:
        y = xs[i] + mod_ref[i, 5:6, :] * accs[i]
        if final:
            y = y * lax.rsqrt(jnp.mean(y * y, axis=-1, keepdims=True) + EPS) * fg_ref[...]
        o_ref[i] = y


def _moe(x1, mod, g2, router_wt, router_b, w1, w3, w2, layer, expand, nblk_ctx, final_g=None):
    nb, t, _ = x1.shape
    nblk = t // TOKEN_BLOCK
    final = final_g is not None
    tok = pl.BlockSpec((MOE_BATCH, TOKEN_BLOCK, D_MODEL), lambda b, j: (b, j, 0))
    full = lambda a: pl.BlockSpec(a.shape, lambda b, j: (0,) * a.ndim, pipeline_mode=pl.Buffered(1))
    fg = (final_g if final else jnp.ones((D_MODEL,), F32)).reshape(1, D_MODEL).astype(F32)
    consts = (router_wt, router_b, expand, fg)
    if final:
        out_shape = jax.ShapeDtypeStruct((nb, t - nblk_ctx * TOKEN_BLOCK, D_MODEL), F32)
        out_spec = pl.BlockSpec((MOE_BATCH, TOKEN_BLOCK, D_MODEL), lambda b, j: (b, jnp.maximum(j - nblk_ctx, 0), 0))
    else:
        out_shape, out_spec = jax.ShapeDtypeStruct(x1.shape, F32), tok
    return pl.pallas_call(
        functools.partial(_moe_kernel, final),
        out_shape=out_shape,
        grid=(nb // MOE_BATCH, nblk),
        in_specs=[tok, pl.BlockSpec((MOE_BATCH, N_MOD, D_MODEL), _mod_index(nblk_ctx, nb, MOE_BATCH)),
                  pl.BlockSpec((1, D_MODEL), lambda b, j: (0, 0))]
        + [_layer_spec(a, layer, pipeline_mode=pl.Buffered(1)) for a in (w1, w3, w2)] + [full(a) for a in consts],
        out_specs=out_spec,
        compiler_params=_cparams(("parallel", "arbitrary"), VMEM_LIMIT),
        name="moe_ffn",
    )(x1, mod, g2, w1, w3, w2, *consts)


def _block_diag_ones(n_blocks, size):
    return jnp.kron(jnp.eye(n_blocks, dtype=F32), jnp.ones((size, size), F32))


def _layer(xcat, cond, n_ctx, layer, lp, wts, router_wt, router_b, consts, final_g=None):
    nb, t, _ = xcat.shape
    nblk_ctx = n_ctx // TOKEN_BLOCK
    mod = _modulation(cond, lp["w_mod"], lp["b_mod"])
    g1 = lp["norm1_g"].reshape(1, D_MODEL)

    fcs, ps, u, dg, s5, dab = _input_projection(xcat, mod, g1, wts["w_mix"], layer, consts["cs"], lp["rw_mu"],
                                                lp["dn_conv"], nblk_ctx)

    ya = jnp.concatenate([_sequence_dft(fcs, 0, n_ctx), _sequence_dft(fcs, n_ctx, t - n_ctx)], axis=1)

    yb = _rwkv_scan(ps, lp["rw_w0"], lp["rw_w_up"], lp["rw_a0"], lp["rw_a_up"], lp["rw_k_k"], lp["rw_k_a"],
                    consts["ones_bd"], nblk_ctx)
    oc = _deltanet_scan(u, dab, lp["dn_a_log"], lp["dn_dt_bias"], consts["ones_bd"], nblk_ctx)
    y5 = _s5_scan(s5, lp["s5_tables"], lp["s5_d"], n_ctx)

    row = lambda a: a.reshape(1, -1).astype(F32)
    x1 = _merge(xcat, mod, g1, ya, yb, ps, oc, dg, y5, wts["w_gate"], wts["w_branch"], wts["w_out"], layer,
                consts["avg_bd"], consts["ones_bd"], row(lp["rw_r_k"]),
                lp["rw_g_up"].astype(BF16), row(lp["rw_ln_g"]), row(lp["rw_ln_b"]),
                row(jnp.tile(lp["dn_norm_g"], N_HEADS)), lp["s5_w_glu"].astype(BF16), row(lp["s5_b_glu"]),
                nblk_ctx)

    return _moe(x1, mod, lp["norm2_g"].reshape(1, D_MODEL), router_wt, router_b, wts["moe_w1"], wts["moe_w3"],
                wts["moe_w2"], layer, consts["expand"], nblk_ctx, final_g)


def _mixer_column_order():
    offs = [0]
    for width in W_IN_SPLITS:
        offs.append(offs[-1] + width)
    order = [0, 1, 2, 3, 6, 4, 5]
    idx = [c for i in order for c in range(offs[i], offs[i + 1])]
    pad = -len(idx) % 128
    keep = [1.0] * len(idx) + [0.0] * pad
    return jnp.asarray(idx + [0] * pad, jnp.int32), jnp.asarray(keep, F32), offs[7]


def kernel(x, c, ctx, c_ctx, w_mod, b_mod, norm1_g, norm2_g, w_in, rw_mu, rw_w0, rw_w_up, rw_a0, rw_a_up, rw_k_k, rw_k_a, rw_r_k, rw_g_up, rw_ln_g, rw_ln_b, dn_conv, dn_a_log, dn_dt_bias, dn_norm_g, s5_lam_re, s5_lam_im, s5_log_step, s5_b_re, s5_b_im, s5_c_re, s5_c_im, s5_d, s5_w_glu, s5_b_glu, w_branch, w_out, router_w, router_b, moe_w1, moe_w3, moe_w2, final_g):
    nb, n_lat, _ = x.shape
    n_ctx = ctx.shape[1]
    depth = w_mod.shape[0]
    assert n_ctx % TOKEN_BLOCK == 0 and n_lat % TOKEN_BLOCK == 0 and (n_lat // 64) % S5_CHUNK == 0
    assert nb % CHUNK_BATCH == 0 and nb % MOE_BATCH == 0

    xcat = jnp.concatenate([ctx, x], axis=1).astype(F32)
    cond_rows = -(-(nb + MOE_BATCH) // 8) * 8
    cond = jnp.zeros((cond_rows, D_MODEL), F32).at[:nb].set(c).at[nb:nb + MOE_BATCH].set(c_ctx)

    j = jnp.arange(FN_GW, dtype=jnp.int32)
    ang = ((j[:, None] * j[None, :]) % FN_GW).astype(F32) * (2.0 * math.pi / FN_GW)
    eye = jnp.eye(BRANCH_W // FN_GW, dtype=F32)
    consts = {
        "cs": jnp.concatenate([jnp.kron(eye, jnp.cos(ang)), jnp.kron(eye, jnp.sin(ang))], axis=1).astype(BF16),
        "ones_bd": _block_diag_ones(N_HEADS, HEAD_DIM).astype(BF16),
        "avg_bd": (_block_diag_ones(N_HEADS, HEAD_DIM) / HEAD_DIM).astype(BF16),
        "expand": jnp.kron(jnp.eye(N_EXPERTS, dtype=F32), jnp.ones((1, D_EXPERT), F32)).astype(BF16),
    }
    router_wt = router_w.T.astype(F32)
    router_bc = router_b.reshape(N_EXPERTS, 1).astype(F32)

    cols, keep, gate0 = _mixer_column_order()
    wts = {
        "w_mix": (jnp.take(w_in, cols, axis=2) * keep).astype(BF16),
        "w_gate": w_in[:, :, gate0:].astype(BF16),
        "w_branch": w_branch.astype(BF16),
        "w_out": w_out.astype(BF16),
        "moe_w1": moe_w1.astype(BF16),
        "moe_w3": moe_w3.astype(BF16),
        "moe_w2": moe_w2.reshape(depth, N_EXPERTS * D_EXPERT, D_MODEL).astype(BF16),
    }
    s5_tables = jax.vmap(_s5_tables)(s5_lam_re, s5_lam_im, s5_log_step, s5_b_re, s5_b_im, s5_c_re, s5_c_im)
    small = dict(w_mod=w_mod, b_mod=b_mod, norm1_g=norm1_g, norm2_g=norm2_g, rw_mu=rw_mu, rw_w0=rw_w0,
                 rw_w_up=rw_w_up, rw_a0=rw_a0, rw_a_up=rw_a_up, rw_k_k=rw_k_k, rw_k_a=rw_k_a, rw_r_k=rw_r_k,
                 rw_g_up=rw_g_up, rw_ln_g=rw_ln_g, rw_ln_b=rw_ln_b, dn_conv=dn_conv, dn_a_log=dn_a_log,
                 dn_dt_bias=dn_dt_bias, dn_norm_g=dn_norm_g, s5_d=s5_d, s5_w_glu=s5_w_glu, s5_b_glu=s5_b_glu)
    for i in range(depth):
        lp = {n: a[i] for n, a in small.items()}
        lp["s5_tables"] = tuple(tb[i] for tb in s5_tables)
        xcat = _layer(xcat, cond, n_ctx, i, lp, wts, router_wt, router_bc, consts,
                      final_g if i == depth - 1 else None)
    return xcat
```

```python
import functools
import math

import jax
import jax.numpy as jnp
from jax import lax
from jax.experimental import pallas as pl
from jax.experimental.pallas import tpu as pltpu

F32 = jnp.float32
BF16 = jnp.bfloat16
HIGHEST = lax.Precision.HIGHEST

D_MODEL = 1024
N_MOD = 6
EPS = 1e-6
BRANCH_W = 256
HEAD_DIM = 64
N_HEADS = 4
FN_GW = 64
RW_COLS = 896
RW_DECAY_SCALE = math.exp(-0.5)
RW_GN_EPS = 64e-5
DN_CONV = 5
S5_GW = 16
S5_GROUPS = 16
S5_STATE = 64
N_EXPERTS = 16
EXPERTS_PER_GROUP = 4
D_EXPERT = 256
W_IN_SPLITS = (256, 896, 768, 256, 8, 8, 256, 4096)

TOKEN_BLOCK = 256
SCAN_CHUNK = 64
S5_CHUNK = 16
HALO = 8
MXU_DEPTH = 256
VMEM_LIMIT = 56 * 1024 * 1024


def _cparams(sem, vmem=None, **kw):
    return pltpu.CompilerParams(dimension_semantics=sem, vmem_limit_bytes=vmem, **kw)


def _dot(a, b):
    return jnp.dot(a.astype(BF16), b.astype(BF16), preferred_element_type=F32)


def _sigmoid(x):
    return jax.nn.sigmoid(x)


def _silu(x):
    return x * jax.nn.sigmoid(x)


def _norm_mod(x, g, scale, shift):
    y = x * lax.rsqrt(jnp.mean(x * x, axis=-1, keepdims=True) + EPS) * g
    return y * (1.0 + scale) + shift


def _layer_spec(a, layer, **kw):
    return pl.BlockSpec((1,) + a.shape[1:], lambda *_: (layer,) + (0,) * (a.ndim - 1), **kw)


def _mod_index(nblk_ctx, n_batch, per_step=1):
    return lambda b, j: (jnp.where(j < nblk_ctx, n_batch // per_step, b), 0, 0)


def _mod_kernel(c_ref, w_ref, b_ref, o_ref):
    o_ref[...] = _dot(_silu(c_ref[...]), w_ref[...]) + b_ref[...]


def _modulation(cond, w_mod, b_mod):
    rows = cond.shape[0]
    n = w_mod.shape[1]
    tn = 512
    out = pl.pallas_call(
        _mod_kernel,
        out_shape=jax.ShapeDtypeStruct((rows, n), F32),
        grid=(n // tn,),
        in_specs=[pl.BlockSpec((rows, D_MODEL), lambda i: (0, 0)),
                  pl.BlockSpec((D_MODEL, tn), lambda i: (0, i)),
                  pl.BlockSpec((1, tn), lambda i: (0, i))],
        out_specs=pl.BlockSpec((rows, tn), lambda i: (0, i)),
        compiler_params=_cparams(("arbitrary",)),
        name="adaln_mod",
    )(cond, w_mod, b_mod.reshape(1, n))
    return out.reshape(rows, N_MOD, D_MODEL)


def _shifted(x, s, halo_prev, halo_next):
    n = x.shape[0]
    if s == 0:
        return x
    rows = lax.broadcasted_iota(jnp.int32, x.shape, 0)
    if s < 0:
        y = pltpu.roll(x, -s, 0)
        for t in range(-s):
            y = jnp.where(rows == t, halo_prev[HALO + s + t:HALO + s + t + 1, :], y)
    else:
        y = pltpu.roll(x, n - s, 0)
        for t in range(s):
            y = jnp.where(rows == n - s + t, halo_next[t:t + 1, :], y)
    return y


def _inproj_kernel(nblk_ctx, nblk, x_ref, xp_ref, xn_ref, mod_ref, g_ref, w_ref, cs_ref, mu_ref, cw_ref,
                   fcs_ref, ps_ref, u_ref, dg_ref, s5_ref, dab_ref):
    j = pl.program_id(1)
    first = jnp.logical_or(j == 0, j == nblk_ctx)
    last = jnp.logical_or(j == nblk_ctx - 1, j == nblk - 1)
    pv = jnp.where(first, 0.0, 1.0).astype(F32)
    nv = jnp.where(last, 0.0, 1.0).astype(F32)

    x_all = jnp.concatenate([xp_ref[0], x_ref[0], xn_ref[0]], axis=0)
    h_all = _norm_mod(x_all, g_ref[...], mod_ref[0, 1:2, :], mod_ref[0, 0:1, :])
    h = h_all[HALO:HALO + TOKEN_BLOCK].astype(BF16)
    h_all = h_all.astype(BF16)

    fn = jnp.dot(h, w_ref[0, :, 0:256], preferred_element_type=F32)
    fcs_ref[0] = _dot(fn, cs_ref[...]).astype(BF16)
    dg_ref[0] = jnp.dot(h, w_ref[0, :, 1920:2176], preferred_element_type=F32)
    s5_ref[0] = jnp.dot(h, w_ref[0, :, 2176:2432], preferred_element_type=F32)
    dab_ref[0] = jnp.dot(h, w_ref[0, :, 2432:2560], preferred_element_type=F32)

    def with_halo(cols):
        p = jnp.dot(h_all, w_ref[0, :, cols], preferred_element_type=F32)
        return p[HALO:HALO + TOKEN_BLOCK], p[0:HALO] * pv, p[HALO + TOKEN_BLOCK:] * nv

    p, hp, hn = with_halo(slice(256, 1152))
    ps_ref[0] = p + mu_ref[0:1, :] * (_shifted(p, -1, hp, hn) - p) + mu_ref[1:2, :] * (_shifted(p, 1, hp, hn) - p)

    q, hp, hn = with_halo(slice(1152, 1920))
    pad = DN_CONV // 2
    acc = cw_ref[pad:pad + 1, :] * q
    for t in range(DN_CONV):
        if t != pad:
            acc = acc + cw_ref[t:t + 1, :] * _shifted(q, t - pad, hp, hn)
    u_ref[0] = _silu(acc)


def _input_projection(xcat, mod, g1, w_mix, layer, cs, mu, conv_w, nblk_ctx):
    nb, t, _ = xcat.shape
    nblk = t // TOKEN_BLOCK
    per = TOKEN_BLOCK // HALO
    last_halo = t // HALO - 1
    widths = (512, RW_COLS, 3 * BRANCH_W, 256, 256, 128)
    dtypes = (BF16, F32, F32, F32, F32, F32)
    tok = lambda w: pl.BlockSpec((1, TOKEN_BLOCK, w), lambda b, j: (b, j, 0))
    full = lambda a: pl.BlockSpec(a.shape, lambda b, j: (0,) * a.ndim)
    return pl.pallas_call(
        functools.partial(_inproj_kernel, nblk_ctx, nblk),
        out_shape=[jax.ShapeDtypeStruct((nb, t, w), dt) for w, dt in zip(widths, dtypes)],
        grid=(nb, nblk),
        in_specs=[tok(D_MODEL),
                  pl.BlockSpec((1, HALO, D_MODEL), lambda b, j: (b, jnp.maximum(j * per - 1, 0), 0)),
                  pl.BlockSpec((1, HALO, D_MODEL), lambda b, j: (b, jnp.minimum((j + 1) * per, last_halo), 0)),
                  pl.BlockSpec((1, N_MOD, D_MODEL), _mod_index(nblk_ctx, nb)),
                  pl.BlockSpec((1, D_MODEL), lambda b, j: (0, 0)),
                  _layer_spec(w_mix, layer), full(cs), full(mu), full(conv_w)],
        out_specs=[tok(w) for w in widths],
        compiler_params=_cparams(("parallel", "arbitrary"), VMEM_LIMIT),
        name="norm1_inproj",
    )(xcat, xcat, xcat, mod, g1, w_mix, cs, mu, conv_w)


def _dft_kernel(n_batch, wc_ref, ws_ref, u_ref, o_ref):
    @pl.when(pl.program_id(1) == 0)
    def _():
        o_ref[...] = jnp.zeros_like(o_ref)

    wc = wc_ref[...]
    ws = ws_ref[...]
    for b in range(n_batch):
        u = u_ref[b]
        o_ref[b] += (jnp.dot(wc, u[:, 0:256], preferred_element_type=F32)
                     + jnp.dot(ws, u[:, 256:512], preferred_element_type=F32))


def _dft_tables(n):
    n2 = math.isqrt(n)
    assert n2 * n2 == n
    k = jnp.arange(n, dtype=jnp.int32)[None, :]
    j = jnp.arange(n2, dtype=jnp.int32)[:, None]
    ang_a = ((j * k) % n2).astype(F32) * (2.0 * math.pi / n2)
    ang_b = ((j * k) % n).astype(F32) * (2.0 * math.pi / n)
    ca, sa, cb, sb = jnp.cos(ang_a)[:, None], jnp.sin(ang_a)[:, None], jnp.cos(ang_b)[None], jnp.sin(ang_b)[None]
    scale = 1.0 / math.sqrt(n * FN_GW)
    cos = ((ca * cb - sa * sb) * scale).reshape(n, n)
    sin = ((sa * cb + ca * sb) * scale).reshape(n, n)
    return cos.astype(BF16), (-sin).astype(BF16)


def _sequence_dft(fcs, row0, n):
    nb = fcs.shape[0]
    wc, ws = _dft_tables(n)
    tm = min(n, 1024)
    tk = min(n, 256)
    assert n % tm == 0 and n % tk == 0 and row0 % tk == 0
    kb0 = row0 // tk
    return pl.pallas_call(
        functools.partial(_dft_kernel, nb),
        out_shape=jax.ShapeDtypeStruct((nb, n, BRANCH_W), F32),
        grid=(n // tm, n // tk),
        in_specs=[pl.BlockSpec((tm, tk), lambda m, k: (m, k)),
                  pl.BlockSpec((tm, tk), lambda m, k: (m, k)),
                  pl.BlockSpec((nb, tk, 512), lambda m, k: (0, k + kb0, 0))],
        out_specs=pl.BlockSpec((nb, tm, BRANCH_W), lambda m, k: (0, m, 0)),
        compiler_params=_cparams(("parallel", "arbitrary"), VMEM_LIMIT),
        name="fourier_seq_dft",
    )(wc, ws, fcs)


def _chunk_order(nc_ctx, nc):
    def order(d, c):
        back = jnp.where(c < nc_ctx, nc_ctx - 1 - c, nc - 1 + nc_ctx - c)
        return jnp.where(d == 0, c, back)
    return order


PAIR = 2 * HEAD_DIM
N_PAIRS = N_HEADS // 2
CHUNKS_PER_BLOCK = TOKEN_BLOCK // SCAN_CHUNK
MAP_W = 4 * PAIR
CHUNK_BATCH = 4
MOE_BATCH = 2
_NN = (((1,), (0,)), ((), ()))
_NT = (((1,), (1,)), ((), ()))
_TN = (((0,), (0,)), ((), ()))
_LOG2_CHUNK = int(math.log2(SCAN_CHUNK))


def _split(x):
    hi = x.astype(BF16)
    return hi, (x - hi.astype(F32)).astype(BF16)


def _mm3(a, b, dims=_NN):
    a_hi, a_lo = _split(a)
    b_hi, b_lo = _split(b)
    dg = lambda x, y: lax.dot_general(x, y, dims, preferred_element_type=F32)
    ca, cb = dims[0][0][0], dims[0][1][0]
    if a.shape[ca] <= MXU_DEPTH // 2:
        return (dg(jnp.concatenate([a_hi, a_lo], axis=ca), jnp.concatenate([b_hi, b_hi], axis=cb))
                + dg(a_hi, b_lo))
    return dg(a_hi, b_hi) + dg(a_hi, b_lo) + dg(a_lo, b_hi)


def _split3(x):
    hi = x.astype(BF16)
    rest = x - hi.astype(F32)
    mid = rest.astype(BF16)
    return hi, mid, (rest - mid.astype(F32)).astype(BF16)


def _select_rows(op, x):
    op = op.astype(BF16)
    hi, mid, lo = _split3(x)
    dot = lambda y: jnp.dot(op, y, preferred_element_type=F32)
    return dot(hi) + dot(mid) + dot(lo)


def _spread_cols(x, sel):
    sel = sel.astype(BF16)
    hi, mid, lo = _split3(x)
    assert x.shape[1] <= MXU_DEPTH // 2
    return (jnp.dot(jnp.concatenate([hi, mid], axis=1), jnp.concatenate([sel, sel], axis=0),
                    preferred_element_type=F32)
            + jnp.dot(lo, sel, preferred_element_type=F32))


def _pair_masks(d):
    row = lax.broadcasted_iota(jnp.int32, (PAIR, PAIR), 0)
    col = lax.broadcasted_iota(jnp.int32, (PAIR, PAIR), 1)
    same = jnp.right_shift(row, _LOG2_CHUNK) == jnp.right_shift(col, _LOG2_CHUNK)
    t_row = lax.broadcasted_iota(jnp.int32, (SCAN_CHUNK, PAIR), 0)
    t_col = jnp.bitwise_and(lax.broadcasted_iota(jnp.int32, (SCAN_CHUNK, PAIR), 1), SCAN_CHUNK - 1)
    delta = (t_row - t_col) * (1 - 2 * d)
    return same, delta >= 0, delta > 0, t_row == t_col


def _block_time_operators(d):
    row = lax.broadcasted_iota(jnp.int32, (TOKEN_BLOCK, TOKEN_BLOCK), 0)
    col = lax.broadcasted_iota(jnp.int32, (TOKEN_BLOCK, TOKEN_BLOCK), 1)
    same = jnp.right_shift(row, _LOG2_CHUNK) == jnp.right_shift(col, _LOG2_CHUNK)
    delta = (jnp.bitwise_and(row, SCAN_CHUNK - 1) - jnp.bitwise_and(col, SCAN_CHUNK - 1)) * (1 - 2 * d)
    cum = jnp.where(jnp.logical_and(same, delta >= 0), 1.0, 0.0).astype(F32)
    return jnp.concatenate([cum, jnp.where(same, 1.0, 0.0).astype(F32)], axis=0)


def _chunk_problems():
    return [(i, slice(SCAN_CHUNK * c, SCAN_CHUNK * (c + 1)), slice(PAIR * pr, PAIR * (pr + 1)))
            for i in range(CHUNK_BATCH) for c in range(CHUNKS_PER_BLOCK) for pr in range(N_PAIRS)]


def _bd(x, same):
    return jnp.where(same, jnp.concatenate([x, x], axis=0), jnp.zeros((), x.dtype))


def _pack(x, same):
    x = jnp.where(same, x, 0.0)
    return x[0:HEAD_DIM] + x[HEAD_DIM:PAIR]


def _mm_pk(a, bs, same):
    a_hi, a_lo = _split(a)
    parts = [_split(b) for b in bs]
    r_hi = jnp.concatenate([_bd(hi, same) for hi, _ in parts], axis=1)
    r_lo = jnp.concatenate([_bd(lo, same) for _, lo in parts], axis=1)
    return (jnp.dot(jnp.concatenate([a_hi, a_lo], axis=1), jnp.concatenate([r_hi, r_hi], axis=0),
                    preferred_element_type=F32)
            + jnp.dot(a_hi, r_lo, preferred_element_type=F32))


def _mm_pk_fast(a, bs, same):
    rhs = jnp.concatenate([_bd(b.astype(BF16), same) for b in bs], axis=1)
    return jnp.dot(a.astype(BF16), rhs, preferred_element_type=F32)


def _mm_pk_nt(a, bs, same):
    a_hi, a_lo = _split(a)
    parts = [_split(b) for b in bs]
    r_hi = jnp.concatenate([_bd(hi, same) for hi, _ in parts], axis=0)
    r_lo = jnp.concatenate([_bd(lo, same) for _, lo in parts], axis=0)
    dg = lambda x, y: lax.dot_general(x, y, _NT, preferred_element_type=F32)
    return dg(jnp.concatenate([a_hi, a_lo], axis=1), jnp.concatenate([r_hi, r_hi], axis=1)) + dg(a_hi, r_lo)


def _unit_tri_inverse(n_pks, eye_pk, same):
    xs = [eye_pk + n for n in n_pks]
    ps = [_mm_pk(n, [n], same) for n in n_pks]
    for level in range(1, _LOG2_CHUNK):
        if level + 1 < _LOG2_CHUNK:
            xps = [_mm_pk(p, [x, p], same) for x, p in zip(xs, ps)]
            xs = [x + xp[:, 0:PAIR] for x, xp in zip(xs, xps)]
            ps = [xp[:, PAIR:2 * PAIR] for xp in xps]
        else:
            xs = [x + _mm_pk(p, [x], same) for x, p in zip(xs, ps)]
    return xs


def _reset_chunk_state(maps_ref, h_ref):
    @pl.when(pl.program_id(2) == 0)
    def _():
        h_ref[...] = jnp.zeros_like(h_ref)
        maps_ref[...] = jnp.zeros_like(maps_ref)


def _apply_chunk_maps(d, maps_ref, h_ref, y_ref, same):
    chains = [(i, pr) for i in range(CHUNK_BATCH) for pr in range(N_PAIRS)]
    states = [h_ref[i, pr] for i, pr in chains]
    for k in range(CHUNKS_PER_BLOCK):
        row0 = pl.multiple_of(jnp.where(d == 0, k, CHUNKS_PER_BLOCK - 1 - k) * SCAN_CHUNK, SCAN_CHUNK)
        rows = pl.ds(row0, SCAN_CHUNK)
        part = lambda i, pr, n: maps_ref[i, rows, MAP_W * pr + PAIR * n:MAP_W * pr + PAIR * (n + 1)]
        outs = [_mm_pk(jnp.concatenate([part(i, pr, 0), part(i, pr, 2)], axis=0), [h], same)
                for (i, pr), h in zip(chains, states)]
        states = [out[0:SCAN_CHUNK] + part(i, pr, 1) for out, (i, pr) in zip(outs, chains)]
        for out, (i, pr) in zip(outs, chains):
            y_ref[0, i, rows, PAIR * pr:PAIR * (pr + 1)] = out[SCAN_CHUNK:] + part(i, pr, 3)
    for (i, pr), h in zip(chains, states):
        h_ref[i, pr] = h


def _chunk_scan_call(kernel, name, operands, in_specs, nb, t, nblk_ctx):
    nblk = t // TOKEN_BLOCK
    order = _chunk_order(nblk_ctx, nblk)
    blk_in = lambda d, j: order(d, jnp.minimum(j, nblk - 1))
    blk_out = lambda d, j: order(d, jnp.maximum(j - 1, 0))
    return pl.pallas_call(
        kernel,
        out_shape=jax.ShapeDtypeStruct((2, nb, t, BRANCH_W), F32),
        grid=(2, nb // CHUNK_BATCH, nblk + 1),
        in_specs=in_specs(blk_in),
        out_specs=pl.BlockSpec((1, CHUNK_BATCH, TOKEN_BLOCK, BRANCH_W), lambda d, b, j: (d, b, blk_out(d, j), 0)),
        scratch_shapes=[pltpu.VMEM((CHUNK_BATCH, TOKEN_BLOCK, N_PAIRS * MAP_W), F32),
                        pltpu.VMEM((CHUNK_BATCH, N_PAIRS, HEAD_DIM, PAIR), F32)],
        compiler_params=_cparams(("parallel", "parallel", "arbitrary"), VMEM_LIMIT),
        name=name,
    )(*operands)


def _rwkv_block_inputs(p, w0, w_up, a0, a_up, k_k, k_a, ones, time_ops):
    r = p[:, 0:256]
    k = p[:, 256:512]
    v = p[:, 512:768]
    wl = p[:, 768:800]
    al = p[:, 800:832]
    lw = -RW_DECAY_SCALE * _sigmoid(w0 + _dot(jnp.tanh(wl), w_up))
    a = _sigmoid(a0 + _dot(al, a_up))
    kkp = k * k_k
    kk = kkp * lax.rsqrt(_mm3(kkp * kkp, ones) + EPS)
    kmod = k * (1.0 + (a - 1.0) * k_a)
    alpha = -(a * kk)
    sums = _select_rows(time_ops, lw)
    g = sums[0:TOKEN_BLOCK]
    g_tot = sums[TOKEN_BLOCK:]
    e_neg = jnp.exp(-g)
    e_tail = jnp.exp(g_tot - g)
    return dict(b=kk * jnp.exp(g - lw), r=r * jnp.exp(g), kh=kmod * e_neg, ah=alpha * e_neg,
                kt=kmod * e_tail, at=alpha * e_tail, v=v, gam=jnp.exp(g_tot))


def _rwkv_chunk_kernel(p_ref, w0_ref, wup_ref, a0_ref, aup_ref, kk_ref, ka_ref, ones_ref, y_ref, o_ref, h_ref):
    d = pl.program_id(0)
    ops = _block_time_operators(d)
    same, incl, strict, eye = _pair_masks(d)
    eye_f = jnp.where(eye, 1.0, 0.0).astype(F32)
    zeros = jnp.zeros((SCAN_CHUNK, PAIR), F32)
    _reset_chunk_state(o_ref, h_ref)
    pre = [_rwkv_block_inputs(p_ref[i], w0_ref[0], wup_ref[0], a0_ref[0], aup_ref[0], kk_ref[...], ka_ref[...],
                              ones_ref[...], ops) for i in range(CHUNK_BATCH)]
    probs = _chunk_problems()
    pk = lambda name: [pre[i][name][rows, lanes] for i, rows, lanes in probs]
    cat = jnp.concatenate
    b_t, r_t, k_h, a_h, k_t, a_t, v_p = pk("b"), pk("r"), pk("kh"), pk("ah"), pk("kt"), pk("at"), pk("v")
    m = [_mm_pk_nt(cat([b, r_], axis=0), [kh, ah], same) for b, r_, kh, ah in zip(b_t, r_t, k_h, a_h)]
    _apply_chunk_maps(d, o_ref, h_ref, y_ref, same)
    a_bk =[jnp.where(strict, x[0:SCAN_CHUNK, 0:PAIR], 0.0) for x in m]
    a_ba = [jnp.where(strict, x[0:SCAN_CHUNK, PAIR:], 0.0) for x in m]
    a_rk = [jnp.where(incl, x[SCAN_CHUNK:, 0:PAIR], 0.0) for x in m]
    a_ra = [jnp.where(incl, x[SCAN_CHUNK:, PAIR:], 0.0) for x in m]
    av = [_mm_pk(cat([x, y], axis=0), [vp], same) for x, y, vp in zip(a_bk, a_rk, v_p)]
    t_inv = _unit_tri_inverse(a_ba, eye_f, same)
    sol = [_mm_pk(t, [b, x[0:SCAN_CHUNK]], same) for t, b, x in zip(t_inv, b_t, av)]
    qy = [cat([r_, x[SCAN_CHUNK:]], axis=1) + _mm_pk_fast(a, [s[:, 0:PAIR], s[:, PAIR:]], same)
          for r_, x, a, s in zip(r_t, av, a_ra, sol)]
    kb = [_mm3(cat([at, kt], axis=0), cat([s, cat([zeros, vp], axis=1)], axis=0), _TN)
          for at, kt, s, vp in zip(a_t, k_t, sol, v_p)]
    for (i, rows, lanes), kb_i, qy_i in zip(probs, kb, qy):
        base = MAP_W * (lanes.start // PAIR)
        o_ref[i, rows, base:base + PAIR] = (_pack(kb_i[:, 0:PAIR], same)
                                               + jnp.where(eye, pre[i]["gam"][rows.start:rows.start + 1, lanes], 0.0))
        o_ref[i, rows, base + PAIR:base + 2 * PAIR] = _pack(kb_i[:, PAIR:], same)
        o_ref[i, rows, base + 2 * PAIR:base + 4 * PAIR] = qy_i


def _rwkv_scan(ps, w0, w_up, a0, a_up, k_k, k_a, ones_bd, nblk_ctx):
    nb, t, _ = ps.shape
    per_dir = lambda shape: pl.BlockSpec((1,) + shape, lambda d, b, j: (d, 0, 0))
    const = lambda shape: pl.BlockSpec(shape, lambda d, b, j: (0, 0))
    in_specs = lambda blk: [pl.BlockSpec((CHUNK_BATCH, TOKEN_BLOCK, RW_COLS), lambda d, b, j: (b, blk(d, j), 0)),
                            per_dir((1, BRANCH_W)), per_dir((32, BRANCH_W)),
                            per_dir((1, BRANCH_W)), per_dir((32, BRANCH_W)),
                            const((1, BRANCH_W)), const((1, BRANCH_W)), const((BRANCH_W, BRANCH_W))]
    operands = (ps, w0.reshape(2, 1, BRANCH_W), w_up, a0.reshape(2, 1, BRANCH_W), a_up,
                k_k.reshape(1, BRANCH_W), k_a.reshape(1, BRANCH_W), ones_bd)
    return _chunk_scan_call(_rwkv_chunk_kernel, "rwkv7_scan", operands, in_specs, nb, t, nblk_ctx)


def _deltanet_block_inputs(u, dab, neg_exp_a, dt_bias, expand_a, expand_b, ones, time_ops):
    log_a8 = neg_exp_a * jax.nn.softplus(dab + dt_bias)
    la = _spread_cols(log_a8, expand_a)
    beta = _spread_cols(_sigmoid(dab), expand_b)
    q = u[:, 0:256]
    k = u[:, 256:512]
    v = u[:, 512:768]
    q = q * lax.rsqrt(_mm3(q * q, ones) + EPS) * (HEAD_DIM ** -0.5)
    k = k * lax.rsqrt(_mm3(k * k, ones) + EPS)
    sums = _select_rows(time_ops, la)
    g = sums[0:TOKEN_BLOCK]
    g_tot = sums[TOKEN_BLOCK:]
    e_g = jnp.exp(g)
    kb = k * beta
    return dict(g=g, q=q, k=k, kb=kb, vb=v * beta, kbe=kb * e_g, qe=q * e_g, kt=k * jnp.exp(g_tot - g),
                gam=jnp.exp(g_tot))


def _deltanet_chunk_kernel(u_ref, dab_ref, nea_ref, dtb_ref, ea_ref, eb_ref, ones_ref, y_ref, o_ref, h_ref):
    d = pl.program_id(0)
    ops = _block_time_operators(d)
    same, incl, strict, eye = _pair_masks(d)
    eye_f = jnp.where(eye, 1.0, 0.0).astype(F32)
    _reset_chunk_state(o_ref, h_ref)
    pre = [_deltanet_block_inputs(u_ref[i], dab_ref[i], nea_ref[...], dtb_ref[...], ea_ref[0], eb_ref[0],
                                  ones_ref[...], ops) for i in range(CHUNK_BATCH)]
    probs = _chunk_problems()
    pk = lambda name: [pre[i][name][rows, lanes] for i, rows, lanes in probs]
    cat = jnp.concatenate
    g_p = pk("g")
    g_t = [_pack(_bd(x, same).T, same) for x in g_p]
    decay = [jnp.exp(jnp.where(incl, x - y, -jnp.inf)) for x, y in zip(g_p, g_t)]
    m = [_mm_pk_nt(cat([kb, q_], axis=0), [k_], same) for kb, q_, k_ in zip(pk("kb"), pk("q"), pk("k"))]
    _apply_chunk_maps(d, o_ref, h_ref, y_ref, same)
    a_low =[jnp.where(strict, x[0:SCAN_CHUNK] * dc, 0.0) for x, dc in zip(m, decay)]
    attn = [x[SCAN_CHUNK:] * dc for x, dc in zip(m, decay)]
    t_inv = _unit_tri_inverse([-x for x in a_low], eye_f, same)
    sol = [_mm_pk(t, [vb, kbe], same) for t, vb, kbe in zip(t_inv, pk("vb"), pk("kbe"))]
    att_sol = [_mm_pk_fast(a, [s[:, 0:PAIR], s[:, PAIR:]], same) for a, s in zip(attn, sol)]
    ks = [_mm3(kt, s, _TN) for kt, s in zip(pk("kt"), sol)]
    for (i, rows, lanes), ks_i, as_i, qe_i in zip(probs, ks, att_sol, pk("qe")):
        base = MAP_W * (lanes.start // PAIR)
        o_ref[i, rows, base:base + PAIR] = (jnp.where(eye, pre[i]["gam"][rows.start:rows.start + 1, lanes], 0.0)
                                               - _pack(ks_i[:, PAIR:], same))
        o_ref[i, rows, base + PAIR:base + 2 * PAIR] = _pack(ks_i[:, 0:PAIR], same)
        o_ref[i, rows, base + 2 * PAIR:base + 3 * PAIR] = qe_i - as_i[:, PAIR:]
        o_ref[i, rows, base + 3 * PAIR:base + 4 * PAIR] = as_i[:, 0:PAIR]


def _deltanet_scan(u, dab, a_log, dt_bias, ones_bd, nblk_ctx):
    nb, t, _ = u.shape
    pad = 128 - 2 * N_HEADS
    neg_exp_a = jnp.pad(-jnp.exp(a_log.reshape(1, -1)), ((0, 0), (0, pad)))
    dtb = jnp.pad(dt_bias.reshape(1, -1), ((0, 0), (0, pad)))
    col = jnp.arange(128)[None, :, None]
    head = (jnp.arange(BRANCH_W) // HEAD_DIM)[None, None, :]
    dirs = jnp.arange(2)[:, None, None]
    expand_a = (col == dirs * N_HEADS + head).astype(F32)
    expand_b = (col == 2 * N_HEADS + dirs * N_HEADS + head).astype(F32)
    const = lambda shape: pl.BlockSpec(shape, lambda d, b, j: (0, 0))
    per_dir = pl.BlockSpec((1, 128, BRANCH_W), lambda d, b, j: (d, 0, 0))
    in_specs = lambda blk: [pl.BlockSpec((CHUNK_BATCH, TOKEN_BLOCK, 3 * BRANCH_W), lambda d, b, j: (b, blk(d, j), 0)),
                            pl.BlockSpec((CHUNK_BATCH, TOKEN_BLOCK, 128), lambda d, b, j: (b, blk(d, j), 0)),
                            const((1, 128)), const((1, 128)), per_dir, per_dir, const((BRANCH_W, BRANCH_W))]
    operands = (u, dab, neg_exp_a, dtb, expand_a, expand_b, ones_bd)
    return _chunk_scan_call(_deltanet_chunk_kernel, "deltanet_scan", operands, in_specs, nb, t, nblk_ctx)


def _s5_kernel(n_batch, nc_ctx, nc, u_ref, toep_ref, winr_ref, wini_ref, wsor_ref, wsoi_ref,
               lr_ref, li_ref, dt_ref, y_ref, injr, inji, xsr, xsi):
    d = pl.program_id(1)
    u = u_ref[0]
    ub = u.astype(BF16)
    injr[...] = jnp.dot(ub, winr_ref[0, 0], preferred_element_type=F32)
    inji[...] = jnp.dot(ub, wini_ref[0, 0], preferred_element_type=F32)
    lam_r = lr_ref[0, 0]
    lam_i = li_ref[0, 0]

    def body(s, carry):
        xr, xi = carry
        back = jnp.where(s < nc_ctx, nc_ctx - 1 - s, nc - 1 + nc_ctx - s)
        row0 = jnp.where(d == 0, s, back) * n_batch
        if n_batch % 8 == 0:
            row0 = pl.multiple_of(row0, 8)
        rows = pl.ds(row0, n_batch)
        xsr[rows, :] = xr
        xsi[rows, :] = xi
        return (lam_r * xr - lam_i * xi + injr[rows, :], lam_r * xi + lam_i * xr + inji[rows, :])

    zero = jnp.zeros((n_batch, S5_STATE), F32)
    lax.fori_loop(0, nc, body, (zero, zero))

    y = (jnp.dot(ub, toep_ref[0, 0], preferred_element_type=F32)
         + _dot(xsr[...], wsor_ref[0, 0]) + _dot(xsi[...], wsoi_ref[0, 0]))

    @pl.when(d == 0)
    def _():
        y_ref[0] = y + u * dt_ref[0]

    @pl.when(d == 1)
    def _():
        y_ref[0] += y


def _s5_tables(lam_re, lam_im, log_step, b_re, b_im, c_re, c_im):
    cs = S5_CHUNK
    lam = lax.complex(lam_re.astype(F32), lam_im.astype(F32))
    step = jnp.exp(log_step.astype(F32))[..., None]
    tau = jnp.arange(cs + 1, dtype=F32)[:, None, None, None]
    lam_pow = jnp.exp(lam[None] * step[None] * tau)
    lam_bar = lam_pow[1]
    b_bar = ((lam_bar - 1.0) / lam)[..., None] * lax.complex(b_re.astype(F32), b_im.astype(F32))
    c_mat = lax.complex(c_re.astype(F32), c_im.astype(F32))
    kern = jnp.real(jnp.einsum("dghp,tdgp,dgpk->tdghk", c_mat, lam_pow[:cs], b_bar))
    i = jnp.arange(cs)
    lag_f = i[None, :] - i[:, None]
    toeps, winr, wini, wsor, wsoi = [], [], [], [], []
    for d in range(2):
        lag = lag_f if d == 0 else -lag_f
        kd = jnp.where((lag >= 0)[:, :, None, None, None], kern[:, d][jnp.clip(lag, 0, cs - 1)], 0.0)
        toeps.append(kd.transpose(2, 0, 4, 1, 3).reshape(S5_GROUPS, cs * S5_GW, cs * S5_GW))
        pw_in = (cs - 1 - i) if d == 0 else i
        e = lam_pow[pw_in, d][..., None] * b_bar[d][None]
        e = e.transpose(1, 0, 3, 2).reshape(S5_GROUPS, cs * S5_GW, S5_STATE)
        winr.append(jnp.real(e))
        wini.append(jnp.imag(e))
        pw_out = (i + 1) if d == 0 else (cs - i)
        m = c_mat[d][None] * lam_pow[pw_out, d][:, :, None, :]
        m = m.transpose(1, 3, 0, 2).reshape(S5_GROUPS, S5_STATE, cs * S5_GW)
        wsor.append(jnp.real(m))
        wsoi.append(-jnp.imag(m))
    stack = lambda xs, dt: jnp.stack(xs).astype(dt)
    lam_c = lam_pow[cs]
    return (stack(toeps, BF16), stack(winr, BF16), stack(wini, BF16), stack(wsor, BF16), stack(wsoi, BF16),
            jnp.real(lam_c)[:, :, None, :], jnp.imag(lam_c)[:, :, None, :])


def _s5_to_chunks(s5, n_ctx):
    nb, t, _ = s5.shape
    cs, g, hw = S5_CHUNK, S5_GROUPS, S5_GW
    n_lat = t - n_ctx
    rows = n_lat // 64
    c = s5[:, :n_ctx].reshape(nb, n_ctx // cs, cs, g, hw)
    c = c.transpose(3, 1, 0, 2, 4).reshape(g, (n_ctx // cs) * nb, cs * hw)
    l = s5[:, n_ctx:].reshape(nb, rows // cs, cs, 64, g, hw)
    l = l.transpose(4, 3, 1, 0, 2, 5).reshape(g, 64 * (rows // cs) * nb, cs * hw)
    return jnp.concatenate([c, l], axis=1)


def _s5_from_chunks(y, nb, n_ctx, n_lat):
    cs, g, hw = S5_CHUNK, S5_GROUPS, S5_GW
    rows = n_lat // 64
    r_ctx = (n_ctx // cs) * nb
    c = y[:, :r_ctx].reshape(g, n_ctx // cs, nb, cs, hw).transpose(2, 1, 3, 0, 4).reshape(nb, n_ctx, g * hw)
    l = y[:, r_ctx:].reshape(g, 64, rows // cs, nb, cs, hw).transpose(3, 2, 4, 1, 0, 5).reshape(nb, n_lat, g * hw)
    return jnp.concatenate([c, l], axis=1)


def _s5_scan(s5, tables, d_skip, n_ctx):
    nb, t, _ = s5.shape
    u = _s5_to_chunks(s5, n_ctx)
    g, r, w = u.shape
    nc = t // S5_CHUNK
    toep, winr, wini, wsor, wsoi, lr, li = tables
    d_tile = jnp.tile(d_skip.astype(F32).reshape(S5_GROUPS, 1, S5_GW), (1, S5_CHUNK, 1)).reshape(g, 1, w)
    per = lambda a, b: pl.BlockSpec((1, 1, a, b), lambda gi, d: (d, gi, 0, 0))
    y = pl.pallas_call(
        functools.partial(_s5_kernel, nb, n_ctx // S5_CHUNK, nc),
        out_shape=jax.ShapeDtypeStruct((g, r, w), F32),
        grid=(g, 2),
        in_specs=[pl.BlockSpec((1, r, w), lambda gi, d: (gi, 0, 0)),
                  per(w, w), per(w, S5_STATE), per(w, S5_STATE), per(S5_STATE, w), per(S5_STATE, w),
                  per(1, S5_STATE), per(1, S5_STATE),
                  pl.BlockSpec((1, 1, w), lambda gi, d: (gi, 0, 0))],
        out_specs=pl.BlockSpec((1, r, w), lambda gi, d: (gi, 0, 0)),
        scratch_shapes=[pltpu.VMEM((r, S5_STATE), F32) for _ in range(4)],
        compiler_params=_cparams(("parallel", "arbitrary"), VMEM_LIMIT),
        name="s5_scan",
    )(u, toep, winr, wini, wsor, wsoi, lr, li, d_tile)
    return _s5_from_chunks(y, nb, n_ctx, t - n_ctx)


def _merge_kernel(x_ref, mod_ref, g1_ref, ya_ref, yb_ref, ps_ref, oc_ref, dg_ref, y5_ref,
                  wg_ref, wb_ref, wo_ref, avg_ref, ones_ref, rk_ref, gup_ref, lng_ref, lnb_ref,
                  dng_ref, wglu_ref, bglu_ref, o_ref):
    x = x_ref[0]
    h = _norm_mod(x, g1_ref[...], mod_ref[0, 1:2, :], mod_ref[0, 0:1, :]).astype(BF16)
    avg = avg_ref[...]

    ps = ps_ref[0]
    r = ps[:, 0:256]
    k = ps[:, 256:512]
    v = ps[:, 512:768]
    gl = ps[:, 832:896]
    y = yb_ref[0, 0] + yb_ref[1, 0]
    dev = y - _dot(y, avg)
    yn = dev * lax.rsqrt(_dot(dev * dev, avg) + RW_GN_EPS) * lng_ref[...] + lnb_ref[...]
    bonus = _dot(r * k * rk_ref[...], ones_ref[...]) * v
    yb = (yn + bonus) * _dot(_sigmoid(gl), gup_ref[...])

    o = oc_ref[0, 0] + oc_ref[1, 0]
    yc = o * lax.rsqrt(_dot(o * o, avg) + EPS) * dng_ref[...] * _silu(dg_ref[0])

    z = jax.nn.gelu(y5_ref[0])
    yd = z * _sigmoid(_dot(z, wglu_ref[...]) + bglu_ref[...])

    m = jnp.zeros((TOKEN_BLOCK, D_MODEL), F32)
    for i, yi in enumerate((ya_ref[0], yb, yc, yd)):
        gate = _sigmoid(jnp.dot(h, wg_ref[0, :, D_MODEL * i:D_MODEL * (i + 1)], preferred_element_type=F32))
        m = m + gate * _dot(yi, wb_ref[0, i])
    o_ref[0] = x + mod_ref[0, 2:3, :] * _dot(m, wo_ref[0])


def _merge(xcat, mod, g1, ya, yb, ps, oc, dg, y5, w_gate, w_branch, w_out, layer, avg_bd, ones_bd,
           r_k, g_up, ln_g, ln_b, dn_g, w_glu, b_glu, nblk_ctx):
    nb, t, _ = xcat.shape
    nblk = t // TOKEN_BLOCK
    tok = lambda w: pl.BlockSpec((1, TOKEN_BLOCK, w), lambda b, j: (b, j, 0))
    tok2 = lambda w: pl.BlockSpec((2, 1, TOKEN_BLOCK, w), lambda b, j: (0, b, j, 0))
    full = lambda a: pl.BlockSpec(a.shape, lambda b, j: (0,) * a.ndim)
    stacked = (w_gate, w_branch, w_out)
    consts = (avg_bd, ones_bd, r_k, g_up, ln_g, ln_b, dn_g, w_glu, b_glu)
    return pl.pallas_call(
        _merge_kernel,
        out_shape=jax.ShapeDtypeStruct(xcat.shape, F32),
        grid=(nb, nblk),
        in_specs=[tok(D_MODEL), pl.BlockSpec((1, N_MOD, D_MODEL), _mod_index(nblk_ctx, nb)),
                  pl.BlockSpec((1, D_MODEL), lambda b, j: (0, 0)),
                  tok(BRANCH_W), tok2(BRANCH_W), tok(RW_COLS), tok2(BRANCH_W), tok(BRANCH_W), tok(BRANCH_W)]
        + [_layer_spec(a, layer) for a in stacked] + [full(a) for a in consts],
        out_specs=tok(D_MODEL),
        compiler_params=_cparams(("parallel", "arbitrary"), VMEM_LIMIT),
        name="merge_branches",
    )(xcat, mod, g1, ya, yb, ps, oc, dg, y5, *stacked, *consts)


def _route(sel, score):
    s = [sel[e:e + 1, :] for e in range(N_EXPERTS)]
    sc = [score[e:e + 1, :] for e in range(N_EXPERTS)]
    n_groups = N_EXPERTS // EXPERTS_PER_GROUP
    group_score = []
    for g in range(n_groups):
        m = s[EXPERTS_PER_GROUP * g:EXPERTS_PER_GROUP * (g + 1)]
        best = None
        for i in range(EXPERTS_PER_GROUP):
            for j in range(i + 1, EXPERTS_PER_GROUP):
                pair = m[i] + m[j]
                best = pair if best is None else jnp.maximum(best, pair)
        group_score.append(best)
    best_g = jnp.zeros(group_score[0].shape, jnp.int32)
    best_v = group_score[0]
    for g in range(1, n_groups):
        upd = group_score[g] > best_v
        best_g = jnp.where(upd, g, best_g)
        best_v = jnp.where(upd, group_score[g], best_v)
    chosen = []
    den = jnp.zeros_like(best_v)
    for e in range(N_EXPERTS):
        g = e // EXPERTS_PER_GROUP
        rank = jnp.zeros(best_g.shape, jnp.int32)
        for j in range(EXPERTS_PER_GROUP * g, EXPERTS_PER_GROUP * (g + 1)):
            if j == e:
                continue
            ahead = (s[j] > s[e]) if j > e else (s[j] >= s[e])
            rank = rank + jnp.where(ahead, 1, 0)
        pick = jnp.logical_and(best_g == g, rank < 2)
        chosen.append(pick)
        den = den + jnp.where(pick, sc[e], 0.0)
    return jnp.concatenate([jnp.where(chosen[e], sc[e] / den, 0.0) for e in range(N_EXPERTS)], axis=0)


def _moe_kernel(final, x_ref, mod_ref, g2_ref, w1_ref, w3_ref, w2_ref, rwt_ref, rb_ref, exp_ref, fg_ref, o_ref):
    steps = range(MOE_BATCH)
    xs = [x_ref[i] for i in steps]
    hs = [_norm_mod(xs[i], g2_ref[...], mod_ref[i, 4:5, :], mod_ref[i, 3:4, :]) for i in steps]
    scores = [_sigmoid(lax.dot_general(rwt_ref[...], h, _NT, preferred_element_type=F32, precision=HIGHEST))
              for h in hs]
    combs = [_route(sc + rb_ref[...], sc).astype(BF16) for sc in scores]
    hbs = [h.astype(BF16) for h in hs]
    accs = [jnp.zeros((TOKEN_BLOCK, D_MODEL), F32) for _ in steps]
    for q in range(N_EXPERTS // EXPERTS_PER_GROUP):
        experts = range(EXPERTS_PER_GROUP * q, EXPERTS_PER_GROUP * (q + 1))
        cols = slice(D_EXPERT * experts[0], D_EXPERT * (experts[-1] + 1))
        for i in steps:
            cw = lax.dot_general(combs[i], exp_ref[:, cols], _TN, preferred_element_type=F32)
            a1 = jnp.concatenate([jnp.dot(hbs[i], w1_ref[0, e], preferred_element_type=F32) for e in experts], axis=1)
            a3 = jnp.concatenate([jnp.dot(hbs[i], w3_ref[0, e], preferred_element_type=F32) for e in experts], axis=1)
            act = (_silu(a1) * a3 * cw).astype(BF16)
            accs[i] = accs[i] + jnp.dot(act, w2_ref[0, cols, :], preferred_element_type=F32)
    for i in steps:
        y = xs[i] + mod_ref[i, 5:6, :] * accs[i]
        if final:
            y = y * lax.rsqrt(jnp.mean(y * y, axis=-1, keepdims=True) + EPS) * fg_ref[...]
        o_ref[i] = y


def _moe(x1, mod, g2, router_wt, router_b, w1, w3, w2, layer, expand, nblk_ctx, final_g=None):
    nb, t, _ = x1.shape
    nblk = t // TOKEN_BLOCK
    final = final_g is not None
    tok = pl.BlockSpec((MOE_BATCH, TOKEN_BLOCK, D_MODEL), lambda b, j: (b, j, 0))
    full = lambda a: pl.BlockSpec(a.shape, lambda b, j: (0,) * a.ndim, pipeline_mode=pl.Buffered(1))
    fg = (final_g if final else jnp.ones((D_MODEL,), F32)).reshape(1, D_MODEL).astype(F32)
    consts = (router_wt, router_b, expand, fg)
    if final:
        out_shape = jax.ShapeDtypeStruct((nb, t - nblk_ctx * TOKEN_BLOCK, D_MODEL), F32)
        out_spec = pl.BlockSpec((MOE_BATCH, TOKEN_BLOCK, D_MODEL), lambda b, j: (b, jnp.maximum(j - nblk_ctx, 0), 0))
    else:
        out_shape, out_spec = jax.ShapeDtypeStruct(x1.shape, F32), tok
    return pl.pallas_call(
        functools.partial(_moe_kernel, final),
        out_shape=out_shape,
        grid=(nb // MOE_BATCH, nblk),
        in_specs=[tok, pl.BlockSpec((MOE_BATCH, N_MOD, D_MODEL), _mod_index(nblk_ctx, nb, MOE_BATCH)),
                  pl.BlockSpec((1, D_MODEL), lambda b, j: (0, 0))]
        + [_layer_spec(a, layer, pipeline_mode=pl.Buffered(1)) for a in (w1, w3, w2)] + [full(a) for a in consts],
        out_specs=out_spec,
        compiler_params=_cparams(("parallel", "arbitrary"), VMEM_LIMIT),
        name="moe_ffn",
    )(x1, mod, g2, w1, w3, w2, *consts)


def _block_diag_ones(n_blocks, size):
    return jnp.kron(jnp.eye(n_blocks, dtype=F32), jnp.ones((size, size), F32))


def _layer(xcat, cond, n_ctx, layer, lp, wts, router_wt, router_b, consts, final_g=None):
    nb, t, _ = xcat.shape
    nblk_ctx = n_ctx // TOKEN_BLOCK
    mod = _modulation(cond, lp["w_mod"], lp["b_mod"])
    g1 = lp["norm1_g"].reshape(1, D_MODEL)

    fcs, ps, u, dg, s5, dab = _input_projection(xcat, mod, g1, wts["w_mix"], layer, consts["cs"], lp["rw_mu"],
                                                lp["dn_conv"], nblk_ctx)

    ya = jnp.concatenate([_sequence_dft(fcs, 0, n_ctx), _sequence_dft(fcs, n_ctx, t - n_ctx)], axis=1)

    yb = _rwkv_scan(ps, lp["rw_w0"], lp["rw_w_up"], lp["rw_a0"], lp["rw_a_up"], lp["rw_k_k"], lp["rw_k_a"],
                    consts["ones_bd"], nblk_ctx)
    oc = _deltanet_scan(u, dab, lp["dn_a_log"], lp["dn_dt_bias"], consts["ones_bd"], nblk_ctx)
    y5 = _s5_scan(s5, lp["s5_tables"], lp["s5_d"], n_ctx)

    row = lambda a: a.reshape(1, -1).astype(F32)
    x1 = _merge(xcat, mod, g1, ya, yb, ps, oc, dg, y5, wts["w_gate"], wts["w_branch"], wts["w_out"], layer,
                consts["avg_bd"], consts["ones_bd"], row(lp["rw_r_k"]),
                lp["rw_g_up"].astype(BF16), row(lp["rw_ln_g"]), row(lp["rw_ln_b"]),
                row(jnp.tile(lp["dn_norm_g"], N_HEADS)), lp["s5_w_glu"].astype(BF16), row(lp["s5_b_glu"]),
                nblk_ctx)

    return _moe(x1, mod, lp["norm2_g"].reshape(1, D_MODEL), router_wt, router_b, wts["moe_w1"], wts["moe_w3"],
                wts["moe_w2"], layer, consts["expand"], nblk_ctx, final_g)


def _mixer_column_order():
    offs = [0]
    for width in W_IN_SPLITS:
        offs.append(offs[-1] + width)
    order = [0, 1, 2, 3, 6, 4, 5]
    idx = [c for i in order for c in range(offs[i], offs[i + 1])]
    pad = -len(idx) % 128
    keep = [1.0] * len(idx) + [0.0] * pad
    return jnp.asarray(idx + [0] * pad, jnp.int32), jnp.asarray(keep, F32), offs[7]


def kernel(x, c, ctx, c_ctx, w_mod, b_mod, norm1_g, norm2_g, w_in, rw_mu, rw_w0, rw_w_up, rw_a0, rw_a_up, rw_k_k, rw_k_a, rw_r_k, rw_g_up, rw_ln_g, rw_ln_b, dn_conv, dn_a_log, dn_dt_bias, dn_norm_g, s5_lam_re, s5_lam_im, s5_log_step, s5_b_re, s5_b_im, s5_c_re, s5_c_im, s5_d, s5_w_glu, s5_b_glu, w_branch, w_out, router_w, router_b, moe_w1, moe_w3, moe_w2, final_g):
    nb, n_lat, _ = x.shape
    n_ctx = ctx.shape[1]
    depth = w_mod.shape[0]
    assert n_ctx % TOKEN_BLOCK == 0 and n_lat % TOKEN_BLOCK == 0 and (n_lat // 64) % S5_CHUNK == 0
    assert nb % CHUNK_BATCH == 0 and nb % MOE_BATCH == 0

    xcat = jnp.concatenate([ctx, x], axis=1).astype(F32)
    cond_rows = -(-(nb + MOE_BATCH) // 8) * 8
    cond = jnp.zeros((cond_rows, D_MODEL), F32).at[:nb].set(c).at[nb:nb + MOE_BATCH].set(c_ctx)

    j = jnp.arange(FN_GW, dtype=jnp.int32)
    ang = ((j[:, None] * j[None, :]) % FN_GW).astype(F32) * (2.0 * math.pi / FN_GW)
    eye = jnp.eye(BRANCH_W // FN_GW, dtype=F32)
    consts = {
        "cs": jnp.concatenate([jnp.kron(eye, jnp.cos(ang)), jnp.kron(eye, jnp.sin(ang))], axis=1).astype(BF16),
        "ones_bd": _block_diag_ones(N_HEADS, HEAD_DIM).astype(BF16),
        "avg_bd": (_block_diag_ones(N_HEADS, HEAD_DIM) / HEAD_DIM).astype(BF16),
        "expand": jnp.kron(jnp.eye(N_EXPERTS, dtype=F32), jnp.ones((1, D_EXPERT), F32)).astype(BF16),
    }
    router_wt = router_w.T.astype(F32)
    router_bc = router_b.reshape(N_EXPERTS, 1).astype(F32)

    cols, keep, gate0 = _mixer_column_order()
    wts = {
        "w_mix": (jnp.take(w_in, cols, axis=2) * keep).astype(BF16),
        "w_gate": w_in[:, :, gate0:].astype(BF16),
        "w_branch": w_branch.astype(BF16),
        "w_out": w_out.astype(BF16),
        "moe_w1": moe_w1.astype(BF16),
        "moe_w3": moe_w3.astype(BF16),
        "moe_w2": moe_w2.reshape(depth, N_EXPERTS * D_EXPERT, D_MODEL).astype(BF16),
    }
    s5_tables = jax.vmap(_s5_tables)(s5_lam_re, s5_lam_im, s5_log_step, s5_b_re, s5_b_im, s5_c_re, s5_c_im)
    small = dict(w_mod=w_mod, b_mod=b_mod, norm1_g=norm1_g, norm2_g=norm2_g, rw_mu=rw_mu, rw_w0=rw_w0,
                 rw_w_up=rw_w_up, rw_a0=rw_a0, rw_a_up=rw_a_up, rw_k_k=rw_k_k, rw_k_a=rw_k_a, rw_r_k=rw_r_k,
                 rw_g_up=rw_g_up, rw_ln_g=rw_ln_g, rw_ln_b=rw_ln_b, dn_conv=dn_conv, dn_a_log=dn_a_log,
                 dn_dt_bias=dn_dt_bias, dn_norm_g=dn_norm_g, s5_d=s5_d, s5_w_glu=s5_w_glu, s5_b_glu=s5_b_glu)
    for i in range(depth):
        lp = {n: a[i] for n, a in small.items()}
        lp["s5_tables"] = tuple(tb[i] for tb in s5_tables)
        xcat = _layer(xcat, cond, n_ctx, i, lp, wts, router_wt, router_bc, consts,
                      final_g if i == depth - 1 else None)
    return xcat
```

```python
import functools
import math

import jax
import jax.numpy as jnp
from jax import lax
from jax.experimental import pallas as pl
from jax.experimental.pallas import tpu as pltpu

F32 = jnp.float32
BF16 = jnp.bfloat16
HIGHEST = lax.Precision.HIGHEST

D_MODEL = 1024
N_MOD = 6
EPS = 1e-6
BRANCH_W = 256
HEAD_DIM = 64
N_HEADS = 4
FN_GW = 64
RW_COLS = 896
RW_DECAY_SCALE = math.exp(-0.5)
RW_GN_EPS = 64e-5
DN_CONV = 5
S5_GW = 16
S5_GROUPS = 16
S5_STATE = 64
N_EXPERTS = 16
EXPERTS_PER_GROUP = 4
D_EXPERT = 256
W_IN_SPLITS = (256, 896, 768, 256, 8, 8, 256, 4096)

TOKEN_BLOCK = 256
SCAN_CHUNK = 64
S5_CHUNK = 16
HALO = 8
MXU_DEPTH = 256
VMEM_LIMIT = 56 * 1024 * 1024


def _cparams(sem, vmem=None, **kw):
    return pltpu.CompilerParams(dimension_semantics=sem, vmem_limit_bytes=vmem, **kw)


def _dot(a, b):
    return jnp.dot(a.astype(BF16), b.astype(BF16), preferred_element_type=F32)


def _sigmoid(x):
    return jax.nn.sigmoid(x)


def _silu(x):
    return x * jax.nn.sigmoid(x)


def _norm_mod(x, g, scale, shift):
    y = x * lax.rsqrt(jnp.mean(x * x, axis=-1, keepdims=True) + EPS) * g
    return y * (1.0 + scale) + shift


def _layer_spec(a, layer, **kw):
    return pl.BlockSpec((1,) + a.shape[1:], lambda *_: (layer,) + (0,) * (a.ndim - 1), **kw)


def _mod_index(nblk_ctx, n_batch, per_step=1):
    return lambda b, j: (jnp.where(j < nblk_ctx, n_batch // per_step, b), 0, 0)


def _mod_kernel(c_ref, w_ref, b_ref, o_ref):
    o_ref[...] = _dot(_silu(c_ref[...]), w_ref[...]) + b_ref[...]


def _modulation(cond, w_mod, b_mod):
    rows = cond.shape[0]
    n = w_mod.shape[1]
    tn = 512
    out = pl.pallas_call(
        _mod_kernel,
        out_shape=jax.ShapeDtypeStruct((rows, n), F32),
        grid=(n // tn,),
        in_specs=[pl.BlockSpec((rows, D_MODEL), lambda i: (0, 0)),
                  pl.BlockSpec((D_MODEL, tn), lambda i: (0, i)),
                  pl.BlockSpec((1, tn), lambda i: (0, i))],
        out_specs=pl.BlockSpec((rows, tn), lambda i: (0, i)),
        compiler_params=_cparams(("arbitrary",)),
        name="adaln_mod",
    )(cond, w_mod, b_mod.reshape(1, n))
    return out.reshape(rows, N_MOD, D_MODEL)


def _shifted(x, s, halo_prev, halo_next):
    n = x.shape[0]
    if s == 0:
        return x
    rows = lax.broadcasted_iota(jnp.int32, x.shape, 0)
    if s < 0:
        y = pltpu.roll(x, -s, 0)
        for t in range(-s):
            y = jnp.where(rows == t, halo_prev[HALO + s + t:HALO + s + t + 1, :], y)
    else:
        y = pltpu.roll(x, n - s, 0)
        for t in range(s):
            y = jnp.where(rows == n - s + t, halo_next[t:t + 1, :], y)
    return y


def _inproj_kernel(nblk_ctx, nblk, x_ref, xp_ref, xn_ref, mod_ref, g_ref, w_ref, cs_ref, mu_ref, cw_ref,
                   fcs_ref, ps_ref, u_ref, dg_ref, s5_ref, dab_ref):
    j = pl.program_id(1)
    first = jnp.logical_or(j == 0, j == nblk_ctx)
    last = jnp.logical_or(j == nblk_ctx - 1, j == nblk - 1)
    pv = jnp.where(first, 0.0, 1.0).astype(F32)
    nv = jnp.where(last, 0.0, 1.0).astype(F32)

    x_all = jnp.concatenate([xp_ref[0], x_ref[0], xn_ref[0]], axis=0)
    h_all = _norm_mod(x_all, g_ref[...], mod_ref[0, 1:2, :], mod_ref[0, 0:1, :])
    h = h_all[HALO:HALO + TOKEN_BLOCK].astype(BF16)
    h_all = h_all.astype(BF16)

    fn = jnp.dot(h, w_ref[0, :, 0:256], preferred_element_type=F32)
    fcs_ref[0] = _dot(fn, cs_ref[...]).astype(BF16)
    dg_ref[0] = jnp.dot(h, w_ref[0, :, 1920:2176], preferred_element_type=F32)
    s5_ref[0] = jnp.dot(h, w_ref[0, :, 2176:2432], preferred_element_type=F32)
    dab_ref[0] = jnp.dot(h, w_ref[0, :, 2432:2560], preferred_element_type=F32)

    def with_halo(cols):
        p = jnp.dot(h_all, w_ref[0, :, cols], preferred_element_type=F32)
        return p[HALO:HALO + TOKEN_BLOCK], p[0:HALO] * pv, p[HALO + TOKEN_BLOCK:] * nv

    p, hp, hn = with_halo(slice(256, 1152))
    ps_ref[0] = p + mu_ref[0:1, :] * (_shifted(p, -1, hp, hn) - p) + mu_ref[1:2, :] * (_shifted(p, 1, hp, hn) - p)

    q, hp, hn = with_halo(slice(1152, 1920))
    pad = DN_CONV // 2
    acc = cw_ref[pad:pad + 1, :] * q
    for t in range(DN_CONV):
        if t != pad:
            acc = acc + cw_ref[t:t + 1, :] * _shifted(q, t - pad, hp, hn)
    u_ref[0] = _silu(acc)


def _input_projection(xcat, mod, g1, w_mix, layer, cs, mu, conv_w, nblk_ctx):
    nb, t, _ = xcat.shape
    nblk = t // TOKEN_BLOCK
    per = TOKEN_BLOCK // HALO
    last_halo = t // HALO - 1
    widths = (512, RW_COLS, 3 * BRANCH_W, 256, 256, 128)
    dtypes = (BF16, F32, F32, F32, F32, F32)
    tok = lambda w: pl.BlockSpec((1, TOKEN_BLOCK, w), lambda b, j: (b, j, 0))
    full = lambda a: pl.BlockSpec(a.shape, lambda b, j: (0,) * a.ndim)
    return pl.pallas_call(
        functools.partial(_inproj_kernel, nblk_ctx, nblk),
        out_shape=[jax.ShapeDtypeStruct((nb, t, w), dt) for w, dt in zip(widths, dtypes)],
        grid=(nb, nblk),
        in_specs=[tok(D_MODEL),
                  pl.BlockSpec((1, HALO, D_MODEL), lambda b, j: (b, jnp.maximum(j * per - 1, 0), 0)),
                  pl.BlockSpec((1, HALO, D_MODEL), lambda b, j: (b, jnp.minimum((j + 1) * per, last_halo), 0)),
                  pl.BlockSpec((1, N_MOD, D_MODEL), _mod_index(nblk_ctx, nb)),
                  pl.BlockSpec((1, D_MODEL), lambda b, j: (0, 0)),
                  _layer_spec(w_mix, layer), full(cs), full(mu), full(conv_w)],
        out_specs=[tok(w) for w in widths],
        compiler_params=_cparams(("parallel", "arbitrary"), VMEM_LIMIT),
        name="norm1_inproj",
    )(xcat, xcat, xcat, mod, g1, w_mix, cs, mu, conv_w)


def _dft_kernel(n_batch, wc_ref, ws_ref, u_ref, o_ref):
    @pl.when(pl.program_id(1) == 0)
    def _():
        o_ref[...] = jnp.zeros_like(o_ref)

    wc = wc_ref[...]
    ws = ws_ref[...]
    for b in range(n_batch):
        u = u_ref[b]
        o_ref[b] += (jnp.dot(wc, u[:, 0:256], preferred_element_type=F32)
                     + jnp.dot(ws, u[:, 256:512], preferred_element_type=F32))


def _dft_tables(n):
    n2 = math.isqrt(n)
    assert n2 * n2 == n
    k = jnp.arange(n, dtype=jnp.int32)[None, :]
    j = jnp.arange(n2, dtype=jnp.int32)[:, None]
    ang_a = ((j * k) % n2).astype(F32) * (2.0 * math.pi / n2)
    ang_b = ((j * k) % n).astype(F32) * (2.0 * math.pi / n)
    ca, sa, cb, sb = jnp.cos(ang_a)[:, None], jnp.sin(ang_a)[:, None], jnp.cos(ang_b)[None], jnp.sin(ang_b)[None]
    scale = 1.0 / math.sqrt(n * FN_GW)
    cos = ((ca * cb - sa * sb) * scale).reshape(n, n)
    sin = ((sa * cb + ca * sb) * scale).reshape(n, n)
    return cos.astype(BF16), (-sin).astype(BF16)


def _sequence_dft(fcs, row0, n):
    nb = fcs.shape[0]
    wc, ws = _dft_tables(n)
    tm = min(n, 1024)
    tk = min(n, 256)
    assert n % tm == 0 and n % tk == 0 and row0 % tk == 0
    kb0 = row0 // tk
    return pl.pallas_call(
        functools.partial(_dft_kernel, nb),
        out_shape=jax.ShapeDtypeStruct((nb, n, BRANCH_W), F32),
        grid=(n // tm, n // tk),
        in_specs=[pl.BlockSpec((tm, tk), lambda m, k: (m, k)),
                  pl.BlockSpec((tm, tk), lambda m, k: (m, k)),
                  pl.BlockSpec((nb, tk, 512), lambda m, k: (0, k + kb0, 0))],
        out_specs=pl.BlockSpec((nb, tm, BRANCH_W), lambda m, k: (0, m, 0)),
        compiler_params=_cparams(("parallel", "arbitrary"), VMEM_LIMIT),
        name="fourier_seq_dft",
    )(wc, ws, fcs)


def _chunk_order(nc_ctx, nc):
    def order(d, c):
        back = jnp.where(c < nc_ctx, nc_ctx - 1 - c, nc - 1 + nc_ctx - c)
        return jnp.where(d == 0, c, back)
    return order


PAIR = 2 * HEAD_DIM
N_PAIRS = N_HEADS // 2
CHUNKS_PER_BLOCK = TOKEN_BLOCK // SCAN_CHUNK
MAP_W = 4 * PAIR
CHUNK_BATCH = 4
MOE_BATCH = 4
_NN = (((1,), (0,)), ((), ()))
_NT = (((1,), (1,)), ((), ()))
_TN = (((0,), (0,)), ((), ()))
_LOG2_CHUNK = int(math.log2(SCAN_CHUNK))


def _split(x):
    hi = x.astype(BF16)
    return hi, (x - hi.astype(F32)).astype(BF16)


def _mm3(a, b, dims=_NN):
    a_hi, a_lo = _split(a)
    b_hi, b_lo = _split(b)
    dg = lambda x, y: lax.dot_general(x, y, dims, preferred_element_type=F32)
    ca, cb = dims[0][0][0], dims[0][1][0]
    if a.shape[ca] <= MXU_DEPTH // 2:
        return (dg(jnp.concatenate([a_hi, a_lo], axis=ca), jnp.concatenate([b_hi, b_hi], axis=cb))
                + dg(a_hi, b_lo))
    return dg(a_hi, b_hi) + dg(a_hi, b_lo) + dg(a_lo, b_hi)


def _split3(x):
    hi = x.astype(BF16)
    rest = x - hi.astype(F32)
    mid = rest.astype(BF16)
    return hi, mid, (rest - mid.astype(F32)).astype(BF16)


def _select_rows(op, x):
    op = op.astype(BF16)
    hi, mid, lo = _split3(x)
    dot = lambda y: jnp.dot(op, y, preferred_element_type=F32)
    return dot(hi) + dot(mid) + dot(lo)


def _spread_cols(x, sel):
    sel = sel.astype(BF16)
    hi, mid, lo = _split3(x)
    assert x.shape[1] <= MXU_DEPTH // 2
    return (jnp.dot(jnp.concatenate([hi, mid], axis=1), jnp.concatenate([sel, sel], axis=0),
                    preferred_element_type=F32)
            + jnp.dot(lo, sel, preferred_element_type=F32))


def _pair_masks(d):
    row = lax.broadcasted_iota(jnp.int32, (PAIR, PAIR), 0)
    col = lax.broadcasted_iota(jnp.int32, (PAIR, PAIR), 1)
    same = jnp.right_shift(row, _LOG2_CHUNK) == jnp.right_shift(col, _LOG2_CHUNK)
    t_row = lax.broadcasted_iota(jnp.int32, (SCAN_CHUNK, PAIR), 0)
    t_col = jnp.bitwise_and(lax.broadcasted_iota(jnp.int32, (SCAN_CHUNK, PAIR), 1), SCAN_CHUNK - 1)
    delta = (t_row - t_col) * (1 - 2 * d)
    return same, delta >= 0, delta > 0, t_row == t_col


def _block_time_operators(d):
    row = lax.broadcasted_iota(jnp.int32, (TOKEN_BLOCK, TOKEN_BLOCK), 0)
    col = lax.broadcasted_iota(jnp.int32, (TOKEN_BLOCK, TOKEN_BLOCK), 1)
    same = jnp.right_shift(row, _LOG2_CHUNK) == jnp.right_shift(col, _LOG2_CHUNK)
    delta = (jnp.bitwise_and(row, SCAN_CHUNK - 1) - jnp.bitwise_and(col, SCAN_CHUNK - 1)) * (1 - 2 * d)
    return jnp.where(jnp.logical_and(same, delta >= 0), 1.0, 0.0).astype(F32)


def _chunk_problems():
    return [(i, slice(SCAN_CHUNK * c, SCAN_CHUNK * (c + 1)), slice(PAIR * pr, PAIR * (pr + 1)))
            for i in range(CHUNK_BATCH) for c in range(CHUNKS_PER_BLOCK) for pr in range(N_PAIRS)]


def _bd(x, same):
    return jnp.where(same, jnp.concatenate([x, x], axis=0), jnp.zeros((), x.dtype))


def _pack(x, same):
    x = jnp.where(same, x, 0.0)
    return x[0:HEAD_DIM] + x[HEAD_DIM:PAIR]


def _mm_pk(a, bs, same):
    a_hi, a_lo = _split(a)
    parts = [_split(b) for b in bs]
    r_hi = jnp.concatenate([_bd(hi, same) for hi, _ in parts], axis=1)
    r_lo = jnp.concatenate([_bd(lo, same) for _, lo in parts], axis=1)
    return (jnp.dot(jnp.concatenate([a_hi, a_lo], axis=1), jnp.concatenate([r_hi, r_hi], axis=0),
                    preferred_element_type=F32)
            + jnp.dot(a_hi, r_lo, preferred_element_type=F32))


def _mm_pk_fast(a, bs, same):
    rhs = jnp.concatenate([_bd(b.astype(BF16), same) for b in bs], axis=1)
    return jnp.dot(a.astype(BF16), rhs, preferred_element_type=F32)


def _mm_pk_nt(a, bs, same):
    a_hi, a_lo = _split(a)
    parts = [_split(b) for b in bs]
    r_hi = jnp.concatenate([_bd(hi, same) for hi, _ in parts], axis=0)
    r_lo = jnp.concatenate([_bd(lo, same) for _, lo in parts], axis=0)
    dg = lambda x, y: lax.dot_general(x, y, _NT, preferred_element_type=F32)
    return dg(jnp.concatenate([a_hi, a_lo], axis=1), jnp.concatenate([r_hi, r_hi], axis=1)) + dg(a_hi, r_lo)


def _head_sums(x, ones):
    hi, lo = _split(x)
    return jnp.dot(hi, ones, preferred_element_type=F32) + jnp.dot(lo, ones, preferred_element_type=F32)


def _chunk_cumsum(d, x):
    g = _select_rows(_block_time_operators(d), x)
    last = [jnp.where(d == 0, g[SCAN_CHUNK * (c + 1) - 1:SCAN_CHUNK * (c + 1)], g[SCAN_CHUNK * c:SCAN_CHUNK * c + 1])
            for c in range(CHUNKS_PER_BLOCK)]
    return g, jnp.concatenate([jnp.broadcast_to(r, (SCAN_CHUNK, r.shape[1])) for r in last], axis=0)


def _unit_tri_inverse(n_pks, eye_pk, same):
    xs = [eye_pk + n for n in n_pks]
    ps = [_mm_pk(n, [n], same) for n in n_pks]
    for level in range(1, _LOG2_CHUNK):
        if level + 1 < _LOG2_CHUNK:
            xps = [_mm_pk(p, [x, p], same) for x, p in zip(xs, ps)]
            xs = [x + xp[:, 0:PAIR] for x, xp in zip(xs, xps)]
            ps = [xp[:, PAIR:2 * PAIR] for xp in xps]
        else:
            xs = [x + _mm_pk(p, [x], same) for x, p in zip(xs, ps)]
    return xs


def _reset_chunk_state(maps_ref, h_ref):
    @pl.when(pl.program_id(2) == 0)
    def _():
        h_ref[...] = jnp.zeros_like(h_ref)
        maps_ref[...] = jnp.zeros_like(maps_ref)


def _apply_chunk_maps(d, maps_ref, h_ref, y_ref, same):
    chains = [(i, pr) for i in range(CHUNK_BATCH) for pr in range(N_PAIRS)]
    states = [h_ref[i, pr] for i, pr in chains]
    for k in range(CHUNKS_PER_BLOCK):
        row0 = pl.multiple_of(jnp.where(d == 0, k, CHUNKS_PER_BLOCK - 1 - k) * SCAN_CHUNK, SCAN_CHUNK)
        rows = pl.ds(row0, SCAN_CHUNK)
        part = lambda i, pr, n: maps_ref[i, rows, MAP_W * pr + PAIR * n:MAP_W * pr + PAIR * (n + 1)]
        outs = [_mm_pk(jnp.concatenate([part(i, pr, 0), part(i, pr, 2)], axis=0), [h], same)
                for (i, pr), h in zip(chains, states)]
        states = [out[0:SCAN_CHUNK] + part(i, pr, 1) for out, (i, pr) in zip(outs, chains)]
        for out, (i, pr) in zip(outs, chains):
            y_ref[0, i, rows, PAIR * pr:PAIR * (pr + 1)] = out[SCAN_CHUNK:] + part(i, pr, 3)
    for (i, pr), h in zip(chains, states):
        h_ref[i, pr] = h


def _chunk_scan_call(kernel, name, operands, in_specs, nb, t, nblk_ctx):
    nblk = t // TOKEN_BLOCK
    order = _chunk_order(nblk_ctx, nblk)
    blk_in = lambda d, j: order(d, jnp.minimum(j, nblk - 1))
    blk_out = lambda d, j: order(d, jnp.maximum(j - 1, 0))
    return pl.pallas_call(
        kernel,
        out_shape=jax.ShapeDtypeStruct((2, nb, t, BRANCH_W), F32),
        grid=(2, nb // CHUNK_BATCH, nblk + 1),
        in_specs=in_specs(blk_in),
        out_specs=pl.BlockSpec((1, CHUNK_BATCH, TOKEN_BLOCK, BRANCH_W), lambda d, b, j: (d, b, blk_out(d, j), 0)),
        scratch_shapes=[pltpu.VMEM((CHUNK_BATCH, TOKEN_BLOCK, N_PAIRS * MAP_W), F32),
                        pltpu.VMEM((CHUNK_BATCH, N_PAIRS, HEAD_DIM, PAIR), F32)],
        compiler_params=_cparams(("parallel", "parallel", "arbitrary"), VMEM_LIMIT),
        name=name,
    )(*operands)


def _rwkv_block_inputs(p, w0, w_up, a0, a_up, k_k, k_a, ones, d):
    r = p[:, 0:256]
    k = p[:, 256:512]
    v = p[:, 512:768]
    wl = p[:, 768:800]
    al = p[:, 800:832]
    lw = -RW_DECAY_SCALE * _sigmoid(w0 + _dot(jnp.tanh(wl), w_up))
    a = _sigmoid(a0 + _dot(al, a_up))
    kkp = k * k_k
    kk = kkp * lax.rsqrt(_head_sums(kkp * kkp, ones) + EPS)
    kmod = k * (1.0 + (a - 1.0) * k_a)
    alpha = -(a * kk)
    g, g_tot = _chunk_cumsum(d, lw)
    e_neg = jnp.exp(-g)
    e_tail = jnp.exp(g_tot - g)
    return dict(b=kk * jnp.exp(g - lw), r=r * jnp.exp(g), kh=kmod * e_neg, ah=alpha * e_neg,
                kt=kmod * e_tail, at=alpha * e_tail, v=v, gam=jnp.exp(g_tot))


def _rwkv_chunk_kernel(p_ref, w0_ref, wup_ref, a0_ref, aup_ref, kk_ref, ka_ref, ones_ref, y_ref, o_ref, h_ref):
    d = pl.program_id(0)
    same, incl, strict, eye = _pair_masks(d)
    eye_f = jnp.where(eye, 1.0, 0.0).astype(F32)
    zeros = jnp.zeros((SCAN_CHUNK, PAIR), F32)
    _reset_chunk_state(o_ref, h_ref)
    pre = [_rwkv_block_inputs(p_ref[i], w0_ref[0], wup_ref[0], a0_ref[0], aup_ref[0], kk_ref[...], ka_ref[...],
                              ones_ref[...], d) for i in range(CHUNK_BATCH)]
    probs = _chunk_problems()
    pk = lambda name: [pre[i][name][rows, lanes] for i, rows, lanes in probs]
    cat = jnp.concatenate
    b_t, r_t, k_h, a_h, k_t, a_t, v_p = pk("b"), pk("r"), pk("kh"), pk("ah"), pk("kt"), pk("at"), pk("v")
    m = [_mm_pk_nt(cat([b, r_], axis=0), [kh, ah], same) for b, r_, kh, ah in zip(b_t, r_t, k_h, a_h)]
    _apply_chunk_maps(d, o_ref, h_ref, y_ref, same)
    a_bk = [jnp.where(strict, x[0:SCAN_CHUNK, 0:PAIR], 0.0) for x in m]
    a_ba = [jnp.where(strict, x[0:SCAN_CHUNK, PAIR:], 0.0) for x in m]
    a_rk = [jnp.where(incl, x[SCAN_CHUNK:, 0:PAIR], 0.0) for x in m]
    a_ra = [jnp.where(incl, x[SCAN_CHUNK:, PAIR:], 0.0) for x in m]
    av = [_mm_pk(cat([x, y], axis=0), [vp], same) for x, y, vp in zip(a_bk, a_rk, v_p)]
    t_inv = _unit_tri_inverse(a_ba, eye_f, same)
    sol = [_mm_pk(t, [b, x[0:SCAN_CHUNK]], same) for t, b, x in zip(t_inv, b_t, av)]
    qy = [cat([r_, x[SCAN_CHUNK:]], axis=1) + _mm_pk_fast(a, [s[:, 0:PAIR], s[:, PAIR:]], same)
          for r_, x, a, s in zip(r_t, av, a_ra, sol)]
    kb = [_mm3(cat([at, kt], axis=0), cat([s, cat([zeros, vp], axis=1)], axis=0), _TN)
          for at, kt, s, vp in zip(a_t, k_t, sol, v_p)]
    for (i, rows, lanes), kb_i, qy_i in zip(probs, kb, qy):
        base = MAP_W * (lanes.start // PAIR)
        o_ref[i, rows, base:base + PAIR] = (_pack(kb_i[:, 0:PAIR], same)
                                               + jnp.where(eye, pre[i]["gam"][rows.start:rows.start + 1, lanes], 0.0))
        o_ref[i, rows, base + PAIR:base + 2 * PAIR] = _pack(kb_i[:, PAIR:], same)
        o_ref[i, rows, base + 2 * PAIR:base + 4 * PAIR] = qy_i


def _rwkv_scan(ps, w0, w_up, a0, a_up, k_k, k_a, ones_bd, nblk_ctx):
    nb, t, _ = ps.shape
    per_dir = lambda shape: pl.BlockSpec((1,) + shape, lambda d, b, j: (d, 0, 0))
    const = lambda shape: pl.BlockSpec(shape, lambda d, b, j: (0, 0))
    in_specs = lambda blk: [pl.BlockSpec((CHUNK_BATCH, TOKEN_BLOCK, RW_COLS), lambda d, b, j: (b, blk(d, j), 0)),
                            per_dir((1, BRANCH_W)), per_dir((32, BRANCH_W)),
                            per_dir((1, BRANCH_W)), per_dir((32, BRANCH_W)),
                            const((1, BRANCH_W)), const((1, BRANCH_W)), const((BRANCH_W, BRANCH_W))]
    operands = (ps, w0.reshape(2, 1, BRANCH_W), w_up, a0.reshape(2, 1, BRANCH_W), a_up,
                k_k.reshape(1, BRANCH_W), k_a.reshape(1, BRANCH_W), ones_bd)
    return _chunk_scan_call(_rwkv_chunk_kernel, "rwkv7_scan", operands, in_specs, nb, t, nblk_ctx)


def _deltanet_block_inputs(u, dab, neg_exp_a, dt_bias, expand_a, expand_b, ones, d):
    log_a8 = neg_exp_a * jax.nn.softplus(dab + dt_bias)
    la = _spread_cols(log_a8, expand_a)
    beta = _spread_cols(_sigmoid(dab), expand_b)
    q = u[:, 0:256]
    k = u[:, 256:512]
    v = u[:, 512:768]
    q = q * lax.rsqrt(_head_sums(q * q, ones) + EPS) * (HEAD_DIM ** -0.5)
    k = k * lax.rsqrt(_head_sums(k * k, ones) + EPS)
    g, g_tot = _chunk_cumsum(d, la)
    e_g = jnp.exp(g)
    kb = k * beta
    return dict(g=g, q=q, k=k, kb=kb, vb=v * beta, kbe=kb * e_g, qe=q * e_g, kt=k * jnp.exp(g_tot - g),
                gam=jnp.exp(g_tot))


def _deltanet_chunk_kernel(u_ref, dab_ref, nea_ref, dtb_ref, ea_ref, eb_ref, ones_ref, y_ref, o_ref, h_ref):
    d = pl.program_id(0)
    same, incl, strict, eye = _pair_masks(d)
    eye_f = jnp.where(eye, 1.0, 0.0).astype(F32)
    _reset_chunk_state(o_ref, h_ref)
    pre = [_deltanet_block_inputs(u_ref[i], dab_ref[i], nea_ref[...], dtb_ref[...], ea_ref[0], eb_ref[0],
                                  ones_ref[...], d) for i in range(CHUNK_BATCH)]
    probs = _chunk_problems()
    pk = lambda name: [pre[i][name][rows, lanes] for i, rows, lanes in probs]
    cat = jnp.concatenate
    g_p = pk("g")
    g_t = [_pack(_bd(x, same).T, same) for x in g_p]
    decay = [jnp.exp(jnp.where(incl, x - y, -jnp.inf)) for x, y in zip(g_p, g_t)]
    m = [_mm_pk_nt(cat([kb, q_], axis=0), [k_], same) for kb, q_, k_ in zip(pk("kb"), pk("q"), pk("k"))]
    _apply_chunk_maps(d, o_ref, h_ref, y_ref, same)
    a_low = [jnp.where(strict, x[0:SCAN_CHUNK] * dc, 0.0) for x, dc in zip(m, decay)]
    attn = [x[SCAN_CHUNK:] * dc for x, dc in zip(m, decay)]
    t_inv = _unit_tri_inverse([-x for x in a_low], eye_f, same)
    sol = [_mm_pk(t, [vb, kbe], same) for t, vb, kbe in zip(t_inv, pk("vb"), pk("kbe"))]
    att_sol = [_mm_pk_fast(a, [s[:, 0:PAIR], s[:, PAIR:]], same) for a, s in zip(attn, sol)]
    ks = [_mm3(kt, s, _TN) for kt, s in zip(pk("kt"), sol)]
    for (i, rows, lanes), ks_i, as_i, qe_i in zip(probs, ks, att_sol, pk("qe")):
        base = MAP_W * (lanes.start // PAIR)
        o_ref[i, rows, base:base + PAIR] = (jnp.where(eye, pre[i]["gam"][rows.start:rows.start + 1, lanes], 0.0)
                                               - _pack(ks_i[:, PAIR:], same))
        o_ref[i, rows, base + PAIR:base + 2 * PAIR] = _pack(ks_i[:, 0:PAIR], same)
        o_ref[i, rows, base + 2 * PAIR:base + 3 * PAIR] = qe_i - as_i[:, PAIR:]
        o_ref[i, rows, base + 3 * PAIR:base + 4 * PAIR] = as_i[:, 0:PAIR]


def _deltanet_scan(u, dab, a_log, dt_bias, ones_bd, nblk_ctx):
    nb, t, _ = u.shape
    pad = 128 - 2 * N_HEADS
    neg_exp_a = jnp.pad(-jnp.exp(a_log.reshape(1, -1)), ((0, 0), (0, pad)))
    dtb = jnp.pad(dt_bias.reshape(1, -1), ((0, 0), (0, pad)))
    col = jnp.arange(128)[None, :, None]
    head = (jnp.arange(BRANCH_W) // HEAD_DIM)[None, None, :]
    dirs = jnp.arange(2)[:, None, None]
    expand_a = (col == dirs * N_HEADS + head).astype(F32)
    expand_b = (col == 2 * N_HEADS + dirs * N_HEADS + head).astype(F32)
    const = lambda shape: pl.BlockSpec(shape, lambda d, b, j: (0, 0))
    per_dir = pl.BlockSpec((1, 128, BRANCH_W), lambda d, b, j: (d, 0, 0))
    in_specs = lambda blk: [pl.BlockSpec((CHUNK_BATCH, TOKEN_BLOCK, 3 * BRANCH_W), lambda d, b, j: (b, blk(d, j), 0)),
                            pl.BlockSpec((CHUNK_BATCH, TOKEN_BLOCK, 128), lambda d, b, j: (b, blk(d, j), 0)),
                            const((1, 128)), const((1, 128)), per_dir, per_dir, const((BRANCH_W, BRANCH_W))]
    operands = (u, dab, neg_exp_a, dtb, expand_a, expand_b, ones_bd)
    return _chunk_scan_call(_deltanet_chunk_kernel, "deltanet_scan", operands, in_specs, nb, t, nblk_ctx)


def _s5_kernel(n_batch, nc_ctx, nc, u_ref, toep_ref, winr_ref, wini_ref, wsor_ref, wsoi_ref,
               lr_ref, li_ref, dt_ref, y_ref, injr, inji, xsr, xsi):
    d = pl.program_id(1)
    u = u_ref[0]
    ub = u.astype(BF16)
    injr[...] = jnp.dot(ub, winr_ref[0, 0], preferred_element_type=F32)
    inji[...] = jnp.dot(ub, wini_ref[0, 0], preferred_element_type=F32)
    lam_r = lr_ref[0, 0]
    lam_i = li_ref[0, 0]

    def body(s, carry):
        xr, xi = carry
        back = jnp.where(s < nc_ctx, nc_ctx - 1 - s, nc - 1 + nc_ctx - s)
        row0 = jnp.where(d == 0, s, back) * n_batch
        if n_batch % 8 == 0:
            row0 = pl.multiple_of(row0, 8)
        rows = pl.ds(row0, n_batch)
        xsr[rows, :] = xr
        xsi[rows, :] = xi
        return (lam_r * xr - lam_i * xi + injr[rows, :], lam_r * xi + lam_i * xr + inji[rows, :])

    zero = jnp.zeros((n_batch, S5_STATE), F32)
    lax.fori_loop(0, nc, body, (zero, zero))

    y = (jnp.dot(ub, toep_ref[0, 0], preferred_element_type=F32)
         + _dot(xsr[...], wsor_ref[0, 0]) + _dot(xsi[...], wsoi_ref[0, 0]))

    @pl.when(d == 0)
    def _():
        y_ref[0] = y + u * dt_ref[0]

    @pl.when(d == 1)
    def _():
        y_ref[0] += y


def _s5_tables(lam_re, lam_im, log_step, b_re, b_im, c_re, c_im):
    cs = S5_CHUNK
    lam = lax.complex(lam_re.astype(F32), lam_im.astype(F32))
    step = jnp.exp(log_step.astype(F32))[..., None]
    tau = jnp.arange(cs + 1, dtype=F32)[:, None, None, None]
    lam_pow = jnp.exp(lam[None] * step[None] * tau)
    lam_bar = lam_pow[1]
    b_bar = ((lam_bar - 1.0) / lam)[..., None] * lax.complex(b_re.astype(F32), b_im.astype(F32))
    c_mat = lax.complex(c_re.astype(F32), c_im.astype(F32))
    kern = jnp.real(jnp.einsum("dghp,tdgp,dgpk->tdghk", c_mat, lam_pow[:cs], b_bar))
    i = jnp.arange(cs)
    lag_f = i[None, :] - i[:, None]
    toeps, winr, wini, wsor, wsoi = [], [], [], [], []
    for d in range(2):
        lag = lag_f if d == 0 else -lag_f
        kd = jnp.where((lag >= 0)[:, :, None, None, None], kern[:, d][jnp.clip(lag, 0, cs - 1)], 0.0)
        toeps.append(kd.transpose(2, 0, 4, 1, 3).reshape(S5_GROUPS, cs * S5_GW, cs * S5_GW))
        pw_in = (cs - 1 - i) if d == 0 else i
        e = lam_pow[pw_in, d][..., None] * b_bar[d][None]
        e = e.transpose(1, 0, 3, 2).reshape(S5_GROUPS, cs * S5_GW, S5_STATE)
        winr.append(jnp.real(e))
        wini.append(jnp.imag(e))
        pw_out = (i + 1) if d == 0 else (cs - i)
        m = c_mat[d][None] * lam_pow[pw_out, d][:, :, None, :]
        m = m.transpose(1, 3, 0, 2).reshape(S5_GROUPS, S5_STATE, cs * S5_GW)
        wsor.append(jnp.real(m))
        wsoi.append(-jnp.imag(m))
    stack = lambda xs, dt: jnp.stack(xs).astype(dt)
    lam_c = lam_pow[cs]
    return (stack(toeps, BF16), stack(winr, BF16), stack(wini, BF16), stack(wsor, BF16), stack(wsoi, BF16),
            jnp.real(lam_c)[:, :, None, :], jnp.imag(lam_c)[:, :, None, :])


def _s5_to_chunks(s5, n_ctx):
    nb, t, _ = s5.shape
    cs, g, hw = S5_CHUNK, S5_GROUPS, S5_GW
    n_lat = t - n_ctx
    rows = n_lat // 64
    c = s5[:, :n_ctx].reshape(nb, n_ctx // cs, cs, g, hw)
    c = c.transpose(3, 1, 0, 2, 4).reshape(g, (n_ctx // cs) * nb, cs * hw)
    l = s5[:, n_ctx:].reshape(nb, rows // cs, cs, 64, g, hw)
    l = l.transpose(4, 3, 1, 0, 2, 5).reshape(g, 64 * (rows // cs) * nb, cs * hw)
    return jnp.concatenate([c, l], axis=1)


def _s5_from_chunks(y, nb, n_ctx, n_lat):
    cs, g, hw = S5_CHUNK, S5_GROUPS, S5_GW
    rows = n_lat // 64
    r_ctx = (n_ctx // cs) * nb
    c = y[:, :r_ctx].reshape(g, n_ctx // cs, nb, cs, hw).transpose(2, 1, 3, 0, 4).reshape(nb, n_ctx, g * hw)
    l = y[:, r_ctx:].reshape(g, 64, rows // cs, nb, cs, hw).transpose(3, 2, 4, 1, 0, 5).reshape(nb, n_lat, g * hw)
    return jnp.concatenate([c, l], axis=1)


def _s5_scan(s5, tables, d_skip, n_ctx):
    nb, t, _ = s5.shape
    u = _s5_to_chunks(s5, n_ctx)
    g, r, w = u.shape
    nc = t // S5_CHUNK
    toep, winr, wini, wsor, wsoi, lr, li = tables
    d_tile = jnp.tile(d_skip.astype(F32).reshape(S5_GROUPS, 1, S5_GW), (1, S5_CHUNK, 1)).reshape(g, 1, w)
    per = lambda a, b: pl.BlockSpec((1, 1, a, b), lambda gi, d: (d, gi, 0, 0))
    y = pl.pallas_call(
        functools.partial(_s5_kernel, nb, n_ctx // S5_CHUNK, nc),
        out_shape=jax.ShapeDtypeStruct((g, r, w), F32),
        grid=(g, 2),
        in_specs=[pl.BlockSpec((1, r, w), lambda gi, d: (gi, 0, 0)),
                  per(w, w), per(w, S5_STATE), per(w, S5_STATE), per(S5_STATE, w), per(S5_STATE, w),
                  per(1, S5_STATE), per(1, S5_STATE),
                  pl.BlockSpec((1, 1, w), lambda gi, d: (gi, 0, 0))],
        out_specs=pl.BlockSpec((1, r, w), lambda gi, d: (gi, 0, 0)),
        scratch_shapes=[pltpu.VMEM((r, S5_STATE), F32) for _ in range(4)],
        compiler_params=_cparams(("parallel", "arbitrary"), VMEM_LIMIT),
        name="s5_scan",
    )(u, toep, winr, wini, wsor, wsoi, lr, li, d_tile)
    return _s5_from_chunks(y, nb, n_ctx, t - n_ctx)


def _merge_kernel(x_ref, mod_ref, g1_ref, ya_ref, yb_ref, ps_ref, oc_ref, dg_ref, y5_ref,
                  wg_ref, wb_ref, wo_ref, avg_ref, ones_ref, rk_ref, gup_ref, lng_ref, lnb_ref,
                  dng_ref, wglu_ref, bglu_ref, o_ref):
    x = x_ref[0]
    h = _norm_mod(x, g1_ref[...], mod_ref[0, 1:2, :], mod_ref[0, 0:1, :]).astype(BF16)
    avg = avg_ref[...]

    ps = ps_ref[0]
    r = ps[:, 0:256]
    k = ps[:, 256:512]
    v = ps[:, 512:768]
    gl = ps[:, 832:896]
    y = yb_ref[0, 0] + yb_ref[1, 0]
    dev = y - _dot(y, avg)
    yn = dev * lax.rsqrt(_dot(dev * dev, avg) + RW_GN_EPS) * lng_ref[...] + lnb_ref[...]
    bonus = _dot(r * k * rk_ref[...], ones_ref[...]) * v
    yb = (yn + bonus) * _dot(_sigmoid(gl), gup_ref[...])

    o = oc_ref[0, 0] + oc_ref[1, 0]
    yc = o * lax.rsqrt(_dot(o * o, avg) + EPS) * dng_ref[...] * _silu(dg_ref[0])

    z = jax.nn.gelu(y5_ref[0])
    yd = z * _sigmoid(_dot(z, wglu_ref[...]) + bglu_ref[...])

    m = jnp.zeros((TOKEN_BLOCK, D_MODEL), F32)
    for i, yi in enumerate((ya_ref[0], yb, yc, yd)):
        gate = _sigmoid(jnp.dot(h, wg_ref[0, :, D_MODEL * i:D_MODEL * (i + 1)], preferred_element_type=F32))
        m = m + gate * _dot(yi, wb_ref[0, i])
    o_ref[0] = x + mod_ref[0, 2:3, :] * _dot(m, wo_ref[0])


def _merge(xcat, mod, g1, ya, yb, ps, oc, dg, y5, w_gate, w_branch, w_out, layer, avg_bd, ones_bd,
           r_k, g_up, ln_g, ln_b, dn_g, w_glu, b_glu, nblk_ctx):
    nb, t, _ = xcat.shape
    nblk = t // TOKEN_BLOCK
    tok = lambda w: pl.BlockSpec((1, TOKEN_BLOCK, w), lambda b, j: (b, j, 0))
    tok2 = lambda w: pl.BlockSpec((2, 1, TOKEN_BLOCK, w), lambda b, j: (0, b, j, 0))
    full = lambda a: pl.BlockSpec(a.shape, lambda b, j: (0,) * a.ndim)
    stacked = (w_gate, w_branch, w_out)
    consts = (avg_bd, ones_bd, r_k, g_up, ln_g, ln_b, dn_g, w_glu, b_glu)
    return pl.pallas_call(
        _merge_kernel,
        out_shape=jax.ShapeDtypeStruct(xcat.shape, F32),
        grid=(nb, nblk),
        in_specs=[tok(D_MODEL), pl.BlockSpec((1, N_MOD, D_MODEL), _mod_index(nblk_ctx, nb)),
                  pl.BlockSpec((1, D_MODEL), lambda b, j: (0, 0)),
                  tok(BRANCH_W), tok2(BRANCH_W), tok(RW_COLS), tok2(BRANCH_W), tok(BRANCH_W), tok(BRANCH_W)]
        + [_layer_spec(a, layer) for a in stacked] + [full(a) for a in consts],
        out_specs=tok(D_MODEL),
        compiler_params=_cparams(("parallel", "arbitrary"), VMEM_LIMIT),
        name="merge_branches",
    )(xcat, mod, g1, ya, yb, ps, oc, dg, y5, *stacked, *consts)


def _route(sel, score):
    s = [sel[e:e + 1, :] for e in range(N_EXPERTS)]
    sc = [score[e:e + 1, :] for e in range(N_EXPERTS)]
    n_groups = N_EXPERTS // EXPERTS_PER_GROUP
    group_score = []
    for g in range(n_groups):
        m = s[EXPERTS_PER_GROUP * g:EXPERTS_PER_GROUP * (g + 1)]
        best = None
        for i in range(EXPERTS_PER_GROUP):
            for j in range(i + 1, EXPERTS_PER_GROUP):
                pair = m[i] + m[j]
                best = pair if best is None else jnp.maximum(best, pair)
        group_score.append(best)
    best_g = jnp.zeros(group_score[0].shape, jnp.int32)
    best_v = group_score[0]
    for g in range(1, n_groups):
        upd = group_score[g] > best_v
        best_g = jnp.where(upd, g, best_g)
        best_v = jnp.where(upd, group_score[g], best_v)
    chosen = []
    den = jnp.zeros_like(best_v)
    for e in range(N_EXPERTS):
        g = e // EXPERTS_PER_GROUP
        rank = jnp.zeros(best_g.shape, jnp.int32)
        for j in range(EXPERTS_PER_GROUP * g, EXPERTS_PER_GROUP * (g + 1)):
            if j == e:
                continue
            ahead = (s[j] > s[e]) if j > e else (s[j] >= s[e])
            rank = rank + jnp.where(ahead, 1, 0)
        pick = jnp.logical_and(best_g == g, rank < 2)
        chosen.append(pick)
        den = den + jnp.where(pick, sc[e], 0.0)
    return jnp.concatenate([jnp.where(chosen[e], sc[e] / den, 0.0) for e in range(N_EXPERTS)], axis=0)


def _moe_kernel(final, x_ref, mod_ref, g2_ref, w1_ref, w3_ref, w2_ref, rwt_ref, rb_ref, exp_ref, fg_ref, o_ref):
    steps = range(MOE_BATCH)
    xs = [x_ref[i] for i in steps]
    hs = [_norm_mod(xs[i], g2_ref[...], mod_ref[i, 4:5, :], mod_ref[i, 3:4, :]) for i in steps]
    scores = [_sigmoid(lax.dot_general(rwt_ref[...], h, _NT, preferred_element_type=F32, precision=HIGHEST))
              for h in hs]
    combs = [_route(sc + rb_ref[...], sc).astype(BF16) for sc in scores]
    hbs = [h.astype(BF16) for h in hs]
    accs = [jnp.zeros((TOKEN_BLOCK, D_MODEL), F32) for _ in steps]
    for q in range(N_EXPERTS // EXPERTS_PER_GROUP):
        experts = range(EXPERTS_PER_GROUP * q, EXPERTS_PER_GROUP * (q + 1))
        cols = slice(D_EXPERT * experts[0], D_EXPERT * (experts[-1] + 1))
        for i in steps:
            cw = lax.dot_general(combs[i], exp_ref[:, cols], _TN, preferred_element_type=F32)
            a1 = jnp.concatenate([jnp.dot(hbs[i], w1_ref[0, e], preferred_element_type=F32) for e in experts], axis=1)
            a3 = jnp.concatenate([jnp.dot(hbs[i], w3_ref[0, e], preferred_element_type=F32) for e in experts], axis=1)
            act = (_silu(a1) * a3 * cw).astype(BF16)
            accs[i] = accs[i] + jnp.dot(act, w2_ref[0, cols, :], preferred_element_type=F32)
    for i in steps:
        y = xs[i] + mod_ref[i, 5:6, :] * accs[i]
        if final:
            y = y * lax.rsqrt(jnp.mean(y * y, axis=-1, keepdims=True) + EPS) * fg_ref[...]
        o_ref[i] = y


def _moe(x1, mod, g2, router_wt, router_b, w1, w3, w2, layer, expand, nblk_ctx, final_g=None):
    nb, t, _ = x1.shape
    nblk = t // TOKEN_BLOCK
    final = final_g is not None
    tok = pl.BlockSpec((MOE_BATCH, TOKEN_BLOCK, D_MODEL), lambda b, j: (b, j, 0))
    full = lambda a: pl.BlockSpec(a.shape, lambda b, j: (0,) * a.ndim, pipeline_mode=pl.Buffered(1))
    fg = (final_g if final else jnp.ones((D_MODEL,), F32)).reshape(1, D_MODEL).astype(F32)
    consts = (router_wt, router_b, expand, fg)
    if final:
        out_shape = jax.ShapeDtypeStruct((nb, t - nblk_ctx * TOKEN_BLOCK, D_MODEL), F32)
        out_spec = pl.BlockSpec((MOE_BATCH, TOKEN_BLOCK, D_MODEL), lambda b, j: (b, jnp.maximum(j - nblk_ctx, 0), 0))
    else:
        out_shape, out_spec = jax.ShapeDtypeStruct(x1.shape, F32), tok
    return pl.pallas_call(
        functools.partial(_moe_kernel, final),
        out_shape=out_shape,
        grid=(nb // MOE_BATCH, nblk),
        in_specs=[tok, pl.BlockSpec((MOE_BATCH, N_MOD, D_MODEL), _mod_index(nblk_ctx, nb, MOE_BATCH)),
                  pl.BlockSpec((1, D_MODEL), lambda b, j: (0, 0))]
        + [_layer_spec(a, layer, pipeline_mode=pl.Buffered(1)) for a in (w1, w3, w2)] + [full(a) for a in consts],
        out_specs=out_spec,
        compiler_params=_cparams(("parallel", "arbitrary"), VMEM_LIMIT),
        name="moe_ffn",
    )(x1, mod, g2, w1, w3, w2, *consts)


def _block_diag_ones(n_blocks, size):
    return jnp.kron(jnp.eye(n_blocks, dtype=F32), jnp.ones((size, size), F32))


def _layer(xcat, cond, n_ctx, layer, lp, wts, router_wt, router_b, consts, final_g=None):
    nb, t, _ = xcat.shape
    nblk_ctx = n_ctx // TOKEN_BLOCK
    mod = _modulation(cond, lp["w_mod"], lp["b_mod"])
    g1 = lp["norm1_g"].reshape(1, D_MODEL)

    fcs, ps, u, dg, s5, dab = _input_projection(xcat, mod, g1, wts["w_mix"], layer, consts["cs"], lp["rw_mu"],
                                                lp["dn_conv"], nblk_ctx)

    ya = jnp.concatenate([_sequence_dft(fcs, 0, n_ctx), _sequence_dft(fcs, n_ctx, t - n_ctx)], axis=1)

    yb = _rwkv_scan(ps, lp["rw_w0"], lp["rw_w_up"], lp["rw_a0"], lp["rw_a_up"], lp["rw_k_k"], lp["rw_k_a"],
                    consts["ones_bd"], nblk_ctx)
    oc = _deltanet_scan(u, dab, lp["dn_a_log"], lp["dn_dt_bias"], consts["ones_bd"], nblk_ctx)
    y5 = _s5_scan(s5, lp["s5_tables"], lp["s5_d"], n_ctx)

    row = lambda a: a.reshape(1, -1).astype(F32)
    x1 = _merge(xcat, mod, g1, ya, yb, ps, oc, dg, y5, wts["w_gate"], wts["w_branch"], wts["w_out"], layer,
                consts["avg_bd"], consts["ones_bd"], row(lp["rw_r_k"]),
                lp["rw_g_up"].astype(BF16), row(lp["rw_ln_g"]), row(lp["rw_ln_b"]),
                row(jnp.tile(lp["dn_norm_g"], N_HEADS)), lp["s5_w_glu"].astype(BF16), row(lp["s5_b_glu"]),
                nblk_ctx)

    return _moe(x1, mod, lp["norm2_g"].reshape(1, D_MODEL), router_wt, router_b, wts["moe_w1"], wts["moe_w3"],
                wts["moe_w2"], layer, consts["expand"], nblk_ctx, final_g)


def _mixer_column_order():
    offs = [0]
    for width in W_IN_SPLITS:
        offs.append(offs[-1] + width)
    order = [0, 1, 2, 3, 6, 4, 5]
    idx = [c for i in order for c in range(offs[i], offs[i + 1])]
    pad = -len(idx) % 128
    keep = [1.0] * len(idx) + [0.0] * pad
    return jnp.asarray(idx + [0] * pad, jnp.int32), jnp.asarray(keep, F32), offs[7]


def kernel(x, c, ctx, c_ctx, w_mod, b_mod, norm1_g, norm2_g, w_in, rw_mu, rw_w0, rw_w_up, rw_a0, rw_a_up, rw_k_k, rw_k_a, rw_r_k, rw_g_up, rw_ln_g, rw_ln_b, dn_conv, dn_a_log, dn_dt_bias, dn_norm_g, s5_lam_re, s5_lam_im, s5_log_step, s5_b_re, s5_b_im, s5_c_re, s5_c_im, s5_d, s5_w_glu, s5_b_glu, w_branch, w_out, router_w, router_b, moe_w1, moe_w3, moe_w2, final_g):
    nb, n_lat, _ = x.shape
    n_ctx = ctx.shape[1]
    depth = w_mod.shape[0]
    assert n_ctx % TOKEN_BLOCK == 0 and n_lat % TOKEN_BLOCK == 0 and (n_lat // 64) % S5_CHUNK == 0
    assert nb % CHUNK_BATCH == 0 and nb % MOE_BATCH == 0

    xcat = jnp.concatenate([ctx, x], axis=1).astype(F32)
    cond_rows = -(-(nb + MOE_BATCH) // 8) * 8
    cond = jnp.zeros((cond_rows, D_MODEL), F32).at[:nb].set(c).at[nb:nb + MOE_BATCH].set(c_ctx)

    j = jnp.arange(FN_GW, dtype=jnp.int32)
    ang = ((j[:, None] * j[None, :]) % FN_GW).astype(F32) * (2.0 * math.pi / FN_GW)
    eye = jnp.eye(BRANCH_W // FN_GW, dtype=F32)
    consts = {
        "cs": jnp.concatenate([jnp.kron(eye, jnp.cos(ang)), jnp.kron(eye, jnp.sin(ang))], axis=1).astype(BF16),
        "ones_bd": _block_diag_ones(N_HEADS, HEAD_DIM).astype(BF16),
        "avg_bd": (_block_diag_ones(N_HEADS, HEAD_DIM) / HEAD_DIM).astype(BF16),
        "expand": jnp.kron(jnp.eye(N_EXPERTS, dtype=F32), jnp.ones((1, D_EXPERT), F32)).astype(BF16),
    }
    router_wt = router_w.T.astype(F32)
    router_bc = router_b.reshape(N_EXPERTS, 1).astype(F32)

    cols, keep, gate0 = _mixer_column_order()
    wts = {
        "w_mix": (jnp.take(w_in, cols, axis=2) * keep).astype(BF16),
        "w_gate": w_in[:, :, gate0:].astype(BF16),
        "w_branch": w_branch.astype(BF16),
        "w_out": w_out.astype(BF16),
        "moe_w1": moe_w1.astype(BF16),
        "moe_w3": moe_w3.astype(BF16),
        "moe_w2": moe_w2.reshape(depth, N_EXPERTS * D_EXPERT, D_MODEL).astype(BF16),
    }
    s5_tables = jax.vmap(_s5_tables)(s5_lam_re, s5_lam_im, s5_log_step, s5_b_re, s5_b_im, s5_c_re, s5_c_im)
    small = dict(w_mod=w_mod, b_mod=b_mod, norm1_g=norm1_g, norm2_g=norm2_g, rw_mu=rw_mu, rw_w0=rw_w0,
                 rw_w_up=rw_w_up, rw_a0=rw_a0, rw_a_up=rw_a_up, rw_k_k=rw_k_k, rw_k_a=rw_k_a, rw_r_k=rw_r_k,
                 rw_g_up=rw_g_up, rw_ln_g=rw_ln_g, rw_ln_b=rw_ln_b, dn_conv=dn_conv, dn_a_log=dn_a_log,
                 dn_dt_bias=dn_dt_bias, dn_norm_g=dn_norm_g, s5_d=s5_d, s5_w_glu=s5_w_glu, s5_b_glu=s5_b_glu)
    for i in range(depth):
        lp = {n: a[i] for n, a in small.items()}
        lp["s5_tables"] = tuple(tb[i] for tb in s5_tables)
        xcat = _layer(xcat, cond, n_ctx, i, lp, wts, router_wt, router_bc, consts,
                      final_g if i == depth - 1 else None)
    return xcat
```

```python
import functools
import math

import jax
import jax.numpy as jnp
from jax import lax
from jax.experimental import pallas as pl
from jax.experimental.pallas import tpu as pltpu

F32 = jnp.float32
BF16 = jnp.bfloat16
HIGHEST = lax.Precision.HIGHEST

D_MODEL = 1024
N_MOD = 6
EPS = 1e-6
BRANCH_W = 256
HEAD_DIM = 64
N_HEADS = 4
FN_GW = 64
RW_COLS = 896
RW_DECAY_SCALE = math.exp(-0.5)
RW_GN_EPS = 64e-5
DN_CONV = 5
S5_GW = 16
S5_GROUPS = 16
S5_STATE = 64
N_EXPERTS = 16
EXPERTS_PER_GROUP = 4
D_EXPERT = 256
W_IN_SPLITS = (256, 896, 768, 256, 8, 8, 256, 4096)

TOKEN_BLOCK = 256
SCAN_CHUNK = 64
S5_CHUNK = 16
HALO = 8
MXU_DEPTH = 256
INPROJ_BATCH = 2
MERGE_BATCH = 2
VMEM_LIMIT = 56 * 1024 * 1024


def _cparams(sem, vmem=None, **kw):
    return pltpu.CompilerParams(dimension_semantics=sem, vmem_limit_bytes=vmem, **kw)


def _dot(a, b):
    return jnp.dot(a.astype(BF16), b.astype(BF16), preferred_element_type=F32)


def _sigmoid(x):
    return jax.nn.sigmoid(x)


def _silu(x):
    return x * jax.nn.sigmoid(x)


def _norm_mod(x, g, scale, shift):
    y = x * lax.rsqrt(jnp.mean(x * x, axis=-1, keepdims=True) + EPS) * g
    return y * (1.0 + scale) + shift


def _layer_spec(a, layer, **kw):
    return pl.BlockSpec((1,) + a.shape[1:], lambda *_: (layer,) + (0,) * (a.ndim - 1), **kw)


def _mod_index(nblk_ctx, n_batch, per_step=1):
    return lambda b, j: (jnp.where(j < nblk_ctx, n_batch // per_step, b), 0, 0)


def _mod_kernel(c_ref, w_ref, b_ref, o_ref):
    o_ref[...] = _dot(_silu(c_ref[...]), w_ref[...]) + b_ref[...]


def _modulation(cond, w_mod, b_mod):
    rows = cond.shape[0]
    n = w_mod.shape[1]
    tn = 512
    out = pl.pallas_call(
        _mod_kernel,
        out_shape=jax.ShapeDtypeStruct((rows, n), F32),
        grid=(n // tn,),
        in_specs=[pl.BlockSpec((rows, D_MODEL), lambda i: (0, 0)),
                  pl.BlockSpec((D_MODEL, tn), lambda i: (0, i)),
                  pl.BlockSpec((1, tn), lambda i: (0, i))],
        out_specs=pl.BlockSpec((rows, tn), lambda i: (0, i)),
        compiler_params=_cparams(("arbitrary",)),
        name="adaln_mod",
    )(cond, w_mod, b_mod.reshape(1, n))
    return out.reshape(rows, N_MOD, D_MODEL)


def _shifted(x, s, halo_prev, halo_next):
    n = x.shape[0]
    if s == 0:
        return x
    rows = lax.broadcasted_iota(jnp.int32, x.shape, 0)
    if s < 0:
        y = pltpu.roll(x, -s, 0)
        for t in range(-s):
            y = jnp.where(rows == t, halo_prev[HALO + s + t:HALO + s + t + 1, :], y)
    else:
        y = pltpu.roll(x, n - s, 0)
        for t in range(s):
            y = jnp.where(rows == n - s + t, halo_next[t:t + 1, :], y)
    return y


def _inproj_kernel(nblk_ctx, nblk, x_ref, xp_ref, xn_ref, mod_ref, g_ref, w_ref, cs_ref, mu_ref, cw_ref,
                   fcs_ref, ps_ref, u_ref, dg_ref, s5_ref, dab_ref):
    j = pl.program_id(1)
    first = jnp.logical_or(j == 0, j == nblk_ctx)
    last = jnp.logical_or(j == nblk_ctx - 1, j == nblk - 1)
    pv = jnp.where(first, 0.0, 1.0).astype(F32)
    nv = jnp.where(last, 0.0, 1.0).astype(F32)

    steps = range(INPROJ_BATCH)
    h_f = [_norm_mod(jnp.concatenate([xp_ref[i], x_ref[i], xn_ref[i]], axis=0), g_ref[...],
                     mod_ref[i, 1:2, :], mod_ref[i, 0:1, :]) for i in steps]
    hs = [x[HALO:HALO + TOKEN_BLOCK].astype(BF16) for x in h_f]
    h_alls = [x.astype(BF16) for x in h_f]

    for i in steps:
        fn = jnp.dot(hs[i], w_ref[0, :, 0:256], preferred_element_type=F32)
        fcs_ref[i] = _dot(fn, cs_ref[...]).astype(BF16)
        dg_ref[i] = jnp.dot(hs[i], w_ref[0, :, 1920:2176], preferred_element_type=F32)
        s5_ref[i] = jnp.dot(hs[i], w_ref[0, :, 2176:2432], preferred_element_type=F32)
        dab_ref[i] = jnp.dot(hs[i], w_ref[0, :, 2432:2560], preferred_element_type=F32)

    def with_halo(h_all, cols):
        p = jnp.dot(h_all, w_ref[0, :, cols], preferred_element_type=F32)
        return p[HALO:HALO + TOKEN_BLOCK], p[0:HALO] * pv, p[HALO + TOKEN_BLOCK:] * nv

    for i, (p, hp, hn) in enumerate([with_halo(x, slice(256, 1152)) for x in h_alls]):
        ps_ref[i] = (p + mu_ref[0:1, :] * (_shifted(p, -1, hp, hn) - p)
                     + mu_ref[1:2, :] * (_shifted(p, 1, hp, hn) - p))

    pad = DN_CONV // 2
    for i, (q, hp, hn) in enumerate([with_halo(x, slice(1152, 1920)) for x in h_alls]):
        acc = cw_ref[pad:pad + 1, :] * q
        for t in range(DN_CONV):
            if t != pad:
                acc = acc + cw_ref[t:t + 1, :] * _shifted(q, t - pad, hp, hn)
        u_ref[i] = _silu(acc)


def _input_projection(xcat, mod, g1, w_mix, layer, cs, mu, conv_w, nblk_ctx):
    nb, t, _ = xcat.shape
    nblk = t // TOKEN_BLOCK
    per = TOKEN_BLOCK // HALO
    last_halo = t // HALO - 1
    widths = (512, RW_COLS, 3 * BRANCH_W, 256, 256, 128)
    dtypes = (BF16, F32, F32, F32, F32, F32)
    tok = lambda w: pl.BlockSpec((INPROJ_BATCH, TOKEN_BLOCK, w), lambda b, j: (b, j, 0))
    full = lambda a: pl.BlockSpec(a.shape, lambda b, j: (0,) * a.ndim)
    return pl.pallas_call(
        functools.partial(_inproj_kernel, nblk_ctx, nblk),
        out_shape=[jax.ShapeDtypeStruct((nb, t, w), dt) for w, dt in zip(widths, dtypes)],
        grid=(nb // INPROJ_BATCH, nblk),
        in_specs=[tok(D_MODEL),
                  pl.BlockSpec((INPROJ_BATCH, HALO, D_MODEL), lambda b, j: (b, jnp.maximum(j * per - 1, 0), 0)),
                  pl.BlockSpec((INPROJ_BATCH, HALO, D_MODEL), lambda b, j: (b, jnp.minimum((j + 1) * per, last_halo), 0)),
                  pl.BlockSpec((INPROJ_BATCH, N_MOD, D_MODEL), _mod_index(nblk_ctx, nb, INPROJ_BATCH)),
                  pl.BlockSpec((1, D_MODEL), lambda b, j: (0, 0)),
                  _layer_spec(w_mix, layer), full(cs), full(mu), full(conv_w)],
        out_specs=[tok(w) for w in widths],
        compiler_params=_cparams(("parallel", "arbitrary"), VMEM_LIMIT),
        name="norm1_inproj",
    )(xcat, xcat, xcat, mod, g1, w_mix, cs, mu, conv_w)


def _dft_kernel(n_batch, wc_ref, ws_ref, u_ref, o_ref):
    @pl.when(pl.program_id(1) == 0)
    def _():
        o_ref[...] = jnp.zeros_like(o_ref)

    wc = wc_ref[...]
    ws = ws_ref[...]
    for b in range(n_batch):
        u = u_ref[b]
        o_ref[b] += (jnp.dot(wc, u[:, 0:256], preferred_element_type=F32)
                     + jnp.dot(ws, u[:, 256:512], preferred_element_type=F32))


def _dft_tables(n):
    n2 = math.isqrt(n)
    assert n2 * n2 == n
    k = jnp.arange(n, dtype=jnp.int32)[None, :]
    j = jnp.arange(n2, dtype=jnp.int32)[:, None]
    ang_a = ((j * k) % n2).astype(F32) * (2.0 * math.pi / n2)
    ang_b = ((j * k) % n).astype(F32) * (2.0 * math.pi / n)
    ca, sa, cb, sb = jnp.cos(ang_a)[:, None], jnp.sin(ang_a)[:, None], jnp.cos(ang_b)[None], jnp.sin(ang_b)[None]
    scale = 1.0 / math.sqrt(n * FN_GW)
    cos = ((ca * cb - sa * sb) * scale).reshape(n, n)
    sin = ((sa * cb + ca * sb) * scale).reshape(n, n)
    return cos.astype(BF16), (-sin).astype(BF16)


def _sequence_dft(fcs, row0, n):
    nb = fcs.shape[0]
    wc, ws = _dft_tables(n)
    tm = min(n, 1024)
    tk = min(n, 256)
    assert n % tm == 0 and n % tk == 0 and row0 % tk == 0
    kb0 = row0 // tk
    return pl.pallas_call(
        functools.partial(_dft_kernel, nb),
        out_shape=jax.ShapeDtypeStruct((nb, n, BRANCH_W), F32),
        grid=(n // tm, n // tk),
        in_specs=[pl.BlockSpec((tm, tk), lambda m, k: (m, k)),
                  pl.BlockSpec((tm, tk), lambda m, k: (m, k)),
                  pl.BlockSpec((nb, tk, 512), lambda m, k: (0, k + kb0, 0))],
        out_specs=pl.BlockSpec((nb, tm, BRANCH_W), lambda m, k: (0, m, 0)),
        compiler_params=_cparams(("parallel", "arbitrary"), VMEM_LIMIT),
        name="fourier_seq_dft",
    )(wc, ws, fcs)


def _chunk_order(nc_ctx, nc):
    def order(d, c):
        back = jnp.where(c < nc_ctx, nc_ctx - 1 - c, nc - 1 + nc_ctx - c)
        return jnp.where(d == 0, c, back)
    return order


PAIR = 2 * HEAD_DIM
N_PAIRS = N_HEADS // 2
CHUNKS_PER_BLOCK = TOKEN_BLOCK // SCAN_CHUNK
MAP_W = 4 * PAIR
CHUNK_BATCH = 4
MOE_BATCH = 4
_NN = (((1,), (0,)), ((), ()))
_NT = (((1,), (1,)), ((), ()))
_TN = (((0,), (0,)), ((), ()))
_LOG2_CHUNK = int(math.log2(SCAN_CHUNK))


def _split(x):
    hi = x.astype(BF16)
    return hi, (x - hi.astype(F32)).astype(BF16)


def _mm3(a, b, dims=_NN):
    a_hi, a_lo = _split(a)
    b_hi, b_lo = _split(b)
    dg = lambda x, y: lax.dot_general(x, y, dims, preferred_element_type=F32)
    ca, cb = dims[0][0][0], dims[0][1][0]
    if a.shape[ca] <= MXU_DEPTH // 2:
        return (dg(jnp.concatenate([a_hi, a_lo], axis=ca), jnp.concatenate([b_hi, b_hi], axis=cb))
                + dg(a_hi, b_lo))
    return dg(a_hi, b_hi) + dg(a_hi, b_lo) + dg(a_lo, b_hi)


def _split3(x):
    hi = x.astype(BF16)
    rest = x - hi.astype(F32)
    mid = rest.astype(BF16)
    return hi, mid, (rest - mid.astype(F32)).astype(BF16)


def _select_rows(op, x):
    op = op.astype(BF16)
    hi, mid, lo = _split3(x)
    dot = lambda y: jnp.dot(op, y, preferred_element_type=F32)
    return dot(hi) + dot(mid) + dot(lo)


def _spread_cols(x, sel):
    sel = sel.astype(BF16)
    hi, mid, lo = _split3(x)
    assert x.shape[1] <= MXU_DEPTH // 2
    return (jnp.dot(jnp.concatenate([hi, mid], axis=1), jnp.concatenate([sel, sel], axis=0),
                    preferred_element_type=F32)
            + jnp.dot(lo, sel, preferred_element_type=F32))


def _pair_masks(d):
    row = lax.broadcasted_iota(jnp.int32, (PAIR, PAIR), 0)
    col = lax.broadcasted_iota(jnp.int32, (PAIR, PAIR), 1)
    same = jnp.right_shift(row, _LOG2_CHUNK) == jnp.right_shift(col, _LOG2_CHUNK)
    t_row = lax.broadcasted_iota(jnp.int32, (SCAN_CHUNK, PAIR), 0)
    t_col = jnp.bitwise_and(lax.broadcasted_iota(jnp.int32, (SCAN_CHUNK, PAIR), 1), SCAN_CHUNK - 1)
    delta = (t_row - t_col) * (1 - 2 * d)
    return same, delta >= 0, delta > 0, t_row == t_col


def _block_time_operators(d):
    row = lax.broadcasted_iota(jnp.int32, (TOKEN_BLOCK, TOKEN_BLOCK), 0)
    col = lax.broadcasted_iota(jnp.int32, (TOKEN_BLOCK, TOKEN_BLOCK), 1)
    same = jnp.right_shift(row, _LOG2_CHUNK) == jnp.right_shift(col, _LOG2_CHUNK)
    delta = (jnp.bitwise_and(row, SCAN_CHUNK - 1) - jnp.bitwise_and(col, SCAN_CHUNK - 1)) * (1 - 2 * d)
    return jnp.where(jnp.logical_and(same, delta >= 0), 1.0, 0.0).astype(F32)


def _chunk_problems():
    return [(i, slice(SCAN_CHUNK * c, SCAN_CHUNK * (c + 1)), slice(PAIR * pr, PAIR * (pr + 1)))
            for i in range(CHUNK_BATCH) for c in range(CHUNKS_PER_BLOCK) for pr in range(N_PAIRS)]


def _bd(x, same):
    return jnp.where(same, jnp.concatenate([x, x], axis=0), jnp.zeros((), x.dtype))


def _pack(x, same):
    x = jnp.where(same, x, 0.0)
    return x[0:HEAD_DIM] + x[HEAD_DIM:PAIR]


def _mm_pk(a, bs, same):
    a_hi, a_lo = _split(a)
    parts = [_split(b) for b in bs]
    r_hi = jnp.concatenate([_bd(hi, same) for hi, _ in parts], axis=1)
    r_lo = jnp.concatenate([_bd(lo, same) for _, lo in parts], axis=1)
    return (jnp.dot(jnp.concatenate([a_hi, a_lo], axis=1), jnp.concatenate([r_hi, r_hi], axis=0),
                    preferred_element_type=F32)
            + jnp.dot(a_hi, r_lo, preferred_element_type=F32))


def _mm_pk_fast(a, bs, same):
    rhs = jnp.concatenate([_bd(b.astype(BF16), same) for b in bs], axis=1)
    return jnp.dot(a.astype(BF16), rhs, preferred_element_type=F32)


def _mm_pk_nt(a, bs, same):
    a_hi, a_lo = _split(a)
    parts = [_split(b) for b in bs]
    r_hi = jnp.concatenate([_bd(hi, same) for hi, _ in parts], axis=0)
    r_lo = jnp.concatenate([_bd(lo, same) for _, lo in parts], axis=0)
    dg = lambda x, y: lax.dot_general(x, y, _NT, preferred_element_type=F32)
    return dg(jnp.concatenate([a_hi, a_lo], axis=1), jnp.concatenate([r_hi, r_hi], axis=1)) + dg(a_hi, r_lo)


def _head_sums(x, ones):
    hi, lo = _split(x)
    return jnp.dot(hi, ones, preferred_element_type=F32) + jnp.dot(lo, ones, preferred_element_type=F32)


def _chunk_cumsum(d, x):
    g = _select_rows(_block_time_operators(d), x)
    last = [jnp.where(d == 0, g[SCAN_CHUNK * (c + 1) - 1:SCAN_CHUNK * (c + 1)], g[SCAN_CHUNK * c:SCAN_CHUNK * c + 1])
            for c in range(CHUNKS_PER_BLOCK)]
    return g, jnp.concatenate([jnp.broadcast_to(r, (SCAN_CHUNK, r.shape[1])) for r in last], axis=0)


def _unit_tri_inverse(n_pks, eye_pk, same):
    xs = [eye_pk + n for n in n_pks]
    ps = [_mm_pk(n, [n], same) for n in n_pks]
    for level in range(1, _LOG2_CHUNK):
        if level + 1 < _LOG2_CHUNK:
            xps = [_mm_pk(p, [x, p], same) for x, p in zip(xs, ps)]
            xs = [x + xp[:, 0:PAIR] for x, xp in zip(xs, xps)]
            ps = [xp[:, PAIR:2 * PAIR] for xp in xps]
        else:
            xs = [x + _mm_pk(p, [x], same) for x, p in zip(xs, ps)]
    return xs


def _reset_chunk_state(maps_ref, h_ref):
    @pl.when(pl.program_id(2) == 0)
    def _():
        h_ref[...] = jnp.zeros_like(h_ref)
        maps_ref[...] = jnp.zeros_like(maps_ref)


def _apply_chunk_maps(d, maps_ref, h_ref, y_ref, same):
    chains = [(i, pr) for i in range(CHUNK_BATCH) for pr in range(N_PAIRS)]
    states = [h_ref[i, pr] for i, pr in chains]
    for k in range(CHUNKS_PER_BLOCK):
        row0 = pl.multiple_of(jnp.where(d == 0, k, CHUNKS_PER_BLOCK - 1 - k) * SCAN_CHUNK, SCAN_CHUNK)
        rows = pl.ds(row0, SCAN_CHUNK)
        part = lambda i, pr, n: maps_ref[i, rows, MAP_W * pr + PAIR * n:MAP_W * pr + PAIR * (n + 1)]
        outs = [_mm_pk(jnp.concatenate([part(i, pr, 0), part(i, pr, 2)], axis=0), [h], same)
                for (i, pr), h in zip(chains, states)]
        states = [out[0:SCAN_CHUNK] + part(i, pr, 1) for out, (i, pr) in zip(outs, chains)]
        for out, (i, pr) in zip(outs, chains):
            y_ref[0, i, rows, PAIR * pr:PAIR * (pr + 1)] = out[SCAN_CHUNK:] + part(i, pr, 3)
    for (i, pr), h in zip(chains, states):
        h_ref[i, pr] = h


def _chunk_scan_call(kernel, name, operands, in_specs, nb, t, nblk_ctx):
    nblk = t // TOKEN_BLOCK
    order = _chunk_order(nblk_ctx, nblk)
    blk_in = lambda d, j: order(d, jnp.minimum(j, nblk - 1))
    blk_out = lambda d, j: order(d, jnp.maximum(j - 1, 0))
    return pl.pallas_call(
        kernel,
        out_shape=jax.ShapeDtypeStruct((2, nb, t, BRANCH_W), F32),
        grid=(2, nb // CHUNK_BATCH, nblk + 1),
        in_specs=in_specs(blk_in),
        out_specs=pl.BlockSpec((1, CHUNK_BATCH, TOKEN_BLOCK, BRANCH_W), lambda d, b, j: (d, b, blk_out(d, j), 0)),
        scratch_shapes=[pltpu.VMEM((CHUNK_BATCH, TOKEN_BLOCK, N_PAIRS * MAP_W), F32),
                        pltpu.VMEM((CHUNK_BATCH, N_PAIRS, HEAD_DIM, PAIR), F32)],
        compiler_params=_cparams(("parallel", "parallel", "arbitrary"), VMEM_LIMIT),
        name=name,
    )(*operands)


def _rwkv_block_inputs(p, w0, w_up, a0, a_up, k_k, k_a, ones, d):
    r = p[:, 0:256]
    k = p[:, 256:512]
    v = p[:, 512:768]
    wl = p[:, 768:800]
    al = p[:, 800:832]
    lw = -RW_DECAY_SCALE * _sigmoid(w0 + _dot(jnp.tanh(wl), w_up))
    a = _sigmoid(a0 + _dot(al, a_up))
    kkp = k * k_k
    kk = kkp * lax.rsqrt(_head_sums(kkp * kkp, ones) + EPS)
    kmod = k * (1.0 + (a - 1.0) * k_a)
    alpha = -(a * kk)
    g, g_tot = _chunk_cumsum(d, lw)
    e_neg = jnp.exp(-g)
    e_tail = jnp.exp(g_tot - g)
    return dict(b=kk * jnp.exp(g - lw), r=r * jnp.exp(g), kh=kmod * e_neg, ah=alpha * e_neg,
                kt=kmod * e_tail, at=alpha * e_tail, v=v, gam=jnp.exp(g_tot))


def _rwkv_chunk_kernel(p_ref, w0_ref, wup_ref, a0_ref, aup_ref, kk_ref, ka_ref, ones_ref, y_ref, o_ref, h_ref):
    d = pl.program_id(0)
    same, incl, strict, eye = _pair_masks(d)
    eye_f = jnp.where(eye, 1.0, 0.0).astype(F32)
    zeros = jnp.zeros((SCAN_CHUNK, PAIR), F32)
    _reset_chunk_state(o_ref, h_ref)
    pre = [_rwkv_block_inputs(p_ref[i], w0_ref[0], wup_ref[0], a0_ref[0], aup_ref[0], kk_ref[...], ka_ref[...],
                              ones_ref[...], d) for i in range(CHUNK_BATCH)]
    probs = _chunk_problems()
    pk = lambda name: [pre[i][name][rows, lanes] for i, rows, lanes in probs]
    cat = jnp.concatenate
    b_t, r_t, k_h, a_h, k_t, a_t, v_p = pk("b"), pk("r"), pk("kh"), pk("ah"), pk("kt"), pk("at"), pk("v")
    m = [_mm_pk_nt(cat([b, r_], axis=0), [kh, ah], same) for b, r_, kh, ah in zip(b_t, r_t, k_h, a_h)]
    _apply_chunk_maps(d, o_ref, h_ref, y_ref, same)
    a_bk = [jnp.where(strict, x[0:SCAN_CHUNK, 0:PAIR], 0.0) for x in m]
    a_ba = [jnp.where(strict, x[0:SCAN_CHUNK, PAIR:], 0.0) for x in m]
    a_rk = [jnp.where(incl, x[SCAN_CHUNK:, 0:PAIR], 0.0) for x in m]
    a_ra = [jnp.where(incl, x[SCAN_CHUNK:, PAIR:], 0.0) for x in m]
    av = [_mm_pk(cat([x, y], axis=0), [vp], same) for x, y, vp in zip(a_bk, a_rk, v_p)]
    t_inv = _unit_tri_inverse(a_ba, eye_f, same)
    sol = [_mm_pk(t, [b, x[0:SCAN_CHUNK]], same) for t, b, x in zip(t_inv, b_t, av)]
    qy = [cat([r_, x[SCAN_CHUNK:]], axis=1) + _mm_pk_fast(a, [s[:, 0:PAIR], s[:, PAIR:]], same)
          for r_, x, a, s in zip(r_t, av, a_ra, sol)]
    kb = [_mm3(cat([at, kt], axis=0), cat([s, cat([zeros, vp], axis=1)], axis=0), _TN)
          for at, kt, s, vp in zip(a_t, k_t, sol, v_p)]
    for (i, rows, lanes), kb_i, qy_i in zip(probs, kb, qy):
        base = MAP_W * (lanes.start // PAIR)
        o_ref[i, rows, base:base + PAIR] = (_pack(kb_i[:, 0:PAIR], same)
                                               + jnp.where(eye, pre[i]["gam"][rows.start:rows.start + 1, lanes], 0.0))
        o_ref[i, rows, base + PAIR:base + 2 * PAIR] = _pack(kb_i[:, PAIR:], same)
        o_ref[i, rows, base + 2 * PAIR:base + 4 * PAIR] = qy_i


def _rwkv_scan(ps, w0, w_up, a0, a_up, k_k, k_a, ones_bd, nblk_ctx):
    nb, t, _ = ps.shape
    per_dir = lambda shape: pl.BlockSpec((1,) + shape, lambda d, b, j: (d, 0, 0))
    const = lambda shape: pl.BlockSpec(shape, lambda d, b, j: (0, 0))
    in_specs = lambda blk: [pl.BlockSpec((CHUNK_BATCH, TOKEN_BLOCK, RW_COLS), lambda d, b, j: (b, blk(d, j), 0)),
                            per_dir((1, BRANCH_W)), per_dir((32, BRANCH_W)),
                            per_dir((1, BRANCH_W)), per_dir((32, BRANCH_W)),
                            const((1, BRANCH_W)), const((1, BRANCH_W)), const((BRANCH_W, BRANCH_W))]
    operands = (ps, w0.reshape(2, 1, BRANCH_W), w_up, a0.reshape(2, 1, BRANCH_W), a_up,
                k_k.reshape(1, BRANCH_W), k_a.reshape(1, BRANCH_W), ones_bd)
    return _chunk_scan_call(_rwkv_chunk_kernel, "rwkv7_scan", operands, in_specs, nb, t, nblk_ctx)


def _deltanet_block_inputs(u, dab, neg_exp_a, dt_bias, expand_a, expand_b, ones, d):
    log_a8 = neg_exp_a * jax.nn.softplus(dab + dt_bias)
    la = _spread_cols(log_a8, expand_a)
    beta = _spread_cols(_sigmoid(dab), expand_b)
    q = u[:, 0:256]
    k = u[:, 256:512]
    v = u[:, 512:768]
    q = q * lax.rsqrt(_head_sums(q * q, ones) + EPS) * (HEAD_DIM ** -0.5)
    k = k * lax.rsqrt(_head_sums(k * k, ones) + EPS)
    g, g_tot = _chunk_cumsum(d, la)
    e_g = jnp.exp(g)
    kb = k * beta
    return dict(g=g, q=q, k=k, kb=kb, vb=v * beta, kbe=kb * e_g, qe=q * e_g, kt=k * jnp.exp(g_tot - g),
                gam=jnp.exp(g_tot))


def _deltanet_chunk_kernel(u_ref, dab_ref, nea_ref, dtb_ref, ea_ref, eb_ref, ones_ref, y_ref, o_ref, h_ref):
    d = pl.program_id(0)
    same, incl, strict, eye = _pair_masks(d)
    eye_f = jnp.where(eye, 1.0, 0.0).astype(F32)
    _reset_chunk_state(o_ref, h_ref)
    pre = [_deltanet_block_inputs(u_ref[i], dab_ref[i], nea_ref[...], dtb_ref[...], ea_ref[0], eb_ref[0],
                                  ones_ref[...], d) for i in range(CHUNK_BATCH)]
    probs = _chunk_problems()
    pk = lambda name: [pre[i][name][rows, lanes] for i, rows, lanes in probs]
    cat = jnp.concatenate
    g_p = pk("g")
    g_t = [_pack(_bd(x, same).T, same) for x in g_p]
    decay = [jnp.exp(jnp.where(incl, x - y, -jnp.inf)) for x, y in zip(g_p, g_t)]
    m = [_mm_pk_nt(cat([kb, q_], axis=0), [k_], same) for kb, q_, k_ in zip(pk("kb"), pk("q"), pk("k"))]
    _apply_chunk_maps(d, o_ref, h_ref, y_ref, same)
    a_low = [jnp.where(strict, x[0:SCAN_CHUNK] * dc, 0.0) for x, dc in zip(m, decay)]
    attn = [x[SCAN_CHUNK:] * dc for x, dc in zip(m, decay)]
    t_inv = _unit_tri_inverse([-x for x in a_low], eye_f, same)
    sol = [_mm_pk(t, [vb, kbe], same) for t, vb, kbe in zip(t_inv, pk("vb"), pk("kbe"))]
    att_sol = [_mm_pk_fast(a, [s[:, 0:PAIR], s[:, PAIR:]], same) for a, s in zip(attn, sol)]
    ks = [_mm3(kt, s, _TN) for kt, s in zip(pk("kt"), sol)]
    for (i, rows, lanes), ks_i, as_i, qe_i in zip(probs, ks, att_sol, pk("qe")):
        base = MAP_W * (lanes.start // PAIR)
        o_ref[i, rows, base:base + PAIR] = (jnp.where(eye, pre[i]["gam"][rows.start:rows.start + 1, lanes], 0.0)
                                               - _pack(ks_i[:, PAIR:], same))
        o_ref[i, rows, base + PAIR:base + 2 * PAIR] = _pack(ks_i[:, 0:PAIR], same)
        o_ref[i, rows, base + 2 * PAIR:base + 3 * PAIR] = qe_i - as_i[:, PAIR:]
        o_ref[i, rows, base + 3 * PAIR:base + 4 * PAIR] = as_i[:, 0:PAIR]


def _deltanet_scan(u, dab, a_log, dt_bias, ones_bd, nblk_ctx):
    nb, t, _ = u.shape
    pad = 128 - 2 * N_HEADS
    neg_exp_a = jnp.pad(-jnp.exp(a_log.reshape(1, -1)), ((0, 0), (0, pad)))
    dtb = jnp.pad(dt_bias.reshape(1, -1), ((0, 0), (0, pad)))
    col = jnp.arange(128)[None, :, None]
    head = (jnp.arange(BRANCH_W) // HEAD_DIM)[None, None, :]
    dirs = jnp.arange(2)[:, None, None]
    expand_a = (col == dirs * N_HEADS + head).astype(F32)
    expand_b = (col == 2 * N_HEADS + dirs * N_HEADS + head).astype(F32)
    const = lambda shape: pl.BlockSpec(shape, lambda d, b, j: (0, 0))
    per_dir = pl.BlockSpec((1, 128, BRANCH_W), lambda d, b, j: (d, 0, 0))
    in_specs = lambda blk: [pl.BlockSpec((CHUNK_BATCH, TOKEN_BLOCK, 3 * BRANCH_W), lambda d, b, j: (b, blk(d, j), 0)),
                            pl.BlockSpec((CHUNK_BATCH, TOKEN_BLOCK, 128), lambda d, b, j: (b, blk(d, j), 0)),
                            const((1, 128)), const((1, 128)), per_dir, per_dir, const((BRANCH_W, BRANCH_W))]
    operands = (u, dab, neg_exp_a, dtb, expand_a, expand_b, ones_bd)
    return _chunk_scan_call(_deltanet_chunk_kernel, "deltanet_scan", operands, in_specs, nb, t, nblk_ctx)


def _s5_kernel(n_batch, nc_ctx, nc, u_ref, toep_ref, winr_ref, wini_ref, wsor_ref, wsoi_ref,
               lr_ref, li_ref, dt_ref, y_ref, injr, inji, xsr, xsi):
    d = pl.program_id(1)
    u = u_ref[0]
    ub = u.astype(BF16)
    injr[...] = jnp.dot(ub, winr_ref[0, 0], preferred_element_type=F32)
    inji[...] = jnp.dot(ub, wini_ref[0, 0], preferred_element_type=F32)
    lam_r = lr_ref[0, 0]
    lam_i = li_ref[0, 0]

    def body(s, carry):
        xr, xi = carry
        back = jnp.where(s < nc_ctx, nc_ctx - 1 - s, nc - 1 + nc_ctx - s)
        row0 = jnp.where(d == 0, s, back) * n_batch
        if n_batch % 8 == 0:
            row0 = pl.multiple_of(row0, 8)
        rows = pl.ds(row0, n_batch)
        xsr[rows, :] = xr
        xsi[rows, :] = xi
        return (lam_r * xr - lam_i * xi + injr[rows, :], lam_r * xi + lam_i * xr + inji[rows, :])

    zero = jnp.zeros((n_batch, S5_STATE), F32)
    lax.fori_loop(0, nc, body, (zero, zero))

    y = (jnp.dot(ub, toep_ref[0, 0], preferred_element_type=F32)
         + _dot(xsr[...], wsor_ref[0, 0]) + _dot(xsi[...], wsoi_ref[0, 0]))

    @pl.when(d == 0)
    def _():
        y_ref[0] = y + u * dt_ref[0]

    @pl.when(d == 1)
    def _():
        y_ref[0] += y


def _s5_tables(lam_re, lam_im, log_step, b_re, b_im, c_re, c_im):
    cs = S5_CHUNK
    lam = lax.complex(lam_re.astype(F32), lam_im.astype(F32))
    step = jnp.exp(log_step.astype(F32))[..., None]
    tau = jnp.arange(cs + 1, dtype=F32)[:, None, None, None]
    lam_pow = jnp.exp(lam[None] * step[None] * tau)
    lam_bar = lam_pow[1]
    b_bar = ((lam_bar - 1.0) / lam)[..., None] * lax.complex(b_re.astype(F32), b_im.astype(F32))
    c_mat = lax.complex(c_re.astype(F32), c_im.astype(F32))
    kern = jnp.real(jnp.einsum("dghp,tdgp,dgpk->tdghk", c_mat, lam_pow[:cs], b_bar))
    i = jnp.arange(cs)
    lag_f = i[None, :] - i[:, None]
    toeps, winr, wini, wsor, wsoi = [], [], [], [], []
    for d in range(2):
        lag = lag_f if d == 0 else -lag_f
        kd = jnp.where((lag >= 0)[:, :, None, None, None], kern[:, d][jnp.clip(lag, 0, cs - 1)], 0.0)
        toeps.append(kd.transpose(2, 0, 4, 1, 3).reshape(S5_GROUPS, cs * S5_GW, cs * S5_GW))
        pw_in = (cs - 1 - i) if d == 0 else i
        e = lam_pow[pw_in, d][..., None] * b_bar[d][None]
        e = e.transpose(1, 0, 3, 2).reshape(S5_GROUPS, cs * S5_GW, S5_STATE)
        winr.append(jnp.real(e))
        wini.append(jnp.imag(e))
        pw_out = (i + 1) if d == 0 else (cs - i)
        m = c_mat[d][None] * lam_pow[pw_out, d][:, :, None, :]
        m = m.transpose(1, 3, 0, 2).reshape(S5_GROUPS, S5_STATE, cs * S5_GW)
        wsor.append(jnp.real(m))
        wsoi.append(-jnp.imag(m))
    stack = lambda xs, dt: jnp.stack(xs).astype(dt)
    lam_c = lam_pow[cs]
    return (stack(toeps, BF16), stack(winr, BF16), stack(wini, BF16), stack(wsor, BF16), stack(wsoi, BF16),
            jnp.real(lam_c)[:, :, None, :], jnp.imag(lam_c)[:, :, None, :])


def _s5_to_chunks(s5, n_ctx):
    nb, t, _ = s5.shape
    cs, g, hw = S5_CHUNK, S5_GROUPS, S5_GW
    n_lat = t - n_ctx
    rows = n_lat // 64
    c = s5[:, :n_ctx].reshape(nb, n_ctx // cs, cs, g, hw)
    c = c.transpose(3, 1, 0, 2, 4).reshape(g, (n_ctx // cs) * nb, cs * hw)
    l = s5[:, n_ctx:].reshape(nb, rows // cs, cs, 64, g, hw)
    l = l.transpose(4, 3, 1, 0, 2, 5).reshape(g, 64 * (rows // cs) * nb, cs * hw)
    return jnp.concatenate([c, l], axis=1)


def _s5_from_chunks(y, nb, n_ctx, n_lat):
    cs, g, hw = S5_CHUNK, S5_GROUPS, S5_GW
    rows = n_lat // 64
    r_ctx = (n_ctx // cs) * nb
    c = y[:, :r_ctx].reshape(g, n_ctx // cs, nb, cs, hw).transpose(2, 1, 3, 0, 4).reshape(nb, n_ctx, g * hw)
    l = y[:, r_ctx:].reshape(g, 64, rows // cs, nb, cs, hw).transpose(3, 2, 4, 1, 0, 5).reshape(nb, n_lat, g * hw)
    return jnp.concatenate([c, l], axis=1)


def _s5_scan(s5, tables, d_skip, n_ctx):
    nb, t, _ = s5.shape
    u = _s5_to_chunks(s5, n_ctx)
    g, r, w = u.shape
    nc = t // S5_CHUNK
    toep, winr, wini, wsor, wsoi, lr, li = tables
    d_tile = jnp.tile(d_skip.astype(F32).reshape(S5_GROUPS, 1, S5_GW), (1, S5_CHUNK, 1)).reshape(g, 1, w)
    per = lambda a, b: pl.BlockSpec((1, 1, a, b), lambda gi, d: (d, gi, 0, 0))
    y = pl.pallas_call(
        functools.partial(_s5_kernel, nb, n_ctx // S5_CHUNK, nc),
        out_shape=jax.ShapeDtypeStruct((g, r, w), F32),
        grid=(g, 2),
        in_specs=[pl.BlockSpec((1, r, w), lambda gi, d: (gi, 0, 0)),
                  per(w, w), per(w, S5_STATE), per(w, S5_STATE), per(S5_STATE, w), per(S5_STATE, w),
                  per(1, S5_STATE), per(1, S5_STATE),
                  pl.BlockSpec((1, 1, w), lambda gi, d: (gi, 0, 0))],
        out_specs=pl.BlockSpec((1, r, w), lambda gi, d: (gi, 0, 0)),
        scratch_shapes=[pltpu.VMEM((r, S5_STATE), F32) for _ in range(4)],
        compiler_params=_cparams(("parallel", "arbitrary"), VMEM_LIMIT),
        name="s5_scan",
    )(u, toep, winr, wini, wsor, wsoi, lr, li, d_tile)
    return _s5_from_chunks(y, nb, n_ctx, t - n_ctx)


def _merge_kernel(x_ref, mod_ref, g1_ref, ya_ref, yb_ref, ps_ref, oc_ref, dg_ref, y5_ref,
                  wg_ref, wb_ref, wo_ref, avg_ref, ones_ref, rk_ref, gup_ref, lng_ref, lnb_ref,
                  dng_ref, wglu_ref, bglu_ref, o_ref):
    avg = avg_ref[...]

    def branches(i):
        ps = ps_ref[i]
        r = ps[:, 0:256]
        k = ps[:, 256:512]
        v = ps[:, 512:768]
        gl = ps[:, 832:896]
        y = yb_ref[0, i] + yb_ref[1, i]
        dev = y - _dot(y, avg)
        yn = dev * lax.rsqrt(_dot(dev * dev, avg) + RW_GN_EPS) * lng_ref[...] + lnb_ref[...]
        bonus = _dot(r * k * rk_ref[...], ones_ref[...]) * v
        yb = (yn + bonus) * _dot(_sigmoid(gl), gup_ref[...])
        o = oc_ref[0, i] + oc_ref[1, i]
        yc = o * lax.rsqrt(_dot(o * o, avg) + EPS) * dng_ref[...] * _silu(dg_ref[i])
        z = jax.nn.gelu(y5_ref[i])
        yd = z * _sigmoid(_dot(z, wglu_ref[...]) + bglu_ref[...])
        return ya_ref[i], yb, yc, yd

    steps = range(MERGE_BATCH)
    xs = [x_ref[i] for i in steps]
    hs = [_norm_mod(xs[i], g1_ref[...], mod_ref[i, 1:2, :], mod_ref[i, 0:1, :]).astype(BF16) for i in steps]
    ys = [branches(i) for i in steps]
    ms = [jnp.zeros((TOKEN_BLOCK, D_MODEL), F32) for _ in steps]
    for n in range(4):
        for i in steps:
            gate = _sigmoid(jnp.dot(hs[i], wg_ref[0, :, D_MODEL * n:D_MODEL * (n + 1)], preferred_element_type=F32))
            ms[i] = ms[i] + gate * _dot(ys[i][n], wb_ref[0, n])
    for i in steps:
        o_ref[i] = xs[i] + mod_ref[i, 2:3, :] * _dot(ms[i], wo_ref[0])


def _merge(xcat, mod, g1, ya, yb, ps, oc, dg, y5, w_gate, w_branch, w_out, layer, avg_bd, ones_bd,
           r_k, g_up, ln_g, ln_b, dn_g, w_glu, b_glu, nblk_ctx):
    nb, t, _ = xcat.shape
    nblk = t // TOKEN_BLOCK
    tok = lambda w: pl.BlockSpec((MERGE_BATCH, TOKEN_BLOCK, w), lambda b, j: (b, j, 0))
    tok2 = lambda w: pl.BlockSpec((2, MERGE_BATCH, TOKEN_BLOCK, w), lambda b, j: (0, b, j, 0))
    full = lambda a: pl.BlockSpec(a.shape, lambda b, j: (0,) * a.ndim)
    stacked = (w_gate, w_branch, w_out)
    consts = (avg_bd, ones_bd, r_k, g_up, ln_g, ln_b, dn_g, w_glu, b_glu)
    return pl.pallas_call(
        _merge_kernel,
        out_shape=jax.ShapeDtypeStruct(xcat.shape, F32),
        grid=(nb // MERGE_BATCH, nblk),
        in_specs=[tok(D_MODEL), pl.BlockSpec((MERGE_BATCH, N_MOD, D_MODEL), _mod_index(nblk_ctx, nb, MERGE_BATCH)),
                  pl.BlockSpec((1, D_MODEL), lambda b, j: (0, 0)),
                  tok(BRANCH_W), tok2(BRANCH_W), tok(RW_COLS), tok2(BRANCH_W), tok(BRANCH_W), tok(BRANCH_W)]
        + [_layer_spec(a, layer) for a in stacked] + [full(a) for a in consts],
        out_specs=tok(D_MODEL),
        compiler_params=_cparams(("parallel", "arbitrary"), VMEM_LIMIT),
        name="merge_branches",
    )(xcat, mod, g1, ya, yb, ps, oc, dg, y5, *stacked, *consts)


def _route(sel, score):
    s = [sel[e:e + 1, :] for e in range(N_EXPERTS)]
    sc = [score[e:e + 1, :] for e in range(N_EXPERTS)]
    n_groups = N_EXPERTS // EXPERTS_PER_GROUP
    group_score = []
    for g in range(n_groups):
        m = s[EXPERTS_PER_GROUP * g:EXPERTS_PER_GROUP * (g + 1)]
        best = None
        for i in range(EXPERTS_PER_GROUP):
            for j in range(i + 1, EXPERTS_PER_GROUP):
                pair = m[i] + m[j]
                best = pair if best is None else jnp.maximum(best, pair)
        group_score.append(best)
    best_g = jnp.zeros(group_score[0].shape, jnp.int32)
    best_v = group_score[0]
    for g in range(1, n_groups):
        upd = group_score[g] > best_v
        best_g = jnp.where(upd, g, best_g)
        best_v = jnp.where(upd, group_score[g], best_v)
    chosen = []
    den = jnp.zeros_like(best_v)
    for e in range(N_EXPERTS):
        g = e // EXPERTS_PER_GROUP
        rank = jnp.zeros(best_g.shape, jnp.int32)
        for j in range(EXPERTS_PER_GROUP * g, EXPERTS_PER_GROUP * (g + 1)):
            if j == e:
                continue
            ahead = (s[j] > s[e]) if j > e else (s[j] >= s[e])
            rank = rank + jnp.where(ahead, 1, 0)
        pick = jnp.logical_and(best_g == g, rank < 2)
        chosen.append(pick)
        den = den + jnp.where(pick, sc[e], 0.0)
    return jnp.concatenate([jnp.where(chosen[e], sc[e] / den, 0.0) for e in range(N_EXPERTS)], axis=0)


def _moe_kernel(final, x_ref, mod_ref, g2_ref, w1_ref, w3_ref, w2_ref, rwt_ref, rb_ref, exp_ref, fg_ref, o_ref):
    steps = range(MOE_BATCH)
    xs = [x_ref[i] for i in steps]
    hs = [_norm_mod(xs[i], g2_ref[...], mod_ref[i, 4:5, :], mod_ref[i, 3:4, :]) for i in steps]
    scores = [_sigmoid(lax.dot_general(rwt_ref[...], h, _NT, preferred_element_type=F32, precision=HIGHEST))
              for h in hs]
    combs = [_route(sc + rb_ref[...], sc).astype(BF16) for sc in scores]
    hbs = [h.astype(BF16) for h in hs]
    accs = [jnp.zeros((TOKEN_BLOCK, D_MODEL), F32) for _ in steps]
    for q in range(N_EXPERTS // EXPERTS_PER_GROUP):
        experts = range(EXPERTS_PER_GROUP * q, EXPERTS_PER_GROUP * (q + 1))
        cols = slice(D_EXPERT * experts[0], D_EXPERT * (experts[-1] + 1))
        for i in steps:
            cw = lax.dot_general(combs[i], exp_ref[:, cols], _TN, preferred_element_type=F32)
            a1 = jnp.concatenate([jnp.dot(hbs[i], w1_ref[0, e], preferred_element_type=F32) for e in experts], axis=1)
            a3 = jnp.concatenate([jnp.dot(hbs[i], w3_ref[0, e], preferred_element_type=F32) for e in experts], axis=1)
            act = (_silu(a1) * a3 * cw).astype(BF16)
            accs[i] = accs[i] + jnp.dot(act, w2_ref[0, cols, :], preferred_element_type=F32)
    for i in steps:
        y = xs[i] + mod_ref[i, 5:6, :] * accs[i]
        if final:
            y = y * lax.rsqrt(jnp.mean(y * y, axis=-1, keepdims=True) + EPS) * fg_ref[...]
        o_ref[i] = y


def _moe(x1, mod, g2, router_wt, router_b, w1, w3, w2, layer, expand, nblk_ctx, final_g=None):
    nb, t, _ = x1.shape
    nblk = t // TOKEN_BLOCK
    final = final_g is not None
    tok = pl.BlockSpec((MOE_BATCH, TOKEN_BLOCK, D_MODEL), lambda b, j: (b, j, 0))
    full = lambda a: pl.BlockSpec(a.shape, lambda b, j: (0,) * a.ndim, pipeline_mode=pl.Buffered(1))
    fg = (final_g if final else jnp.ones((D_MODEL,), F32)).reshape(1, D_MODEL).astype(F32)
    consts = (router_wt, router_b, expand, fg)
    if final:
        out_shape = jax.ShapeDtypeStruct((nb, t - nblk_ctx * TOKEN_BLOCK, D_MODEL), F32)
        out_spec = pl.BlockSpec((MOE_BATCH, TOKEN_BLOCK, D_MODEL), lambda b, j: (b, jnp.maximum(j - nblk_ctx, 0), 0))
    else:
        out_shape, out_spec = jax.ShapeDtypeStruct(x1.shape, F32), tok
    return pl.pallas_call(
        functools.partial(_moe_kernel, final),
        out_shape=out_shape,
        grid=(nb // MOE_BATCH, nblk),
        in_specs=[tok, pl.BlockSpec((MOE_BATCH, N_MOD, D_MODEL), _mod_index(nblk_ctx, nb, MOE_BATCH)),
                  pl.BlockSpec((1, D_MODEL), lambda b, j: (0, 0))]
        + [_layer_spec(a, layer, pipeline_mode=pl.Buffered(1)) for a in (w1, w3, w2)] + [full(a) for a in consts],
        out_specs=out_spec,
        compiler_params=_cparams(("parallel", "arbitrary"), VMEM_LIMIT),
        name="moe_ffn",
    )(x1, mod, g2, w1, w3, w2, *consts)


def _block_diag_ones(n_blocks, size):
    return jnp.kron(jnp.eye(n_blocks, dtype=F32), jnp.ones((size, size), F32))


def _layer(xcat, cond, n_ctx, layer, lp, wts, router_wt, router_b, consts, final_g=None):
    nb, t, _ = xcat.shape
    nblk_ctx = n_ctx // TOKEN_BLOCK
    mod = _modulation(cond, lp["w_mod"], lp["b_mod"])
    g1 = lp["norm1_g"].reshape(1, D_MODEL)

    fcs, ps, u, dg, s5, dab = _input_projection(xcat, mod, g1, wts["w_mix"], layer, consts["cs"], lp["rw_mu"],
                                                lp["dn_conv"], nblk_ctx)

    ya = jnp.concatenate([_sequence_dft(fcs, 0, n_ctx), _sequence_dft(fcs, n_ctx, t - n_ctx)], axis=1)

    yb = _rwkv_scan(ps, lp["rw_w0"], lp["rw_w_up"], lp["rw_a0"], lp["rw_a_up"], lp["rw_k_k"], lp["rw_k_a"],
                    consts["ones_bd"], nblk_ctx)
    oc = _deltanet_scan(u, dab, lp["dn_a_log"], lp["dn_dt_bias"], consts["ones_bd"], nblk_ctx)
    y5 = _s5_scan(s5, lp["s5_tables"], lp["s5_d"], n_ctx)

    row = lambda a: a.reshape(1, -1).astype(F32)
    x1 = _merge(xcat, mod, g1, ya, yb, ps, oc, dg, y5, wts["w_gate"], wts["w_branch"], wts["w_out"], layer,
                consts["avg_bd"], consts["ones_bd"], row(lp["rw_r_k"]),
                lp["rw_g_up"].astype(BF16), row(lp["rw_ln_g"]), row(lp["rw_ln_b"]),
                row(jnp.tile(lp["dn_norm_g"], N_HEADS)), lp["s5_w_glu"].astype(BF16), row(lp["s5_b_glu"]),
                nblk_ctx)

    return _moe(x1, mod, lp["norm2_g"].reshape(1, D_MODEL), router_wt, router_b, wts["moe_w1"], wts["moe_w3"],
                wts["moe_w2"], layer, consts["expand"], nblk_ctx, final_g)


def _mixer_column_order():
    offs = [0]
    for width in W_IN_SPLITS:
        offs.append(offs[-1] + width)
    order = [0, 1, 2, 3, 6, 4, 5]
    idx = [c for i in order for c in range(offs[i], offs[i + 1])]
    pad = -len(idx) % 128
    keep = [1.0] * len(idx) + [0.0] * pad
    return jnp.asarray(idx + [0] * pad, jnp.int32), jnp.asarray(keep, F32), offs[7]


def kernel(x, c, ctx, c_ctx, w_mod, b_mod, norm1_g, norm2_g, w_in, rw_mu, rw_w0, rw_w_up, rw_a0, rw_a_up, rw_k_k, rw_k_a, rw_r_k, rw_g_up, rw_ln_g, rw_ln_b, dn_conv, dn_a_log, dn_dt_bias, dn_norm_g, s5_lam_re, s5_lam_im, s5_log_step, s5_b_re, s5_b_im, s5_c_re, s5_c_im, s5_d, s5_w_glu, s5_b_glu, w_branch, w_out, router_w, router_b, moe_w1, moe_w3, moe_w2, final_g):
    nb, n_lat, _ = x.shape
    n_ctx = ctx.shape[1]
    depth = w_mod.shape[0]
    assert n_ctx % TOKEN_BLOCK == 0 and n_lat % TOKEN_BLOCK == 0 and (n_lat // 64) % S5_CHUNK == 0
    assert nb % CHUNK_BATCH == 0 and nb % MOE_BATCH == 0 and nb % INPROJ_BATCH == 0 and nb % MERGE_BATCH == 0

    xcat = jnp.concatenate([ctx, x], axis=1).astype(F32)
    ctx_rows = max(MOE_BATCH, INPROJ_BATCH, MERGE_BATCH)
    cond_rows = -(-(nb + ctx_rows) // 8) * 8
    cond = jnp.zeros((cond_rows, D_MODEL), F32).at[:nb].set(c).at[nb:nb + ctx_rows].set(c_ctx)

    j = jnp.arange(FN_GW, dtype=jnp.int32)
    ang = ((j[:, None] * j[None, :]) % FN_GW).astype(F32) * (2.0 * math.pi / FN_GW)
    eye = jnp.eye(BRANCH_W // FN_GW, dtype=F32)
    consts = {
        "cs": jnp.concatenate([jnp.kron(eye, jnp.cos(ang)), jnp.kron(eye, jnp.sin(ang))], axis=1).astype(BF16),
        "ones_bd": _block_diag_ones(N_HEADS, HEAD_DIM).astype(BF16),
        "avg_bd": (_block_diag_ones(N_HEADS, HEAD_DIM) / HEAD_DIM).astype(BF16),
        "expand": jnp.kron(jnp.eye(N_EXPERTS, dtype=F32), jnp.ones((1, D_EXPERT), F32)).astype(BF16),
    }
    router_wt = router_w.T.astype(F32)
    router_bc = router_b.reshape(N_EXPERTS, 1).astype(F32)

    cols, keep, gate0 = _mixer_column_order()
    wts = {
        "w_mix": (jnp.take(w_in, cols, axis=2) * keep).astype(BF16),
        "w_gate": w_in[:, :, gate0:].astype(BF16),
        "w_branch": w_branch.astype(BF16),
        "w_out": w_out.astype(BF16),
        "moe_w1": moe_w1.astype(BF16),
        "moe_w3": moe_w3.astype(BF16),
        "moe_w2": moe_w2.reshape(depth, N_EXPERTS * D_EXPERT, D_MODEL).astype(BF16),
    }
    s5_tables = jax.vmap(_s5_tables)(s5_lam_re, s5_lam_im, s5_log_step, s5_b_re, s5_b_im, s5_c_re, s5_c_im)
    small = dict(w_mod=w_mod, b_mod=b_mod, norm1_g=norm1_g, norm2_g=norm2_g, rw_mu=rw_mu, rw_w0=rw_w0,
                 rw_w_up=rw_w_up, rw_a0=rw_a0, rw_a_up=rw_a_up, rw_k_k=rw_k_k, rw_k_a=rw_k_a, rw_r_k=rw_r_k,
                 rw_g_up=rw_g_up, rw_ln_g=rw_ln_g, rw_ln_b=rw_ln_b, dn_conv=dn_conv, dn_a_log=dn_a_log,
                 dn_dt_bias=dn_dt_bias, dn_norm_g=dn_norm_g, s5_d=s5_d, s5_w_glu=s5_w_glu, s5_b_glu=s5_b_glu)
    for i in range(depth):
        lp = {n: a[i] for n, a in small.items()}
        lp["s5_tables"] = tuple(tb[i] for tb in s5_tables)
        xcat = _layer(xcat, cond, n_ctx, i, lp, wts, router_wt, router_bc, consts,
                      final_g if i == depth - 1 else None)
    return xcat
```

```python
import functools
import math

import jax
import jax.numpy as jnp
from jax import lax
from jax.experimental import pallas as pl
from jax.experimental.pallas import tpu as pltpu

F32 = jnp.float32
BF16 = jnp.bfloat16
HIGHEST = lax.Precision.HIGHEST

D_MODEL = 1024
N_MOD = 6
EPS = 1e-6
BRANCH_W = 256
HEAD_DIM = 64
N_HEADS = 4
FN_GW = 64
RW_COLS = 896
RW_DECAY_SCALE = math.exp(-0.5)
RW_GN_EPS = 64e-5
DN_CONV = 5
S5_GW = 16
S5_GROUPS = 16
S5_STATE = 64
N_EXPERTS = 16
EXPERTS_PER_GROUP = 4
D_EXPERT = 256
W_IN_SPLITS = (256, 896, 768, 256, 8, 8, 256, 4096)

TOKEN_BLOCK = 256
SCAN_CHUNK = 64
S5_CHUNK = 16
HALO = 8
MXU_DEPTH = 256
INPROJ_BATCH = 2
MERGE_BATCH = 2
VMEM_LIMIT = 56 * 1024 * 1024


def _cparams(sem, vmem=None, **kw):
    return pltpu.CompilerParams(dimension_semantics=sem, vmem_limit_bytes=vmem, **kw)


def _dot(a, b):
    return jnp.dot(a.astype(BF16), b.astype(BF16), preferred_element_type=F32)


def _sigmoid(x):
    return jax.nn.sigmoid(x)


def _silu(x):
    return x * jax.nn.sigmoid(x)


def _norm_mod(x, g, scale, shift):
    y = x * lax.rsqrt(jnp.mean(x * x, axis=-1, keepdims=True) + EPS) * g
    return y * (1.0 + scale) + shift


def _layer_spec(a, layer, **kw):
    return pl.BlockSpec((1,) + a.shape[1:], lambda *_: (layer,) + (0,) * (a.ndim - 1), **kw)


def _mod_index(nblk_ctx, n_batch, per_step=1):
    return lambda b, j: (jnp.where(j < nblk_ctx, n_batch // per_step, b), 0, 0)


def _mod_kernel(c_ref, w_ref, b_ref, o_ref):
    o_ref[...] = _dot(_silu(c_ref[...]), w_ref[...]) + b_ref[...]


def _modulation(cond, w_mod, b_mod):
    rows = cond.shape[0]
    n = w_mod.shape[1]
    tn = 512
    out = pl.pallas_call(
        _mod_kernel,
        out_shape=jax.ShapeDtypeStruct((rows, n), F32),
        grid=(n // tn,),
        in_specs=[pl.BlockSpec((rows, D_MODEL), lambda i: (0, 0)),
                  pl.BlockSpec((D_MODEL, tn), lambda i: (0, i)),
                  pl.BlockSpec((1, tn), lambda i: (0, i))],
        out_specs=pl.BlockSpec((rows, tn), lambda i: (0, i)),
        compiler_params=_cparams(("arbitrary",)),
        name="adaln_mod",
    )(cond, w_mod, b_mod.reshape(1, n))
    return out.reshape(rows, N_MOD, D_MODEL)


def _shifted(x, s, halo_prev, halo_next):
    n = x.shape[0]
    if s == 0:
        return x
    rows = lax.broadcasted_iota(jnp.int32, x.shape, 0)
    if s < 0:
        y = pltpu.roll(x, -s, 0)
        for t in range(-s):
            y = jnp.where(rows == t, halo_prev[HALO + s + t:HALO + s + t + 1, :], y)
    else:
        y = pltpu.roll(x, n - s, 0)
        for t in range(s):
            y = jnp.where(rows == n - s + t, halo_next[t:t + 1, :], y)
    return y


def _inproj_kernel(nblk_ctx, nblk, x_ref, xp_ref, xn_ref, mod_ref, g_ref, w_ref, cs_ref, mu_ref, cw_ref,
                   fcs_ref, ps_ref, u_ref, dg_ref, s5_ref, dab_ref):
    j = pl.program_id(1)
    first = jnp.logical_or(j == 0, j == nblk_ctx)
    last = jnp.logical_or(j == nblk_ctx - 1, j == nblk - 1)
    pv = jnp.where(first, 0.0, 1.0).astype(F32)
    nv = jnp.where(last, 0.0, 1.0).astype(F32)

    steps = range(INPROJ_BATCH)
    h_f = [_norm_mod(jnp.concatenate([xp_ref[i], x_ref[i], xn_ref[i]], axis=0), g_ref[...],
                     mod_ref[i, 1:2, :], mod_ref[i, 0:1, :]) for i in steps]
    hs = [x[HALO:HALO + TOKEN_BLOCK].astype(BF16) for x in h_f]
    h_alls = [x.astype(BF16) for x in h_f]

    for i in steps:
        fn = jnp.dot(hs[i], w_ref[0, :, 0:256], preferred_element_type=F32)
        fcs_ref[i] = _dot(fn, cs_ref[...]).astype(BF16)
        dg_ref[i] = jnp.dot(hs[i], w_ref[0, :, 1920:2176], preferred_element_type=F32)
        s5_ref[i] = jnp.dot(hs[i], w_ref[0, :, 2176:2432], preferred_element_type=F32)
        dab_ref[i] = jnp.dot(hs[i], w_ref[0, :, 2432:2560], preferred_element_type=F32)

    def with_halo(h_all, cols):
        p = jnp.dot(h_all, w_ref[0, :, cols], preferred_element_type=F32)
        return p[HALO:HALO + TOKEN_BLOCK], p[0:HALO] * pv, p[HALO + TOKEN_BLOCK:] * nv

    for i, (p, hp, hn) in enumerate([with_halo(x, slice(256, 1152)) for x in h_alls]):
        ps_ref[i] = (p + mu_ref[0:1, :] * (_shifted(p, -1, hp, hn) - p)
                     + mu_ref[1:2, :] * (_shifted(p, 1, hp, hn) - p))

    pad = DN_CONV // 2
    for i, (q, hp, hn) in enumerate([with_halo(x, slice(1152, 1920)) for x in h_alls]):
        acc = cw_ref[pad:pad + 1, :] * q
        for t in range(DN_CONV):
            if t != pad:
                acc = acc + cw_ref[t:t + 1, :] * _shifted(q, t - pad, hp, hn)
        u_ref[i] = _silu(acc)


def _input_projection(xcat, mod, g1, w_mix, layer, cs, mu, conv_w, nblk_ctx):
    nb, t, _ = xcat.shape
    nblk = t // TOKEN_BLOCK
    per = TOKEN_BLOCK // HALO
    last_halo = t // HALO - 1
    widths = (512, RW_COLS, 3 * BRANCH_W, 256, 256, 128)
    dtypes = (BF16, F32, F32, F32, F32, F32)
    tok = lambda w: pl.BlockSpec((INPROJ_BATCH, TOKEN_BLOCK, w), lambda b, j: (b, j, 0))
    full = lambda a: pl.BlockSpec(a.shape, lambda b, j: (0,) * a.ndim)
    return pl.pallas_call(
        functools.partial(_inproj_kernel, nblk_ctx, nblk),
        out_shape=[jax.ShapeDtypeStruct((nb, t, w), dt) for w, dt in zip(widths, dtypes)],
        grid=(nb // INPROJ_BATCH, nblk),
        in_specs=[tok(D_MODEL),
                  pl.BlockSpec((INPROJ_BATCH, HALO, D_MODEL), lambda b, j: (b, jnp.maximum(j * per - 1, 0), 0)),
                  pl.BlockSpec((INPROJ_BATCH, HALO, D_MODEL), lambda b, j: (b, jnp.minimum((j + 1) * per, last_halo), 0)),
                  pl.BlockSpec((INPROJ_BATCH, N_MOD, D_MODEL), _mod_index(nblk_ctx, nb, INPROJ_BATCH)),
                  pl.BlockSpec((1, D_MODEL), lambda b, j: (0, 0)),
                  _layer_spec(w_mix, layer), full(cs), full(mu), full(conv_w)],
        out_specs=[tok(w) for w in widths],
        compiler_params=_cparams(("parallel", "arbitrary"), VMEM_LIMIT),
        name="norm1_inproj",
    )(xcat, xcat, xcat, mod, g1, w_mix, cs, mu, conv_w)


def _dft_kernel(n_batch, wc_ref, ws_ref, u_ref, o_ref):
    @pl.when(pl.program_id(1) == 0)
    def _():
        o_ref[...] = jnp.zeros_like(o_ref)

    wc = wc_ref[...]
    ws = ws_ref[...]
    for b in range(n_batch):
        u = u_ref[b]
        o_ref[b] += (jnp.dot(wc, u[:, 0:256], preferred_element_type=F32)
                     + jnp.dot(ws, u[:, 256:512], preferred_element_type=F32))


def _dft_tables(n):
    n2 = math.isqrt(n)
    assert n2 * n2 == n
    k = jnp.arange(n, dtype=jnp.int32)[None, :]
    j = jnp.arange(n2, dtype=jnp.int32)[:, None]
    ang_a = ((j * k) % n2).astype(F32) * (2.0 * math.pi / n2)
    ang_b = ((j * k) % n).astype(F32) * (2.0 * math.pi / n)
    ca, sa, cb, sb = jnp.cos(ang_a)[:, None], jnp.sin(ang_a)[:, None], jnp.cos(ang_b)[None], jnp.sin(ang_b)[None]
    scale = 1.0 / math.sqrt(n * FN_GW)
    cos = ((ca * cb - sa * sb) * scale).reshape(n, n)
    sin = ((sa * cb + ca * sb) * scale).reshape(n, n)
    return cos.astype(BF16), (-sin).astype(BF16)


def _sequence_dft(fcs, row0, n):
    nb = fcs.shape[0]
    wc, ws = _dft_tables(n)
    tm = min(n, 1024)
    tk = min(n, 256)
    assert n % tm == 0 and n % tk == 0 and row0 % tk == 0
    kb0 = row0 // tk
    return pl.pallas_call(
        functools.partial(_dft_kernel, nb),
        out_shape=jax.ShapeDtypeStruct((nb, n, BRANCH_W), F32),
        grid=(n // tm, n // tk),
        in_specs=[pl.BlockSpec((tm, tk), lambda m, k: (m, k)),
                  pl.BlockSpec((tm, tk), lambda m, k: (m, k)),
                  pl.BlockSpec((nb, tk, 512), lambda m, k: (0, k + kb0, 0))],
        out_specs=pl.BlockSpec((nb, tm, BRANCH_W), lambda m, k: (0, m, 0)),
        compiler_params=_cparams(("parallel", "arbitrary"), VMEM_LIMIT),
        name="fourier_seq_dft",
    )(wc, ws, fcs)


def _chunk_order(nc_ctx, nc):
    def order(d, c):
        back = jnp.where(c < nc_ctx, nc_ctx - 1 - c, nc - 1 + nc_ctx - c)
        return jnp.where(d == 0, c, back)
    return order


PAIR = 2 * HEAD_DIM
N_PAIRS = N_HEADS // 2
CHUNKS_PER_BLOCK = TOKEN_BLOCK // SCAN_CHUNK
MAP_W = 4 * PAIR
CHUNK_BATCH = 4
MOE_BATCH = 4
_NN = (((1,), (0,)), ((), ()))
_NT = (((1,), (1,)), ((), ()))
_TN = (((0,), (0,)), ((), ()))
_LOG2_CHUNK = int(math.log2(SCAN_CHUNK))


def _split(x):
    hi = x.astype(BF16)
    return hi, (x - hi.astype(F32)).astype(BF16)


def _mm3(a, b, dims=_NN):
    a_hi, a_lo = _split(a)
    b_hi, b_lo = _split(b)
    dg = lambda x, y: lax.dot_general(x, y, dims, preferred_element_type=F32)
    ca, cb = dims[0][0][0], dims[0][1][0]
    if a.shape[ca] <= MXU_DEPTH // 2:
        return (dg(jnp.concatenate([a_hi, a_lo], axis=ca), jnp.concatenate([b_hi, b_hi], axis=cb))
                + dg(a_hi, b_lo))
    return dg(a_hi, b_hi) + dg(a_hi, b_lo) + dg(a_lo, b_hi)


def _split3(x):
    hi = x.astype(BF16)
    rest = x - hi.astype(F32)
    mid = rest.astype(BF16)
    return hi, mid, (rest - mid.astype(F32)).astype(BF16)


def _select_rows(op, x):
    op = op.astype(BF16)
    hi, mid, lo = _split3(x)
    dot = lambda y: jnp.dot(op, y, preferred_element_type=F32)
    return dot(hi) + dot(mid) + dot(lo)


def _spread_cols(x, sel):
    sel = sel.astype(BF16)
    hi, mid, lo = _split3(x)
    assert x.shape[1] <= MXU_DEPTH // 2
    return (jnp.dot(jnp.concatenate([hi, mid], axis=1), jnp.concatenate([sel, sel], axis=0),
                    preferred_element_type=F32)
            + jnp.dot(lo, sel, preferred_element_type=F32))


def _pair_masks(d):
    row = lax.broadcasted_iota(jnp.int32, (PAIR, PAIR), 0)
    col = lax.broadcasted_iota(jnp.int32, (PAIR, PAIR), 1)
    same = jnp.right_shift(row, _LOG2_CHUNK) == jnp.right_shift(col, _LOG2_CHUNK)
    t_row = lax.broadcasted_iota(jnp.int32, (SCAN_CHUNK, PAIR), 0)
    t_col = jnp.bitwise_and(lax.broadcasted_iota(jnp.int32, (SCAN_CHUNK, PAIR), 1), SCAN_CHUNK - 1)
    delta = (t_row - t_col) * (1 - 2 * d)
    return same, delta >= 0, delta > 0, t_row == t_col


def _block_time_operators(d):
    row = lax.broadcasted_iota(jnp.int32, (TOKEN_BLOCK, TOKEN_BLOCK), 0)
    col = lax.broadcasted_iota(jnp.int32, (TOKEN_BLOCK, TOKEN_BLOCK), 1)
    same = jnp.right_shift(row, _LOG2_CHUNK) == jnp.right_shift(col, _LOG2_CHUNK)
    delta = (jnp.bitwise_and(row, SCAN_CHUNK - 1) - jnp.bitwise_and(col, SCAN_CHUNK - 1)) * (1 - 2 * d)
    return jnp.where(jnp.logical_and(same, delta >= 0), 1.0, 0.0).astype(F32)


def _chunk_problems():
    return [(i, slice(SCAN_CHUNK * c, SCAN_CHUNK * (c + 1)), slice(PAIR * pr, PAIR * (pr + 1)))
            for i in range(CHUNK_BATCH) for c in range(CHUNKS_PER_BLOCK) for pr in range(N_PAIRS)]


def _bd(x, same):
    return jnp.where(same, jnp.concatenate([x, x], axis=0), jnp.zeros((), x.dtype))


def _pack(x, same):
    x = jnp.where(same, x, 0.0)
    return x[0:HEAD_DIM] + x[HEAD_DIM:PAIR]


def _mm_pk(a, bs, same):
    a_hi, a_lo = _split(a)
    parts = [_split(b) for b in bs]
    r_hi = jnp.concatenate([_bd(hi, same) for hi, _ in parts], axis=1)
    r_lo = jnp.concatenate([_bd(lo, same) for _, lo in parts], axis=1)
    return (jnp.dot(jnp.concatenate([a_hi, a_lo], axis=1), jnp.concatenate([r_hi, r_hi], axis=0),
                    preferred_element_type=F32)
            + jnp.dot(a_hi, r_lo, preferred_element_type=F32))


def _mm_pk_fast(a, bs, same):
    rhs = jnp.concatenate([_bd(b.astype(BF16), same) for b in bs], axis=1)
    return jnp.dot(a.astype(BF16), rhs, preferred_element_type=F32)


def _mm_pk_nt(a, bs, same):
    a_hi, a_lo = _split(a)
    parts = [_split(b) for b in bs]
    r_hi = jnp.concatenate([_bd(hi, same) for hi, _ in parts], axis=0)
    r_lo = jnp.concatenate([_bd(lo, same) for _, lo in parts], axis=0)
    dg = lambda x, y: lax.dot_general(x, y, _NT, preferred_element_type=F32)
    return dg(jnp.concatenate([a_hi, a_lo], axis=1), jnp.concatenate([r_hi, r_hi], axis=1)) + dg(a_hi, r_lo)


def _head_sums(x, ones):
    hi, lo = _split(x)
    return jnp.dot(hi, ones, preferred_element_type=F32) + jnp.dot(lo, ones, preferred_element_type=F32)


def _chunk_cumsum(d, x):
    g = _select_rows(_block_time_operators(d), x)
    last = [jnp.where(d == 0, g[SCAN_CHUNK * (c + 1) - 1:SCAN_CHUNK * (c + 1)], g[SCAN_CHUNK * c:SCAN_CHUNK * c + 1])
            for c in range(CHUNKS_PER_BLOCK)]
    return g, jnp.concatenate([jnp.broadcast_to(r, (SCAN_CHUNK, r.shape[1])) for r in last], axis=0)


def _unit_tri_inverse(n_pks, eye_pk, same):
    xs = [eye_pk + n for n in n_pks]
    ps = [_mm_pk(n, [n], same) for n in n_pks]
    for level in range(1, _LOG2_CHUNK):
        if level + 1 < _LOG2_CHUNK:
            xps = [_mm_pk(p, [x, p], same) for x, p in zip(xs, ps)]
            xs = [x + xp[:, 0:PAIR] for x, xp in zip(xs, xps)]
            ps = [xp[:, PAIR:2 * PAIR] for xp in xps]
        else:
            xs = [x + _mm_pk(p, [x], same) for x, p in zip(xs, ps)]
    return xs


def _reset_chunk_state(maps_ref, h_ref):
    @pl.when(pl.program_id(2) == 0)
    def _():
        h_ref[...] = jnp.zeros_like(h_ref)
        maps_ref[...] = jnp.zeros_like(maps_ref)


def _apply_chunk_maps(d, maps_ref, h_ref, y_ref, same):
    chains = [(i, pr) for i in range(CHUNK_BATCH) for pr in range(N_PAIRS)]
    states = [h_ref[i, pr] for i, pr in chains]
    for k in range(CHUNKS_PER_BLOCK):
        row0 = pl.multiple_of(jnp.where(d == 0, k, CHUNKS_PER_BLOCK - 1 - k) * SCAN_CHUNK, SCAN_CHUNK)
        rows = pl.ds(row0, SCAN_CHUNK)
        part = lambda i, pr, n: maps_ref[i, rows, MAP_W * pr + PAIR * n:MAP_W * pr + PAIR * (n + 1)]
        outs = [_mm_pk(jnp.concatenate([part(i, pr, 0), part(i, pr, 2)], axis=0), [h], same)
                for (i, pr), h in zip(chains, states)]
        states = [out[0:SCAN_CHUNK] + part(i, pr, 1) for out, (i, pr) in zip(outs, chains)]
        for out, (i, pr) in zip(outs, chains):
            y_ref[0, i, rows, PAIR * pr:PAIR * (pr + 1)] = out[SCAN_CHUNK:] + part(i, pr, 3)
    for (i, pr), h in zip(chains, states):
        h_ref[i, pr] = h


def _chunk_scan_call(kernel, name, operands, in_specs, nb, t, nblk_ctx):
    nblk = t // TOKEN_BLOCK
    order = _chunk_order(nblk_ctx, nblk)
    blk_in = lambda d, j: order(d, jnp.minimum(j, nblk - 1))
    blk_out = lambda d, j: order(d, jnp.maximum(j - 1, 0))
    return pl.pallas_call(
        kernel,
        out_shape=jax.ShapeDtypeStruct((2, nb, t, BRANCH_W), F32),
        grid=(2, nb // CHUNK_BATCH, nblk + 1),
        in_specs=in_specs(blk_in),
        out_specs=pl.BlockSpec((1, CHUNK_BATCH, TOKEN_BLOCK, BRANCH_W), lambda d, b, j: (d, b, blk_out(d, j), 0)),
        scratch_shapes=[pltpu.VMEM((CHUNK_BATCH, TOKEN_BLOCK, N_PAIRS * MAP_W), F32),
                        pltpu.VMEM((CHUNK_BATCH, N_PAIRS, HEAD_DIM, PAIR), F32)],
        compiler_params=_cparams(("parallel", "parallel", "arbitrary"), VMEM_LIMIT),
        name=name,
    )(*operands)


def _rwkv_block_inputs(p, w0, w_up, a0, a_up, k_k, k_a, ones, d):
    r = p[:, 0:256]
    k = p[:, 256:512]
    v = p[:, 512:768]
    wl = p[:, 768:800]
    al = p[:, 800:832]
    lw = -RW_DECAY_SCALE * _sigmoid(w0 + _dot(jnp.tanh(wl), w_up))
    a = _sigmoid(a0 + _dot(al, a_up))
    kkp = k * k_k
    kk = kkp * lax.rsqrt(_head_sums(kkp * kkp, ones) + EPS)
    kmod = k * (1.0 + (a - 1.0) * k_a)
    alpha = -(a * kk)
    g, g_tot = _chunk_cumsum(d, lw)
    e_neg = jnp.exp(-g)
    e_tail = jnp.exp(g_tot - g)
    return dict(b=kk * jnp.exp(g - lw), r=r * jnp.exp(g), kh=kmod * e_neg, ah=alpha * e_neg,
                kt=kmod * e_tail, at=alpha * e_tail, v=v, gam=jnp.exp(g_tot))


def _rwkv_chunk_kernel(p_ref, w0_ref, wup_ref, a0_ref, aup_ref, kk_ref, ka_ref, ones_ref, y_ref, o_ref, h_ref):
    d = pl.program_id(0)
    same, incl, strict, eye = _pair_masks(d)
    eye_f = jnp.where(eye, 1.0, 0.0).astype(F32)
    zeros = jnp.zeros((SCAN_CHUNK, PAIR), F32)
    _reset_chunk_state(o_ref, h_ref)
    pre = [_rwkv_block_inputs(p_ref[i], w0_ref[0], wup_ref[0], a0_ref[0], aup_ref[0], kk_ref[...], ka_ref[...],
                              ones_ref[...], d) for i in range(CHUNK_BATCH)]
    probs = _chunk_problems()
    pk = lambda name: [pre[i][name][rows, lanes] for i, rows, lanes in probs]
    cat = jnp.concatenate
    b_t, r_t, k_h, a_h, k_t, a_t, v_p = pk("b"), pk("r"), pk("kh"), pk("ah"), pk("kt"), pk("at"), pk("v")
    m = [_mm_pk_nt(cat([b, r_], axis=0), [kh, ah], same) for b, r_, kh, ah in zip(b_t, r_t, k_h, a_h)]
    _apply_chunk_maps(d, o_ref, h_ref, y_ref, same)
    a_bk = [jnp.where(strict, x[0:SCAN_CHUNK, 0:PAIR], 0.0) for x in m]
    a_ba = [jnp.where(strict, x[0:SCAN_CHUNK, PAIR:], 0.0) for x in m]
    a_rk = [jnp.where(incl, x[SCAN_CHUNK:, 0:PAIR], 0.0) for x in m]
    a_ra = [jnp.where(incl, x[SCAN_CHUNK:, PAIR:], 0.0) for x in m]
    av = [_mm_pk(cat([x, y], axis=0), [vp], same) for x, y, vp in zip(a_bk, a_rk, v_p)]
    t_inv = _unit_tri_inverse(a_ba, eye_f, same)
    sol = [_mm_pk(t, [b, x[0:SCAN_CHUNK]], same) for t, b, x in zip(t_inv, b_t, av)]
    qy = [cat([r_, x[SCAN_CHUNK:]], axis=1) + _mm_pk_fast(a, [s[:, 0:PAIR], s[:, PAIR:]], same)
          for r_, x, a, s in zip(r_t, av, a_ra, sol)]
    kb = [_mm3(cat([at, kt], axis=0), cat([s, cat([zeros, vp], axis=1)], axis=0), _TN)
          for at, kt, s, vp in zip(a_t, k_t, sol, v_p)]
    for (i, rows, lanes), kb_i, qy_i in zip(probs, kb, qy):
        base = MAP_W * (lanes.start // PAIR)
        o_ref[i, rows, base:base + PAIR] = (_pack(kb_i[:, 0:PAIR], same)
                                               + jnp.where(eye, pre[i]["gam"][rows.start:rows.start + 1, lanes], 0.0))
        o_ref[i, rows, base + PAIR:base + 2 * PAIR] = _pack(kb_i[:, PAIR:], same)
        o_ref[i, rows, base + 2 * PAIR:base + 4 * PAIR] = qy_i


def _rwkv_scan(ps, w0, w_up, a0, a_up, k_k, k_a, ones_bd, nblk_ctx):
    nb, t, _ = ps.shape
    per_dir = lambda shape: pl.BlockSpec((1,) + shape, lambda d, b, j: (d, 0, 0))
    const = lambda shape: pl.BlockSpec(shape, lambda d, b, j: (0, 0))
    in_specs = lambda blk: [pl.BlockSpec((CHUNK_BATCH, TOKEN_BLOCK, RW_COLS), lambda d, b, j: (b, blk(d, j), 0)),
                            per_dir((1, BRANCH_W)), per_dir((32, BRANCH_W)),
                            per_dir((1, BRANCH_W)), per_dir((32, BRANCH_W)),
                            const((1, BRANCH_W)), const((1, BRANCH_W)), const((BRANCH_W, BRANCH_W))]
    operands = (ps, w0.reshape(2, 1, BRANCH_W), w_up, a0.reshape(2, 1, BRANCH_W), a_up,
                k_k.reshape(1, BRANCH_W), k_a.reshape(1, BRANCH_W), ones_bd)
    return _chunk_scan_call(_rwkv_chunk_kernel, "rwkv7_scan", operands, in_specs, nb, t, nblk_ctx)


def _deltanet_block_inputs(u, dab, neg_exp_a, dt_bias, expand_a, expand_b, ones, d):
    log_a8 = neg_exp_a * jax.nn.softplus(dab + dt_bias)
    la = _spread_cols(log_a8, expand_a)
    beta = _spread_cols(_sigmoid(dab), expand_b)
    q = u[:, 0:256]
    k = u[:, 256:512]
    v = u[:, 512:768]
    q = q * lax.rsqrt(_head_sums(q * q, ones) + EPS) * (HEAD_DIM ** -0.5)
    k = k * lax.rsqrt(_head_sums(k * k, ones) + EPS)
    g, g_tot = _chunk_cumsum(d, la)
    e_g = jnp.exp(g)
    kb = k * beta
    return dict(g=g, q=q, k=k, kb=kb, vb=v * beta, kbe=kb * e_g, qe=q * e_g, kt=k * jnp.exp(g_tot - g),
                gam=jnp.exp(g_tot))


def _deltanet_chunk_kernel(u_ref, dab_ref, nea_ref, dtb_ref, ea_ref, eb_ref, ones_ref, y_ref, o_ref, h_ref):
    d = pl.program_id(0)
    same, incl, strict, eye = _pair_masks(d)
    eye_f = jnp.where(eye, 1.0, 0.0).astype(F32)
    _reset_chunk_state(o_ref, h_ref)
    pre = [_deltanet_block_inputs(u_ref[i], dab_ref[i], nea_ref[...], dtb_ref[...], ea_ref[0], eb_ref[0],
                                  ones_ref[...], d) for i in range(CHUNK_BATCH)]
    probs = _chunk_problems()
    pk = lambda name: [pre[i][name][rows, lanes] for i, rows, lanes in probs]
    cat = jnp.concatenate
    g_p = pk("g")
    g_t = [_pack(_bd(x, same).T, same) for x in g_p]
    decay = [jnp.exp(jnp.where(incl, x - y, -jnp.inf)) for x, y in zip(g_p, g_t)]
    m = [_mm_pk_nt(cat([kb, q_], axis=0), [k_], same) for kb, q_, k_ in zip(pk("kb"), pk("q"), pk("k"))]
    _apply_chunk_maps(d, o_ref, h_ref, y_ref, same)
    a_low = [jnp.where(strict, x[0:SCAN_CHUNK] * dc, 0.0) for x, dc in zip(m, decay)]
    attn = [x[SCAN_CHUNK:] * dc for x, dc in zip(m, decay)]
    t_inv = _unit_tri_inverse([-x for x in a_low], eye_f, same)
    sol = [_mm_pk(t, [vb, kbe], same) for t, vb, kbe in zip(t_inv, pk("vb"), pk("kbe"))]
    att_sol = [_mm_pk_fast(a, [s[:, 0:PAIR], s[:, PAIR:]], same) for a, s in zip(attn, sol)]
    ks = [_mm3(kt, s, _TN) for kt, s in zip(pk("kt"), sol)]
    for (i, rows, lanes), ks_i, as_i, qe_i in zip(probs, ks, att_sol, pk("qe")):
        base = MAP_W * (lanes.start // PAIR)
        o_ref[i, rows, base:base + PAIR] = (jnp.where(eye, pre[i]["gam"][rows.start:rows.start + 1, lanes], 0.0)
                                               - _pack(ks_i[:, PAIR:], same))
        o_ref[i, rows, base + PAIR:base + 2 * PAIR] = _pack(ks_i[:, 0:PAIR], same)
        o_ref[i, rows, base + 2 * PAIR:base + 3 * PAIR] = qe_i - as_i[:, PAIR:]
        o_ref[i, rows, base + 3 * PAIR:base + 4 * PAIR] = as_i[:, 0:PAIR]


def _deltanet_scan(u, dab, a_log, dt_bias, ones_bd, nblk_ctx):
    nb, t, _ = u.shape
    pad = 128 - 2 * N_HEADS
    neg_exp_a = jnp.pad(-jnp.exp(a_log.reshape(1, -1)), ((0, 0), (0, pad)))
    dtb = jnp.pad(dt_bias.reshape(1, -1), ((0, 0), (0, pad)))
    col = jnp.arange(128)[None, :, None]
    head = (jnp.arange(BRANCH_W) // HEAD_DIM)[None, None, :]
    dirs = jnp.arange(2)[:, None, None]
    expand_a = (col == dirs * N_HEADS + head).astype(F32)
    expand_b = (col == 2 * N_HEADS + dirs * N_HEADS + head).astype(F32)
    const = lambda shape: pl.BlockSpec(shape, lambda d, b, j: (0, 0))
    per_dir = pl.BlockSpec((1, 128, BRANCH_W), lambda d, b, j: (d, 0, 0))
    in_specs = lambda blk: [pl.BlockSpec((CHUNK_BATCH, TOKEN_BLOCK, 3 * BRANCH_W), lambda d, b, j: (b, blk(d, j), 0)),
                            pl.BlockSpec((CHUNK_BATCH, TOKEN_BLOCK, 128), lambda d, b, j: (b, blk(d, j), 0)),
                            const((1, 128)), const((1, 128)), per_dir, per_dir, const((BRANCH_W, BRANCH_W))]
    operands = (u, dab, neg_exp_a, dtb, expand_a, expand_b, ones_bd)
    return _chunk_scan_call(_deltanet_chunk_kernel, "deltanet_scan", operands, in_specs, nb, t, nblk_ctx)


def _s5_kernel(n_batch, nc_ctx, nc, u_ref, toep_ref, winr_ref, wini_ref, wsor_ref, wsoi_ref,
               lr_ref, li_ref, dt_ref, y_ref, injr, inji, xsr, xsi):
    u = u_ref[0]
    ub = u.astype(BF16)
    for d in range(2):
        injr[d] = jnp.dot(ub, winr_ref[d, 0], preferred_element_type=F32)
        inji[d] = jnp.dot(ub, wini_ref[d, 0], preferred_element_type=F32)
    lam = [(lr_ref[d, 0], li_ref[d, 0]) for d in range(2)]

    def body(s, carry):
        back = jnp.where(s < nc_ctx, nc_ctx - 1 - s, nc - 1 + nc_ctx - s)
        out = []
        for d, chunk in enumerate((s, back)):
            xr, xi = carry[2 * d], carry[2 * d + 1]
            row0 = chunk * n_batch
            if n_batch % 8 == 0:
                row0 = pl.multiple_of(row0, 8)
            rows = pl.ds(row0, n_batch)
            xsr[d, rows, :] = xr
            xsi[d, rows, :] = xi
            lam_r, lam_i = lam[d]
            out += [lam_r * xr - lam_i * xi + injr[d, rows, :], lam_r * xi + lam_i * xr + inji[d, rows, :]]
        return tuple(out)

    zero = jnp.zeros((n_batch, S5_STATE), F32)
    lax.fori_loop(0, nc, body, (zero,) * 4)

    y = u * dt_ref[0]
    for d in range(2):
        y = y + (jnp.dot(ub, toep_ref[d, 0], preferred_element_type=F32)
                 + _dot(xsr[d], wsor_ref[d, 0]) + _dot(xsi[d], wsoi_ref[d, 0]))
    y_ref[0] = y


def _s5_tables(lam_re, lam_im, log_step, b_re, b_im, c_re, c_im):
    cs = S5_CHUNK
    lam = lax.complex(lam_re.astype(F32), lam_im.astype(F32))
    step = jnp.exp(log_step.astype(F32))[..., None]
    tau = jnp.arange(cs + 1, dtype=F32)[:, None, None, None]
    lam_pow = jnp.exp(lam[None] * step[None] * tau)
    lam_bar = lam_pow[1]
    b_bar = ((lam_bar - 1.0) / lam)[..., None] * lax.complex(b_re.astype(F32), b_im.astype(F32))
    c_mat = lax.complex(c_re.astype(F32), c_im.astype(F32))
    kern = jnp.real(jnp.einsum("dghp,tdgp,dgpk->tdghk", c_mat, lam_pow[:cs], b_bar))
    i = jnp.arange(cs)
    lag_f = i[None, :] - i[:, None]
    toeps, winr, wini, wsor, wsoi = [], [], [], [], []
    for d in range(2):
        lag = lag_f if d == 0 else -lag_f
        kd = jnp.where((lag >= 0)[:, :, None, None, None], kern[:, d][jnp.clip(lag, 0, cs - 1)], 0.0)
        toeps.append(kd.transpose(2, 0, 4, 1, 3).reshape(S5_GROUPS, cs * S5_GW, cs * S5_GW))
        pw_in = (cs - 1 - i) if d == 0 else i
        e = lam_pow[pw_in, d][..., None] * b_bar[d][None]
        e = e.transpose(1, 0, 3, 2).reshape(S5_GROUPS, cs * S5_GW, S5_STATE)
        winr.append(jnp.real(e))
        wini.append(jnp.imag(e))
        pw_out = (i + 1) if d == 0 else (cs - i)
        m = c_mat[d][None] * lam_pow[pw_out, d][:, :, None, :]
        m = m.transpose(1, 3, 0, 2).reshape(S5_GROUPS, S5_STATE, cs * S5_GW)
        wsor.append(jnp.real(m))
        wsoi.append(-jnp.imag(m))
    stack = lambda xs, dt: jnp.stack(xs).astype(dt)
    lam_c = lam_pow[cs]
    return (stack(toeps, BF16), stack(winr, BF16), stack(wini, BF16), stack(wsor, BF16), stack(wsoi, BF16),
            jnp.real(lam_c)[:, :, None, :], jnp.imag(lam_c)[:, :, None, :])


def _s5_to_chunks(s5, n_ctx):
    nb, t, _ = s5.shape
    cs, g, hw = S5_CHUNK, S5_GROUPS, S5_GW
    n_lat = t - n_ctx
    rows = n_lat // 64
    c = s5[:, :n_ctx].reshape(nb, n_ctx // cs, cs, g, hw)
    c = c.transpose(3, 1, 0, 2, 4).reshape(g, (n_ctx // cs) * nb, cs * hw)
    l = s5[:, n_ctx:].reshape(nb, rows // cs, cs, 64, g, hw)
    l = l.transpose(4, 3, 1, 0, 2, 5).reshape(g, 64 * (rows // cs) * nb, cs * hw)
    return jnp.concatenate([c, l], axis=1)


def _s5_from_chunks(y, nb, n_ctx, n_lat):
    cs, g, hw = S5_CHUNK, S5_GROUPS, S5_GW
    rows = n_lat // 64
    r_ctx = (n_ctx // cs) * nb
    c = y[:, :r_ctx].reshape(g, n_ctx // cs, nb, cs, hw).transpose(2, 1, 3, 0, 4).reshape(nb, n_ctx, g * hw)
    l = y[:, r_ctx:].reshape(g, 64, rows // cs, nb, cs, hw).transpose(3, 2, 4, 1, 0, 5).reshape(nb, n_lat, g * hw)
    return jnp.concatenate([c, l], axis=1)


def _s5_scan(s5, tables, d_skip, n_ctx):
    nb, t, _ = s5.shape
    u = _s5_to_chunks(s5, n_ctx)
    g, r, w = u.shape
    nc = t // S5_CHUNK
    toep, winr, wini, wsor, wsoi, lr, li = tables
    d_tile = jnp.tile(d_skip.astype(F32).reshape(S5_GROUPS, 1, S5_GW), (1, S5_CHUNK, 1)).reshape(g, 1, w)
    per = lambda a, b: pl.BlockSpec((2, 1, a, b), lambda gi: (0, gi, 0, 0))
    y = pl.pallas_call(
        functools.partial(_s5_kernel, nb, n_ctx // S5_CHUNK, nc),
        out_shape=jax.ShapeDtypeStruct((g, r, w), F32),
        grid=(g,),
        in_specs=[pl.BlockSpec((1, r, w), lambda gi: (gi, 0, 0)),
                  per(w, w), per(w, S5_STATE), per(w, S5_STATE), per(S5_STATE, w), per(S5_STATE, w),
                  per(1, S5_STATE), per(1, S5_STATE),
                  pl.BlockSpec((1, 1, w), lambda gi: (gi, 0, 0))],
        out_specs=pl.BlockSpec((1, r, w), lambda gi: (gi, 0, 0)),
        scratch_shapes=[pltpu.VMEM((2, r, S5_STATE), F32) for _ in range(4)],
        compiler_params=_cparams(("parallel",), VMEM_LIMIT),
        name="s5_scan",
    )(u, toep, winr, wini, wsor, wsoi, lr, li, d_tile)
    return _s5_from_chunks(y, nb, n_ctx, t - n_ctx)


def _merge_kernel(x_ref, mod_ref, g1_ref, ya_ref, yb_ref, ps_ref, oc_ref, dg_ref, y5_ref,
                  wg_ref, wb_ref, wo_ref, avg_ref, ones_ref, rk_ref, gup_ref, lng_ref, lnb_ref,
                  dng_ref, wglu_ref, bglu_ref, o_ref):
    avg = avg_ref[...]

    def branches(i):
        ps = ps_ref[i]
        r = ps[:, 0:256]
        k = ps[:, 256:512]
        v = ps[:, 512:768]
        gl = ps[:, 832:896]
        y = yb_ref[0, i] + yb_ref[1, i]
        dev = y - _dot(y, avg)
        yn = dev * lax.rsqrt(_dot(dev * dev, avg) + RW_GN_EPS) * lng_ref[...] + lnb_ref[...]
        bonus = _dot(r * k * rk_ref[...], ones_ref[...]) * v
        yb = (yn + bonus) * _dot(_sigmoid(gl), gup_ref[...])
        o = oc_ref[0, i] + oc_ref[1, i]
        yc = o * lax.rsqrt(_dot(o * o, avg) + EPS) * dng_ref[...] * _silu(dg_ref[i])
        z = jax.nn.gelu(y5_ref[i])
        yd = z * _sigmoid(_dot(z, wglu_ref[...]) + bglu_ref[...])
        return ya_ref[i], yb, yc, yd

    steps = range(MERGE_BATCH)
    xs = [x_ref[i] for i in steps]
    hs = [_norm_mod(xs[i], g1_ref[...], mod_ref[i, 1:2, :], mod_ref[i, 0:1, :]).astype(BF16) for i in steps]
    ys = [branches(i) for i in steps]
    ms = [jnp.zeros((TOKEN_BLOCK, D_MODEL), F32) for _ in steps]
    for n in range(4):
        for i in steps:
            gate = _sigmoid(jnp.dot(hs[i], wg_ref[0, :, D_MODEL * n:D_MODEL * (n + 1)], preferred_element_type=F32))
            ms[i] = ms[i] + gate * _dot(ys[i][n], wb_ref[0, n])
    for i in steps:
        o_ref[i] = xs[i] + mod_ref[i, 2:3, :] * _dot(ms[i], wo_ref[0])


def _merge(xcat, mod, g1, ya, yb, ps, oc, dg, y5, w_gate, w_branch, w_out, layer, avg_bd, ones_bd,
           r_k, g_up, ln_g, ln_b, dn_g, w_glu, b_glu, nblk_ctx):
    nb, t, _ = xcat.shape
    nblk = t // TOKEN_BLOCK
    tok = lambda w: pl.BlockSpec((MERGE_BATCH, TOKEN_BLOCK, w), lambda b, j: (b, j, 0))
    tok2 = lambda w: pl.BlockSpec((2, MERGE_BATCH, TOKEN_BLOCK, w), lambda b, j: (0, b, j, 0))
    full = lambda a: pl.BlockSpec(a.shape, lambda b, j: (0,) * a.ndim)
    stacked = (w_gate, w_branch, w_out)
    consts = (avg_bd, ones_bd, r_k, g_up, ln_g, ln_b, dn_g, w_glu, b_glu)
    return pl.pallas_call(
        _merge_kernel,
        out_shape=jax.ShapeDtypeStruct(xcat.shape, F32),
        grid=(nb // MERGE_BATCH, nblk),
        in_specs=[tok(D_MODEL), pl.BlockSpec((MERGE_BATCH, N_MOD, D_MODEL), _mod_index(nblk_ctx, nb, MERGE_BATCH)),
                  pl.BlockSpec((1, D_MODEL), lambda b, j: (0, 0)),
                  tok(BRANCH_W), tok2(BRANCH_W), tok(RW_COLS), tok2(BRANCH_W), tok(BRANCH_W), tok(BRANCH_W)]
        + [_layer_spec(a, layer) for a in stacked] + [full(a) for a in consts],
        out_specs=tok(D_MODEL),
        compiler_params=_cparams(("parallel", "arbitrary"), VMEM_LIMIT),
        name="merge_branches",
    )(xcat, mod, g1, ya, yb, ps, oc, dg, y5, *stacked, *consts)


def _route(sel, score):
    s = [sel[e:e + 1, :] for e in range(N_EXPERTS)]
    sc = [score[e:e + 1, :] for e in range(N_EXPERTS)]
    n_groups = N_EXPERTS // EXPERTS_PER_GROUP
    group_score = []
    for g in range(n_groups):
        m = s[EXPERTS_PER_GROUP * g:EXPERTS_PER_GROUP * (g + 1)]
        best = None
        for i in range(EXPERTS_PER_GROUP):
            for j in range(i + 1, EXPERTS_PER_GROUP):
                pair = m[i] + m[j]
                best = pair if best is None else jnp.maximum(best, pair)
        group_score.append(best)
    best_g = jnp.zeros(group_score[0].shape, jnp.int32)
    best_v = group_score[0]
    for g in range(1, n_groups):
        upd = group_score[g] > best_v
        best_g = jnp.where(upd, g, best_g)
        best_v = jnp.where(upd, group_score[g], best_v)
    chosen = []
    den = jnp.zeros_like(best_v)
    for e in range(N_EXPERTS):
        g = e // EXPERTS_PER_GROUP
        rank = jnp.zeros(best_g.shape, jnp.int32)
        for j in range(EXPERTS_PER_GROUP * g, EXPERTS_PER_GROUP * (g + 1)):
            if j == e:
                continue
            ahead = (s[j] > s[e]) if j > e else (s[j] >= s[e])
            rank = rank + jnp.where(ahead, 1, 0)
        pick = jnp.logical_and(best_g == g, rank < 2)
        chosen.append(pick)
        den = den + jnp.where(pick, sc[e], 0.0)
    return jnp.concatenate([jnp.where(chosen[e], sc[e] / den, 0.0) for e in range(N_EXPERTS)], axis=0)


def _moe_kernel(final, x_ref, mod_ref, g2_ref, w1_ref, w3_ref, w2_ref, rwt_ref, rb_ref, exp_ref, fg_ref, o_ref):
    steps = range(MOE_BATCH)
    xs = [x_ref[i] for i in steps]
    hs = [_norm_mod(xs[i], g2_ref[...], mod_ref[i, 4:5, :], mod_ref[i, 3:4, :]) for i in steps]
    scores = [_sigmoid(lax.dot_general(rwt_ref[...], h, _NT, preferred_element_type=F32, precision=HIGHEST))
              for h in hs]
    combs = [_route(sc + rb_ref[...], sc).astype(BF16) for sc in scores]
    hbs = [h.astype(BF16) for h in hs]
    accs = [jnp.zeros((TOKEN_BLOCK, D_MODEL), F32) for _ in steps]
    for q in range(N_EXPERTS // EXPERTS_PER_GROUP):
        experts = range(EXPERTS_PER_GROUP * q, EXPERTS_PER_GROUP * (q + 1))
        cols = slice(D_EXPERT * experts[0], D_EXPERT * (experts[-1] + 1))
        for i in steps:
            cw = lax.dot_general(combs[i], exp_ref[:, cols], _TN, preferred_element_type=F32)
            a1 = jnp.concatenate([jnp.dot(hbs[i], w1_ref[0, e], preferred_element_type=F32) for e in experts], axis=1)
            a3 = jnp.concatenate([jnp.dot(hbs[i], w3_ref[0, e], preferred_element_type=F32) for e in experts], axis=1)
            act = (_silu(a1) * a3 * cw).astype(BF16)
            accs[i] = accs[i] + jnp.dot(act, w2_ref[0, cols, :], preferred_element_type=F32)
    for i in steps:
        y = xs[i] + mod_ref[i, 5:6, :] * accs[i]
        if final:
            y = y * lax.rsqrt(jnp.mean(y * y, axis=-1, keepdims=True) + EPS) * fg_ref[...]
        o_ref[i] = y


def _moe(x1, mod, g2, router_wt, router_b, w1, w3, w2, layer, expand, nblk_ctx, final_g=None):
    nb, t, _ = x1.shape
    nblk = t // TOKEN_BLOCK
    final = final_g is not None
    tok = pl.BlockSpec((MOE_BATCH, TOKEN_BLOCK, D_MODEL), lambda b, j: (b, j, 0))
    full = lambda a: pl.BlockSpec(a.shape, lambda b, j: (0,) * a.ndim, pipeline_mode=pl.Buffered(1))
    fg = (final_g if final else jnp.ones((D_MODEL,), F32)).reshape(1, D_MODEL).astype(F32)
    consts = (router_wt, router_b, expand, fg)
    if final:
        out_shape = jax.ShapeDtypeStruct((nb, t - nblk_ctx * TOKEN_BLOCK, D_MODEL), F32)
        out_spec = pl.BlockSpec((MOE_BATCH, TOKEN_BLOCK, D_MODEL), lambda b, j: (b, jnp.maximum(j - nblk_ctx, 0), 0))
    else:
        out_shape, out_spec = jax.ShapeDtypeStruct(x1.shape, F32), tok
    return pl.pallas_call(
        functools.partial(_moe_kernel, final),
        out_shape=out_shape,
        grid=(nb // MOE_BATCH, nblk),
        in_specs=[tok, pl.BlockSpec((MOE_BATCH, N_MOD, D_MODEL), _mod_index(nblk_ctx, nb, MOE_BATCH)),
                  pl.BlockSpec((1, D_MODEL), lambda b, j: (0, 0))]
        + [_layer_spec(a, layer, pipeline_mode=pl.Buffered(1)) for a in (w1, w3, w2)] + [full(a) for a in consts],
        out_specs=out_spec,
        compiler_params=_cparams(("parallel", "arbitrary"), VMEM_LIMIT),
        name="moe_ffn",
    )(x1, mod, g2, w1, w3, w2, *consts)


def _block_diag_ones(n_blocks, size):
    return jnp.kron(jnp.eye(n_blocks, dtype=F32), jnp.ones((size, size), F32))


def _layer(xcat, cond, n_ctx, layer, lp, wts, router_wt, router_b, consts, final_g=None):
    nb, t, _ = xcat.shape
    nblk_ctx = n_ctx // TOKEN_BLOCK
    mod = _modulation(cond, lp["w_mod"], lp["b_mod"])
    g1 = lp["norm1_g"].reshape(1, D_MODEL)

    fcs, ps, u, dg, s5, dab = _input_projection(xcat, mod, g1, wts["w_mix"], layer, consts["cs"], lp["rw_mu"],
                                                lp["dn_conv"], nblk_ctx)

    ya = jnp.concatenate([_sequence_dft(fcs, 0, n_ctx), _sequence_dft(fcs, n_ctx, t - n_ctx)], axis=1)

    yb = _rwkv_scan(ps, lp["rw_w0"], lp["rw_w_up"], lp["rw_a0"], lp["rw_a_up"], lp["rw_k_k"], lp["rw_k_a"],
                    consts["ones_bd"], nblk_ctx)
    oc = _deltanet_scan(u, dab, lp["dn_a_log"], lp["dn_dt_bias"], consts["ones_bd"], nblk_ctx)
    y5 = _s5_scan(s5, lp["s5_tables"], lp["s5_d"], n_ctx)

    row = lambda a: a.reshape(1, -1).astype(F32)
    x1 = _merge(xcat, mod, g1, ya, yb, ps, oc, dg, y5, wts["w_gate"], wts["w_branch"], wts["w_out"], layer,
                consts["avg_bd"], consts["ones_bd"], row(lp["rw_r_k"]),
                lp["rw_g_up"].astype(BF16), row(lp["rw_ln_g"]), row(lp["rw_ln_b"]),
                row(jnp.tile(lp["dn_norm_g"], N_HEADS)), lp["s5_w_glu"].astype(BF16), row(lp["s5_b_glu"]),
                nblk_ctx)

    return _moe(x1, mod, lp["norm2_g"].reshape(1, D_MODEL), router_wt, router_b, wts["moe_w1"], wts["moe_w3"],
                wts["moe_w2"], layer, consts["expand"], nblk_ctx, final_g)


def _mixer_column_order():
    offs = [0]
    for width in W_IN_SPLITS:
        offs.append(offs[-1] + width)
    order = [0, 1, 2, 3, 6, 4, 5]
    idx = [c for i in order for c in range(offs[i], offs[i + 1])]
    pad = -len(idx) % 128
    keep = [1.0] * len(idx) + [0.0] * pad
    return jnp.asarray(idx + [0] * pad, jnp.int32), jnp.asarray(keep, F32), offs[7]


def kernel(x, c, ctx, c_ctx, w_mod, b_mod, norm1_g, norm2_g, w_in, rw_mu, rw_w0, rw_w_up, rw_a0, rw_a_up, rw_k_k, rw_k_a, rw_r_k, rw_g_up, rw_ln_g, rw_ln_b, dn_conv, dn_a_log, dn_dt_bias, dn_norm_g, s5_lam_re, s5_lam_im, s5_log_step, s5_b_re, s5_b_im, s5_c_re, s5_c_im, s5_d, s5_w_glu, s5_b_glu, w_branch, w_out, router_w, router_b, moe_w1, moe_w3, moe_w2, final_g):
    nb, n_lat, _ = x.shape
    n_ctx = ctx.shape[1]
    depth = w_mod.shape[0]
    assert n_ctx % TOKEN_BLOCK == 0 and n_lat % TOKEN_BLOCK == 0 and (n_lat // 64) % S5_CHUNK == 0
    assert nb % CHUNK_BATCH == 0 and nb % MOE_BATCH == 0 and nb % INPROJ_BATCH == 0 and nb % MERGE_BATCH == 0

    xcat = jnp.concatenate([ctx, x], axis=1).astype(F32)
    ctx_rows = max(MOE_BATCH, INPROJ_BATCH, MERGE_BATCH)
    cond_rows = -(-(nb + ctx_rows) // 8) * 8
    cond = jnp.zeros((cond_rows, D_MODEL), F32).at[:nb].set(c).at[nb:nb + ctx_rows].set(c_ctx)

    j = jnp.arange(FN_GW, dtype=jnp.int32)
    ang = ((j[:, None] * j[None, :]) % FN_GW).astype(F32) * (2.0 * math.pi / FN_GW)
    eye = jnp.eye(BRANCH_W // FN_GW, dtype=F32)
    consts = {
        "cs": jnp.concatenate([jnp.kron(eye, jnp.cos(ang)), jnp.kron(eye, jnp.sin(ang))], axis=1).astype(BF16),
        "ones_bd": _block_diag_ones(N_HEADS, HEAD_DIM).astype(BF16),
        "avg_bd": (_block_diag_ones(N_HEADS, HEAD_DIM) / HEAD_DIM).astype(BF16),
        "expand": jnp.kron(jnp.eye(N_EXPERTS, dtype=F32), jnp.ones((1, D_EXPERT), F32)).astype(BF16),
    }
    router_wt = router_w.T.astype(F32)
    router_bc = router_b.reshape(N_EXPERTS, 1).astype(F32)

    cols, keep, gate0 = _mixer_column_order()
    wts = {
        "w_mix": (jnp.take(w_in, cols, axis=2) * keep).astype(BF16),
        "w_gate": w_in[:, :, gate0:].astype(BF16),
        "w_branch": w_branch.astype(BF16),
        "w_out": w_out.astype(BF16),
        "moe_w1": moe_w1.astype(BF16),
        "moe_w3": moe_w3.astype(BF16),
        "moe_w2": moe_w2.reshape(depth, N_EXPERTS * D_EXPERT, D_MODEL).astype(BF16),
    }
    s5_tables = jax.vmap(_s5_tables)(s5_lam_re, s5_lam_im, s5_log_step, s5_b_re, s5_b_im, s5_c_re, s5_c_im)
    small = dict(w_mod=w_mod, b_mod=b_mod, norm1_g=norm1_g, norm2_g=norm2_g, rw_mu=rw_mu, rw_w0=rw_w0,
                 rw_w_up=rw_w_up, rw_a0=rw_a0, rw_a_up=rw_a_up, rw_k_k=rw_k_k, rw_k_a=rw_k_a, rw_r_k=rw_r_k,
                 rw_g_up=rw_g_up, rw_ln_g=rw_ln_g, rw_ln_b=rw_ln_b, dn_conv=dn_conv, dn_a_log=dn_a_log,
                 dn_dt_bias=dn_dt_bias, dn_norm_g=dn_norm_g, s5_d=s5_d, s5_w_glu=s5_w_glu, s5_b_glu=s5_b_glu)
    for i in range(depth):
        lp = {n: a[i] for n, a in small.items()}
        lp["s5_tables"] = tuple(tb[i] for tb in s5_tables)
        xcat = _layer(xcat, cond, n_ctx, i, lp, wts, router_wt, router_bc, consts,
                      final_g if i == depth - 1 else None)
    return xcat
```

```python
import functools
import math

import jax
import jax.numpy as jnp
from jax import lax
from jax.experimental import pallas as pl
from jax.experimental.pallas import tpu as pltpu

F32 = jnp.float32
BF16 = jnp.bfloat16
HIGHEST = lax.Precision.HIGHEST

D_MODEL = 1024
N_MOD = 6
EPS = 1e-6
BRANCH_W = 256
HEAD_DIM = 64
N_HEADS = 4
FN_GW = 64
RW_COLS = 896
RW_DECAY_SCALE = math.exp(-0.5)
RW_GN_EPS = 64e-5
DN_CONV = 5
S5_GW = 16
S5_GROUPS = 16
S5_STATE = 64
N_EXPERTS = 16
EXPERTS_PER_GROUP = 4
D_EXPERT = 256
W_IN_SPLITS = (256, 896, 768, 256, 8, 8, 256, 4096)

TOKEN_BLOCK = 256
SCAN_CHUNK = 64
S5_CHUNK = 16
HALO = 8
MXU_DEPTH = 256
INPROJ_BATCH = 2
MERGE_BATCH = 2
VMEM_LIMIT = 56 * 1024 * 1024


def _cparams(sem, vmem=None, **kw):
    return pltpu.CompilerParams(dimension_semantics=sem, vmem_limit_bytes=vmem, **kw)


def _dot(a, b):
    return jnp.dot(a.astype(BF16), b.astype(BF16), preferred_element_type=F32)


def _sigmoid(x):
    return jax.nn.sigmoid(x)


def _silu(x):
    return x * jax.nn.sigmoid(x)


def _norm_mod(x, g, scale, shift):
    y = x * lax.rsqrt(jnp.mean(x * x, axis=-1, keepdims=True) + EPS) * g
    return y * (1.0 + scale) + shift


def _layer_spec(a, layer, **kw):
    return pl.BlockSpec((1,) + a.shape[1:], lambda *_: (layer,) + (0,) * (a.ndim - 1), **kw)


def _mod_index(nblk_ctx, n_batch, per_step=1):
    return lambda b, j: (jnp.where(j < nblk_ctx, n_batch // per_step, b), 0, 0)


def _mod_kernel(c_ref, w_ref, b_ref, o_ref):
    o_ref[...] = _dot(_silu(c_ref[...]), w_ref[...]) + b_ref[...]


def _modulation(cond, w_mod, b_mod):
    rows = cond.shape[0]
    n = w_mod.shape[1]
    tn = 512
    out = pl.pallas_call(
        _mod_kernel,
        out_shape=jax.ShapeDtypeStruct((rows, n), F32),
        grid=(n // tn,),
        in_specs=[pl.BlockSpec((rows, D_MODEL), lambda i: (0, 0)),
                  pl.BlockSpec((D_MODEL, tn), lambda i: (0, i)),
                  pl.BlockSpec((1, tn), lambda i: (0, i))],
        out_specs=pl.BlockSpec((rows, tn), lambda i: (0, i)),
        compiler_params=_cparams(("arbitrary",)),
        name="adaln_mod",
    )(cond, w_mod, b_mod.reshape(1, n))
    return out.reshape(rows, N_MOD, D_MODEL)


def _shifted(x, s, halo_prev, halo_next):
    n = x.shape[0]
    if s == 0:
        return x
    rows = lax.broadcasted_iota(jnp.int32, x.shape, 0)
    if s < 0:
        y = pltpu.roll(x, -s, 0)
        for t in range(-s):
            y = jnp.where(rows == t, halo_prev[HALO + s + t:HALO + s + t + 1, :], y)
    else:
        y = pltpu.roll(x, n - s, 0)
        for t in range(s):
            y = jnp.where(rows == n - s + t, halo_next[t:t + 1, :], y)
    return y


def _inproj_kernel(nblk_ctx, nblk, x_ref, xp_ref, xn_ref, mod_ref, g_ref, w_ref, cs_ref, mu_ref, cw_ref,
                   fcs_ref, ps_ref, u_ref, dg_ref, s5_ref, dab_ref):
    j = pl.program_id(1)
    first = jnp.logical_or(j == 0, j == nblk_ctx)
    last = jnp.logical_or(j == nblk_ctx - 1, j == nblk - 1)
    pv = jnp.where(first, 0.0, 1.0).astype(F32)
    nv = jnp.where(last, 0.0, 1.0).astype(F32)

    steps = range(INPROJ_BATCH)
    h_f = [_norm_mod(jnp.concatenate([xp_ref[i], x_ref[i], xn_ref[i]], axis=0), g_ref[...],
                     mod_ref[i, 1:2, :], mod_ref[i, 0:1, :]) for i in steps]
    hs = [x[HALO:HALO + TOKEN_BLOCK].astype(BF16) for x in h_f]
    h_alls = [x.astype(BF16) for x in h_f]

    for i in steps:
        fn = jnp.dot(hs[i], w_ref[0, :, 0:256], preferred_element_type=F32)
        fcs_ref[i] = _dot(fn, cs_ref[...]).astype(BF16)
        dg_ref[i] = jnp.dot(hs[i], w_ref[0, :, 1920:2176], preferred_element_type=F32)
        s5_ref[i] = jnp.dot(hs[i], w_ref[0, :, 2176:2432], preferred_element_type=F32)
        dab_ref[i] = jnp.dot(hs[i], w_ref[0, :, 2432:2560], preferred_element_type=F32)

    def with_halo(h_all, cols):
        p = jnp.dot(h_all, w_ref[0, :, cols], preferred_element_type=F32)
        return p[HALO:HALO + TOKEN_BLOCK], p[0:HALO] * pv, p[HALO + TOKEN_BLOCK:] * nv

    for i, (p, hp, hn) in enumerate([with_halo(x, slice(256, 1152)) for x in h_alls]):
        ps_ref[i] = (p + mu_ref[0:1, :] * (_shifted(p, -1, hp, hn) - p)
                     + mu_ref[1:2, :] * (_shifted(p, 1, hp, hn) - p))

    pad = DN_CONV // 2
    for i, (q, hp, hn) in enumerate([with_halo(x, slice(1152, 1920)) for x in h_alls]):
        acc = cw_ref[pad:pad + 1, :] * q
        for t in range(DN_CONV):
            if t != pad:
                acc = acc + cw_ref[t:t + 1, :] * _shifted(q, t - pad, hp, hn)
        u_ref[i] = _silu(acc)


def _input_projection(xcat, mod, g1, w_mix, layer, cs, mu, conv_w, nblk_ctx):
    nb, t, _ = xcat.shape
    nblk = t // TOKEN_BLOCK
    per = TOKEN_BLOCK // HALO
    last_halo = t // HALO - 1
    widths = (512, RW_COLS, 3 * BRANCH_W, 256, 256, 128)
    dtypes = (BF16, F32, F32, F32, F32, F32)
    tok = lambda w: pl.BlockSpec((INPROJ_BATCH, TOKEN_BLOCK, w), lambda b, j: (b, j, 0))
    full = lambda a: pl.BlockSpec(a.shape, lambda b, j: (0,) * a.ndim)
    return pl.pallas_call(
        functools.partial(_inproj_kernel, nblk_ctx, nblk),
        out_shape=[jax.ShapeDtypeStruct((nb, t, w), dt) for w, dt in zip(widths, dtypes)],
        grid=(nb // INPROJ_BATCH, nblk),
        in_specs=[tok(D_MODEL),
                  pl.BlockSpec((INPROJ_BATCH, HALO, D_MODEL), lambda b, j: (b, jnp.maximum(j * per - 1, 0), 0)),
                  pl.BlockSpec((INPROJ_BATCH, HALO, D_MODEL), lambda b, j: (b, jnp.minimum((j + 1) * per, last_halo), 0)),
                  pl.BlockSpec((INPROJ_BATCH, N_MOD, D_MODEL), _mod_index(nblk_ctx, nb, INPROJ_BATCH)),
                  pl.BlockSpec((1, D_MODEL), lambda b, j: (0, 0)),
                  _layer_spec(w_mix, layer), full(cs), full(mu), full(conv_w)],
        out_specs=[tok(w) for w in widths],
        compiler_params=_cparams(("parallel", "arbitrary"), VMEM_LIMIT),
        name="norm1_inproj",
    )(xcat, xcat, xcat, mod, g1, w_mix, cs, mu, conv_w)


def _dft_kernel(n_batch, wc_ref, ws_ref, u_ref, o_ref):
    @pl.when(pl.program_id(1) == 0)
    def _():
        o_ref[...] = jnp.zeros_like(o_ref)

    w = jnp.concatenate([wc_ref[...], ws_ref[...]], axis=1)
    for b in range(n_batch):
        u = u_ref[b]
        o_ref[b] += jnp.dot(w, jnp.concatenate([u[:, 0:256], u[:, 256:512]], axis=0), preferred_element_type=F32)


def _dft_tables(n):
    n2 = math.isqrt(n)
    assert n2 * n2 == n
    k = jnp.arange(n, dtype=jnp.int32)[None, :]
    j = jnp.arange(n2, dtype=jnp.int32)[:, None]
    ang_a = ((j * k) % n2).astype(F32) * (2.0 * math.pi / n2)
    ang_b = ((j * k) % n).astype(F32) * (2.0 * math.pi / n)
    ca, sa, cb, sb = jnp.cos(ang_a)[:, None], jnp.sin(ang_a)[:, None], jnp.cos(ang_b)[None], jnp.sin(ang_b)[None]
    scale = 1.0 / math.sqrt(n * FN_GW)
    cos = ((ca * cb - sa * sb) * scale).reshape(n, n)
    sin = ((sa * cb + ca * sb) * scale).reshape(n, n)
    return cos.astype(BF16), (-sin).astype(BF16)


def _sequence_dft(fcs, row0, n):
    nb = fcs.shape[0]
    wc, ws = _dft_tables(n)
    tm = min(n, 1024)
    tk = min(n, 256)
    assert n % tm == 0 and n % tk == 0 and row0 % tk == 0
    kb0 = row0 // tk
    return pl.pallas_call(
        functools.partial(_dft_kernel, nb),
        out_shape=jax.ShapeDtypeStruct((nb, n, BRANCH_W), F32),
        grid=(n // tm, n // tk),
        in_specs=[pl.BlockSpec((tm, tk), lambda m, k: (m, k)),
                  pl.BlockSpec((tm, tk), lambda m, k: (m, k)),
                  pl.BlockSpec((nb, tk, 512), lambda m, k: (0, k + kb0, 0))],
        out_specs=pl.BlockSpec((nb, tm, BRANCH_W), lambda m, k: (0, m, 0)),
        compiler_params=_cparams(("parallel", "arbitrary"), VMEM_LIMIT),
        name="fourier_seq_dft",
    )(wc, ws, fcs)


def _chunk_order(nc_ctx, nc):
    def order(d, c):
        back = jnp.where(c < nc_ctx, nc_ctx - 1 - c, nc - 1 + nc_ctx - c)
        return jnp.where(d == 0, c, back)
    return order


PAIR = 2 * HEAD_DIM
N_PAIRS = N_HEADS // 2
CHUNKS_PER_BLOCK = TOKEN_BLOCK // SCAN_CHUNK
MAP_W = 4 * PAIR
CHUNK_BATCH = 4
MOE_BATCH = 4
_NN = (((1,), (0,)), ((), ()))
_NT = (((1,), (1,)), ((), ()))
_TN = (((0,), (0,)), ((), ()))
_LOG2_CHUNK = int(math.log2(SCAN_CHUNK))


def _split(x):
    hi = x.astype(BF16)
    return hi, (x - hi.astype(F32)).astype(BF16)


def _mm3(a, b, dims=_NN):
    a_hi, a_lo = _split(a)
    b_hi, b_lo = _split(b)
    dg = lambda x, y: lax.dot_general(x, y, dims, preferred_element_type=F32)
    ca, cb = dims[0][0][0], dims[0][1][0]
    if a.shape[ca] <= MXU_DEPTH // 2:
        return (dg(jnp.concatenate([a_hi, a_lo], axis=ca), jnp.concatenate([b_hi, b_hi], axis=cb))
                + dg(a_hi, b_lo))
    return dg(a_hi, b_hi) + dg(a_hi, b_lo) + dg(a_lo, b_hi)


def _split3(x):
    hi = x.astype(BF16)
    rest = x - hi.astype(F32)
    mid = rest.astype(BF16)
    return hi, mid, (rest - mid.astype(F32)).astype(BF16)


def _select_rows(op, x):
    op = op.astype(BF16)
    hi, mid, lo = _split3(x)
    dot = lambda y: jnp.dot(op, y, preferred_element_type=F32)
    return dot(hi) + dot(mid) + dot(lo)


def _spread_cols(x, sel):
    sel = sel.astype(BF16)
    hi, mid, lo = _split3(x)
    assert x.shape[1] <= MXU_DEPTH // 2
    return (jnp.dot(jnp.concatenate([hi, mid], axis=1), jnp.concatenate([sel, sel], axis=0),
                    preferred_element_type=F32)
            + jnp.dot(lo, sel, preferred_element_type=F32))


def _pair_masks(d):
    row = lax.broadcasted_iota(jnp.int32, (PAIR, PAIR), 0)
    col = lax.broadcasted_iota(jnp.int32, (PAIR, PAIR), 1)
    same = jnp.right_shift(row, _LOG2_CHUNK) == jnp.right_shift(col, _LOG2_CHUNK)
    t_row = lax.broadcasted_iota(jnp.int32, (SCAN_CHUNK, PAIR), 0)
    t_col = jnp.bitwise_and(lax.broadcasted_iota(jnp.int32, (SCAN_CHUNK, PAIR), 1), SCAN_CHUNK - 1)
    delta = (t_row - t_col) * (1 - 2 * d)
    return same, delta >= 0, delta > 0, t_row == t_col


def _block_time_operators(d):
    row = lax.broadcasted_iota(jnp.int32, (TOKEN_BLOCK, TOKEN_BLOCK), 0)
    col = lax.broadcasted_iota(jnp.int32, (TOKEN_BLOCK, TOKEN_BLOCK), 1)
    same = jnp.right_shift(row, _LOG2_CHUNK) == jnp.right_shift(col, _LOG2_CHUNK)
    delta = (jnp.bitwise_and(row, SCAN_CHUNK - 1) - jnp.bitwise_and(col, SCAN_CHUNK - 1)) * (1 - 2 * d)
    return jnp.where(jnp.logical_and(same, delta >= 0), 1.0, 0.0).astype(F32)


def _chunk_problems():
    return [(i, slice(SCAN_CHUNK * c, SCAN_CHUNK * (c + 1)), slice(PAIR * pr, PAIR * (pr + 1)))
            for i in range(CHUNK_BATCH) for c in range(CHUNKS_PER_BLOCK) for pr in range(N_PAIRS)]


def _bd(x, same):
    return jnp.where(same, jnp.concatenate([x, x], axis=0), jnp.zeros((), x.dtype))


def _pack(x, same):
    x = jnp.where(same, x, 0.0)
    return x[0:HEAD_DIM] + x[HEAD_DIM:PAIR]


def _mm_pk(a, bs, same):
    a_hi, a_lo = _split(a)
    parts = [_split(b) for b in bs]
    r_hi = jnp.concatenate([_bd(hi, same) for hi, _ in parts], axis=1)
    r_lo = jnp.concatenate([_bd(lo, same) for _, lo in parts], axis=1)
    return (jnp.dot(jnp.concatenate([a_hi, a_lo], axis=1), jnp.concatenate([r_hi, r_hi], axis=0),
                    preferred_element_type=F32)
            + jnp.dot(a_hi, r_lo, preferred_element_type=F32))


def _mm_pk_fast(a, bs, same):
    rhs = jnp.concatenate([_bd(b.astype(BF16), same) for b in bs], axis=1)
    return jnp.dot(a.astype(BF16), rhs, preferred_element_type=F32)


def _mm_pk_nt(a, bs, same):
    a_hi, a_lo = _split(a)
    parts = [_split(b) for b in bs]
    r_hi = jnp.concatenate([_bd(hi, same) for hi, _ in parts], axis=0)
    r_lo = jnp.concatenate([_bd(lo, same) for _, lo in parts], axis=0)
    dg = lambda x, y: lax.dot_general(x, y, _NT, preferred_element_type=F32)
    return dg(jnp.concatenate([a_hi, a_lo], axis=1), jnp.concatenate([r_hi, r_hi], axis=1)) + dg(a_hi, r_lo)


def _head_sums(x, ones):
    hi, lo = _split(x)
    return jnp.dot(hi, ones, preferred_element_type=F32) + jnp.dot(lo, ones, preferred_element_type=F32)


def _chunk_cumsum(d, x):
    g = _select_rows(_block_time_operators(d), x)
    last = [jnp.where(d == 0, g[SCAN_CHUNK * (c + 1) - 1:SCAN_CHUNK * (c + 1)], g[SCAN_CHUNK * c:SCAN_CHUNK * c + 1])
            for c in range(CHUNKS_PER_BLOCK)]
    return g, jnp.concatenate([jnp.broadcast_to(r, (SCAN_CHUNK, r.shape[1])) for r in last], axis=0)


def _unit_tri_inverse(n_pks, eye_pk, same):
    xs = [eye_pk + n for n in n_pks]
    ps = [_mm_pk(n, [n], same) for n in n_pks]
    for level in range(1, _LOG2_CHUNK):
        if level + 1 < _LOG2_CHUNK:
            xps = [_mm_pk(p, [x, p], same) for x, p in zip(xs, ps)]
            xs = [x + xp[:, 0:PAIR] for x, xp in zip(xs, xps)]
            ps = [xp[:, PAIR:2 * PAIR] for xp in xps]
        else:
            xs = [x + _mm_pk(p, [x], same) for x, p in zip(xs, ps)]
    return xs


def _reset_chunk_state(maps_ref, h_ref):
    @pl.when(pl.program_id(2) == 0)
    def _():
        h_ref[...] = jnp.zeros_like(h_ref)
        maps_ref[...] = jnp.zeros_like(maps_ref)


def _apply_chunk_maps(d, maps_ref, h_ref, y_ref, same):
    chains = [(i, pr) for i in range(CHUNK_BATCH) for pr in range(N_PAIRS)]
    states = [h_ref[i, pr] for i, pr in chains]
    for k in range(CHUNKS_PER_BLOCK):
        row0 = pl.multiple_of(jnp.where(d == 0, k, CHUNKS_PER_BLOCK - 1 - k) * SCAN_CHUNK, SCAN_CHUNK)
        rows = pl.ds(row0, SCAN_CHUNK)
        part = lambda i, pr, n: maps_ref[i, rows, MAP_W * pr + PAIR * n:MAP_W * pr + PAIR * (n + 1)]
        outs = [_mm_pk(jnp.concatenate([part(i, pr, 0), part(i, pr, 2)], axis=0), [h], same)
                for (i, pr), h in zip(chains, states)]
        states = [out[0:SCAN_CHUNK] + part(i, pr, 1) for out, (i, pr) in zip(outs, chains)]
        for out, (i, pr) in zip(outs, chains):
            y_ref[0, i, rows, PAIR * pr:PAIR * (pr + 1)] = out[SCAN_CHUNK:] + part(i, pr, 3)
    for (i, pr), h in zip(chains, states):
        h_ref[i, pr] = h


def _chunk_scan_call(kernel, name, operands, in_specs, nb, t, nblk_ctx):
    nblk = t // TOKEN_BLOCK
    order = _chunk_order(nblk_ctx, nblk)
    blk_in = lambda d, j: order(d, jnp.minimum(j, nblk - 1))
    blk_out = lambda d, j: order(d, jnp.maximum(j - 1, 0))
    return pl.pallas_call(
        kernel,
        out_shape=jax.ShapeDtypeStruct((2, nb, t, BRANCH_W), F32),
        grid=(2, nb // CHUNK_BATCH, nblk + 1),
        in_specs=in_specs(blk_in),
        out_specs=pl.BlockSpec((1, CHUNK_BATCH, TOKEN_BLOCK, BRANCH_W), lambda d, b, j: (d, b, blk_out(d, j), 0)),
        scratch_shapes=[pltpu.VMEM((CHUNK_BATCH, TOKEN_BLOCK, N_PAIRS * MAP_W), F32),
                        pltpu.VMEM((CHUNK_BATCH, N_PAIRS, HEAD_DIM, PAIR), F32)],
        compiler_params=_cparams(("parallel", "parallel", "arbitrary"), VMEM_LIMIT),
        name=name,
    )(*operands)


def _rwkv_block_inputs(p, w0, w_up, a0, a_up, k_k, k_a, ones, d):
    r = p[:, 0:256]
    k = p[:, 256:512]
    v = p[:, 512:768]
    wl = p[:, 768:800]
    al = p[:, 800:832]
    lw = -RW_DECAY_SCALE * _sigmoid(w0 + _dot(jnp.tanh(wl), w_up))
    a = _sigmoid(a0 + _dot(al, a_up))
    kkp = k * k_k
    kk = kkp * lax.rsqrt(_head_sums(kkp * kkp, ones) + EPS)
    kmod = k * (1.0 + (a - 1.0) * k_a)
    alpha = -(a * kk)
    g, g_tot = _chunk_cumsum(d, lw)
    e_neg = jnp.exp(-g)
    e_tail = jnp.exp(g_tot - g)
    return dict(b=kk * jnp.exp(g - lw), r=r * jnp.exp(g), kh=kmod * e_neg, ah=alpha * e_neg,
                kt=kmod * e_tail, at=alpha * e_tail, v=v, gam=jnp.exp(g_tot))


def _rwkv_chunk_kernel(p_ref, w0_ref, wup_ref, a0_ref, aup_ref, kk_ref, ka_ref, ones_ref, y_ref, o_ref, h_ref):
    d = pl.program_id(0)
    same, incl, strict, eye = _pair_masks(d)
    eye_f = jnp.where(eye, 1.0, 0.0).astype(F32)
    zeros = jnp.zeros((SCAN_CHUNK, PAIR), F32)
    _reset_chunk_state(o_ref, h_ref)
    pre = [_rwkv_block_inputs(p_ref[i], w0_ref[0], wup_ref[0], a0_ref[0], aup_ref[0], kk_ref[...], ka_ref[...],
                              ones_ref[...], d) for i in range(CHUNK_BATCH)]
    probs = _chunk_problems()
    pk = lambda name: [pre[i][name][rows, lanes] for i, rows, lanes in probs]
    cat = jnp.concatenate
    b_t, r_t, k_h, a_h, k_t, a_t, v_p = pk("b"), pk("r"), pk("kh"), pk("ah"), pk("kt"), pk("at"), pk("v")
    m = [_mm_pk_nt(cat([b, r_], axis=0), [kh, ah], same) for b, r_, kh, ah in zip(b_t, r_t, k_h, a_h)]
    _apply_chunk_maps(d, o_ref, h_ref, y_ref, same)
    a_bk = [jnp.where(strict, x[0:SCAN_CHUNK, 0:PAIR], 0.0) for x in m]
    a_ba = [jnp.where(strict, x[0:SCAN_CHUNK, PAIR:], 0.0) for x in m]
    a_rk = [jnp.where(incl, x[SCAN_CHUNK:, 0:PAIR], 0.0) for x in m]
    a_ra = [jnp.where(incl, x[SCAN_CHUNK:, PAIR:], 0.0) for x in m]
    av = [_mm_pk(cat([x, y], axis=0), [vp], same) for x, y, vp in zip(a_bk, a_rk, v_p)]
    t_inv = _unit_tri_inverse(a_ba, eye_f, same)
    sol = [_mm_pk(t, [b, x[0:SCAN_CHUNK]], same) for t, b, x in zip(t_inv, b_t, av)]
    qy = [cat([r_, x[SCAN_CHUNK:]], axis=1) + _mm_pk_fast(a, [s[:, 0:PAIR], s[:, PAIR:]], same)
          for r_, x, a, s in zip(r_t, av, a_ra, sol)]
    kb = [_mm3(cat([at, kt], axis=0), cat([s, cat([zeros, vp], axis=1)], axis=0), _TN)
          for at, kt, s, vp in zip(a_t, k_t, sol, v_p)]
    for (i, rows, lanes), kb_i, qy_i in zip(probs, kb, qy):
        base = MAP_W * (lanes.start // PAIR)
        o_ref[i, rows, base:base + PAIR] = (_pack(kb_i[:, 0:PAIR], same)
                                               + jnp.where(eye, pre[i]["gam"][rows.start:rows.start + 1, lanes], 0.0))
        o_ref[i, rows, base + PAIR:base + 2 * PAIR] = _pack(kb_i[:, PAIR:], same)
        o_ref[i, rows, base + 2 * PAIR:base + 4 * PAIR] = qy_i


def _rwkv_scan(ps, w0, w_up, a0, a_up, k_k, k_a, ones_bd, nblk_ctx):
    nb, t, _ = ps.shape
    per_dir = lambda shape: pl.BlockSpec((1,) + shape, lambda d, b, j: (d, 0, 0))
    const = lambda shape: pl.BlockSpec(shape, lambda d, b, j: (0, 0))
    in_specs = lambda blk: [pl.BlockSpec((CHUNK_BATCH, TOKEN_BLOCK, RW_COLS), lambda d, b, j: (b, blk(d, j), 0)),
                            per_dir((1, BRANCH_W)), per_dir((32, BRANCH_W)),
                            per_dir((1, BRANCH_W)), per_dir((32, BRANCH_W)),
                            const((1, BRANCH_W)), const((1, BRANCH_W)), const((BRANCH_W, BRANCH_W))]
    operands = (ps, w0.reshape(2, 1, BRANCH_W), w_up, a0.reshape(2, 1, BRANCH_W), a_up,
                k_k.reshape(1, BRANCH_W), k_a.reshape(1, BRANCH_W), ones_bd)
    return _chunk_scan_call(_rwkv_chunk_kernel, "rwkv7_scan", operands, in_specs, nb, t, nblk_ctx)


def _deltanet_block_inputs(u, dab, neg_exp_a, dt_bias, expand_a, expand_b, ones, d):
    log_a8 = neg_exp_a * jax.nn.softplus(dab + dt_bias)
    la = _spread_cols(log_a8, expand_a)
    beta = _spread_cols(_sigmoid(dab), expand_b)
    q = u[:, 0:256]
    k = u[:, 256:512]
    v = u[:, 512:768]
    q = q * lax.rsqrt(_head_sums(q * q, ones) + EPS) * (HEAD_DIM ** -0.5)
    k = k * lax.rsqrt(_head_sums(k * k, ones) + EPS)
    g, g_tot = _chunk_cumsum(d, la)
    e_g = jnp.exp(g)
    kb = k * beta
    return dict(g=g, q=q, k=k, kb=kb, vb=v * beta, kbe=kb * e_g, qe=q * e_g, kt=k * jnp.exp(g_tot - g),
                gam=jnp.exp(g_tot))


def _deltanet_chunk_kernel(u_ref, dab_ref, nea_ref, dtb_ref, ea_ref, eb_ref, ones_ref, y_ref, o_ref, h_ref):
    d = pl.program_id(0)
    same, incl, strict, eye = _pair_masks(d)
    eye_f = jnp.where(eye, 1.0, 0.0).astype(F32)
    _reset_chunk_state(o_ref, h_ref)
    pre = [_deltanet_block_inputs(u_ref[i], dab_ref[i], nea_ref[...], dtb_ref[...], ea_ref[0], eb_ref[0],
                                  ones_ref[...], d) for i in range(CHUNK_BATCH)]
    probs = _chunk_problems()
    pk = lambda name: [pre[i][name][rows, lanes] for i, rows, lanes in probs]
    cat = jnp.concatenate
    g_p = pk("g")
    g_t = [_pack(_bd(x, same).T, same) for x in g_p]
    decay = [jnp.exp(jnp.where(incl, x - y, -jnp.inf)) for x, y in zip(g_p, g_t)]
    m = [_mm_pk_nt(cat([kb, q_], axis=0), [k_], same) for kb, q_, k_ in zip(pk("kb"), pk("q"), pk("k"))]
    _apply_chunk_maps(d, o_ref, h_ref, y_ref, same)
    a_low = [jnp.where(strict, x[0:SCAN_CHUNK] * dc, 0.0) for x, dc in zip(m, decay)]
    attn = [x[SCAN_CHUNK:] * dc for x, dc in zip(m, decay)]
    t_inv = _unit_tri_inverse([-x for x in a_low], eye_f, same)
    sol = [_mm_pk(t, [vb, kbe], same) for t, vb, kbe in zip(t_inv, pk("vb"), pk("kbe"))]
    att_sol = [_mm_pk_fast(a, [s[:, 0:PAIR], s[:, PAIR:]], same) for a, s in zip(attn, sol)]
    ks = [_mm3(kt, s, _TN) for kt, s in zip(pk("kt"), sol)]
    for (i, rows, lanes), ks_i, as_i, qe_i in zip(probs, ks, att_sol, pk("qe")):
        base = MAP_W * (lanes.start // PAIR)
        o_ref[i, rows, base:base + PAIR] = (jnp.where(eye, pre[i]["gam"][rows.start:rows.start + 1, lanes], 0.0)
                                               - _pack(ks_i[:, PAIR:], same))
        o_ref[i, rows, base + PAIR:base + 2 * PAIR] = _pack(ks_i[:, 0:PAIR], same)
        o_ref[i, rows, base + 2 * PAIR:base + 3 * PAIR] = qe_i - as_i[:, PAIR:]
        o_ref[i, rows, base + 3 * PAIR:base + 4 * PAIR] = as_i[:, 0:PAIR]


def _deltanet_scan(u, dab, a_log, dt_bias, ones_bd, nblk_ctx):
    nb, t, _ = u.shape
    pad = 128 - 2 * N_HEADS
    neg_exp_a = jnp.pad(-jnp.exp(a_log.reshape(1, -1)), ((0, 0), (0, pad)))
    dtb = jnp.pad(dt_bias.reshape(1, -1), ((0, 0), (0, pad)))
    col = jnp.arange(128)[None, :, None]
    head = (jnp.arange(BRANCH_W) // HEAD_DIM)[None, None, :]
    dirs = jnp.arange(2)[:, None, None]
    expand_a = (col == dirs * N_HEADS + head).astype(F32)
    expand_b = (col == 2 * N_HEADS + dirs * N_HEADS + head).astype(F32)
    const = lambda shape: pl.BlockSpec(shape, lambda d, b, j: (0, 0))
    per_dir = pl.BlockSpec((1, 128, BRANCH_W), lambda d, b, j: (d, 0, 0))
    in_specs = lambda blk: [pl.BlockSpec((CHUNK_BATCH, TOKEN_BLOCK, 3 * BRANCH_W), lambda d, b, j: (b, blk(d, j), 0)),
                            pl.BlockSpec((CHUNK_BATCH, TOKEN_BLOCK, 128), lambda d, b, j: (b, blk(d, j), 0)),
                            const((1, 128)), const((1, 128)), per_dir, per_dir, const((BRANCH_W, BRANCH_W))]
    operands = (u, dab, neg_exp_a, dtb, expand_a, expand_b, ones_bd)
    return _chunk_scan_call(_deltanet_chunk_kernel, "deltanet_scan", operands, in_specs, nb, t, nblk_ctx)


def _s5_kernel(n_batch, nc_ctx, nc, u_ref, toep_ref, winr_ref, wini_ref, wsor_ref, wsoi_ref,
               lr_ref, li_ref, dt_ref, y_ref, injr, inji, xsr, xsi):
    u = u_ref[0]
    ub = u.astype(BF16)
    for d in range(2):
        injr[d] = jnp.dot(ub, winr_ref[d, 0], preferred_element_type=F32)
        inji[d] = jnp.dot(ub, wini_ref[d, 0], preferred_element_type=F32)
    lam = [(lr_ref[d, 0], li_ref[d, 0]) for d in range(2)]

    def body(s, carry):
        back = jnp.where(s < nc_ctx, nc_ctx - 1 - s, nc - 1 + nc_ctx - s)
        out = []
        for d, chunk in enumerate((s, back)):
            xr, xi = carry[2 * d], carry[2 * d + 1]
            row0 = chunk * n_batch
            if n_batch % 8 == 0:
                row0 = pl.multiple_of(row0, 8)
            rows = pl.ds(row0, n_batch)
            xsr[d, rows, :] = xr
            xsi[d, rows, :] = xi
            lam_r, lam_i = lam[d]
            out += [lam_r * xr - lam_i * xi + injr[d, rows, :], lam_r * xi + lam_i * xr + inji[d, rows, :]]
        return tuple(out)

    zero = jnp.zeros((n_batch, S5_STATE), F32)
    lax.fori_loop(0, nc, body, (zero,) * 4)

    y = u * dt_ref[0]
    for d in range(2):
        y = y + (jnp.dot(ub, toep_ref[d, 0], preferred_element_type=F32)
                 + _dot(xsr[d], wsor_ref[d, 0]) + _dot(xsi[d], wsoi_ref[d, 0]))
    y_ref[0] = y


def _s5_tables(lam_re, lam_im, log_step, b_re, b_im, c_re, c_im):
    cs = S5_CHUNK
    lam = lax.complex(lam_re.astype(F32), lam_im.astype(F32))
    step = jnp.exp(log_step.astype(F32))[..., None]
    tau = jnp.arange(cs + 1, dtype=F32)[:, None, None, None]
    lam_pow = jnp.exp(lam[None] * step[None] * tau)
    lam_bar = lam_pow[1]
    b_bar = ((lam_bar - 1.0) / lam)[..., None] * lax.complex(b_re.astype(F32), b_im.astype(F32))
    c_mat = lax.complex(c_re.astype(F32), c_im.astype(F32))
    kern = jnp.real(jnp.einsum("dghp,tdgp,dgpk->tdghk", c_mat, lam_pow[:cs], b_bar))
    i = jnp.arange(cs)
    lag_f = i[None, :] - i[:, None]
    toeps, winr, wini, wsor, wsoi = [], [], [], [], []
    for d in range(2):
        lag = lag_f if d == 0 else -lag_f
        kd = jnp.where((lag >= 0)[:, :, None, None, None], kern[:, d][jnp.clip(lag, 0, cs - 1)], 0.0)
        toeps.append(kd.transpose(2, 0, 4, 1, 3).reshape(S5_GROUPS, cs * S5_GW, cs * S5_GW))
        pw_in = (cs - 1 - i) if d == 0 else i
        e = lam_pow[pw_in, d][..., None] * b_bar[d][None]
        e = e.transpose(1, 0, 3, 2).reshape(S5_GROUPS, cs * S5_GW, S5_STATE)
        winr.append(jnp.real(e))
        wini.append(jnp.imag(e))
        pw_out = (i + 1) if d == 0 else (cs - i)
        m = c_mat[d][None] * lam_pow[pw_out, d][:, :, None, :]
        m = m.transpose(1, 3, 0, 2).reshape(S5_GROUPS, S5_STATE, cs * S5_GW)
        wsor.append(jnp.real(m))
        wsoi.append(-jnp.imag(m))
    stack = lambda xs, dt: jnp.stack(xs).astype(dt)
    lam_c = lam_pow[cs]
    return (stack(toeps, BF16), stack(winr, BF16), stack(wini, BF16), stack(wsor, BF16), stack(wsoi, BF16),
            jnp.real(lam_c)[:, :, None, :], jnp.imag(lam_c)[:, :, None, :])


def _s5_to_chunks(s5, n_ctx):
    nb, t, _ = s5.shape
    cs, g, hw = S5_CHUNK, S5_GROUPS, S5_GW
    n_lat = t - n_ctx
    rows = n_lat // 64
    c = s5[:, :n_ctx].reshape(nb, n_ctx // cs, cs, g, hw)
    c = c.transpose(3, 1, 0, 2, 4).reshape(g, (n_ctx // cs) * nb, cs * hw)
    l = s5[:, n_ctx:].reshape(nb, rows // cs, cs, 64, g, hw)
    l = l.transpose(4, 3, 1, 0, 2, 5).reshape(g, 64 * (rows // cs) * nb, cs * hw)
    return jnp.concatenate([c, l], axis=1)


def _s5_from_chunks(y, nb, n_ctx, n_lat):
    cs, g, hw = S5_CHUNK, S5_GROUPS, S5_GW
    rows = n_lat // 64
    r_ctx = (n_ctx // cs) * nb
    c = y[:, :r_ctx].reshape(g, n_ctx // cs, nb, cs, hw).transpose(2, 1, 3, 0, 4).reshape(nb, n_ctx, g * hw)
    l = y[:, r_ctx:].reshape(g, 64, rows // cs, nb, cs, hw).transpose(3, 2, 4, 1, 0, 5).reshape(nb, n_lat, g * hw)
    return jnp.concatenate([c, l], axis=1)


def _s5_scan(s5, tables, d_skip, n_ctx):
    nb, t, _ = s5.shape
    u = _s5_to_chunks(s5, n_ctx)
    g, r, w = u.shape
    nc = t // S5_CHUNK
    toep, winr, wini, wsor, wsoi, lr, li = tables
    d_tile = jnp.tile(d_skip.astype(F32).reshape(S5_GROUPS, 1, S5_GW), (1, S5_CHUNK, 1)).reshape(g, 1, w)
    per = lambda a, b: pl.BlockSpec((2, 1, a, b), lambda gi: (0, gi, 0, 0))
    y = pl.pallas_call(
        functools.partial(_s5_kernel, nb, n_ctx // S5_CHUNK, nc),
        out_shape=jax.ShapeDtypeStruct((g, r, w), F32),
        grid=(g,),
        in_specs=[pl.BlockSpec((1, r, w), lambda gi: (gi, 0, 0)),
                  per(w, w), per(w, S5_STATE), per(w, S5_STATE), per(S5_STATE, w), per(S5_STATE, w),
                  per(1, S5_STATE), per(1, S5_STATE),
                  pl.BlockSpec((1, 1, w), lambda gi: (gi, 0, 0))],
        out_specs=pl.BlockSpec((1, r, w), lambda gi: (gi, 0, 0)),
        scratch_shapes=[pltpu.VMEM((2, r, S5_STATE), F32) for _ in range(4)],
        compiler_params=_cparams(("parallel",), VMEM_LIMIT),
        name="s5_scan",
    )(u, toep, winr, wini, wsor, wsoi, lr, li, d_tile)
    return _s5_from_chunks(y, nb, n_ctx, t - n_ctx)


def _merge_kernel(x_ref, mod_ref, g1_ref, ya_ref, yb_ref, ps_ref, oc_ref, dg_ref, y5_ref,
                  wg_ref, wb_ref, wo_ref, avg_ref, ones_ref, rk_ref, gup_ref, lng_ref, lnb_ref,
                  dng_ref, wglu_ref, bglu_ref, o_ref):
    avg = avg_ref[...]

    def branches(i):
        ps = ps_ref[i]
        r = ps[:, 0:256]
        k = ps[:, 256:512]
        v = ps[:, 512:768]
        gl = ps[:, 832:896]
        y = yb_ref[0, i] + yb_ref[1, i]
        dev = y - _dot(y, avg)
        yn = dev * lax.rsqrt(_dot(dev * dev, avg) + RW_GN_EPS) * lng_ref[...] + lnb_ref[...]
        bonus = _dot(r * k * rk_ref[...], ones_ref[...]) * v
        yb = (yn + bonus) * _dot(_sigmoid(gl), gup_ref[...])
        o = oc_ref[0, i] + oc_ref[1, i]
        yc = o * lax.rsqrt(_dot(o * o, avg) + EPS) * dng_ref[...] * _silu(dg_ref[i])
        z = jax.nn.gelu(y5_ref[i])
        yd = z * _sigmoid(_dot(z, wglu_ref[...]) + bglu_ref[...])
        return ya_ref[i], yb, yc, yd

    steps = range(MERGE_BATCH)
    xs = [x_ref[i] for i in steps]
    hs = [_norm_mod(xs[i], g1_ref[...], mod_ref[i, 1:2, :], mod_ref[i, 0:1, :]).astype(BF16) for i in steps]
    ys = [branches(i) for i in steps]
    ms = [jnp.zeros((TOKEN_BLOCK, D_MODEL), F32) for _ in steps]
    for n in range(4):
        for i in steps:
            gate = _sigmoid(jnp.dot(hs[i], wg_ref[0, :, D_MODEL * n:D_MODEL * (n + 1)], preferred_element_type=F32))
            ms[i] = ms[i] + gate * _dot(ys[i][n], wb_ref[0, n])
    for i in steps:
        o_ref[i] = xs[i] + mod_ref[i, 2:3, :] * _dot(ms[i], wo_ref[0])


def _merge(xcat, mod, g1, ya, yb, ps, oc, dg, y5, w_gate, w_branch, w_out, layer, avg_bd, ones_bd,
           r_k, g_up, ln_g, ln_b, dn_g, w_glu, b_glu, nblk_ctx):
    nb, t, _ = xcat.shape
    nblk = t // TOKEN_BLOCK
    tok = lambda w: pl.BlockSpec((MERGE_BATCH, TOKEN_BLOCK, w), lambda b, j: (b, j, 0))
    tok2 = lambda w: pl.BlockSpec((2, MERGE_BATCH, TOKEN_BLOCK, w), lambda b, j: (0, b, j, 0))
    full = lambda a: pl.BlockSpec(a.shape, lambda b, j: (0,) * a.ndim)
    stacked = (w_gate, w_branch, w_out)
    consts = (avg_bd, ones_bd, r_k, g_up, ln_g, ln_b, dn_g, w_glu, b_glu)
    return pl.pallas_call(
        _merge_kernel,
        out_shape=jax.ShapeDtypeStruct(xcat.shape, F32),
        grid=(nb // MERGE_BATCH, nblk),
        in_specs=[tok(D_MODEL), pl.BlockSpec((MERGE_BATCH, N_MOD, D_MODEL), _mod_index(nblk_ctx, nb, MERGE_BATCH)),
                  pl.BlockSpec((1, D_MODEL), lambda b, j: (0, 0)),
                  tok(BRANCH_W), tok2(BRANCH_W), tok(RW_COLS), tok2(BRANCH_W), tok(BRANCH_W), tok(BRANCH_W)]
        + [_layer_spec(a, layer) for a in stacked] + [full(a) for a in consts],
        out_specs=tok(D_MODEL),
        compiler_params=_cparams(("parallel", "arbitrary"), VMEM_LIMIT),
        name="merge_branches",
    )(xcat, mod, g1, ya, yb, ps, oc, dg, y5, *stacked, *consts)


def _route(sel, score):
    s = [sel[e:e + 1, :] for e in range(N_EXPERTS)]
    sc = [score[e:e + 1, :] for e in range(N_EXPERTS)]
    n_groups = N_EXPERTS // EXPERTS_PER_GROUP
    group_score = []
    for g in range(n_groups):
        m = s[EXPERTS_PER_GROUP * g:EXPERTS_PER_GROUP * (g + 1)]
        best = None
        for i in range(EXPERTS_PER_GROUP):
            for j in range(i + 1, EXPERTS_PER_GROUP):
                pair = m[i] + m[j]
                best = pair if best is None else jnp.maximum(best, pair)
        group_score.append(best)
    best_g = jnp.zeros(group_score[0].shape, jnp.int32)
    best_v = group_score[0]
    for g in range(1, n_groups):
        upd = group_score[g] > best_v
        best_g = jnp.where(upd, g, best_g)
        best_v = jnp.where(upd, group_score[g], best_v)
    chosen = []
    den = jnp.zeros_like(best_v)
    for e in range(N_EXPERTS):
        g = e // EXPERTS_PER_GROUP
        rank = jnp.zeros(best_g.shape, jnp.int32)
        for j in range(EXPERTS_PER_GROUP * g, EXPERTS_PER_GROUP * (g + 1)):
            if j == e:
                continue
            ahead = (s[j] > s[e]) if j > e else (s[j] >= s[e])
            rank = rank + jnp.where(ahead, 1, 0)
        pick = jnp.logical_and(best_g == g, rank < 2)
        chosen.append(pick)
        den = den + jnp.where(pick, sc[e], 0.0)
    return jnp.concatenate([jnp.where(chosen[e], sc[e] / den, 0.0) for e in range(N_EXPERTS)], axis=0)


def _moe_kernel(final, x_ref, mod_ref, g2_ref, w1_ref, w3_ref, w2_ref, rwt_ref, rb_ref, exp_ref, fg_ref, o_ref):
    steps = range(MOE_BATCH)
    xs = [x_ref[i] for i in steps]
    hs = [_norm_mod(xs[i], g2_ref[...], mod_ref[i, 4:5, :], mod_ref[i, 3:4, :]) for i in steps]
    scores = [_sigmoid(lax.dot_general(rwt_ref[...], h, _NT, preferred_element_type=F32, precision=HIGHEST))
              for h in hs]
    combs = [_route(sc + rb_ref[...], sc).astype(BF16) for sc in scores]
    hbs = [h.astype(BF16) for h in hs]
    accs = [jnp.zeros((TOKEN_BLOCK, D_MODEL), F32) for _ in steps]
    for q in range(N_EXPERTS // EXPERTS_PER_GROUP):
        experts = range(EXPERTS_PER_GROUP * q, EXPERTS_PER_GROUP * (q + 1))
        cols = slice(D_EXPERT * experts[0], D_EXPERT * (experts[-1] + 1))
        for i in steps:
            cw = lax.dot_general(combs[i], exp_ref[:, cols], _TN, preferred_element_type=F32)
            a1 = jnp.concatenate([jnp.dot(hbs[i], w1_ref[0, e], preferred_element_type=F32) for e in experts], axis=1)
            a3 = jnp.concatenate([jnp.dot(hbs[i], w3_ref[0, e], preferred_element_type=F32) for e in experts], axis=1)
            act = (_silu(a1) * a3 * cw).astype(BF16)
            accs[i] = accs[i] + jnp.dot(act, w2_ref[0, cols, :], preferred_element_type=F32)
    for i in steps:
        y = xs[i] + mod_ref[i, 5:6, :] * accs[i]
        if final:
            y = y * lax.rsqrt(jnp.mean(y * y, axis=-1, keepdims=True) + EPS) * fg_ref[...]
        o_ref[i] = y


def _moe(x1, mod, g2, router_wt, router_b, w1, w3, w2, layer, expand, nblk_ctx, final_g=None):
    nb, t, _ = x1.shape
    nblk = t // TOKEN_BLOCK
    final = final_g is not None
    tok = pl.BlockSpec((MOE_BATCH, TOKEN_BLOCK, D_MODEL), lambda b, j: (b, j, 0))
    full = lambda a: pl.BlockSpec(a.shape, lambda b, j: (0,) * a.ndim, pipeline_mode=pl.Buffered(1))
    fg = (final_g if final else jnp.ones((D_MODEL,), F32)).reshape(1, D_MODEL).astype(F32)
    consts = (router_wt, router_b, expand, fg)
    if final:
        out_shape = jax.ShapeDtypeStruct((nb, t - nblk_ctx * TOKEN_BLOCK, D_MODEL), F32)
        out_spec = pl.BlockSpec((MOE_BATCH, TOKEN_BLOCK, D_MODEL), lambda b, j: (b, jnp.maximum(j - nblk_ctx, 0), 0))
    else:
        out_shape, out_spec = jax.ShapeDtypeStruct(x1.shape, F32), tok
    return pl.pallas_call(
        functools.partial(_moe_kernel, final),
        out_shape=out_shape,
        grid=(nb // MOE_BATCH, nblk),
        in_specs=[tok, pl.BlockSpec((MOE_BATCH, N_MOD, D_MODEL), _mod_index(nblk_ctx, nb, MOE_BATCH)),
                  pl.BlockSpec((1, D_MODEL), lambda b, j: (0, 0))]
        + [_layer_spec(a, layer, pipeline_mode=pl.Buffered(1)) for a in (w1, w3, w2)] + [full(a) for a in consts],
        out_specs=out_spec,
        compiler_params=_cparams(("parallel", "arbitrary"), VMEM_LIMIT),
        name="moe_ffn",
    )(x1, mod, g2, w1, w3, w2, *consts)


def _block_diag_ones(n_blocks, size):
    return jnp.kron(jnp.eye(n_blocks, dtype=F32), jnp.ones((size, size), F32))


def _layer(xcat, cond, n_ctx, layer, lp, wts, router_wt, router_b, consts, final_g=None):
    nb, t, _ = xcat.shape
    nblk_ctx = n_ctx // TOKEN_BLOCK
    mod = _modulation(cond, lp["w_mod"], lp["b_mod"])
    g1 = lp["norm1_g"].reshape(1, D_MODEL)

    fcs, ps, u, dg, s5, dab = _input_projection(xcat, mod, g1, wts["w_mix"], layer, consts["cs"], lp["rw_mu"],
                                                lp["dn_conv"], nblk_ctx)

    ya = jnp.concatenate([_sequence_dft(fcs, 0, n_ctx), _sequence_dft(fcs, n_ctx, t - n_ctx)], axis=1)

    yb = _rwkv_scan(ps, lp["rw_w0"], lp["rw_w_up"], lp["rw_a0"], lp["rw_a_up"], lp["rw_k_k"], lp["rw_k_a"],
                    consts["ones_bd"], nblk_ctx)
    oc = _deltanet_scan(u, dab, lp["dn_a_log"], lp["dn_dt_bias"], consts["ones_bd"], nblk_ctx)
    y5 = _s5_scan(s5, lp["s5_tables"], lp["s5_d"], n_ctx)

    row = lambda a: a.reshape(1, -1).astype(F32)
    x1 = _merge(xcat, mod, g1, ya, yb, ps, oc, dg, y5, wts["w_gate"], wts["w_branch"], wts["w_out"], layer,
                consts["avg_bd"], consts["ones_bd"], row(lp["rw_r_k"]),
                lp["rw_g_up"].astype(BF16), row(lp["rw_ln_g"]), row(lp["rw_ln_b"]),
                row(jnp.tile(lp["dn_norm_g"], N_HEADS)), lp["s5_w_glu"].astype(BF16), row(lp["s5_b_glu"]),
                nblk_ctx)

    return _moe(x1, mod, lp["norm2_g"].reshape(1, D_MODEL), router_wt, router_b, wts["moe_w1"], wts["moe_w3"],
                wts["moe_w2"], layer, consts["expand"], nblk_ctx, final_g)


def _mixer_column_order():
    offs = [0]
    for width in W_IN_SPLITS:
        offs.append(offs[-1] + width)
    order = [0, 1, 2, 3, 6, 4, 5]
    idx = [c for i in order for c in range(offs[i], offs[i + 1])]
    pad = -len(idx) % 128
    keep = [1.0] * len(idx) + [0.0] * pad
    return jnp.asarray(idx + [0] * pad, jnp.int32), jnp.asarray(keep, F32), offs[7]


def kernel(x, c, ctx, c_ctx, w_mod, b_mod, norm1_g, norm2_g, w_in, rw_mu, rw_w0, rw_w_up, rw_a0, rw_a_up, rw_k_k, rw_k_a, rw_r_k, rw_g_up, rw_ln_g, rw_ln_b, dn_conv, dn_a_log, dn_dt_bias, dn_norm_g, s5_lam_re, s5_lam_im, s5_log_step, s5_b_re, s5_b_im, s5_c_re, s5_c_im, s5_d, s5_w_glu, s5_b_glu, w_branch, w_out, router_w, router_b, moe_w1, moe_w3, moe_w2, final_g):
    nb, n_lat, _ = x.shape
    n_ctx = ctx.shape[1]
    depth = w_mod.shape[0]
    assert n_ctx % TOKEN_BLOCK == 0 and n_lat % TOKEN_BLOCK == 0 and (n_lat // 64) % S5_CHUNK == 0
    assert nb % CHUNK_BATCH == 0 and nb % MOE_BATCH == 0 and nb % INPROJ_BATCH == 0 and nb % MERGE_BATCH == 0

    xcat = jnp.concatenate([ctx, x], axis=1).astype(F32)
    ctx_rows = max(MOE_BATCH, INPROJ_BATCH, MERGE_BATCH)
    cond_rows = -(-(nb + ctx_rows) // 8) * 8
    cond = jnp.zeros((cond_rows, D_MODEL), F32).at[:nb].set(c).at[nb:nb + ctx_rows].set(c_ctx)

    j = jnp.arange(FN_GW, dtype=jnp.int32)
    ang = ((j[:, None] * j[None, :]) % FN_GW).astype(F32) * (2.0 * math.pi / FN_GW)
    eye = jnp.eye(BRANCH_W // FN_GW, dtype=F32)
    consts = {
        "cs": jnp.concatenate([jnp.kron(eye, jnp.cos(ang)), jnp.kron(eye, jnp.sin(ang))], axis=1).astype(BF16),
        "ones_bd": _block_diag_ones(N_HEADS, HEAD_DIM).astype(BF16),
        "avg_bd": (_block_diag_ones(N_HEADS, HEAD_DIM) / HEAD_DIM).astype(BF16),
        "expand": jnp.kron(jnp.eye(N_EXPERTS, dtype=F32), jnp.ones((1, D_EXPERT), F32)).astype(BF16),
    }
    router_wt = router_w.T.astype(F32)
    router_bc = router_b.reshape(N_EXPERTS, 1).astype(F32)

    cols, keep, gate0 = _mixer_column_order()
    wts = {
        "w_mix": (jnp.take(w_in, cols, axis=2) * keep).astype(BF16),
        "w_gate": w_in[:, :, gate0:].astype(BF16),
        "w_branch": w_branch.astype(BF16),
        "w_out": w_out.astype(BF16),
        "moe_w1": moe_w1.astype(BF16),
        "moe_w3": moe_w3.astype(BF16),
        "moe_w2": moe_w2.reshape(depth, N_EXPERTS * D_EXPERT, D_MODEL).astype(BF16),
    }
    s5_tables = jax.vmap(_s5_tables)(s5_lam_re, s5_lam_im, s5_log_step, s5_b_re, s5_b_im, s5_c_re, s5_c_im)
    small = dict(w_mod=w_mod, b_mod=b_mod, norm1_g=norm1_g, norm2_g=norm2_g, rw_mu=rw_mu, rw_w0=rw_w0,
                 rw_w_up=rw_w_up, rw_a0=rw_a0, rw_a_up=rw_a_up, rw_k_k=rw_k_k, rw_k_a=rw_k_a, rw_r_k=rw_r_k,
                 rw_g_up=rw_g_up, rw_ln_g=rw_ln_g, rw_ln_b=rw_ln_b, dn_conv=dn_conv, dn_a_log=dn_a_log,
                 dn_dt_bias=dn_dt_bias, dn_norm_g=dn_norm_g, s5_d=s5_d, s5_w_glu=s5_w_glu, s5_b_glu=s5_b_glu)
    for i in range(depth):
        lp = {n: a[i] for n, a in small.items()}
        lp["s5_tables"] = tuple(tb[i] for tb in s5_tables)
        xcat = _layer(xcat, cond, n_ctx, i, lp, wts, router_wt, router_bc, consts,
                      final_g if i == depth - 1 else None)
    return xcat
```

```python
import functools
import math

import jax
import jax.numpy as jnp
from jax import lax
from jax.experimental import pallas as pl
from jax.experimental.pallas import tpu as pltpu

F32 = jnp.float32
BF16 = jnp.bfloat16
HIGHEST = lax.Precision.HIGHEST

D_MODEL = 1024
N_MOD = 6
EPS = 1e-6
BRANCH_W = 256
HEAD_DIM = 64
N_HEADS = 4
FN_GW = 64
RW_COLS = 896
RW_DECAY_SCALE = math.exp(-0.5)
RW_GN_EPS = 64e-5
DN_CONV = 5
S5_GW = 16
S5_GROUPS = 16
S5_STATE = 64
N_EXPERTS = 16
EXPERTS_PER_GROUP = 4
D_EXPERT = 256
W_IN_SPLITS = (256, 896, 768, 256, 8, 8, 256, 4096)

TOKEN_BLOCK = 256
SCAN_CHUNK = 64
S5_CHUNK = 16
HALO = 8
MXU_DEPTH = 256
INPROJ_BATCH = 4
MERGE_BATCH = 2
VMEM_LIMIT = 56 * 1024 * 1024


def _cparams(sem, vmem=None, **kw):
    return pltpu.CompilerParams(dimension_semantics=sem, vmem_limit_bytes=vmem, **kw)


def _dot(a, b):
    return jnp.dot(a.astype(BF16), b.astype(BF16), preferred_element_type=F32)


def _sigmoid(x):
    return jax.nn.sigmoid(x)


def _silu(x):
    return x * jax.nn.sigmoid(x)


def _norm_mod(x, g, scale, shift):
    y = x * lax.rsqrt(jnp.mean(x * x, axis=-1, keepdims=True) + EPS) * g
    return y * (1.0 + scale) + shift


def _layer_spec(a, layer, **kw):
    return pl.BlockSpec((1,) + a.shape[1:], lambda *_: (layer,) + (0,) * (a.ndim - 1), **kw)


def _mod_index(nblk_ctx, n_batch, per_step=1):
    return lambda b, j: (jnp.where(j < nblk_ctx, n_batch // per_step, b), 0, 0)


def _mod_kernel(c_ref, w_ref, b_ref, o_ref):
    o_ref[...] = _dot(_silu(c_ref[...]), w_ref[...]) + b_ref[...]


def _modulation(cond, w_mod, b_mod):
    rows = cond.shape[0]
    n = w_mod.shape[1]
    tn = 512
    out = pl.pallas_call(
        _mod_kernel,
        out_shape=jax.ShapeDtypeStruct((rows, n), F32),
        grid=(n // tn,),
        in_specs=[pl.BlockSpec((rows, D_MODEL), lambda i: (0, 0)),
                  pl.BlockSpec((D_MODEL, tn), lambda i: (0, i)),
                  pl.BlockSpec((1, tn), lambda i: (0, i))],
        out_specs=pl.BlockSpec((rows, tn), lambda i: (0, i)),
        compiler_params=_cparams(("arbitrary",)),
        name="adaln_mod",
    )(cond, w_mod, b_mod.reshape(1, n))
    return out.reshape(rows, N_MOD, D_MODEL)


def _shifted(x, s, halo_prev, halo_next):
    n = x.shape[0]
    if s == 0:
        return x
    rows = lax.broadcasted_iota(jnp.int32, x.shape, 0)
    if s < 0:
        y = pltpu.roll(x, -s, 0)
        for t in range(-s):
            y = jnp.where(rows == t, halo_prev[HALO + s + t:HALO + s + t + 1, :], y)
    else:
        y = pltpu.roll(x, n - s, 0)
        for t in range(s):
            y = jnp.where(rows == n - s + t, halo_next[t:t + 1, :], y)
    return y


def _inproj_kernel(nblk_ctx, nblk, x_ref, xp_ref, xn_ref, mod_ref, g_ref, w_ref, cs_ref, mu_ref, cw_ref,
                   fcs_ref, ps_ref, u_ref, dg_ref, s5_ref, dab_ref):
    j = pl.program_id(1)
    first = jnp.logical_or(j == 0, j == nblk_ctx)
    last = jnp.logical_or(j == nblk_ctx - 1, j == nblk - 1)
    pv = jnp.where(first, 0.0, 1.0).astype(F32)
    nv = jnp.where(last, 0.0, 1.0).astype(F32)

    steps = range(INPROJ_BATCH)
    h_f = [_norm_mod(jnp.concatenate([xp_ref[i], x_ref[i], xn_ref[i]], axis=0), g_ref[...],
                     mod_ref[i, 1:2, :], mod_ref[i, 0:1, :]) for i in steps]
    hs = [x[HALO:HALO + TOKEN_BLOCK].astype(BF16) for x in h_f]
    h_alls = [x.astype(BF16) for x in h_f]

    for i in steps:
        fn = jnp.dot(hs[i], w_ref[0, :, 0:256], preferred_element_type=F32)
        fcs_ref[i] = _dot(fn, cs_ref[...]).astype(BF16)
        dg_ref[i] = jnp.dot(hs[i], w_ref[0, :, 1920:2176], preferred_element_type=F32)
        s5_ref[i] = jnp.dot(hs[i], w_ref[0, :, 2176:2432], preferred_element_type=F32)
        dab_ref[i] = jnp.dot(hs[i], w_ref[0, :, 2432:2560], preferred_element_type=F32)

    def with_halo(h_all, cols):
        p = jnp.dot(h_all, w_ref[0, :, cols], preferred_element_type=F32)
        return p[HALO:HALO + TOKEN_BLOCK], p[0:HALO] * pv, p[HALO + TOKEN_BLOCK:] * nv

    for i, (p, hp, hn) in enumerate([with_halo(x, slice(256, 1152)) for x in h_alls]):
        ps_ref[i] = (p + mu_ref[0:1, :] * (_shifted(p, -1, hp, hn) - p)
                     + mu_ref[1:2, :] * (_shifted(p, 1, hp, hn) - p))

    pad = DN_CONV // 2
    for i, (q, hp, hn) in enumerate([with_halo(x, slice(1152, 1920)) for x in h_alls]):
        acc = cw_ref[pad:pad + 1, :] * q
        for t in range(DN_CONV):
            if t != pad:
                acc = acc + cw_ref[t:t + 1, :] * _shifted(q, t - pad, hp, hn)
        u_ref[i] = _silu(acc)


def _input_projection(xcat, mod, g1, w_mix, layer, cs, mu, conv_w, nblk_ctx):
    nb, t, _ = xcat.shape
    nblk = t // TOKEN_BLOCK
    per = TOKEN_BLOCK // HALO
    last_halo = t // HALO - 1
    widths = (512, RW_COLS, 3 * BRANCH_W, 256, 256, 128)
    dtypes = (BF16, F32, F32, F32, F32, F32)
    tok = lambda w: pl.BlockSpec((INPROJ_BATCH, TOKEN_BLOCK, w), lambda b, j: (b, j, 0))
    full = lambda a: pl.BlockSpec(a.shape, lambda b, j: (0,) * a.ndim)
    return pl.pallas_call(
        functools.partial(_inproj_kernel, nblk_ctx, nblk),
        out_shape=[jax.ShapeDtypeStruct((nb, t, w), dt) for w, dt in zip(widths, dtypes)],
        grid=(nb // INPROJ_BATCH, nblk),
        in_specs=[tok(D_MODEL),
                  pl.BlockSpec((INPROJ_BATCH, HALO, D_MODEL), lambda b, j: (b, jnp.maximum(j * per - 1, 0), 0)),
                  pl.BlockSpec((INPROJ_BATCH, HALO, D_MODEL), lambda b, j: (b, jnp.minimum((j + 1) * per, last_halo), 0)),
                  pl.BlockSpec((INPROJ_BATCH, N_MOD, D_MODEL), _mod_index(nblk_ctx, nb, INPROJ_BATCH)),
                  pl.BlockSpec((1, D_MODEL), lambda b, j: (0, 0)),
                  _layer_spec(w_mix, layer), full(cs), full(mu), full(conv_w)],
        out_specs=[tok(w) for w in widths],
        compiler_params=_cparams(("parallel", "arbitrary"), VMEM_LIMIT),
        name="norm1_inproj",
    )(xcat, xcat, xcat, mod, g1, w_mix, cs, mu, conv_w)


def _dft_kernel(n_batch, wc_ref, ws_ref, u_ref, o_ref):
    @pl.when(pl.program_id(1) == 0)
    def _():
        o_ref[...] = jnp.zeros_like(o_ref)

    w = jnp.concatenate([wc_ref[...], ws_ref[...]], axis=1)
    for b in range(n_batch):
        u = u_ref[b]
        o_ref[b] += jnp.dot(w, jnp.concatenate([u[:, 0:256], u[:, 256:512]], axis=0), preferred_element_type=F32)


def _dft_tables(n):
    n2 = math.isqrt(n)
    assert n2 * n2 == n
    k = jnp.arange(n, dtype=jnp.int32)[None, :]
    j = jnp.arange(n2, dtype=jnp.int32)[:, None]
    ang_a = ((j * k) % n2).astype(F32) * (2.0 * math.pi / n2)
    ang_b = ((j * k) % n).astype(F32) * (2.0 * math.pi / n)
    ca, sa, cb, sb = jnp.cos(ang_a)[:, None], jnp.sin(ang_a)[:, None], jnp.cos(ang_b)[None], jnp.sin(ang_b)[None]
    scale = 1.0 / math.sqrt(n * FN_GW)
    cos = ((ca * cb - sa * sb) * scale).reshape(n, n)
    sin = ((sa * cb + ca * sb) * scale).reshape(n, n)
    return cos.astype(BF16), (-sin).astype(BF16)


def _sequence_dft(fcs, row0, n):
    nb = fcs.shape[0]
    wc, ws = _dft_tables(n)
    tm = min(n, 1024)
    tk = min(n, 256)
    assert n % tm == 0 and n % tk == 0 and row0 % tk == 0
    kb0 = row0 // tk
    return pl.pallas_call(
        functools.partial(_dft_kernel, nb),
        out_shape=jax.ShapeDtypeStruct((nb, n, BRANCH_W), F32),
        grid=(n // tm, n // tk),
        in_specs=[pl.BlockSpec((tm, tk), lambda m, k: (m, k)),
                  pl.BlockSpec((tm, tk), lambda m, k: (m, k)),
                  pl.BlockSpec((nb, tk, 512), lambda m, k: (0, k + kb0, 0))],
        out_specs=pl.BlockSpec((nb, tm, BRANCH_W), lambda m, k: (0, m, 0)),
        compiler_params=_cparams(("parallel", "arbitrary"), VMEM_LIMIT),
        name="fourier_seq_dft",
    )(wc, ws, fcs)


def _chunk_order(nc_ctx, nc):
    def order(d, c):
        back = jnp.where(c < nc_ctx, nc_ctx - 1 - c, nc - 1 + nc_ctx - c)
        return jnp.where(d == 0, c, back)
    return order


PAIR = 2 * HEAD_DIM
N_PAIRS = N_HEADS // 2
CHUNKS_PER_BLOCK = TOKEN_BLOCK // SCAN_CHUNK
MAP_W = 4 * PAIR
CHUNK_BATCH = 4
MOE_BATCH = 4
_NN = (((1,), (0,)), ((), ()))
_NT = (((1,), (1,)), ((), ()))
_TN = (((0,), (0,)), ((), ()))
_LOG2_CHUNK = int(math.log2(SCAN_CHUNK))


def _split(x):
    hi = x.astype(BF16)
    return hi, (x - hi.astype(F32)).astype(BF16)


def _mm3(a, b, dims=_NN):
    a_hi, a_lo = _split(a)
    b_hi, b_lo = _split(b)
    dg = lambda x, y: lax.dot_general(x, y, dims, preferred_element_type=F32)
    ca, cb = dims[0][0][0], dims[0][1][0]
    if a.shape[ca] <= MXU_DEPTH // 2:
        return (dg(jnp.concatenate([a_hi, a_lo], axis=ca), jnp.concatenate([b_hi, b_hi], axis=cb))
                + dg(a_hi, b_lo))
    return dg(a_hi, b_hi) + dg(a_hi, b_lo) + dg(a_lo, b_hi)


def _split3(x):
    hi = x.astype(BF16)
    rest = x - hi.astype(F32)
    mid = rest.astype(BF16)
    return hi, mid, (rest - mid.astype(F32)).astype(BF16)


def _select_rows(op, x):
    op = op.astype(BF16)
    hi, mid, lo = _split3(x)
    dot = lambda y: jnp.dot(op, y, preferred_element_type=F32)
    return dot(hi) + dot(mid) + dot(lo)


def _spread_cols(x, sel):
    sel = sel.astype(BF16)
    hi, mid, lo = _split3(x)
    assert x.shape[1] <= MXU_DEPTH // 2
    return (jnp.dot(jnp.concatenate([hi, mid], axis=1), jnp.concatenate([sel, sel], axis=0),
                    preferred_element_type=F32)
            + jnp.dot(lo, sel, preferred_element_type=F32))


def _pair_masks(d):
    row = lax.broadcasted_iota(jnp.int32, (PAIR, PAIR), 0)
    col = lax.broadcasted_iota(jnp.int32, (PAIR, PAIR), 1)
    same = jnp.right_shift(row, _LOG2_CHUNK) == jnp.right_shift(col, _LOG2_CHUNK)
    t_row = lax.broadcasted_iota(jnp.int32, (SCAN_CHUNK, PAIR), 0)
    t_col = jnp.bitwise_and(lax.broadcasted_iota(jnp.int32, (SCAN_CHUNK, PAIR), 1), SCAN_CHUNK - 1)
    delta = (t_row - t_col) * (1 - 2 * d)
    return same, delta >= 0, delta > 0, t_row == t_col


def _block_time_operators(d):
    row = lax.broadcasted_iota(jnp.int32, (TOKEN_BLOCK, TOKEN_BLOCK), 0)
    col = lax.broadcasted_iota(jnp.int32, (TOKEN_BLOCK, TOKEN_BLOCK), 1)
    same = jnp.right_shift(row, _LOG2_CHUNK) == jnp.right_shift(col, _LOG2_CHUNK)
    delta = (jnp.bitwise_and(row, SCAN_CHUNK - 1) - jnp.bitwise_and(col, SCAN_CHUNK - 1)) * (1 - 2 * d)
    return jnp.where(jnp.logical_and(same, delta >= 0), 1.0, 0.0).astype(F32)


def _chunk_problems():
    return [(i, slice(SCAN_CHUNK * c, SCAN_CHUNK * (c + 1)), slice(PAIR * pr, PAIR * (pr + 1)))
            for i in range(CHUNK_BATCH) for c in range(CHUNKS_PER_BLOCK) for pr in range(N_PAIRS)]


def _bd(x, same):
    return jnp.where(same, jnp.concatenate([x, x], axis=0), jnp.zeros((), x.dtype))


def _pack(x, same):
    x = jnp.where(same, x, 0.0)
    return x[0:HEAD_DIM] + x[HEAD_DIM:PAIR]


def _mm_pk(a, bs, same):
    a_hi, a_lo = _split(a)
    parts = [_split(b) for b in bs]
    r_hi = jnp.concatenate([_bd(hi, same) for hi, _ in parts], axis=1)
    r_lo = jnp.concatenate([_bd(lo, same) for _, lo in parts], axis=1)
    return (jnp.dot(jnp.concatenate([a_hi, a_lo], axis=1), jnp.concatenate([r_hi, r_hi], axis=0),
                    preferred_element_type=F32)
            + jnp.dot(a_hi, r_lo, preferred_element_type=F32))


def _mm_pk_fast(a, bs, same):
    rhs = jnp.concatenate([_bd(b.astype(BF16), same) for b in bs], axis=1)
    return jnp.dot(a.astype(BF16), rhs, preferred_element_type=F32)


def _mm_pk_nt(a, bs, same):
    a_hi, a_lo = _split(a)
    parts = [_split(b) for b in bs]
    r_hi = jnp.concatenate([_bd(hi, same) for hi, _ in parts], axis=0)
    r_lo = jnp.concatenate([_bd(lo, same) for _, lo in parts], axis=0)
    dg = lambda x, y: lax.dot_general(x, y, _NT, preferred_element_type=F32)
    return dg(jnp.concatenate([a_hi, a_lo], axis=1), jnp.concatenate([r_hi, r_hi], axis=1)) + dg(a_hi, r_lo)


def _head_sums(x, ones):
    hi, lo = _split(x)
    return jnp.dot(hi, ones, preferred_element_type=F32) + jnp.dot(lo, ones, preferred_element_type=F32)


def _chunk_cumsum(d, x):
    g = _select_rows(_block_time_operators(d), x)
    last = [jnp.where(d == 0, g[SCAN_CHUNK * (c + 1) - 1:SCAN_CHUNK * (c + 1)], g[SCAN_CHUNK * c:SCAN_CHUNK * c + 1])
            for c in range(CHUNKS_PER_BLOCK)]
    return g, jnp.concatenate([jnp.broadcast_to(r, (SCAN_CHUNK, r.shape[1])) for r in last], axis=0)


def _unit_tri_inverse(n_pks, eye_pk, same):
    xs = [eye_pk + n for n in n_pks]
    ps = [_mm_pk(n, [n], same) for n in n_pks]
    for level in range(1, _LOG2_CHUNK):
        if level + 1 < _LOG2_CHUNK:
            xps = [_mm_pk(p, [x, p], same) for x, p in zip(xs, ps)]
            xs = [x + xp[:, 0:PAIR] for x, xp in zip(xs, xps)]
            ps = [xp[:, PAIR:2 * PAIR] for xp in xps]
        else:
            xs = [x + _mm_pk(p, [x], same) for x, p in zip(xs, ps)]
    return xs


def _reset_chunk_state(maps_ref, h_ref):
    @pl.when(pl.program_id(2) == 0)
    def _():
        h_ref[...] = jnp.zeros_like(h_ref)
        maps_ref[...] = jnp.zeros_like(maps_ref)


def _apply_chunk_maps(d, maps_ref, h_ref, y_ref, same):
    chains = [(i, pr) for i in range(CHUNK_BATCH) for pr in range(N_PAIRS)]
    states = [h_ref[i, pr] for i, pr in chains]
    for k in range(CHUNKS_PER_BLOCK):
        row0 = pl.multiple_of(jnp.where(d == 0, k, CHUNKS_PER_BLOCK - 1 - k) * SCAN_CHUNK, SCAN_CHUNK)
        rows = pl.ds(row0, SCAN_CHUNK)
        part = lambda i, pr, n: maps_ref[i, rows, MAP_W * pr + PAIR * n:MAP_W * pr + PAIR * (n + 1)]
        outs = [_mm_pk(jnp.concatenate([part(i, pr, 0), part(i, pr, 2)], axis=0), [h], same)
                for (i, pr), h in zip(chains, states)]
        states = [out[0:SCAN_CHUNK] + part(i, pr, 1) for out, (i, pr) in zip(outs, chains)]
        for out, (i, pr) in zip(outs, chains):
            y_ref[0, i, rows, PAIR * pr:PAIR * (pr + 1)] = out[SCAN_CHUNK:] + part(i, pr, 3)
    for (i, pr), h in zip(chains, states):
        h_ref[i, pr] = h


def _chunk_scan_call(kernel, name, operands, in_specs, nb, t, nblk_ctx):
    nblk = t // TOKEN_BLOCK
    order = _chunk_order(nblk_ctx, nblk)
    blk_in = lambda d, j: order(d, jnp.minimum(j, nblk - 1))
    blk_out = lambda d, j: order(d, jnp.maximum(j - 1, 0))
    return pl.pallas_call(
        kernel,
        out_shape=jax.ShapeDtypeStruct((2, nb, t, BRANCH_W), F32),
        grid=(2, nb // CHUNK_BATCH, nblk + 1),
        in_specs=in_specs(blk_in),
        out_specs=pl.BlockSpec((1, CHUNK_BATCH, TOKEN_BLOCK, BRANCH_W), lambda d, b, j: (d, b, blk_out(d, j), 0)),
        scratch_shapes=[pltpu.VMEM((CHUNK_BATCH, TOKEN_BLOCK, N_PAIRS * MAP_W), F32),
                        pltpu.VMEM((CHUNK_BATCH, N_PAIRS, HEAD_DIM, PAIR), F32)],
        compiler_params=_cparams(("parallel", "parallel", "arbitrary"), VMEM_LIMIT),
        name=name,
    )(*operands)


def _rwkv_block_inputs(p, w0, w_up, a0, a_up, k_k, k_a, ones, d):
    r = p[:, 0:256]
    k = p[:, 256:512]
    v = p[:, 512:768]
    wl = p[:, 768:800]
    al = p[:, 800:832]
    lw = -RW_DECAY_SCALE * _sigmoid(w0 + _dot(jnp.tanh(wl), w_up))
    a = _sigmoid(a0 + _dot(al, a_up))
    kkp = k * k_k
    kk = kkp * lax.rsqrt(_head_sums(kkp * kkp, ones) + EPS)
    kmod = k * (1.0 + (a - 1.0) * k_a)
    alpha = -(a * kk)
    g, g_tot = _chunk_cumsum(d, lw)
    e_neg = jnp.exp(-g)
    e_tail = jnp.exp(g_tot - g)
    return dict(b=kk * jnp.exp(g - lw), r=r * jnp.exp(g), kh=kmod * e_neg, ah=alpha * e_neg,
                kt=kmod * e_tail, at=alpha * e_tail, v=v, gam=jnp.exp(g_tot))


def _rwkv_chunk_kernel(p_ref, w0_ref, wup_ref, a0_ref, aup_ref, kk_ref, ka_ref, ones_ref, y_ref, o_ref, h_ref):
    d = pl.program_id(0)
    same, incl, strict, eye = _pair_masks(d)
    eye_f = jnp.where(eye, 1.0, 0.0).astype(F32)
    zeros = jnp.zeros((SCAN_CHUNK, PAIR), F32)
    _reset_chunk_state(o_ref, h_ref)
    pre = [_rwkv_block_inputs(p_ref[i], w0_ref[0], wup_ref[0], a0_ref[0], aup_ref[0], kk_ref[...], ka_ref[...],
                              ones_ref[...], d) for i in range(CHUNK_BATCH)]
    probs = _chunk_problems()
    pk = lambda name: [pre[i][name][rows, lanes] for i, rows, lanes in probs]
    cat = jnp.concatenate
    b_t, r_t, k_h, a_h, k_t, a_t, v_p = pk("b"), pk("r"), pk("kh"), pk("ah"), pk("kt"), pk("at"), pk("v")
    m = [_mm_pk_nt(cat([b, r_], axis=0), [kh, ah], same) for b, r_, kh, ah in zip(b_t, r_t, k_h, a_h)]
    _apply_chunk_maps(d, o_ref, h_ref, y_ref, same)
    a_bk = [jnp.where(strict, x[0:SCAN_CHUNK, 0:PAIR], 0.0) for x in m]
    a_ba = [jnp.where(strict, x[0:SCAN_CHUNK, PAIR:], 0.0) for x in m]
    a_rk = [jnp.where(incl, x[SCAN_CHUNK:, 0:PAIR], 0.0) for x in m]
    a_ra = [jnp.where(incl, x[SCAN_CHUNK:, PAIR:], 0.0) for x in m]
    av = [_mm_pk(cat([x, y], axis=0), [vp], same) for x, y, vp in zip(a_bk, a_rk, v_p)]
    t_inv = _unit_tri_inverse(a_ba, eye_f, same)
    sol = [_mm_pk(t, [b, x[0:SCAN_CHUNK]], same) for t, b, x in zip(t_inv, b_t, av)]
    qy = [cat([r_, x[SCAN_CHUNK:]], axis=1) + _mm_pk_fast(a, [s[:, 0:PAIR], s[:, PAIR:]], same)
          for r_, x, a, s in zip(r_t, av, a_ra, sol)]
    kb = [_mm3(cat([at, kt], axis=0), cat([s, cat([zeros, vp], axis=1)], axis=0), _TN)
          for at, kt, s, vp in zip(a_t, k_t, sol, v_p)]
    for (i, rows, lanes), kb_i, qy_i in zip(probs, kb, qy):
        base = MAP_W * (lanes.start // PAIR)
        o_ref[i, rows, base:base + PAIR] = (_pack(kb_i[:, 0:PAIR], same)
                                               + jnp.where(eye, pre[i]["gam"][rows.start:rows.start + 1, lanes], 0.0))
        o_ref[i, rows, base + PAIR:base + 2 * PAIR] = _pack(kb_i[:, PAIR:], same)
        o_ref[i, rows, base + 2 * PAIR:base + 4 * PAIR] = qy_i


def _rwkv_scan(ps, w0, w_up, a0, a_up, k_k, k_a, ones_bd, nblk_ctx):
    nb, t, _ = ps.shape
    per_dir = lambda shape: pl.BlockSpec((1,) + shape, lambda d, b, j: (d, 0, 0))
    const = lambda shape: pl.BlockSpec(shape, lambda d, b, j: (0, 0))
    in_specs = lambda blk: [pl.BlockSpec((CHUNK_BATCH, TOKEN_BLOCK, RW_COLS), lambda d, b, j: (b, blk(d, j), 0)),
                            per_dir((1, BRANCH_W)), per_dir((32, BRANCH_W)),
                            per_dir((1, BRANCH_W)), per_dir((32, BRANCH_W)),
                            const((1, BRANCH_W)), const((1, BRANCH_W)), const((BRANCH_W, BRANCH_W))]
    operands = (ps, w0.reshape(2, 1, BRANCH_W), w_up, a0.reshape(2, 1, BRANCH_W), a_up,
                k_k.reshape(1, BRANCH_W), k_a.reshape(1, BRANCH_W), ones_bd)
    return _chunk_scan_call(_rwkv_chunk_kernel, "rwkv7_scan", operands, in_specs, nb, t, nblk_ctx)


def _deltanet_block_inputs(u, dab, neg_exp_a, dt_bias, expand_a, expand_b, ones, d):
    log_a8 = neg_exp_a * jax.nn.softplus(dab + dt_bias)
    la = _spread_cols(log_a8, expand_a)
    beta = _spread_cols(_sigmoid(dab), expand_b)
    q = u[:, 0:256]
    k = u[:, 256:512]
    v = u[:, 512:768]
    q = q * lax.rsqrt(_head_sums(q * q, ones) + EPS) * (HEAD_DIM ** -0.5)
    k = k * lax.rsqrt(_head_sums(k * k, ones) + EPS)
    g, g_tot = _chunk_cumsum(d, la)
    e_g = jnp.exp(g)
    kb = k * beta
    return dict(g=g, q=q, k=k, kb=kb, vb=v * beta, kbe=kb * e_g, qe=q * e_g, kt=k * jnp.exp(g_tot - g),
                gam=jnp.exp(g_tot))


def _deltanet_chunk_kernel(u_ref, dab_ref, nea_ref, dtb_ref, ea_ref, eb_ref, ones_ref, y_ref, o_ref, h_ref):
    d = pl.program_id(0)
    same, incl, strict, eye = _pair_masks(d)
    eye_f = jnp.where(eye, 1.0, 0.0).astype(F32)
    _reset_chunk_state(o_ref, h_ref)
    pre = [_deltanet_block_inputs(u_ref[i], dab_ref[i], nea_ref[...], dtb_ref[...], ea_ref[0], eb_ref[0],
                                  ones_ref[...], d) for i in range(CHUNK_BATCH)]
    probs = _chunk_problems()
    pk = lambda name: [pre[i][name][rows, lanes] for i, rows, lanes in probs]
    cat = jnp.concatenate
    g_p = pk("g")
    g_t = [_pack(_bd(x, same).T, same) for x in g_p]
    decay = [jnp.exp(jnp.where(incl, x - y, -jnp.inf)) for x, y in zip(g_p, g_t)]
    m = [_mm_pk_nt(cat([kb, q_], axis=0), [k_], same) for kb, q_, k_ in zip(pk("kb"), pk("q"), pk("k"))]
    _apply_chunk_maps(d, o_ref, h_ref, y_ref, same)
    a_low = [jnp.where(strict, x[0:SCAN_CHUNK] * dc, 0.0) for x, dc in zip(m, decay)]
    attn = [x[SCAN_CHUNK:] * dc for x, dc in zip(m, decay)]
    t_inv = _unit_tri_inverse([-x for x in a_low], eye_f, same)
    sol = [_mm_pk(t, [vb, kbe], same) for t, vb, kbe in zip(t_inv, pk("vb"), pk("kbe"))]
    att_sol = [_mm_pk_fast(a, [s[:, 0:PAIR], s[:, PAIR:]], same) for a, s in zip(attn, sol)]
    ks = [_mm3(kt, s, _TN) for kt, s in zip(pk("kt"), sol)]
    for (i, rows, lanes), ks_i, as_i, qe_i in zip(probs, ks, att_sol, pk("qe")):
        base = MAP_W * (lanes.start // PAIR)
        o_ref[i, rows, base:base + PAIR] = (jnp.where(eye, pre[i]["gam"][rows.start:rows.start + 1, lanes], 0.0)
                                               - _pack(ks_i[:, PAIR:], same))
        o_ref[i, rows, base + PAIR:base + 2 * PAIR] = _pack(ks_i[:, 0:PAIR], same)
        o_ref[i, rows, base + 2 * PAIR:base + 3 * PAIR] = qe_i - as_i[:, PAIR:]
        o_ref[i, rows, base + 3 * PAIR:base + 4 * PAIR] = as_i[:, 0:PAIR]


def _deltanet_scan(u, dab, a_log, dt_bias, ones_bd, nblk_ctx):
    nb, t, _ = u.shape
    pad = 128 - 2 * N_HEADS
    neg_exp_a = jnp.pad(-jnp.exp(a_log.reshape(1, -1)), ((0, 0), (0, pad)))
    dtb = jnp.pad(dt_bias.reshape(1, -1), ((0, 0), (0, pad)))
    col = jnp.arange(128)[None, :, None]
    head = (jnp.arange(BRANCH_W) // HEAD_DIM)[None, None, :]
    dirs = jnp.arange(2)[:, None, None]
    expand_a = (col == dirs * N_HEADS + head).astype(F32)
    expand_b = (col == 2 * N_HEADS + dirs * N_HEADS + head).astype(F32)
    const = lambda shape: pl.BlockSpec(shape, lambda d, b, j: (0, 0))
    per_dir = pl.BlockSpec((1, 128, BRANCH_W), lambda d, b, j: (d, 0, 0))
    in_specs = lambda blk: [pl.BlockSpec((CHUNK_BATCH, TOKEN_BLOCK, 3 * BRANCH_W), lambda d, b, j: (b, blk(d, j), 0)),
                            pl.BlockSpec((CHUNK_BATCH, TOKEN_BLOCK, 128), lambda d, b, j: (b, blk(d, j), 0)),
                            const((1, 128)), const((1, 128)), per_dir, per_dir, const((BRANCH_W, BRANCH_W))]
    operands = (u, dab, neg_exp_a, dtb, expand_a, expand_b, ones_bd)
    return _chunk_scan_call(_deltanet_chunk_kernel, "deltanet_scan", operands, in_specs, nb, t, nblk_ctx)


def _s5_kernel(n_batch, nc_ctx, nc, u_ref, toep_ref, winr_ref, wini_ref, wsor_ref, wsoi_ref,
               lr_ref, li_ref, dt_ref, y_ref, injr, inji, xsr, xsi):
    u = u_ref[0]
    ub = u.astype(BF16)
    for d in range(2):
        injr[d] = jnp.dot(ub, winr_ref[d, 0], preferred_element_type=F32)
        inji[d] = jnp.dot(ub, wini_ref[d, 0], preferred_element_type=F32)
    lam = [(lr_ref[d, 0], li_ref[d, 0]) for d in range(2)]

    def body(s, carry):
        back = jnp.where(s < nc_ctx, nc_ctx - 1 - s, nc - 1 + nc_ctx - s)
        out = []
        for d, chunk in enumerate((s, back)):
            xr, xi = carry[2 * d], carry[2 * d + 1]
            row0 = chunk * n_batch
            if n_batch % 8 == 0:
                row0 = pl.multiple_of(row0, 8)
            rows = pl.ds(row0, n_batch)
            xsr[d, rows, :] = xr
            xsi[d, rows, :] = xi
            lam_r, lam_i = lam[d]
            out += [lam_r * xr - lam_i * xi + injr[d, rows, :], lam_r * xi + lam_i * xr + inji[d, rows, :]]
        return tuple(out)

    zero = jnp.zeros((n_batch, S5_STATE), F32)
    lax.fori_loop(0, nc, body, (zero,) * 4)

    y = u * dt_ref[0]
    for d in range(2):
        y = y + (jnp.dot(ub, toep_ref[d, 0], preferred_element_type=F32)
                 + _dot(xsr[d], wsor_ref[d, 0]) + _dot(xsi[d], wsoi_ref[d, 0]))
    y_ref[0] = y


def _s5_tables(lam_re, lam_im, log_step, b_re, b_im, c_re, c_im):
    cs = S5_CHUNK
    lam = lax.complex(lam_re.astype(F32), lam_im.astype(F32))
    step = jnp.exp(log_step.astype(F32))[..., None]
    tau = jnp.arange(cs + 1, dtype=F32)[:, None, None, None]
    lam_pow = jnp.exp(lam[None] * step[None] * tau)
    lam_bar = lam_pow[1]
    b_bar = ((lam_bar - 1.0) / lam)[..., None] * lax.complex(b_re.astype(F32), b_im.astype(F32))
    c_mat = lax.complex(c_re.astype(F32), c_im.astype(F32))
    kern = jnp.real(jnp.einsum("dghp,tdgp,dgpk->tdghk", c_mat, lam_pow[:cs], b_bar))
    i = jnp.arange(cs)
    lag_f = i[None, :] - i[:, None]
    toeps, winr, wini, wsor, wsoi = [], [], [], [], []
    for d in range(2):
        lag = lag_f if d == 0 else -lag_f
        kd = jnp.where((lag >= 0)[:, :, None, None, None], kern[:, d][jnp.clip(lag, 0, cs - 1)], 0.0)
        toeps.append(kd.transpose(2, 0, 4, 1, 3).reshape(S5_GROUPS, cs * S5_GW, cs * S5_GW))
        pw_in = (cs - 1 - i) if d == 0 else i
        e = lam_pow[pw_in, d][..., None] * b_bar[d][None]
        e = e.transpose(1, 0, 3, 2).reshape(S5_GROUPS, cs * S5_GW, S5_STATE)
        winr.append(jnp.real(e))
        wini.append(jnp.imag(e))
        pw_out = (i + 1) if d == 0 else (cs - i)
        m = c_mat[d][None] * lam_pow[pw_out, d][:, :, None, :]
        m = m.transpose(1, 3, 0, 2).reshape(S5_GROUPS, S5_STATE, cs * S5_GW)
        wsor.append(jnp.real(m))
        wsoi.append(-jnp.imag(m))
    stack = lambda xs, dt: jnp.stack(xs).astype(dt)
    lam_c = lam_pow[cs]
    return (stack(toeps, BF16), stack(winr, BF16), stack(wini, BF16), stack(wsor, BF16), stack(wsoi, BF16),
            jnp.real(lam_c)[:, :, None, :], jnp.imag(lam_c)[:, :, None, :])


def _s5_to_chunks(s5, n_ctx):
    nb, t, _ = s5.shape
    cs, g, hw = S5_CHUNK, S5_GROUPS, S5_GW
    n_lat = t - n_ctx
    rows = n_lat // 64
    c = s5[:, :n_ctx].reshape(nb, n_ctx // cs, cs, g, hw)
    c = c.transpose(3, 1, 0, 2, 4).reshape(g, (n_ctx // cs) * nb, cs * hw)
    l = s5[:, n_ctx:].reshape(nb, rows // cs, cs, 64, g, hw)
    l = l.transpose(4, 3, 1, 0, 2, 5).reshape(g, 64 * (rows // cs) * nb, cs * hw)
    return jnp.concatenate([c, l], axis=1)


def _s5_from_chunks(y, nb, n_ctx, n_lat):
    cs, g, hw = S5_CHUNK, S5_GROUPS, S5_GW
    rows = n_lat // 64
    r_ctx = (n_ctx // cs) * nb
    c = y[:, :r_ctx].reshape(g, n_ctx // cs, nb, cs, hw).transpose(2, 1, 3, 0, 4).reshape(nb, n_ctx, g * hw)
    l = y[:, r_ctx:].reshape(g, 64, rows // cs, nb, cs, hw).transpose(3, 2, 4, 1, 0, 5).reshape(nb, n_lat, g * hw)
    return jnp.concatenate([c, l], axis=1)


def _s5_scan(s5, tables, d_skip, n_ctx):
    nb, t, _ = s5.shape
    u = _s5_to_chunks(s5, n_ctx)
    g, r, w = u.shape
    nc = t // S5_CHUNK
    toep, winr, wini, wsor, wsoi, lr, li = tables
    d_tile = jnp.tile(d_skip.astype(F32).reshape(S5_GROUPS, 1, S5_GW), (1, S5_CHUNK, 1)).reshape(g, 1, w)
    per = lambda a, b: pl.BlockSpec((2, 1, a, b), lambda gi: (0, gi, 0, 0))
    y = pl.pallas_call(
        functools.partial(_s5_kernel, nb, n_ctx // S5_CHUNK, nc),
        out_shape=jax.ShapeDtypeStruct((g, r, w), F32),
        grid=(g,),
        in_specs=[pl.BlockSpec((1, r, w), lambda gi: (gi, 0, 0)),
                  per(w, w), per(w, S5_STATE), per(w, S5_STATE), per(S5_STATE, w), per(S5_STATE, w),
                  per(1, S5_STATE), per(1, S5_STATE),
                  pl.BlockSpec((1, 1, w), lambda gi: (gi, 0, 0))],
        out_specs=pl.BlockSpec((1, r, w), lambda gi: (gi, 0, 0)),
        scratch_shapes=[pltpu.VMEM((2, r, S5_STATE), F32) for _ in range(4)],
        compiler_params=_cparams(("parallel",), VMEM_LIMIT),
        name="s5_scan",
    )(u, toep, winr, wini, wsor, wsoi, lr, li, d_tile)
    return _s5_from_chunks(y, nb, n_ctx, t - n_ctx)


def _merge_kernel(x_ref, mod_ref, g1_ref, ya_ref, yb_ref, ps_ref, oc_ref, dg_ref, y5_ref,
                  wg_ref, wb_ref, wo_ref, avg_ref, ones_ref, rk_ref, gup_ref, lng_ref, lnb_ref,
                  dng_ref, wglu_ref, bglu_ref, o_ref):
    avg = avg_ref[...]

    def branches(i):
        ps = ps_ref[i]
        r = ps[:, 0:256]
        k = ps[:, 256:512]
        v = ps[:, 512:768]
        gl = ps[:, 832:896]
        y = yb_ref[0, i] + yb_ref[1, i]
        dev = y - _dot(y, avg)
        yn = dev * lax.rsqrt(_dot(dev * dev, avg) + RW_GN_EPS) * lng_ref[...] + lnb_ref[...]
        bonus = _dot(r * k * rk_ref[...], ones_ref[...]) * v
        yb = (yn + bonus) * _dot(_sigmoid(gl), gup_ref[...])
        o = oc_ref[0, i] + oc_ref[1, i]
        yc = o * lax.rsqrt(_dot(o * o, avg) + EPS) * dng_ref[...] * _silu(dg_ref[i])
        z = jax.nn.gelu(y5_ref[i])
        yd = z * _sigmoid(_dot(z, wglu_ref[...]) + bglu_ref[...])
        return ya_ref[i], yb, yc, yd

    steps = range(MERGE_BATCH)
    xs = [x_ref[i] for i in steps]
    hs = [_norm_mod(xs[i], g1_ref[...], mod_ref[i, 1:2, :], mod_ref[i, 0:1, :]).astype(BF16) for i in steps]
    ys = [branches(i) for i in steps]
    ms = [jnp.zeros((TOKEN_BLOCK, D_MODEL), F32) for _ in steps]
    for n in range(4):
        for i in steps:
            gate = _sigmoid(jnp.dot(hs[i], wg_ref[0, :, D_MODEL * n:D_MODEL * (n + 1)], preferred_element_type=F32))
            ms[i] = ms[i] + gate * _dot(ys[i][n], wb_ref[0, n])
    for i in steps:
        o_ref[i] = xs[i] + mod_ref[i, 2:3, :] * _dot(ms[i], wo_ref[0])


def _merge(xcat, mod, g1, ya, yb, ps, oc, dg, y5, w_gate, w_branch, w_out, layer, avg_bd, ones_bd,
           r_k, g_up, ln_g, ln_b, dn_g, w_glu, b_glu, nblk_ctx):
    nb, t, _ = xcat.shape
    nblk = t // TOKEN_BLOCK
    tok = lambda w: pl.BlockSpec((MERGE_BATCH, TOKEN_BLOCK, w), lambda b, j: (b, j, 0))
    tok2 = lambda w: pl.BlockSpec((2, MERGE_BATCH, TOKEN_BLOCK, w), lambda b, j: (0, b, j, 0))
    full = lambda a: pl.BlockSpec(a.shape, lambda b, j: (0,) * a.ndim)
    stacked = (w_gate, w_branch, w_out)
    consts = (avg_bd, ones_bd, r_k, g_up, ln_g, ln_b, dn_g, w_glu, b_glu)
    return pl.pallas_call(
        _merge_kernel,
        out_shape=jax.ShapeDtypeStruct(xcat.shape, F32),
        grid=(nb // MERGE_BATCH, nblk),
        in_specs=[tok(D_MODEL), pl.BlockSpec((MERGE_BATCH, N_MOD, D_MODEL), _mod_index(nblk_ctx, nb, MERGE_BATCH)),
                  pl.BlockSpec((1, D_MODEL), lambda b, j: (0, 0)),
                  tok(BRANCH_W), tok2(BRANCH_W), tok(RW_COLS), tok2(BRANCH_W), tok(BRANCH_W), tok(BRANCH_W)]
        + [_layer_spec(a, layer) for a in stacked] + [full(a) for a in consts],
        out_specs=tok(D_MODEL),
        compiler_params=_cparams(("parallel", "arbitrary"), VMEM_LIMIT),
        name="merge_branches",
    )(xcat, mod, g1, ya, yb, ps, oc, dg, y5, *stacked, *consts)


def _route(sel, score):
    s = [sel[e:e + 1, :] for e in range(N_EXPERTS)]
    sc = [score[e:e + 1, :] for e in range(N_EXPERTS)]
    n_groups = N_EXPERTS // EXPERTS_PER_GROUP
    group_score = []
    for g in range(n_groups):
        m = s[EXPERTS_PER_GROUP * g:EXPERTS_PER_GROUP * (g + 1)]
        best = None
        for i in range(EXPERTS_PER_GROUP):
            for j in range(i + 1, EXPERTS_PER_GROUP):
                pair = m[i] + m[j]
                best = pair if best is None else jnp.maximum(best, pair)
        group_score.append(best)
    best_g = jnp.zeros(group_score[0].shape, jnp.int32)
    best_v = group_score[0]
    for g in range(1, n_groups):
        upd = group_score[g] > best_v
        best_g = jnp.where(upd, g, best_g)
        best_v = jnp.where(upd, group_score[g], best_v)
    chosen = []
    den = jnp.zeros_like(best_v)
    for e in range(N_EXPERTS):
        g = e // EXPERTS_PER_GROUP
        rank = jnp.zeros(best_g.shape, jnp.int32)
        for j in range(EXPERTS_PER_GROUP * g, EXPERTS_PER_GROUP * (g + 1)):
            if j == e:
                continue
            ahead = (s[j] > s[e]) if j > e else (s[j] >= s[e])
            rank = rank + jnp.where(ahead, 1, 0)
        pick = jnp.logical_and(best_g == g, rank < 2)
        chosen.append(pick)
        den = den + jnp.where(pick, sc[e], 0.0)
    return jnp.concatenate([jnp.where(chosen[e], sc[e] / den, 0.0) for e in range(N_EXPERTS)], axis=0)


def _moe_kernel(final, x_ref, mod_ref, g2_ref, w1_ref, w3_ref, w2_ref, rwt_ref, rb_ref, exp_ref, fg_ref, o_ref):
    steps = range(MOE_BATCH)
    xs = [x_ref[i] for i in steps]
    hs = [_norm_mod(xs[i], g2_ref[...], mod_ref[i, 4:5, :], mod_ref[i, 3:4, :]) for i in steps]
    scores = [_sigmoid(lax.dot_general(rwt_ref[...], h, _NT, preferred_element_type=F32, precision=HIGHEST))
              for h in hs]
    combs = [_route(sc + rb_ref[...], sc).astype(BF16) for sc in scores]
    hbs = [h.astype(BF16) for h in hs]
    accs = [jnp.zeros((TOKEN_BLOCK, D_MODEL), F32) for _ in steps]
    for q in range(N_EXPERTS // EXPERTS_PER_GROUP):
        experts = range(EXPERTS_PER_GROUP * q, EXPERTS_PER_GROUP * (q + 1))
        cols = slice(D_EXPERT * experts[0], D_EXPERT * (experts[-1] + 1))
        for i in steps:
            cw = lax.dot_general(combs[i], exp_ref[:, cols], _TN, preferred_element_type=F32)
            a1 = jnp.concatenate([jnp.dot(hbs[i], w1_ref[0, e], preferred_element_type=F32) for e in experts], axis=1)
            a3 = jnp.concatenate([jnp.dot(hbs[i], w3_ref[0, e], preferred_element_type=F32) for e in experts], axis=1)
            act = (_silu(a1) * a3 * cw).astype(BF16)
            accs[i] = accs[i] + jnp.dot(act, w2_ref[0, cols, :], preferred_element_type=F32)
    for i in steps:
        y = xs[i] + mod_ref[i, 5:6, :] * accs[i]
        if final:
            y = y * lax.rsqrt(jnp.mean(y * y, axis=-1, keepdims=True) + EPS) * fg_ref[...]
        o_ref[i] = y


def _moe(x1, mod, g2, router_wt, router_b, w1, w3, w2, layer, expand, nblk_ctx, final_g=None):
    nb, t, _ = x1.shape
    nblk = t // TOKEN_BLOCK
    final = final_g is not None
    tok = pl.BlockSpec((MOE_BATCH, TOKEN_BLOCK, D_MODEL), lambda b, j: (b, j, 0))
    full = lambda a: pl.BlockSpec(a.shape, lambda b, j: (0,) * a.ndim, pipeline_mode=pl.Buffered(1))
    fg = (final_g if final else jnp.ones((D_MODEL,), F32)).reshape(1, D_MODEL).astype(F32)
    consts = (router_wt, router_b, expand, fg)
    if final:
        out_shape = jax.ShapeDtypeStruct((nb, t - nblk_ctx * TOKEN_BLOCK, D_MODEL), F32)
        out_spec = pl.BlockSpec((MOE_BATCH, TOKEN_BLOCK, D_MODEL), lambda b, j: (b, jnp.maximum(j - nblk_ctx, 0), 0))
    else:
        out_shape, out_spec = jax.ShapeDtypeStruct(x1.shape, F32), tok
    return pl.pallas_call(
        functools.partial(_moe_kernel, final),
        out_shape=out_shape,
        grid=(nb // MOE_BATCH, nblk),
        in_specs=[tok, pl.BlockSpec((MOE_BATCH, N_MOD, D_MODEL), _mod_index(nblk_ctx, nb, MOE_BATCH)),
                  pl.BlockSpec((1, D_MODEL), lambda b, j: (0, 0))]
        + [_layer_spec(a, layer, pipeline_mode=pl.Buffered(1)) for a in (w1, w3, w2)] + [full(a) for a in consts],
        out_specs=out_spec,
        compiler_params=_cparams(("parallel", "arbitrary"), VMEM_LIMIT),
        name="moe_ffn",
    )(x1, mod, g2, w1, w3, w2, *consts)


def _block_diag_ones(n_blocks, size):
    return jnp.kron(jnp.eye(n_blocks, dtype=F32), jnp.ones((size, size), F32))


def _layer(xcat, cond, n_ctx, layer, lp, wts, router_wt, router_b, consts, final_g=None):
    nb, t, _ = xcat.shape
    nblk_ctx = n_ctx // TOKEN_BLOCK
    mod = _modulation(cond, lp["w_mod"], lp["b_mod"])
    g1 = lp["norm1_g"].reshape(1, D_MODEL)

    fcs, ps, u, dg, s5, dab = _input_projection(xcat, mod, g1, wts["w_mix"], layer, consts["cs"], lp["rw_mu"],
                                                lp["dn_conv"], nblk_ctx)

    ya = jnp.concatenate([_sequence_dft(fcs, 0, n_ctx), _sequence_dft(fcs, n_ctx, t - n_ctx)], axis=1)

    yb = _rwkv_scan(ps, lp["rw_w0"], lp["rw_w_up"], lp["rw_a0"], lp["rw_a_up"], lp["rw_k_k"], lp["rw_k_a"],
                    consts["ones_bd"], nblk_ctx)
    oc = _deltanet_scan(u, dab, lp["dn_a_log"], lp["dn_dt_bias"], consts["ones_bd"], nblk_ctx)
    y5 = _s5_scan(s5, lp["s5_tables"], lp["s5_d"], n_ctx)

    row = lambda a: a.reshape(1, -1).astype(F32)
    x1 = _merge(xcat, mod, g1, ya, yb, ps, oc, dg, y5, wts["w_gate"], wts["w_branch"], wts["w_out"], layer,
                consts["avg_bd"], consts["ones_bd"], row(lp["rw_r_k"]),
                lp["rw_g_up"].astype(BF16), row(lp["rw_ln_g"]), row(lp["rw_ln_b"]),
                row(jnp.tile(lp["dn_norm_g"], N_HEADS)), lp["s5_w_glu"].astype(BF16), row(lp["s5_b_glu"]),
                nblk_ctx)

    return _moe(x1, mod, lp["norm2_g"].reshape(1, D_MODEL), router_wt, router_b, wts["moe_w1"], wts["moe_w3"],
                wts["moe_w2"], layer, consts["expand"], nblk_ctx, final_g)


def _mixer_column_order():
    offs = [0]
    for width in W_IN_SPLITS:
        offs.append(offs[-1] + width)
    order = [0, 1, 2, 3, 6, 4, 5]
    idx = [c for i in order for c in range(offs[i], offs[i + 1])]
    pad = -len(idx) % 128
    keep = [1.0] * len(idx) + [0.0] * pad
    return jnp.asarray(idx + [0] * pad, jnp.int32), jnp.asarray(keep, F32), offs[7]


def kernel(x, c, ctx, c_ctx, w_mod, b_mod, norm1_g, norm2_g, w_in, rw_mu, rw_w0, rw_w_up, rw_a0, rw_a_up, rw_k_k, rw_k_a, rw_r_k, rw_g_up, rw_ln_g, rw_ln_b, dn_conv, dn_a_log, dn_dt_bias, dn_norm_g, s5_lam_re, s5_lam_im, s5_log_step, s5_b_re, s5_b_im, s5_c_re, s5_c_im, s5_d, s5_w_glu, s5_b_glu, w_branch, w_out, router_w, router_b, moe_w1, moe_w3, moe_w2, final_g):
    nb, n_lat, _ = x.shape
    n_ctx = ctx.shape[1]
    depth = w_mod.shape[0]
    assert n_ctx % TOKEN_BLOCK == 0 and n_lat % TOKEN_BLOCK == 0 and (n_lat // 64) % S5_CHUNK == 0
    assert nb % CHUNK_BATCH == 0 and nb % MOE_BATCH == 0 and nb % INPROJ_BATCH == 0 and nb % MERGE_BATCH == 0

    xcat = jnp.concatenate([ctx, x], axis=1).astype(F32)
    ctx_rows = max(MOE_BATCH, INPROJ_BATCH, MERGE_BATCH)
    cond_rows = -(-(nb + ctx_rows) // 8) * 8
    cond = jnp.zeros((cond_rows, D_MODEL), F32).at[:nb].set(c).at[nb:nb + ctx_rows].set(c_ctx)

    j = jnp.arange(FN_GW, dtype=jnp.int32)
    ang = ((j[:, None] * j[None, :]) % FN_GW).astype(F32) * (2.0 * math.pi / FN_GW)
    eye = jnp.eye(BRANCH_W // FN_GW, dtype=F32)
    consts = {
        "cs": jnp.concatenate([jnp.kron(eye, jnp.cos(ang)), jnp.kron(eye, jnp.sin(ang))], axis=1).astype(BF16),
        "ones_bd": _block_diag_ones(N_HEADS, HEAD_DIM).astype(BF16),
        "avg_bd": (_block_diag_ones(N_HEADS, HEAD_DIM) / HEAD_DIM).astype(BF16),
        "expand": jnp.kron(jnp.eye(N_EXPERTS, dtype=F32), jnp.ones((1, D_EXPERT), F32)).astype(BF16),
    }
    router_wt = router_w.T.astype(F32)
    router_bc = router_b.reshape(N_EXPERTS, 1).astype(F32)

    cols, keep, gate0 = _mixer_column_order()
    wts = {
        "w_mix": (jnp.take(w_in, cols, axis=2) * keep).astype(BF16),
        "w_gate": w_in[:, :, gate0:].astype(BF16),
        "w_branch": w_branch.astype(BF16),
        "w_out": w_out.astype(BF16),
        "moe_w1": moe_w1.astype(BF16),
        "moe_w3": moe_w3.astype(BF16),
        "moe_w2": moe_w2.reshape(depth, N_EXPERTS * D_EXPERT, D_MODEL).astype(BF16),
    }
    s5_tables = jax.vmap(_s5_tables)(s5_lam_re, s5_lam_im, s5_log_step, s5_b_re, s5_b_im, s5_c_re, s5_c_im)
    small = dict(w_mod=w_mod, b_mod=b_mod, norm1_g=norm1_g, norm2_g=norm2_g, rw_mu=rw_mu, rw_w0=rw_w0,
                 rw_w_up=rw_w_up, rw_a0=rw_a0, rw_a_up=rw_a_up, rw_k_k=rw_k_k, rw_k_a=rw_k_a, rw_r_k=rw_r_k,
                 rw_g_up=rw_g_up, rw_ln_g=rw_ln_g, rw_ln_b=rw_ln_b, dn_conv=dn_conv, dn_a_log=dn_a_log,
                 dn_dt_bias=dn_dt_bias, dn_norm_g=dn_norm_g, s5_d=s5_d, s5_w_glu=s5_w_glu, s5_b_glu=s5_b_glu)
    for i in range(depth):
        lp = {n: a[i] for n, a in small.items()}
        lp["s5_tables"] = tuple(tb[i] for tb in s5_tables)
        xcat = _layer(xcat, cond, n_ctx, i, lp, wts, router_wt, router_bc, consts,
                      final_g if i == depth - 1 else None)
    return xcat
```
